```python
import math
import jax, jax.numpy as jnp
from jax import lax
import numpy as np

D_MODEL = 1024
BATCH = 16
SEQ = 2048
DEPTH = 2

MEM_LEN = 256
EPS = 1e-6
RWKV_WIDTH = D_MODEL // 2
RWKV_HEAD = 64
RWKV_HEADS = RWKV_WIDTH // RWKV_HEAD
DECAY_RANK = 64
ICLR_RANK = 64
GATE_RANK = 128
VRES_RANK = 32
GN_EPS = 64e-5
DIFF_WIDTH = D_MODEL - RWKV_WIDTH
DIFF_HEADS = 4
DIFF_VDIM = DIFF_WIDTH // DIFF_HEADS
DIFF_QDIM = DIFF_VDIM // 2
ROPE_DIMS = DIFF_QDIM // 4
ROPE_THETA = 500000.0
Q_BLOCK = 128
XATTN_HEADS = 4
XATTN_HEAD = D_MODEL // XATTN_HEADS
D_FF_DENSE = 2816
N_EXPERTS = 8
TOP_K = 2
D_FF_EXPERT = 3584
N_DENSE = (DEPTH + 1) // 2
N_MOE = DEPTH // 2
RWKV_COLS = 3 * RWKV_WIDTH + DECAY_RANK + ICLR_RANK + GATE_RANK
DIFF_COLS = 3 * DIFF_WIDTH
IN_COLS = RWKV_COLS + DIFF_COLS
RWKV_SPLITS = (RWKV_WIDTH, 2 * RWKV_WIDTH, 3 * RWKV_WIDTH, 3 * RWKV_WIDTH + DECAY_RANK, 3 * RWKV_WIDTH + DECAY_RANK + ICLR_RANK)

kernel_name = "hybrid_rwkv7_diffattn_moe_block"


def rms_norm(x, g):
    xf = x.astype(jnp.float32)
    y = xf * lax.rsqrt(jnp.mean(xf * xf, axis=-1, keepdims=True) + EPS)
    return (y * g.astype(jnp.float32)).astype(x.dtype)


def token_shift(p, mu):
    prev = jnp.pad(p, ((0, 0), (1, 0), (0, 0)))[:, :-1]
    return p + (prev - p) * mu


def rwkv7_step(state, inp):
    r, w, k, v, kk, a = inp
    sa = jnp.einsum('bhij,bhj->bhi', state, -kk)
    state = state * w[:, :, None, :] + sa[..., None] * (kk * a)[:, :, None, :] + v[..., None] * k[:, :, None, :]
    y = jnp.einsum('bhij,bhj->bhi', state, r)
    return state, y


def rwkv7_time_mix(p, v_first, v_mix_logit, w0, w_up, a0, a_up, g_up, k_k, k_a, r_k, gn_w, gn_b):
    B, S, _ = p.shape
    f32 = jnp.float32
    r, k, v, d_w, d_a, d_g = jnp.split(p, RWKV_SPLITS, axis=-1)
    w_log = -jax.nn.softplus(-(w0 + jnp.tanh(d_w) @ w_up).astype(f32)) - 0.5
    decay = jnp.exp(-jnp.exp(w_log))
    a = jax.nn.sigmoid(a0 + d_a @ a_up)
    g = jax.nn.sigmoid(d_g) @ g_up
    if v_first is not None:
        v = v + (v_first - v) * jax.nn.sigmoid(v_mix_logit)
    heads = lambda t: t.reshape(B, S, RWKV_HEADS, RWKV_HEAD)
    kk = heads(k * k_k).astype(f32)
    kk = kk * lax.rsqrt(jnp.maximum(jnp.sum(kk * kk, -1, keepdims=True), 1e-24))
    k = k * (1 + (a - 1) * k_a)
    rh, kh, vh, ah = heads(r), heads(k), heads(v), heads(a)
    seq_major = lambda t: jnp.swapaxes(t.astype(f32), 0, 1)
    xs = (seq_major(rh), seq_major(heads(decay)), seq_major(kh), seq_major(vh), seq_major(kk), seq_major(ah))
    state0 = jnp.zeros((B, RWKV_HEADS, RWKV_HEAD, RWKV_HEAD), f32)
    _, y = lax.scan(rwkv7_step, state0, xs)
    y = jnp.swapaxes(y, 0, 1)
    mean = jnp.mean(y, -1, keepdims=True)
    var = jnp.mean(jnp.square(y - mean), -1, keepdims=True)
    y = ((y - mean) * lax.rsqrt(var + GN_EPS)).reshape(B, S, RWKV_WIDTH)
    y = y.astype(p.dtype) * gn_w + gn_b
    bonus = jnp.sum(rh * kh * r_k, -1, keepdims=True) * vh
    y = (y + bonus.reshape(B, S, RWKV_WIDTH)) * g
    return y, v


def rope_partial(x, cos, sin):
    half = ROPE_DIMS // 2
    x1 = x[..., :half]
    x2 = x[..., half:ROPE_DIMS]
    return jnp.concatenate([x1 * cos - x2 * sin, x2 * cos + x1 * sin, x[..., ROPE_DIMS:]], axis=-1)


def diff_attention(p, cos, sin, q_g, k_g, lq1, lk1, lq2, lk2, subln_g, lam_init):
    B, S, _ = p.shape
    q, k, v = jnp.split(p, [DIFF_WIDTH, 2 * DIFF_WIDTH], axis=-1)
    q = q.reshape(B, S, DIFF_HEADS, 2, DIFF_QDIM)
    k = k.reshape(B, S, DIFF_HEADS, 2, DIFF_QDIM)
    v = v.reshape(B, S, DIFF_HEADS, DIFF_VDIM)
    q = rope_partial(rms_norm(q, q_g), cos, sin)
    k = rope_partial(rms_norm(k, k_g), cos, sin)
    f32 = jnp.float32
    lam = (jnp.exp(jnp.sum(lq1.astype(f32) * lk1.astype(f32))) - jnp.exp(jnp.sum(lq2.astype(f32) * lk2.astype(f32))) + lam_init)
    q = jnp.transpose(q, (0, 2, 3, 1, 4)) * (DIFF_QDIM ** -0.5)
    k = jnp.transpose(k, (0, 2, 3, 1, 4))
    v = jnp.transpose(v, (0, 2, 1, 3))
    n_blocks = S // Q_BLOCK
    qb = jnp.moveaxis(q.reshape(B, DIFF_HEADS, 2, n_blocks, Q_BLOCK, DIFF_QDIM), 3, 0)
    key_pos = jnp.arange(S)

    def one_block(args):
        q_blk, i = args
        s = jnp.einsum('bhmqd,bhmkd->bhmqk', q_blk, k).astype(f32)
        q_pos = i * Q_BLOCK + jnp.arange(Q_BLOCK)
        s = jnp.where(key_pos[None, :] <= q_pos[:, None], s, -1e30)
        pr = jax.nn.softmax(s, axis=-1)
        attn = pr[:, :, 0] - lam * pr[:, :, 1]
        return jnp.einsum('bhqk,bhkd->bhqd', attn.astype(v.dtype), v)

    o = lax.map(one_block, (qb, jnp.arange(n_blocks)))
    o = jnp.transpose(o, (1, 0, 3, 2, 4)).reshape(B, S, DIFF_HEADS, DIFF_VDIM)
    o = rms_norm(o, subln_g) * (1 - lam_init)
    return o.reshape(B, S, DIFF_WIDTH)


def memory_cross_attention(h, m, wq, wkv, wo, q_g, k_g):
    B, S, _ = h.shape
    q = rms_norm((h @ wq).reshape(B, S, XATTN_HEADS, XATTN_HEAD), q_g)
    k, v = jnp.split(m @ wkv, 2, axis=-1)
    k = rms_norm(k.reshape(B, MEM_LEN, XATTN_HEADS, XATTN_HEAD), k_g)
    v = v.reshape(B, MEM_LEN, XATTN_HEADS, XATTN_HEAD)
    s = jnp.einsum('bqhd,bkhd->bhqk', q, k).astype(jnp.float32) * (XATTN_HEAD ** -0.5)
    pr = jax.nn.softmax(s, axis=-1)
    o = jnp.einsum('bhqk,bkhd->bqhd', pr.astype(v.dtype), v).reshape(B, S, D_MODEL)
    return o @ wo


def swiglu(h, w_gu, w_down):
    gate, up = jnp.split(h @ w_gu, 2, axis=-1)
    return (jax.nn.silu(gate) * up) @ w_down


def moe_swiglu(h, w_router, w_gu, w_down):
    logits = (h @ w_router).astype(jnp.float32)
    vals, idx = lax.top_k(logits, TOP_K)
    wts = jax.nn.softmax(vals, axis=-1)
    gates = jnp.einsum('bske,bsk->bse', jax.nn.one_hot(idx, N_EXPERTS, dtype=jnp.float32), wts).astype(h.dtype)
    out = jnp.zeros_like(h)
    for e in range(N_EXPERTS):
        out = out + gates[..., e:e + 1] * swiglu(h, w_gu[e], w_down[e])
    return out


def setup_inputs(seed: int = 0) -> dict:
    key = jax.random.key(seed)
    ks = iter(jax.random.split(key, 64))
    f32 = jnp.float32

    def nrm(shape, scale):
        return scale * jax.random.normal(next(ks), shape, f32)

    def gain(shape):
        return 1.0 + nrm(shape, 0.02)

    def unif(shape, lo, hi):
        return jax.random.uniform(next(ks), shape, f32, lo, hi)

    L, Lv = DEPTH, DEPTH - 1
    x = nrm((BATCH, SEQ, D_MODEL), 1.0)
    mem = nrm((BATCH, MEM_LEN, D_MODEL), 1.0)
    offsets = jax.random.randint(next(ks), (BATCH, 1), 0, 4096, dtype=jnp.int32)
    positions = offsets + jnp.arange(SEQ, dtype=jnp.int32)[None, :]
    return {
        "x": x,
        "mem": mem,
        "positions": positions,
        "mix_norm": gain((L, D_MODEL)),
        "in_proj": nrm((L, D_MODEL, IN_COLS), D_MODEL ** -0.5),
        "shift_mu": unif((L, RWKV_COLS), 0.0, 1.0),
        "decay_bias": unif((L, RWKV_WIDTH), -5.0, -1.0),
        "decay_up": nrm((L, DECAY_RANK, RWKV_WIDTH), 0.1 * DECAY_RANK ** -0.5),
        "iclr_bias": nrm((L, RWKV_WIDTH), 0.1),
        "iclr_up": nrm((L, ICLR_RANK, RWKV_WIDTH), ICLR_RANK ** -0.5),
        "gate_up": nrm((L, GATE_RANK, RWKV_WIDTH), GATE_RANK ** -0.5),
        "key_kk_scale": 0.85 + nrm((L, RWKV_WIDTH), 0.05),
        "key_iclr_scale": 1.0 + nrm((L, RWKV_WIDTH), 0.05),
        "bonus_rk": nrm((L, RWKV_HEADS, RWKV_HEAD), 0.1),
        "rwkv_gn_w": gain((L, RWKV_WIDTH)),
        "rwkv_gn_b": nrm((L, RWKV_WIDTH), 0.02),
        "vres_down": nrm((Lv, D_MODEL, VRES_RANK), D_MODEL ** -0.5),
        "vres_shift_mu": unif((Lv, VRES_RANK), 0.0, 1.0),
        "vres_bias": 0.5 + nrm((Lv, RWKV_WIDTH), 0.1),
        "vres_up": nrm((Lv, VRES_RANK, RWKV_WIDTH), 0.5 * VRES_RANK ** -0.5),
        "q_norm": gain((L, DIFF_QDIM)),
        "k_norm": gain((L, DIFF_QDIM)),
        "lambda_q1": nrm((L, DIFF_QDIM), 0.1),
        "lambda_k1": nrm((L, DIFF_QDIM), 0.1),
        "lambda_q2": nrm((L, DIFF_QDIM), 0.1),
        "lambda_k2": nrm((L, DIFF_QDIM), 0.1),
        "diff_subln": gain((L, DIFF_VDIM)),
        "out_proj": nrm((L, D_MODEL, D_MODEL), D_MODEL ** -0.5),
        "xattn_norm": gain((L, D_MODEL)),
        "mem_norm": gain((L, D_MODEL)),
        "xattn_wq": nrm((L, D_MODEL, D_MODEL), D_MODEL ** -0.5),
        "xattn_wkv": nrm((L, D_MODEL, 2 * D_MODEL), D_MODEL ** -0.5),
        "xattn_wo": nrm((L, D_MODEL, D_MODEL), D_MODEL ** -0.5),
        "xattn_q_norm": gain((L, XATTN_HEAD)),
        "xattn_k_norm": gain((L, XATTN_HEAD)),
        "ffn_norm": gain((L, D_MODEL)),
        "dense_w_gu": nrm((N_DENSE, D_MODEL, 2 * D_FF_DENSE), D_MODEL ** -0.5),
        "dense_w_down": nrm((N_DENSE, D_FF_DENSE, D_MODEL), D_FF_DENSE ** -0.5),
        "router": nrm((N_MOE, D_MODEL, N_EXPERTS), D_MODEL ** -0.5),
        "expert_w_gu": nrm((N_MOE, N_EXPERTS, D_MODEL, 2 * D_FF_EXPERT), D_MODEL ** -0.5),
        "expert_w_down": nrm((N_MOE, N_EXPERTS, D_FF_EXPERT, D_MODEL), D_FF_EXPERT ** -0.5),
    }


def reference(x, mem, positions, mix_norm, in_proj, shift_mu, decay_bias, decay_up, iclr_bias, iclr_up, gate_up,
              key_kk_scale, key_iclr_scale, bonus_rk, rwkv_gn_w, rwkv_gn_b, vres_down, vres_shift_mu, vres_bias,
              vres_up, q_norm, k_norm, lambda_q1, lambda_k1, lambda_q2, lambda_k2, diff_subln, out_proj, xattn_norm,
              mem_norm, xattn_wq, xattn_wkv, xattn_wo, xattn_q_norm, xattn_k_norm, ffn_norm, dense_w_gu,
              dense_w_down, router, expert_w_gu, expert_w_down):
    f32 = jnp.float32
    inv_freq = ROPE_THETA ** (-jnp.arange(0, ROPE_DIMS, 2, dtype=f32) / ROPE_DIMS)
    ang = positions.astype(f32)[..., None] * inv_freq
    cos = jnp.cos(ang)[:, :, None, None, :].astype(x.dtype)
    sin = jnp.sin(ang)[:, :, None, None, :].astype(x.dtype)
    v_first = None
    for l in range(DEPTH):
        h = rms_norm(x, mix_norm[l])
        w_in = in_proj[l] if l == 0 else jnp.concatenate([in_proj[l], vres_down[l - 1]], axis=1)
        proj = h @ w_in
        p_rwkv = token_shift(proj[..., :RWKV_COLS], shift_mu[l])
        p_diff = proj[..., RWKV_COLS:IN_COLS]
        v_mix_logit = None
        if l > 0:
            v_mix_logit = vres_bias[l - 1] + token_shift(proj[..., IN_COLS:], vres_shift_mu[l - 1]) @ vres_up[l - 1]
        y_rwkv, v_l = rwkv7_time_mix(p_rwkv, v_first, v_mix_logit, decay_bias[l], decay_up[l], iclr_bias[l],
                                     iclr_up[l], gate_up[l], key_kk_scale[l], key_iclr_scale[l], bonus_rk[l],
                                     rwkv_gn_w[l], rwkv_gn_b[l])
        if l == 0:
            v_first = v_l
        lam_init = 0.8 - 0.6 * math.exp(-0.3 * l)
        y_diff = diff_attention(p_diff, cos, sin, q_norm[l], k_norm[l], lambda_q1[l], lambda_k1[l], lambda_q2[l],
                                lambda_k2[l], diff_subln[l], lam_init)
        x = x + jnp.concatenate([y_rwkv, y_diff], axis=-1) @ out_proj[l]
        h = rms_norm(x, xattn_norm[l])
        m = rms_norm(mem, mem_norm[l])
        x = x + memory_cross_attention(h, m, xattn_wq[l], xattn_wkv[l], xattn_wo[l], xattn_q_norm[l], xattn_k_norm[l])
        h = rms_norm(x, ffn_norm[l])
        if l % 2 == 0:
            x = x + swiglu(h, dense_w_gu[l // 2], dense_w_down[l // 2])
        else:
            x = x + moe_swiglu(h, router[l // 2], expert_w_gu[l // 2], expert_w_down[l // 2])
    return x
```

```python
import functools
import math

import jax
import jax.numpy as jnp
from jax import lax
from jax.experimental import pallas as pl
from jax.experimental.pallas import tpu as pltpu

F32 = jnp.float32
BF16 = jnp.bfloat16
HIGHEST = lax.Precision.HIGHEST

EPS = 1e-6
GN_EPS = 64e-5
ROPE_THETA = 500000.0
RWKV_HEAD = 64
DIFF_HEADS = 4
XATTN_HEADS = 4
TOP_K = 2
LANES = 128
CHUNK = 128
VMEM_LIMIT = 56 * 1024 * 1024


def _cparams(*sem):
    return pltpu.CompilerParams(dimension_semantics=sem, vmem_limit_bytes=VMEM_LIMIT)


def _pick(n, prefs):
    for p in prefs:
        if n % p == 0:
            return p
    return n


def _dot(a, b, prec=None):
    return jnp.dot(a, b, preferred_element_type=F32, precision=prec)


def _dot_nt(a, b, prec=None):
    return lax.dot_general(a, b, (((1,), (1,)), ((), ())), preferred_element_type=F32, precision=prec)


def _rms(x, g):
    return x * lax.rsqrt(jnp.mean(x * x, axis=-1, keepdims=True) + EPS) * g


def _norm_mm_kernel(x_ref, g_ref, w_ref, o_ref, h_ref):
    @pl.when(pl.program_id(1) == 0)
    def _():
        h_ref[...] = _rms(x_ref[...], g_ref[...]).astype(BF16)

    o_ref[...] = _dot(h_ref[...], w_ref[...])


def norm_matmul(x, g, w):
    m, k = x.shape
    n = w.shape[1]
    tm = _pick(m, (1024, 512, 256, 128))
    tn = _pick(n, (512, 256, 128))
    return pl.pallas_call(
        _norm_mm_kernel,
        grid=(m // tm, n // tn),
        in_specs=[pl.BlockSpec((tm, k), lambda i, j: (i, 0)),
                  pl.BlockSpec((1, k), lambda i, j: (0, 0)),
                  pl.BlockSpec((k, tn), lambda i, j: (0, j))],
        out_specs=pl.BlockSpec((tm, tn), lambda i, j: (i, j)),
        out_shape=jax.ShapeDtypeStruct((m, n), F32),
        scratch_shapes=[pltpu.VMEM((tm, k), BF16)],
        compiler_params=_cparams("parallel", "arbitrary"),
    )(x, g.reshape(1, k), w)


def _mm_res_kernel(*refs, n_in):
    a_refs, w_refs = refs[:n_in], refs[n_in:2 * n_in]
    res_ref, o_ref = refs[2 * n_in], refs[2 * n_in + 1]
    acc = res_ref[...]
    for a_ref, w_ref in zip(a_refs, w_refs):
        acc = acc + _dot(a_ref[...].astype(BF16), w_ref[...])
    o_ref[...] = acc


def matmul_residual(a_list, w_list, res):
    m, n = res.shape
    tm = _pick(m, (1024, 512, 256, 128))
    tn = _pick(n, (512, 256, 128))
    n_in = len(a_list)
    in_specs = ([pl.BlockSpec((tm, a.shape[1]), lambda i, j: (i, 0)) for a in a_list]
                + [pl.BlockSpec((w.shape[0], tn), lambda i, j: (0, j)) for w in w_list]
                + [pl.BlockSpec((tm, tn), lambda i, j: (i, j))])
    return pl.pallas_call(
        functools.partial(_mm_res_kernel, n_in=n_in),
        grid=(m // tm, n // tn),
        in_specs=in_specs,
        out_specs=pl.BlockSpec((tm, tn), lambda i, j: (i, j)),
        out_shape=jax.ShapeDtypeStruct((m, n), F32),
        compiler_params=_cparams("parallel", "arbitrary"),
    )(*a_list, *w_list, res)


def _token_shift(p, carry_ref, mu):
    rows = p.shape[0]
    prev = pltpu.roll(p, 1, axis=0)
    rid = lax.broadcasted_iota(jnp.int32, p.shape, 0)
    prev = jnp.where(rid == 0, carry_ref[...], prev)
    carry_ref[...] = p[rows - 1:rows, :]
    return p + (prev - p) * mu


def _rwkv_prep_kernel(*refs, width, has_vres):
    if has_vres:
        (p_ref, pv_ref, vf_ref, mu_ref, w0_ref, a0_ref, wwa_ref, gup_ref, vmu_ref, vb_ref, vup_ref,
         r_ref, lw_ref, k_ref, v_ref, a_ref, g_ref, carry_ref, carry_v_ref) = refs
    else:
        (p_ref, mu_ref, w0_ref, a0_ref, wwa_ref, gup_ref,
         r_ref, lw_ref, k_ref, v_ref, a_ref, g_ref, carry_ref) = refs

    @pl.when(pl.program_id(1) == 0)
    def _():
        carry_ref[...] = jnp.zeros_like(carry_ref)
        if has_vres:
            carry_v_ref[...] = jnp.zeros_like(carry_v_ref)

    sh = _token_shift(p_ref[0], carry_ref, mu_ref[...])
    r_ref[0] = sh[:, :width]
    k_ref[0] = sh[:, width:2 * width]
    v = sh[:, 2 * width:3 * width]
    dwa = sh[:, 3 * width:3 * width + LANES]
    lane = lax.broadcasted_iota(jnp.int32, dwa.shape, 1)
    dwa = jnp.where(lane < LANES // 2, jnp.tanh(dwa), dwa)
    wa = _dot(dwa.astype(BF16), wwa_ref[...])
    z = -(w0_ref[...] + wa[:, :width])
    softplus = jnp.maximum(z, 0.0) + jnp.log(1.0 + jnp.exp(-jnp.abs(z)))
    lw_ref[0] = -jnp.exp(-softplus - 0.5)
    a_ref[0] = jax.nn.sigmoid(a0_ref[...] + wa[:, width:])
    dg = jax.nn.sigmoid(sh[:, 3 * width + LANES:3 * width + 2 * LANES])
    g_ref[0] = _dot(dg.astype(BF16), gup_ref[...])
    if has_vres:
        shv = _token_shift(pv_ref[0], carry_v_ref, vmu_ref[...])
        mix = jax.nn.sigmoid(vb_ref[...] + _dot(shv.astype(BF16), vup_ref[...]))
        v = v + (vf_ref[0] - v) * mix
    v_ref[0] = v


def rwkv_prep(proj, width, mu, w0, a0, wwa, gup, vres=None):
    b, s, _ = proj.shape
    ts = _pick(s, (512, 256, 128))
    ncol = 3 * width + 2 * LANES
    assert mu.shape[-1] == ncol
    row = lambda a: a.reshape(1, -1)
    full = lambda a: pl.BlockSpec(a.shape, lambda i, j: (0,) * a.ndim)
    seq = lambda w: pl.BlockSpec((1, ts, w), lambda i, j: (i, j, 0))
    params = [row(mu), row(w0), row(a0), wwa, gup]
    inputs = [proj]
    in_specs = [seq(ncol)]
    scratch = [pltpu.VMEM((1, ncol), F32)]
    if vres is not None:
        v_first, vcol_block, vmu, vb, vup = vres
        inputs += [proj, v_first]
        in_specs += [pl.BlockSpec((1, ts, LANES), lambda i, j: (i, j, vcol_block)), seq(width)]
        params += [row(vmu), row(vb), vup]
        scratch.append(pltpu.VMEM((1, LANES), F32))
    in_specs += [full(p) for p in params]
    out = jax.ShapeDtypeStruct((b, s, width), F32)
    return pl.pallas_call(
        functools.partial(_rwkv_prep_kernel, width=width, has_vres=vres is not None),
        grid=(b, s // ts),
        in_specs=in_specs,
        out_specs=[seq(width)] * 6,
        out_shape=[out] * 6,
        scratch_shapes=scratch,
        compiler_params=_cparams("parallel", "arbitrary"),
    )(*inputs, *params)


def _rwkv_chunk_kernel(r_ref, lw_ref, k_ref, v_ref, a_ref, g_ref, kk_ref, ka_ref, rk_ref, gw_ref, gb_ref,
                       y_ref, state_ref, *, n_pairs):
    c = CHUNK
    half = LANES // 2

    @pl.when(pl.program_id(1) == 0)
    def _():
        state_ref[...] = jnp.zeros_like(state_ref)

    row = lax.broadcasted_iota(jnp.int32, (c, c), 0)
    col = lax.broadcasted_iota(jnp.int32, (c, c), 1)
    tri_incl = row >= col
    tri_strict = row > col
    ltri = tri_incl.astype(F32)
    same_head = (row < half) == (col < half)
    gsum_mat = same_head.astype(F32)
    lo = col < half

    def gsum(x):
        return _dot(x, gsum_mat, HIGHEST)

    def sel(x0, x1):
        return jnp.where(lo, x0, x1)

    for p in range(n_pairs):
        sl = slice(p * LANES, (p + 1) * LANES)
        r, lw, k, v, a, g = (ref[0, :, sl] for ref in (r_ref, lw_ref, k_ref, v_ref, a_ref, g_ref))
        k_k, k_a, r_k, gn_w, gn_b = (ref[:, sl] for ref in (kk_ref, ka_ref, rk_ref, gw_ref, gb_ref))

        cum = _dot(ltri, lw, HIGHEST)
        cmid = cum[c // 2 - 1:c // 2, :]
        clast = cum[c - 1:c, :]
        ci = cum - cmid
        e_neg = jnp.exp(-ci)
        p_mid = jnp.exp(cmid)
        kk = k * k_k
        kkn = kk * lax.rsqrt(jnp.maximum(gsum(kk * kk), 1e-24))
        k2 = k * (1.0 + (a - 1.0) * k_a)
        beta = kkn * a
        al_m = -kkn * jnp.exp(ci - lw)
        r_m = r * jnp.exp(ci)
        be_m = beta * e_neg
        k_m = k2 * e_neg
        e_end = jnp.exp(clast - cum)
        be_end = beta * e_end
        k_end = k2 * e_end

        zero = jnp.zeros_like(al_m)
        lhs = jnp.concatenate([sel(al_m, zero), sel(r_m, zero), sel(zero, al_m), sel(zero, r_m)], axis=0)
        rhs = jnp.concatenate([be_m, k_m], axis=0)
        big = _dot_nt(lhs.astype(BF16), rhs.astype(BF16))
        a_ab = [jnp.where(tri_strict, big[(2 * h) * c:(2 * h + 1) * c, :c], 0.0) for h in range(2)]
        a_ak = [jnp.where(tri_strict, big[(2 * h) * c:(2 * h + 1) * c, c:], 0.0) for h in range(2)]
        a_rb = [jnp.where(tri_incl, big[(2 * h + 1) * c:(2 * h + 2) * c, :c], 0.0) for h in range(2)]
        a_rk = [jnp.where(tri_incl, big[(2 * h + 1) * c:(2 * h + 2) * c, c:], 0.0) for h in range(2)]

        s0 = state_ref[p]
        s0_b = s0.astype(BF16)
        v_b = v.astype(BF16)
        x = _dot_nt((al_m * p_mid).astype(BF16), s0_b) + sel(_dot(a_ak[0].astype(BF16), v_b),
                                                              _dot(a_ak[1].astype(BF16), v_b))
        pw = [m.astype(BF16) for m in a_ab]
        n_steps = c.bit_length() - 1
        for step in range(n_steps):
            x_b = x.astype(BF16)
            x = x + sel(_dot(pw[0], x_b), _dot(pw[1], x_b))
            if step + 1 < n_steps:
                pw = [_dot(m, m).astype(BF16) for m in pw]
        u_b = x.astype(BF16)
        uv = jnp.concatenate([u_b, v_b], axis=0)
        y = _dot_nt((r_m * p_mid).astype(BF16), s0_b) + sel(
            _dot(jnp.concatenate([a_rb[0], a_rk[0]], axis=1).astype(BF16), uv),
            _dot(jnp.concatenate([a_rb[1], a_rk[1]], axis=1).astype(BF16), uv))
        upd = _dot(jnp.concatenate([x.T, v.T], axis=1).astype(BF16),
                   jnp.concatenate([be_end, k_end], axis=0).astype(BF16))
        state_ref[p] = s0 * jnp.exp(clast) + jnp.where(same_head, upd, 0.0)

        mean = gsum(y) * (1.0 / half)
        yc = y - mean
        var = gsum(yc * yc) * (1.0 / half)
        out = yc * lax.rsqrt(var + GN_EPS) * gn_w + gn_b
        bonus = gsum(r * k2 * r_k) * v
        y_ref[0, :, sl] = (out + bonus) * g


def rwkv_chunked(r, lw, k, v, a, g, k_k, k_a, r_k, gn_w, gn_b):
    b, s, width = r.shape
    assert RWKV_HEAD * 2 == LANES and width % LANES == 0 and s % CHUNK == 0
    n_pairs = width // LANES
    seq = pl.BlockSpec((1, CHUNK, width), lambda i, j: (i, j, 0))
    par = pl.BlockSpec((1, width), lambda i, j: (0, 0))
    row = lambda t: t.reshape(1, width)
    return pl.pallas_call(
        functools.partial(_rwkv_chunk_kernel, n_pairs=n_pairs),
        grid=(b, s // CHUNK),
        in_specs=[seq] * 6 + [par] * 5,
        out_specs=seq,
        out_shape=jax.ShapeDtypeStruct((b, s, width), F32),
        scratch_shapes=[pltpu.VMEM((n_pairs, LANES, LANES), F32)],
        compiler_params=_cparams("parallel", "arbitrary"),
    )(r, lw, k, v, a, g, row(k_k), row(k_a), row(r_k), row(gn_w), row(gn_b))


def _diff_attn_kernel(q_ref, k_ref, v_ref, cf_ref, sa_ref, sb_ref, qg_ref, kg_ref, lam_ref, sub_ref,
                      o_ref, kp_ref, vp_ref, *, tq, lam_init):
    qi = pl.program_id(2)
    half = LANES // 2
    lane_row = lax.broadcasted_iota(jnp.int32, (LANES, LANES), 0)
    lane_col = lax.broadcasted_iota(jnp.int32, (LANES, LANES), 1)
    gmat = ((lane_row < half) == (lane_col < half)).astype(F32)

    def norm_rope(x, g, cf, sa, sb):
        ms = _dot(x * x, gmat, HIGHEST) * (1.0 / half)
        x = x * lax.rsqrt(ms + EPS) * g
        return x * cf + pltpu.roll(x, LANES - 8, axis=1) * sa + pltpu.roll(x, 8, axis=1) * sb

    @pl.when(qi == 0)
    def _():
        kp_ref[...] = norm_rope(k_ref[0], kg_ref[...], cf_ref[0], sa_ref[0], sb_ref[0]).astype(BF16)
        vp_ref[...] = v_ref[0].astype(BF16)

    rows = pl.ds(pl.multiple_of(qi * tq, tq), tq)
    q = norm_rope(q_ref[0], qg_ref[...], cf_ref[0, rows, :], sa_ref[0, rows, :], sb_ref[0, rows, :])
    q = q * (half ** -0.5)
    lane = lax.broadcasted_iota(jnp.int32, q.shape, 1)
    qs = [jnp.where(lane < half, q, 0.0).astype(BF16), jnp.where(lane >= half, q, 0.0).astype(BF16)]

    def block(j, carry, masked):
        kv_rows = pl.ds(pl.multiple_of(j * tq, tq), tq)
        kb = kp_ref[kv_rows, :]
        vb = vp_ref[kv_rows, :]
        new = []
        for m in range(2):
            mx, l, acc = carry[3 * m:3 * m + 3]
            s = _dot_nt(qs[m], kb)
            if masked:
                r_id = lax.broadcasted_iota(jnp.int32, s.shape, 0)
                c_id = lax.broadcasted_iota(jnp.int32, s.shape, 1)
                s = jnp.where(c_id <= r_id, s, -1e30)
            mx_new = jnp.maximum(mx, jnp.max(s, axis=-1, keepdims=True))
            alpha = jnp.exp(mx - mx_new)
            pr = jnp.exp(s - mx_new)
            l = alpha * l + jnp.sum(pr, axis=-1, keepdims=True)
            acc = alpha * acc + _dot(pr.astype(BF16), vb)
            new += [mx_new, l, acc]
        return tuple(new)

    init = (jnp.full((tq, 1), -1e30, F32), jnp.zeros((tq, 1), F32), jnp.zeros((tq, LANES), F32)) * 2
    carry = lax.fori_loop(0, qi, lambda j, cr: block(j, cr, False), init)
    _, l1, acc1, _, l2, acc2 = block(qi, carry, True)

    lv = lam_ref[...]
    lam = (jnp.exp(jnp.sum(lv[0:1] * lv[1:2], axis=-1, keepdims=True))
           - jnp.exp(jnp.sum(lv[2:3] * lv[3:4], axis=-1, keepdims=True)) + lam_init)
    o = acc1 / l1 - lam * (acc2 / l2)
    o_ref[0] = _rms(o, sub_ref[...]) * (1.0 - lam_init)


def diff_attention(proj, col0, tables, q_g, k_g, lam_vecs, subln_g, lam_init):
    b, s, _ = proj.shape
    h = DIFF_HEADS
    tq = _pick(s, (512, 256, 128))
    cf, sa, sb = tables
    qspec = pl.BlockSpec((1, tq, LANES), lambda bi, hi, qi: (bi, qi, col0 + hi))
    kspec = pl.BlockSpec((1, s, LANES), lambda bi, hi, qi: (bi, 0, col0 + h + hi))
    vspec = pl.BlockSpec((1, s, LANES), lambda bi, hi, qi: (bi, 0, col0 + 2 * h + hi))
    tspec = pl.BlockSpec((1, s, LANES), lambda bi, hi, qi: (bi, 0, 0))
    gspec = pl.BlockSpec((1, LANES), lambda bi, hi, qi: (0, 0))
    lspec = pl.BlockSpec(lam_vecs.shape, lambda bi, hi, qi: (0, 0))
    tile2 = lambda g: jnp.tile(g, 2).reshape(1, LANES)
    return pl.pallas_call(
        functools.partial(_diff_attn_kernel, tq=tq, lam_init=lam_init),
        grid=(b, h, s // tq),
        in_specs=[qspec, kspec, vspec, tspec, tspec, tspec, gspec, gspec, lspec, gspec],
        out_specs=pl.BlockSpec((1, tq, LANES), lambda bi, hi, qi: (bi, qi, hi)),
        out_shape=jax.ShapeDtypeStruct((b, s, h * LANES), F32),
        scratch_shapes=[pltpu.VMEM((s, LANES), BF16), pltpu.VMEM((s, LANES), BF16)],
        compiler_params=_cparams("parallel", "parallel", "arbitrary"),
    )(proj, proj, proj, cf, sa, sb, tile2(q_g), tile2(k_g), lam_vecs, subln_g.reshape(1, LANES))


def rope_tables(positions):
    half = LANES // 2
    rope_dims = half // 4
    inv_freq = ROPE_THETA ** (-jnp.arange(0, rope_dims, 2, dtype=F32) / rope_dims)
    ang = positions.astype(F32)[..., None] * inv_freq
    cos, sin = jnp.cos(ang), jnp.sin(ang)
    pad = jnp.zeros(cos.shape[:-1] + (half - rope_dims,), F32)
    zero = jnp.zeros_like(cos)
    cf = jnp.concatenate([cos, cos, pad + 1.0], axis=-1)
    sa = jnp.concatenate([-sin, zero, pad], axis=-1)
    sb = jnp.concatenate([zero, sin, pad], axis=-1)
    return tuple(jnp.concatenate([t, t], axis=-1) for t in (cf, sa, sb))


def _xattn_kernel(q_ref, kv_ref, qg_ref, kg_ref, o_ref, *, heads):
    d = q_ref.shape[-1] // heads
    for h in range(heads):
        q = _rms(q_ref[0, :, h * d:(h + 1) * d], qg_ref[...]) * (d ** -0.5)
        k = _rms(kv_ref[0, :, h * d:(h + 1) * d], kg_ref[...])
        v = kv_ref[0, :, (heads + h) * d:(heads + h + 1) * d]
        s = _dot_nt(q.astype(BF16), k.astype(BF16))
        s = s - jnp.max(s, axis=-1, keepdims=True)
        pr = jnp.exp(s)
        o = _dot(pr.astype(BF16), v.astype(BF16)) / jnp.sum(pr, axis=-1, keepdims=True)
        o_ref[0, :, h * d:(h + 1) * d] = o


def cross_attention(q, kv, q_g, k_g):
    b, s, dm = q.shape
    mlen = kv.shape[1]
    ts = _pick(s, (512, 256, 128))
    return pl.pallas_call(
        functools.partial(_xattn_kernel, heads=XATTN_HEADS),
        grid=(b, s // ts),
        in_specs=[pl.BlockSpec((1, ts, dm), lambda i, j: (i, j, 0)),
                  pl.BlockSpec((1, mlen, 2 * dm), lambda i, j: (i, 0, 0)),
                  pl.BlockSpec((1, q_g.shape[0]), lambda i, j: (0, 0)),
                  pl.BlockSpec((1, k_g.shape[0]), lambda i, j: (0, 0))],
        out_specs=pl.BlockSpec((1, ts, dm), lambda i, j: (i, j, 0)),
        out_shape=jax.ShapeDtypeStruct((b, s, dm), F32),
        compiler_params=_cparams("parallel", "arbitrary"),
    )(q, kv, q_g.reshape(1, -1), k_g.reshape(1, -1))


def _ffn_kernel(x_ref, g_ref, wg_ref, wu_ref, wd_ref, o_ref, h_ref):
    j = pl.program_id(1)

    @pl.when(j == 0)
    def _():
        h_ref[...] = _rms(x_ref[...], g_ref[...]).astype(BF16)
        o_ref[...] = x_ref[...]

    h = h_ref[...]
    gate = _dot(h, wg_ref[...])
    up = _dot(h, wu_ref[...])
    act = gate * jax.nn.sigmoid(gate) * up
    o_ref[...] += _dot(act.astype(BF16), wd_ref[...])


def ffn_dense(x, g, w_gu, w_down):
    m, d = x.shape
    f = w_down.shape[0]
    tm = _pick(m, (1024, 512, 256, 128))
    tf = _pick(f, (256, 128))
    nf = f // tf
    return pl.pallas_call(
        _ffn_kernel,
        grid=(m // tm, nf),
        in_specs=[pl.BlockSpec((tm, d), lambda i, j: (i, 0)),
                  pl.BlockSpec((1, d), lambda i, j: (0, 0)),
                  pl.BlockSpec((d, tf), lambda i, j: (0, j)),
                  pl.BlockSpec((d, tf), lambda i, j: (0, j + nf)),
                  pl.BlockSpec((tf, d), lambda i, j: (j, 0))],
        out_specs=pl.BlockSpec((tm, d), lambda i, j: (i, 0)),
        out_shape=jax.ShapeDtypeStruct((m, d), F32),
        scratch_shapes=[pltpu.VMEM((tm, d), BF16)],
        compiler_params=_cparams("parallel", "arbitrary"),
    )(x, g.reshape(1, d), w_gu, w_gu, w_down)


def _router_kernel(x_ref, g_ref, wr_ref, o_ref, *, n_experts):
    h = _rms(x_ref[...], g_ref[...])
    logits = _dot(h, wr_ref[...], HIGHEST)
    lane = lax.broadcasted_iota(jnp.int32, logits.shape, 1)
    neg = -jnp.inf
    l1 = jnp.where(lane < n_experts, logits, neg)
    m1 = jnp.max(l1, axis=-1, keepdims=True)
    i1 = jnp.min(jnp.where(l1 == m1, lane, LANES), axis=-1, keepdims=True)
    l2 = jnp.where(lane == i1, neg, l1)
    m2 = jnp.max(l2, axis=-1, keepdims=True)
    i2 = jnp.min(jnp.where(l2 == m2, lane, LANES), axis=-1, keepdims=True)
    e2 = jnp.exp(m2 - m1)
    w1 = 1.0 / (1.0 + e2)
    o_ref[...] = jnp.where(lane == i1, w1, 0.0) + jnp.where(lane == i2, e2 * w1, 0.0)


def router_gates(x, g, w_router):
    m, d = x.shape
    n_experts = w_router.shape[1]
    wr = jnp.zeros((d, LANES), F32).at[:, :n_experts].set(w_router)
    tm = _pick(m, (1024, 512, 256, 128))
    return pl.pallas_call(
        functools.partial(_router_kernel, n_experts=n_experts),
        grid=(m // tm,),
        in_specs=[pl.BlockSpec((tm, d), lambda i: (i, 0)),
                  pl.BlockSpec((1, d), lambda i: (0, 0)),
                  pl.BlockSpec((d, LANES), lambda i: (0, 0))],
        out_specs=pl.BlockSpec((tm, LANES), lambda i: (i, 0)),
        out_shape=jax.ShapeDtypeStruct((m, LANES), F32),
        compiler_params=_cparams("parallel"),
    )(x, g.reshape(1, d), wr)


def _moe_kernel(x_ref, g_ref, gates_ref, wg_ref, wu_ref, wd_ref, o_ref, h_ref):
    e = pl.program_id(1)
    j = pl.program_id(2)

    @pl.when((e == 0) & (j == 0))
    def _():
        h_ref[...] = _rms(x_ref[...], g_ref[...]).astype(BF16)
        o_ref[...] = x_ref[...]

    gates = gates_ref[...]
    lane = lax.broadcasted_iota(jnp.int32, gates.shape, 1)
    gcol = jnp.sum(jnp.where(lane == e, gates, 0.0), axis=-1, keepdims=True)
    h = h_ref[...]
    gate = _dot(h, wg_ref[0])
    up = _dot(h, wu_ref[0])
    act = gate * jax.nn.sigmoid(gate) * up * gcol
    o_ref[...] += _dot(act.astype(BF16), wd_ref[0])


def ffn_moe(x, g, gates, w_gu, w_down):
    m, d = x.shape
    n_experts, f, _ = w_down.shape
    tm = _pick(m, (1024, 512, 256, 128))
    tf = _pick(f, (256, 128))
    nf = f // tf
    return pl.pallas_call(
        _moe_kernel,
        grid=(m // tm, n_experts, nf),
        in_specs=[pl.BlockSpec((tm, d), lambda i, e, j: (i, 0)),
                  pl.BlockSpec((1, d), lambda i, e, j: (0, 0)),
                  pl.BlockSpec((tm, LANES), lambda i, e, j: (i, 0)),
                  pl.BlockSpec((1, d, tf), lambda i, e, j: (e, 0, j)),
                  pl.BlockSpec((1, d, tf), lambda i, e, j: (e, 0, j + nf)),
                  pl.BlockSpec((1, tf, d), lambda i, e, j: (e, j, 0))],
        out_specs=pl.BlockSpec((tm, d), lambda i, e, j: (i, 0)),
        out_shape=jax.ShapeDtypeStruct((m, d), F32),
        scratch_shapes=[pltpu.VMEM((tm, d), BF16)],
        compiler_params=_cparams("parallel", "arbitrary", "arbitrary"),
    )(x, g.reshape(1, d), gates, w_gu, w_gu, w_down)


def _block_diag(w_a, w_b):
    za = jnp.zeros_like(w_a)
    zb = jnp.zeros_like(w_b)
    return jnp.concatenate([jnp.concatenate([w_a, za], axis=1), jnp.concatenate([zb, w_b], axis=1)], axis=0)


def kernel(x, mem, positions, mix_norm, in_proj, shift_mu, decay_bias, decay_up, iclr_bias, iclr_up, gate_up,
           key_kk_scale, key_iclr_scale, bonus_rk, rwkv_gn_w, rwkv_gn_b, vres_down, vres_shift_mu, vres_bias,
           vres_up, q_norm, k_norm, lambda_q1, lambda_k1, lambda_q2, lambda_k2, diff_subln, out_proj, xattn_norm,
           mem_norm, xattn_wq, xattn_wkv, xattn_wo, xattn_q_norm, xattn_k_norm, ffn_norm, dense_w_gu,
           dense_w_down, router, expert_w_gu, expert_w_down):
    b, s, d = x.shape
    depth = mix_norm.shape[0]
    width = decay_bias.shape[1]
    rwkv_cols = shift_mu.shape[1]
    in_cols = in_proj.shape[2]
    vres_rank = vres_down.shape[2] if depth > 1 else 0
    assert decay_up.shape[1] == LANES // 2 and iclr_up.shape[1] == LANES // 2 and gate_up.shape[1] == LANES
    assert rwkv_cols == 3 * width + 2 * LANES and rwkv_cols % LANES == 0 and in_cols % LANES == 0
    assert vres_rank <= LANES
    diff_col0 = rwkv_cols // LANES
    tables = rope_tables(positions)
    xf = x.reshape(b * s, d)
    memf = mem.reshape(-1, d)
    v_first = None
    for l in range(depth):
        w_in = in_proj[l]
        if l > 0:
            pad = jnp.zeros((d, LANES - vres_rank), F32)
            w_in = jnp.concatenate([w_in, vres_down[l - 1], pad], axis=1)
            extra = (-w_in.shape[1]) % 256
            w_in = jnp.concatenate([w_in, jnp.zeros((d, extra), F32)], axis=1)
        proj = norm_matmul(xf, mix_norm[l], w_in.astype(BF16)).reshape(b, s, -1)
        vres = None
        if l > 0:
            vmu = jnp.zeros((LANES,), F32).at[:vres_rank].set(vres_shift_mu[l - 1])
            vup = jnp.zeros((LANES, width), F32).at[:vres_rank].set(vres_up[l - 1]).astype(BF16)
            vres = (v_first, in_cols // LANES, vmu, vres_bias[l - 1], vup)
        wwa = _block_diag(decay_up[l], iclr_up[l]).astype(BF16)
        r_, lw_, k_, v_, a_, g_ = rwkv_prep(proj, width, shift_mu[l], decay_bias[l], iclr_bias[l], wwa,
                                            gate_up[l].astype(BF16), vres)
        if l == 0:
            v_first = v_
        y_rwkv = rwkv_chunked(r_, lw_, k_, v_, a_, g_, key_kk_scale[l], key_iclr_scale[l],
                              bonus_rk[l].reshape(-1), rwkv_gn_w[l], rwkv_gn_b[l])
        lam_init = 0.8 - 0.6 * math.exp(-0.3 * l)
        lam_vecs = jnp.stack([lambda_q1[l], lambda_k1[l], lambda_q2[l], lambda_k2[l]])
        y_diff = diff_attention(proj, diff_col0, tables, q_norm[l], k_norm[l], lam_vecs, diff_subln[l], lam_init)
        w_out = out_proj[l].astype(BF16)
        xf = matmul_residual([y_rwkv.reshape(b * s, -1), y_diff.reshape(b * s, -1)],
                             [w_out[:width], w_out[width:]], xf)
        q = norm_matmul(xf, xattn_norm[l], xattn_wq[l].astype(BF16)).reshape(b, s, d)
        kv = norm_matmul(memf, mem_norm[l], xattn_wkv[l].astype(BF16)).reshape(b, -1, 2 * d)
        o = cross_attention(q, kv, xattn_q_norm[l], xattn_k_norm[l])
        xf = matmul_residual([o.reshape(b * s, d)], [xattn_wo[l].astype(BF16)], xf)
        if l % 2 == 0:
            xf = ffn_dense(xf, ffn_norm[l], dense_w_gu[l // 2].astype(BF16), dense_w_down[l // 2].astype(BF16))
        else:
            gates = router_gates(xf, ffn_norm[l], router[l // 2])
            xf = ffn_moe(xf, ffn_norm[l], gates, expert_w_gu[l // 2].astype(BF16),
                         expert_w_down[l // 2].astype(BF16))
    return xf.reshape(b, s, d)
```

```python
import functools
import math

import jax
import jax.numpy as jnp
from jax import lax
from jax.experimental import pallas as pl
from jax.experimental.pallas import tpu as pltpu

F32 = jnp.float32
BF16 = jnp.bfloat16
HIGHEST = lax.Precision.HIGHEST

EPS = 1e-6
GN_EPS = 64e-5
ROPE_THETA = 500000.0
RWKV_HEAD = 64
DIFF_HEADS = 4
XATTN_HEADS = 4
TOP_K = 2
LANES = 128
CHUNK = 128
VMEM_LIMIT = 56 * 1024 * 1024
MOE_GROUP_TILE = 512
MOE_TOKEN_TILE = 512


def _cparams(*sem):
    return pltpu.CompilerParams(dimension_semantics=sem, vmem_limit_bytes=VMEM_LIMIT)


def _pick(n, prefs):
    for p in prefs:
        if n % p == 0:
            return p
    return n


def _dot(a, b, prec=None):
    return jnp.dot(a, b, preferred_element_type=F32, precision=prec)


def _dot_nt(a, b, prec=None):
    return lax.dot_general(a, b, (((1,), (1,)), ((), ())), preferred_element_type=F32, precision=prec)


def _rms(x, g):
    return x * lax.rsqrt(jnp.mean(x * x, axis=-1, keepdims=True) + EPS) * g


def _norm_mm_kernel(x_ref, g_ref, w_ref, o_ref, h_ref):
    @pl.when(pl.program_id(1) == 0)
    def _():
        h_ref[...] = _rms(x_ref[...], g_ref[...]).astype(BF16)

    o_ref[...] = _dot(h_ref[...], w_ref[...])


def norm_matmul(x, g, w):
    m, k = x.shape
    n = w.shape[1]
    tm = _pick(m, (1024, 512, 256, 128))
    tn = _pick(n, (512, 256, 128))
    return pl.pallas_call(
        _norm_mm_kernel,
        grid=(m // tm, n // tn),
        in_specs=[pl.BlockSpec((tm, k), lambda i, j: (i, 0)),
                  pl.BlockSpec((1, k), lambda i, j: (0, 0)),
                  pl.BlockSpec((k, tn), lambda i, j: (0, j))],
        out_specs=pl.BlockSpec((tm, tn), lambda i, j: (i, j)),
        out_shape=jax.ShapeDtypeStruct((m, n), F32),
        scratch_shapes=[pltpu.VMEM((tm, k), BF16)],
        compiler_params=_cparams("parallel", "arbitrary"),
    )(x, g.reshape(1, k), w)


def _mm_res_kernel(*refs, n_in):
    a_refs, w_refs = refs[:n_in], refs[n_in:2 * n_in]
    res_ref, o_ref = refs[2 * n_in], refs[2 * n_in + 1]
    acc = res_ref[...]
    for a_ref, w_ref in zip(a_refs, w_refs):
        acc = acc + _dot(a_ref[...].astype(BF16), w_ref[...])
    o_ref[...] = acc


def matmul_residual(a_list, w_list, res):
    m, n = res.shape
    tm = _pick(m, (1024, 512, 256, 128))
    tn = _pick(n, (512, 256, 128))
    n_in = len(a_list)
    in_specs = ([pl.BlockSpec((tm, a.shape[1]), lambda i, j: (i, 0)) for a in a_list]
                + [pl.BlockSpec((w.shape[0], tn), lambda i, j: (0, j)) for w in w_list]
                + [pl.BlockSpec((tm, tn), lambda i, j: (i, j))])
    return pl.pallas_call(
        functools.partial(_mm_res_kernel, n_in=n_in),
        grid=(m // tm, n // tn),
        in_specs=in_specs,
        out_specs=pl.BlockSpec((tm, tn), lambda i, j: (i, j)),
        out_shape=jax.ShapeDtypeStruct((m, n), F32),
        compiler_params=_cparams("parallel", "arbitrary"),
    )(*a_list, *w_list, res)


def _token_shift(p, carry_ref, mu):
    rows = p.shape[0]
    prev = pltpu.roll(p, 1, axis=0)
    rid = lax.broadcasted_iota(jnp.int32, p.shape, 0)
    prev = jnp.where(rid == 0, carry_ref[...], prev)
    carry_ref[...] = p[rows - 1:rows, :]
    return p + (prev - p) * mu


def _rwkv_prep_kernel(*refs, width, has_vres):
    if has_vres:
        (p_ref, pv_ref, vf_ref, mu_ref, w0_ref, a0_ref, wwa_ref, gup_ref, vmu_ref, vb_ref, vup_ref,
         r_ref, lw_ref, k_ref, v_ref, a_ref, g_ref, carry_ref, carry_v_ref) = refs
    else:
        (p_ref, mu_ref, w0_ref, a0_ref, wwa_ref, gup_ref,
         r_ref, lw_ref, k_ref, v_ref, a_ref, g_ref, carry_ref) = refs

    @pl.when(pl.program_id(1) == 0)
    def _():
        carry_ref[...] = jnp.zeros_like(carry_ref)
        if has_vres:
            carry_v_ref[...] = jnp.zeros_like(carry_v_ref)

    sh = _token_shift(p_ref[0], carry_ref, mu_ref[...])
    r_ref[0] = sh[:, :width]
    k_ref[0] = sh[:, width:2 * width]
    v = sh[:, 2 * width:3 * width]
    dwa = sh[:, 3 * width:3 * width + LANES]
    lane = lax.broadcasted_iota(jnp.int32, dwa.shape, 1)
    dwa = jnp.where(lane < LANES // 2, jnp.tanh(dwa), dwa)
    wa = _dot(dwa.astype(BF16), wwa_ref[...])
    z = -(w0_ref[...] + wa[:, :width])
    softplus = jnp.maximum(z, 0.0) + jnp.log(1.0 + jnp.exp(-jnp.abs(z)))
    lw_ref[0] = -jnp.exp(-softplus - 0.5)
    a_ref[0] = jax.nn.sigmoid(a0_ref[...] + wa[:, width:])
    dg = jax.nn.sigmoid(sh[:, 3 * width + LANES:3 * width + 2 * LANES])
    g_ref[0] = _dot(dg.astype(BF16), gup_ref[...])
    if has_vres:
        shv = _token_shift(pv_ref[0], carry_v_ref, vmu_ref[...])
        mix = jax.nn.sigmoid(vb_ref[...] + _dot(shv.astype(BF16), vup_ref[...]))
        v = v + (vf_ref[0] - v) * mix
    v_ref[0] = v


def rwkv_prep(proj, width, mu, w0, a0, wwa, gup, vres=None):
    b, s, _ = proj.shape
    ts = _pick(s, (512, 256, 128))
    ncol = 3 * width + 2 * LANES
    assert mu.shape[-1] == ncol
    row = lambda a: a.reshape(1, -1)
    full = lambda a: pl.BlockSpec(a.shape, lambda i, j: (0,) * a.ndim)
    seq = lambda w: pl.BlockSpec((1, ts, w), lambda i, j: (i, j, 0))
    params = [row(mu), row(w0), row(a0), wwa, gup]
    inputs = [proj]
    in_specs = [seq(ncol)]
    scratch = [pltpu.VMEM((1, ncol), F32)]
    if vres is not None:
        v_first, vcol_block, vmu, vb, vup = vres
        inputs += [proj, v_first]
        in_specs += [pl.BlockSpec((1, ts, LANES), lambda i, j: (i, j, vcol_block)), seq(width)]
        params += [row(vmu), row(vb), vup]
        scratch.append(pltpu.VMEM((1, LANES), F32))
    in_specs += [full(p) for p in params]
    out = jax.ShapeDtypeStruct((b, s, width), F32)
    return pl.pallas_call(
        functools.partial(_rwkv_prep_kernel, width=width, has_vres=vres is not None),
        grid=(b, s // ts),
        in_specs=in_specs,
        out_specs=[seq(width)] * 6,
        out_shape=[out] * 6,
        scratch_shapes=scratch,
        compiler_params=_cparams("parallel", "arbitrary"),
    )(*inputs, *params)


def _rwkv_chunk_kernel(r_ref, lw_ref, k_ref, v_ref, a_ref, g_ref, kk_ref, ka_ref, rk_ref, gw_ref, gb_ref,
                       y_ref, state_ref, *, n_pairs):
    c = CHUNK
    half = LANES // 2

    @pl.when(pl.program_id(1) == 0)
    def _():
        state_ref[...] = jnp.zeros_like(state_ref)

    row = lax.broadcasted_iota(jnp.int32, (c, c), 0)
    col = lax.broadcasted_iota(jnp.int32, (c, c), 1)
    tri_incl = row >= col
    tri_strict = row > col
    ltri = tri_incl.astype(F32)
    same_head = (row < half) == (col < half)
    gsum_mat = same_head.astype(F32)
    lo = col < half

    def gsum(x):
        return _dot(x, gsum_mat, HIGHEST)

    def sel(x0, x1):
        return jnp.where(lo, x0, x1)

    for p in range(n_pairs):
        sl = slice(p * LANES, (p + 1) * LANES)
        r, lw, k, v, a, g = (ref[0, :, sl] for ref in (r_ref, lw_ref, k_ref, v_ref, a_ref, g_ref))
        k_k, k_a, r_k, gn_w, gn_b = (ref[:, sl] for ref in (kk_ref, ka_ref, rk_ref, gw_ref, gb_ref))

        cum = _dot(ltri, lw, HIGHEST)
        cmid = cum[c // 2 - 1:c // 2, :]
        clast = cum[c - 1:c, :]
        ci = cum - cmid
        e_neg = jnp.exp(-ci)
        p_mid = jnp.exp(cmid)
        kk = k * k_k
        kkn = kk * lax.rsqrt(jnp.maximum(gsum(kk * kk), 1e-24))
        k2 = k * (1.0 + (a - 1.0) * k_a)
        beta = kkn * a
        al_m = -kkn * jnp.exp(ci - lw)
        r_m = r * jnp.exp(ci)
        be_m = beta * e_neg
        k_m = k2 * e_neg
        e_end = jnp.exp(clast - cum)
        be_end = beta * e_end
        k_end = k2 * e_end

        zero = jnp.zeros_like(al_m)
        lhs = jnp.concatenate([sel(al_m, zero), sel(r_m, zero), sel(zero, al_m), sel(zero, r_m)], axis=0)
        rhs = jnp.concatenate([be_m, k_m], axis=0)
        big = _dot_nt(lhs.astype(BF16), rhs.astype(BF16))
        a_ab = [jnp.where(tri_strict, big[(2 * h) * c:(2 * h + 1) * c, :c], 0.0) for h in range(2)]
        a_ak = [jnp.where(tri_strict, big[(2 * h) * c:(2 * h + 1) * c, c:], 0.0) for h in range(2)]
        a_rb = [jnp.where(tri_incl, big[(2 * h + 1) * c:(2 * h + 2) * c, :c], 0.0) for h in range(2)]
        a_rk = [jnp.where(tri_incl, big[(2 * h + 1) * c:(2 * h + 2) * c, c:], 0.0) for h in range(2)]

        s0 = state_ref[p]
        s0_b = s0.astype(BF16)
        v_b = v.astype(BF16)
        x = _dot_nt((al_m * p_mid).astype(BF16), s0_b) + sel(_dot(a_ak[0].astype(BF16), v_b),
                                                              _dot(a_ak[1].astype(BF16), v_b))
        pw = [m.astype(BF16) for m in a_ab]
        n_steps = c.bit_length() - 1
        for step in range(n_steps):
            x_b = x.astype(BF16)
            x = x + sel(_dot(pw[0], x_b), _dot(pw[1], x_b))
            if step + 1 < n_steps:
                pw = [_dot(m, m).astype(BF16) for m in pw]
        u_b = x.astype(BF16)
        uv = jnp.concatenate([u_b, v_b], axis=0)
        y = _dot_nt((r_m * p_mid).astype(BF16), s0_b) + sel(
            _dot(jnp.concatenate([a_rb[0], a_rk[0]], axis=1).astype(BF16), uv),
            _dot(jnp.concatenate([a_rb[1], a_rk[1]], axis=1).astype(BF16), uv))
        upd = _dot(jnp.concatenate([x.T, v.T], axis=1).astype(BF16),
                   jnp.concatenate([be_end, k_end], axis=0).astype(BF16))
        state_ref[p] = s0 * jnp.exp(clast) + jnp.where(same_head, upd, 0.0)

        mean = gsum(y) * (1.0 / half)
        yc = y - mean
        var = gsum(yc * yc) * (1.0 / half)
        out = yc * lax.rsqrt(var + GN_EPS) * gn_w + gn_b
        bonus = gsum(r * k2 * r_k) * v
        y_ref[0, :, sl] = (out + bonus) * g


def rwkv_chunked(r, lw, k, v, a, g, k_k, k_a, r_k, gn_w, gn_b):
    b, s, width = r.shape
    assert RWKV_HEAD * 2 == LANES and width % LANES == 0 and s % CHUNK == 0
    n_pairs = width // LANES
    seq = pl.BlockSpec((1, CHUNK, width), lambda i, j: (i, j, 0))
    par = pl.BlockSpec((1, width), lambda i, j: (0, 0))
    row = lambda t: t.reshape(1, width)
    return pl.pallas_call(
        functools.partial(_rwkv_chunk_kernel, n_pairs=n_pairs),
        grid=(b, s // CHUNK),
        in_specs=[seq] * 6 + [par] * 5,
        out_specs=seq,
        out_shape=jax.ShapeDtypeStruct((b, s, width), F32),
        scratch_shapes=[pltpu.VMEM((n_pairs, LANES, LANES), F32)],
        compiler_params=_cparams("parallel", "arbitrary"),
    )(r, lw, k, v, a, g, row(k_k), row(k_a), row(r_k), row(gn_w), row(gn_b))


def _diff_attn_kernel(q_ref, k_ref, v_ref, cf_ref, sa_ref, sb_ref, qg_ref, kg_ref, lam_ref, sub_ref,
                      o_ref, kp_ref, vp_ref, *, tq, lam_init):
    qi = pl.program_id(2)
    half = LANES // 2
    lane_row = lax.broadcasted_iota(jnp.int32, (LANES, LANES), 0)
    lane_col = lax.broadcasted_iota(jnp.int32, (LANES, LANES), 1)
    gmat = ((lane_row < half) == (lane_col < half)).astype(F32)

    def norm_rope(x, g, cf, sa, sb):
        ms = _dot(x * x, gmat, HIGHEST) * (1.0 / half)
        x = x * lax.rsqrt(ms + EPS) * g
        return x * cf + pltpu.roll(x, LANES - 8, axis=1) * sa + pltpu.roll(x, 8, axis=1) * sb

    @pl.when(qi == 0)
    def _():
        kp_ref[...] = norm_rope(k_ref[0], kg_ref[...], cf_ref[0], sa_ref[0], sb_ref[0]).astype(BF16)
        vp_ref[...] = v_ref[0].astype(BF16)

    rows = pl.ds(pl.multiple_of(qi * tq, tq), tq)
    q = norm_rope(q_ref[0], qg_ref[...], cf_ref[0, rows, :], sa_ref[0, rows, :], sb_ref[0, rows, :])
    q = q * (half ** -0.5)
    lane = lax.broadcasted_iota(jnp.int32, q.shape, 1)
    qs = [jnp.where(lane < half, q, 0.0).astype(BF16), jnp.where(lane >= half, q, 0.0).astype(BF16)]

    def block(j, carry, masked):
        kv_rows = pl.ds(pl.multiple_of(j * tq, tq), tq)
        kb = kp_ref[kv_rows, :]
        vb = vp_ref[kv_rows, :]
        new = []
        for m in range(2):
            mx, l, acc = carry[3 * m:3 * m + 3]
            s = _dot_nt(qs[m], kb)
            if masked:
                r_id = lax.broadcasted_iota(jnp.int32, s.shape, 0)
                c_id = lax.broadcasted_iota(jnp.int32, s.shape, 1)
                s = jnp.where(c_id <= r_id, s, -1e30)
            mx_new = jnp.maximum(mx, jnp.max(s, axis=-1, keepdims=True))
            alpha = jnp.exp(mx - mx_new)
            pr = jnp.exp(s - mx_new)
            l = alpha * l + jnp.sum(pr, axis=-1, keepdims=True)
            acc = alpha * acc + _dot(pr.astype(BF16), vb)
            new += [mx_new, l, acc]
        return tuple(new)

    init = (jnp.full((tq, 1), -1e30, F32), jnp.zeros((tq, 1), F32), jnp.zeros((tq, LANES), F32)) * 2
    carry = lax.fori_loop(0, qi, lambda j, cr: block(j, cr, False), init)
    _, l1, acc1, _, l2, acc2 = block(qi, carry, True)

    lv = lam_ref[...]
    lam = (jnp.exp(jnp.sum(lv[0:1] * lv[1:2], axis=-1, keepdims=True))
           - jnp.exp(jnp.sum(lv[2:3] * lv[3:4], axis=-1, keepdims=True)) + lam_init)
    o = acc1 / l1 - lam * (acc2 / l2)
    o_ref[0] = _rms(o, sub_ref[...]) * (1.0 - lam_init)


def diff_attention(proj, col0, tables, q_g, k_g, lam_vecs, subln_g, lam_init):
    b, s, _ = proj.shape
    h = DIFF_HEADS
    tq = _pick(s, (512, 256, 128))
    cf, sa, sb = tables
    qspec = pl.BlockSpec((1, tq, LANES), lambda bi, hi, qi: (bi, qi, col0 + hi))
    kspec = pl.BlockSpec((1, s, LANES), lambda bi, hi, qi: (bi, 0, col0 + h + hi))
    vspec = pl.BlockSpec((1, s, LANES), lambda bi, hi, qi: (bi, 0, col0 + 2 * h + hi))
    tspec = pl.BlockSpec((1, s, LANES), lambda bi, hi, qi: (bi, 0, 0))
    gspec = pl.BlockSpec((1, LANES), lambda bi, hi, qi: (0, 0))
    lspec = pl.BlockSpec(lam_vecs.shape, lambda bi, hi, qi: (0, 0))
    tile2 = lambda g: jnp.tile(g, 2).reshape(1, LANES)
    return pl.pallas_call(
        functools.partial(_diff_attn_kernel, tq=tq, lam_init=lam_init),
        grid=(b, h, s // tq),
        in_specs=[qspec, kspec, vspec, tspec, tspec, tspec, gspec, gspec, lspec, gspec],
        out_specs=pl.BlockSpec((1, tq, LANES), lambda bi, hi, qi: (bi, qi, hi)),
        out_shape=jax.ShapeDtypeStruct((b, s, h * LANES), F32),
        scratch_shapes=[pltpu.VMEM((s, LANES), BF16), pltpu.VMEM((s, LANES), BF16)],
        compiler_params=_cparams("parallel", "parallel", "arbitrary"),
    )(proj, proj, proj, cf, sa, sb, tile2(q_g), tile2(k_g), lam_vecs, subln_g.reshape(1, LANES))


def rope_tables(positions):
    half = LANES // 2
    rope_dims = half // 4
    inv_freq = ROPE_THETA ** (-jnp.arange(0, rope_dims, 2, dtype=F32) / rope_dims)
    ang = positions.astype(F32)[..., None] * inv_freq
    cos, sin = jnp.cos(ang), jnp.sin(ang)
    pad = jnp.zeros(cos.shape[:-1] + (half - rope_dims,), F32)
    zero = jnp.zeros_like(cos)
    cf = jnp.concatenate([cos, cos, pad + 1.0], axis=-1)
    sa = jnp.concatenate([-sin, zero, pad], axis=-1)
    sb = jnp.concatenate([zero, sin, pad], axis=-1)
    return tuple(jnp.concatenate([t, t], axis=-1) for t in (cf, sa, sb))


def _xattn_kernel(q_ref, kv_ref, qg_ref, kg_ref, o_ref, *, heads):
    d = q_ref.shape[-1] // heads
    for h in range(heads):
        q = _rms(q_ref[0, :, h * d:(h + 1) * d], qg_ref[...]) * (d ** -0.5)
        k = _rms(kv_ref[0, :, h * d:(h + 1) * d], kg_ref[...])
        v = kv_ref[0, :, (heads + h) * d:(heads + h + 1) * d]
        s = _dot_nt(q.astype(BF16), k.astype(BF16))
        s = s - jnp.max(s, axis=-1, keepdims=True)
        pr = jnp.exp(s)
        o = _dot(pr.astype(BF16), v.astype(BF16)) / jnp.sum(pr, axis=-1, keepdims=True)
        o_ref[0, :, h * d:(h + 1) * d] = o


def cross_attention(q, kv, q_g, k_g):
    b, s, dm = q.shape
    mlen = kv.shape[1]
    ts = _pick(s, (512, 256, 128))
    return pl.pallas_call(
        functools.partial(_xattn_kernel, heads=XATTN_HEADS),
        grid=(b, s // ts),
        in_specs=[pl.BlockSpec((1, ts, dm), lambda i, j: (i, j, 0)),
                  pl.BlockSpec((1, mlen, 2 * dm), lambda i, j: (i, 0, 0)),
                  pl.BlockSpec((1, q_g.shape[0]), lambda i, j: (0, 0)),
                  pl.BlockSpec((1, k_g.shape[0]), lambda i, j: (0, 0))],
        out_specs=pl.BlockSpec((1, ts, dm), lambda i, j: (i, j, 0)),
        out_shape=jax.ShapeDtypeStruct((b, s, dm), F32),
        compiler_params=_cparams("parallel", "arbitrary"),
    )(q, kv, q_g.reshape(1, -1), k_g.reshape(1, -1))


def _ffn_kernel(x_ref, g_ref, wg_ref, wu_ref, wd_ref, o_ref, h_ref):
    j = pl.program_id(1)

    @pl.when(j == 0)
    def _():
        h_ref[...] = _rms(x_ref[...], g_ref[...]).astype(BF16)
        o_ref[...] = x_ref[...]

    h = h_ref[...]
    gate = _dot(h, wg_ref[...])
    up = _dot(h, wu_ref[...])
    act = gate * jax.nn.sigmoid(gate) * up
    o_ref[...] += _dot(act.astype(BF16), wd_ref[...])


def ffn_dense(x, g, w_gu, w_down):
    m, d = x.shape
    f = w_down.shape[0]
    tm = _pick(m, (1024, 512, 256, 128))
    tf = _pick(f, (256, 128))
    nf = f // tf
    return pl.pallas_call(
        _ffn_kernel,
        grid=(m // tm, nf),
        in_specs=[pl.BlockSpec((tm, d), lambda i, j: (i, 0)),
                  pl.BlockSpec((1, d), lambda i, j: (0, 0)),
                  pl.BlockSpec((d, tf), lambda i, j: (0, j)),
                  pl.BlockSpec((d, tf), lambda i, j: (0, j + nf)),
                  pl.BlockSpec((tf, d), lambda i, j: (j, 0))],
        out_specs=pl.BlockSpec((tm, d), lambda i, j: (i, 0)),
        out_shape=jax.ShapeDtypeStruct((m, d), F32),
        scratch_shapes=[pltpu.VMEM((tm, d), BF16)],
        compiler_params=_cparams("parallel", "arbitrary"),
    )(x, g.reshape(1, d), w_gu, w_gu, w_down)


META_I1, META_I2, META_W1, META_W2, META_R1, META_R2 = range(6)


def _lane_pick(x, lane, idx):
    return jnp.sum(jnp.where(lane == idx, x, 0.0), axis=-1, keepdims=True)


def _router_kernel(x_ref, g_ref, wr_ref, meta_ref, cnt_ref, tri_ref, *, n_experts):
    tm = x_ref.shape[0]

    @pl.when(pl.program_id(0) == 0)
    def _():
        cnt_ref[...] = jnp.zeros_like(cnt_ref)
        r_id = lax.broadcasted_iota(jnp.int32, (tm, tm), 0)
        c_id = lax.broadcasted_iota(jnp.int32, (tm, tm), 1)
        tri_ref[...] = (r_id > c_id).astype(BF16)

    h = _rms(x_ref[...], g_ref[...])
    logits = _dot(h, wr_ref[...], HIGHEST)
    lane = lax.broadcasted_iota(jnp.int32, logits.shape, 1).astype(F32)
    neg = -jnp.inf
    l1 = jnp.where(lane < n_experts, logits, neg)
    m1 = jnp.max(l1, axis=-1, keepdims=True)
    i1 = jnp.min(jnp.where(l1 == m1, lane, float(LANES)), axis=-1, keepdims=True)
    l2 = jnp.where(lane == i1, neg, l1)
    m2 = jnp.max(l2, axis=-1, keepdims=True)
    i2 = jnp.min(jnp.where(l2 == m2, lane, float(LANES)), axis=-1, keepdims=True)
    e2 = jnp.exp(m2 - m1)
    w1 = 1.0 / (1.0 + e2)
    onehot = jnp.where((lane == i1) | (lane == i2), 1.0, 0.0)
    before = _dot(tri_ref[...], onehot.astype(BF16)) + cnt_ref[0:1, :]
    cnt_ref[0:1, :] += jnp.sum(onehot, axis=0, keepdims=True)
    meta = jnp.zeros_like(logits)
    for slot, val in ((META_I1, i1), (META_I2, i2), (META_W1, w1), (META_W2, e2 * w1),
                      (META_R1, _lane_pick(before, lane, i1)), (META_R2, _lane_pick(before, lane, i2))):
        meta = jnp.where(lane == slot, val, meta)
    meta_ref[...] = meta


def router_top2(x, g, w_router):
    m, d = x.shape
    n_experts = w_router.shape[1]
    wr = jnp.zeros((d, LANES), F32).at[:, :n_experts].set(w_router)
    tm = _pick(m, (1024, 512, 256, 128))
    return pl.pallas_call(
        functools.partial(_router_kernel, n_experts=n_experts),
        grid=(m // tm,),
        in_specs=[pl.BlockSpec((tm, d), lambda i: (i, 0)),
                  pl.BlockSpec((1, d), lambda i: (0, 0)),
                  pl.BlockSpec((d, LANES), lambda i: (0, 0))],
        out_specs=[pl.BlockSpec((tm, LANES), lambda i: (i, 0)),
                   pl.BlockSpec((8, LANES), lambda i: (0, 0))],
        out_shape=[jax.ShapeDtypeStruct((m, LANES), F32), jax.ShapeDtypeStruct((8, LANES), F32)],
        scratch_shapes=[pltpu.VMEM((tm, tm), BF16)],
        compiler_params=_cparams("arbitrary"),
    )(x, g.reshape(1, d), wr)


def _idx_copy(dest_hbm, idx_smem, isem, tile, slot):
    return pltpu.make_async_copy(dest_hbm.at[tile], idx_smem.at[slot], isem.at[slot])


def _dispatch_kernel(dest_hbm, x_ref, zeros_hbm, xs_hbm, idx_smem, isem, sem, *, tm):
    del zeros_hbm
    i = pl.program_id(0)
    slot = i % 2

    @pl.when(i == 0)
    def _():
        _idx_copy(dest_hbm, idx_smem, isem, 0, 0).start()

    @pl.when(i + 1 < pl.num_programs(0))
    def _():
        _idx_copy(dest_hbm, idx_smem, isem, i + 1, 1 - slot).start()

    _idx_copy(dest_hbm, idx_smem, isem, i, slot).wait()

    def row_copy(t, k):
        d = idx_smem[slot, TOP_K * t + k]
        return pltpu.make_async_copy(x_ref.at[pl.ds(t, 1)], xs_hbm.at[pl.ds(d, 1)], sem)

    def issue(t, carry):
        for k in range(TOP_K):
            row_copy(t, k).start()
        return carry

    lax.fori_loop(0, tm, issue, 0, unroll=8)
    for _ in range(TOP_K):
        pltpu.make_async_copy(x_ref, xs_hbm.at[pl.ds(0, tm)], sem).wait()


def moe_dispatch(x, dest, n_rows):
    m, d = x.shape
    n_tiles, per_tile = dest.shape
    tm = per_tile // TOP_K
    zeros = jnp.zeros((n_rows, d), F32)
    return pl.pallas_call(
        functools.partial(_dispatch_kernel, tm=tm),
        grid=(n_tiles,),
        in_specs=[pl.BlockSpec(memory_space=pl.ANY),
                  pl.BlockSpec((tm, d), lambda i: (i, 0)),
                  pl.BlockSpec(memory_space=pl.ANY)],
        out_specs=pl.BlockSpec(memory_space=pl.ANY),
        out_shape=jax.ShapeDtypeStruct((n_rows, d), F32),
        scratch_shapes=[pltpu.SMEM((2, per_tile), jnp.int32), pltpu.SemaphoreType.DMA((2,)),
                        pltpu.SemaphoreType.DMA(())],
        input_output_aliases={2: 0},
        compiler_params=_cparams("arbitrary"),
    )(dest, x, zeros)


def _moe_ffn_kernel(te_ref, nu_ref, x_ref, g_ref, wg_ref, wu_ref, wd_ref, o_ref, h_ref):
    del te_ref
    j = pl.program_id(1)

    @pl.when(pl.program_id(0) < nu_ref[0])
    def _():
        @pl.when(j == 0)
        def _():
            h_ref[...] = _rms(x_ref[...], g_ref[...]).astype(BF16)

        h = h_ref[...]
        gate = _dot(h, wg_ref[0])
        up = _dot(h, wu_ref[0])
        act = gate * jax.nn.sigmoid(gate) * up
        y = _dot(act.astype(BF16), wd_ref[0])

        @pl.when(j == 0)
        def _():
            o_ref[...] = y

        @pl.when(j > 0)
        def _():
            o_ref[...] += y

    @pl.when((pl.program_id(0) >= nu_ref[0]) & (j == 0))
    def _():
        o_ref[...] = jnp.zeros_like(o_ref)


def moe_grouped_ffn(xs, g, tile_expert, n_used, w_gu, w_down, tm):
    p, d = xs.shape
    f = w_down.shape[1]
    tf = _pick(f, (256, 128))
    nf = f // tf
    last = lambda n, nu: jnp.minimum(n, nu[0] - 1)
    col = lambda n, j, nu: jnp.where(n < nu[0], j, nf - 1)
    grid_spec = pltpu.PrefetchScalarGridSpec(
        num_scalar_prefetch=2,
        grid=(p // tm, nf),
        in_specs=[pl.BlockSpec((tm, d), lambda n, j, te, nu: (last(n, nu), 0)),
                  pl.BlockSpec((1, d), lambda n, j, te, nu: (0, 0)),
                  pl.BlockSpec((1, d, tf), lambda n, j, te, nu: (te[n], 0, col(n, j, nu))),
                  pl.BlockSpec((1, d, tf), lambda n, j, te, nu: (te[n], 0, col(n, j, nu) + nf)),
                  pl.BlockSpec((1, tf, d), lambda n, j, te, nu: (te[n], col(n, j, nu), 0))],
        out_specs=pl.BlockSpec((tm, d), lambda n, j, te, nu: (n, 0)),
        scratch_shapes=[pltpu.VMEM((tm, d), BF16)],
    )
    return pl.pallas_call(
        _moe_ffn_kernel,
        grid_spec=grid_spec,
        out_shape=jax.ShapeDtypeStruct((p, d), F32),
        compiler_params=_cparams("arbitrary", "arbitrary"),
    )(tile_expert, n_used, xs, g.reshape(1, d), w_gu, w_gu, w_down)


def _combine_kernel(dest_hbm, x_ref, meta_ref, ys_hbm, o_ref, idx_smem, ybuf, isem, sem, *, tm):
    i = pl.program_id(0)
    slot = i % 2

    def row_copy(sl, t, k):
        d = idx_smem[sl, TOP_K * t + k]
        return pltpu.make_async_copy(ys_hbm.at[pl.ds(d, 1)], ybuf.at[sl, k, pl.ds(t, 1)], sem.at[sl])

    def gather_tile(tile, sl):
        _idx_copy(dest_hbm, idx_smem, isem, tile, sl).start()
        _idx_copy(dest_hbm, idx_smem, isem, tile, sl).wait()

        def issue(t, carry):
            for k in range(TOP_K):
                row_copy(sl, t, k).start()
            return carry

        lax.fori_loop(0, tm, issue, 0, unroll=8)

    @pl.when(i == 0)
    def _():
        gather_tile(0, 0)

    @pl.when(i + 1 < pl.num_programs(0))
    def _():
        gather_tile(i + 1, 1 - slot)

    for k in range(TOP_K):
        pltpu.make_async_copy(ys_hbm.at[pl.ds(0, tm)], ybuf.at[slot, k], sem.at[slot]).wait()
    meta = meta_ref[...]
    lane = lax.broadcasted_iota(jnp.int32, meta.shape, 1)
    w1 = _lane_pick(meta, lane, META_W1)
    w2 = _lane_pick(meta, lane, META_W2)
    o_ref[...] = x_ref[...] + w1 * ybuf[slot, 0] + w2 * ybuf[slot, 1]


def moe_combine(x, meta, dest, ys):
    m, d = x.shape
    n_tiles, per_tile = dest.shape
    tm = per_tile // TOP_K
    return pl.pallas_call(
        functools.partial(_combine_kernel, tm=tm),
        grid=(n_tiles,),
        in_specs=[pl.BlockSpec(memory_space=pl.ANY),
                  pl.BlockSpec((tm, d), lambda i: (i, 0)),
                  pl.BlockSpec((tm, LANES), lambda i: (i, 0)),
                  pl.BlockSpec(memory_space=pl.ANY)],
        out_specs=pl.BlockSpec((tm, d), lambda i: (i, 0)),
        out_shape=jax.ShapeDtypeStruct((m, d), F32),
        scratch_shapes=[pltpu.SMEM((2, per_tile), jnp.int32), pltpu.VMEM((2, TOP_K, tm, d), F32),
                        pltpu.SemaphoreType.DMA((2,)), pltpu.SemaphoreType.DMA((2,))],
        compiler_params=_cparams("arbitrary"),
    )(dest, x, meta, ys)


def ffn_moe(x, g, w_router, w_gu, w_down):
    m, d = x.shape
    n_experts = w_down.shape[0]
    tg = _pick(m, (MOE_GROUP_TILE, 256, 128))
    td = _pick(m, (MOE_TOKEN_TILE, 256, 128))
    meta, counts = router_top2(x, g, w_router)
    counts = counts[0, :n_experts].astype(jnp.int32)
    padded = (counts + tg - 1) // tg * tg
    ends = jnp.cumsum(padded)
    offsets = ends - padded
    n_tiles = (TOP_K * m) // tg + n_experts
    n_used = (ends[-1] // tg).astype(jnp.int32)
    tile_id = jnp.minimum(jnp.arange(n_tiles, dtype=jnp.int32), n_used - 1)
    tile_expert = jnp.sum(ends[None, :] <= (tile_id * tg)[:, None], axis=1).astype(jnp.int32)
    picks = meta[:, META_I1:META_I2 + 1].astype(jnp.int32)
    ranks = meta[:, META_R1:META_R2 + 1].astype(jnp.int32)
    dest = (offsets[picks] + ranks).reshape(m // td, TOP_K * td)
    xs = moe_dispatch(x, dest, n_tiles * tg)
    ys = moe_grouped_ffn(xs, g, tile_expert, n_used.reshape(1), w_gu, w_down, tg)
    return moe_combine(x, meta, dest, ys)


def _block_diag(w_a, w_b):
    za = jnp.zeros_like(w_a)
    zb = jnp.zeros_like(w_b)
    return jnp.concatenate([jnp.concatenate([w_a, za], axis=1), jnp.concatenate([zb, w_b], axis=1)], axis=0)


def kernel(x, mem, positions, mix_norm, in_proj, shift_mu, decay_bias, decay_up, iclr_bias, iclr_up, gate_up,
           key_kk_scale, key_iclr_scale, bonus_rk, rwkv_gn_w, rwkv_gn_b, vres_down, vres_shift_mu, vres_bias,
           vres_up, q_norm, k_norm, lambda_q1, lambda_k1, lambda_q2, lambda_k2, diff_subln, out_proj, xattn_norm,
           mem_norm, xattn_wq, xattn_wkv, xattn_wo, xattn_q_norm, xattn_k_norm, ffn_norm, dense_w_gu,
           dense_w_down, router, expert_w_gu, expert_w_down):
    b, s, d = x.shape
    depth = mix_norm.shape[0]
    width = decay_bias.shape[1]
    rwkv_cols = shift_mu.shape[1]
    in_cols = in_proj.shape[2]
    vres_rank = vres_down.shape[2] if depth > 1 else 0
    assert decay_up.shape[1] == LANES // 2 and iclr_up.shape[1] == LANES // 2 and gate_up.shape[1] == LANES
    assert rwkv_cols == 3 * width + 2 * LANES and rwkv_cols % LANES == 0 and in_cols % LANES == 0
    assert vres_rank <= LANES
    diff_col0 = rwkv_cols // LANES
    tables = rope_tables(positions)
    xf = x.reshape(b * s, d)
    memf = mem.reshape(-1, d)
    v_first = None
    for l in range(depth):
        w_in = in_proj[l]
        if l > 0:
            pad = jnp.zeros((d, LANES - vres_rank), F32)
            w_in = jnp.concatenate([w_in, vres_down[l - 1], pad], axis=1)
            extra = (-w_in.shape[1]) % 256
            w_in = jnp.concatenate([w_in, jnp.zeros((d, extra), F32)], axis=1)
        proj = norm_matmul(xf, mix_norm[l], w_in.astype(BF16)).reshape(b, s, -1)
        vres = None
        if l > 0:
            vmu = jnp.zeros((LANES,), F32).at[:vres_rank].set(vres_shift_mu[l - 1])
            vup = jnp.zeros((LANES, width), F32).at[:vres_rank].set(vres_up[l - 1]).astype(BF16)
            vres = (v_first, in_cols // LANES, vmu, vres_bias[l - 1], vup)
        wwa = _block_diag(decay_up[l], iclr_up[l]).astype(BF16)
        r_, lw_, k_, v_, a_, g_ = rwkv_prep(proj, width, shift_mu[l], decay_bias[l], iclr_bias[l], wwa,
                                            gate_up[l].astype(BF16), vres)
        if l == 0:
            v_first = v_
        y_rwkv = rwkv_chunked(r_, lw_, k_, v_, a_, g_, key_kk_scale[l], key_iclr_scale[l],
                              bonus_rk[l].reshape(-1), rwkv_gn_w[l], rwkv_gn_b[l])
        lam_init = 0.8 - 0.6 * math.exp(-0.3 * l)
        lam_vecs = jnp.stack([lambda_q1[l], lambda_k1[l], lambda_q2[l], lambda_k2[l]])
        y_diff = diff_attention(proj, diff_col0, tables, q_norm[l], k_norm[l], lam_vecs, diff_subln[l], lam_init)
        w_out = out_proj[l].astype(BF16)
        xf = matmul_residual([y_rwkv.reshape(b * s, -1), y_diff.reshape(b * s, -1)],
                             [w_out[:width], w_out[width:]], xf)
        q = norm_matmul(xf, xattn_norm[l], xattn_wq[l].astype(BF16)).reshape(b, s, d)
        kv = norm_matmul(memf, mem_norm[l], xattn_wkv[l].astype(BF16)).reshape(b, -1, 2 * d)
        o = cross_attention(q, kv, xattn_q_norm[l], xattn_k_norm[l])
        xf = matmul_residual([o.reshape(b * s, d)], [xattn_wo[l].astype(BF16)], xf)
        if l % 2 == 0:
            xf = ffn_dense(xf, ffn_norm[l], dense_w_gu[l // 2].astype(BF16), dense_w_down[l // 2].astype(BF16))
        else:
            xf = ffn_moe(xf, ffn_norm[l], router[l // 2], expert_w_gu[l // 2].astype(BF16),
                         expert_w_down[l // 2].astype(BF16))
    return xf.reshape(b, s, d)
```

```python
import functools
import math

import jax
import jax.numpy as jnp
from jax import lax
from jax.experimental import pallas as pl
from jax.experimental.pallas import tpu as pltpu

F32 = jnp.float32
BF16 = jnp.bfloat16
HIGHEST = lax.Precision.HIGHEST

EPS = 1e-6
GN_EPS = 64e-5
ROPE_THETA = 500000.0
RWKV_HEAD = 64
DIFF_HEADS = 4
XATTN_HEADS = 4
TOP_K = 2
LANES = 128
CHUNK = 128
VMEM_LIMIT = 56 * 1024 * 1024
ATTN_ROW_PARTS = 2
WEIGHT_TILE_BYTES =4 * 1024 * 1024
MOE_GROUP_TILE = 512
MOE_TOKEN_TILE = 512


def _cparams(*sem):
    return pltpu.CompilerParams(dimension_semantics=sem, vmem_limit_bytes=VMEM_LIMIT)


def _pick(n, prefs):
    for p in prefs:
        if n % p == 0:
            return p
    return n


def _col_tile(k, n):
    best = LANES
    for t in range(LANES, n + 1, LANES):
        if n % t == 0 and k * t * 2 <= WEIGHT_TILE_BYTES:
            best = t
    return best


def _dot(a, b, prec=None):
    return jnp.dot(a, b, preferred_element_type=F32, precision=prec)


def _dot_nt(a, b, prec=None):
    return lax.dot_general(a, b, (((1,), (1,)), ((), ())), preferred_element_type=F32, precision=prec)


def _rms(x, g):
    return x * lax.rsqrt(jnp.mean(x * x, axis=-1, keepdims=True) + EPS) * g


def _norm_mm_kernel(x_ref, g_ref, w_ref, o_ref, h_ref):
    @pl.when(pl.program_id(1) == 0)
    def _():
        h_ref[...] = _rms(x_ref[...], g_ref[...]).astype(BF16)

    o_ref[...] = _dot(h_ref[...], w_ref[...])


def norm_matmul(x, g, w):
    m, k = x.shape
    n = w.shape[1]
    tm = _pick(m, (1024, 512, 256, 128))
    tn = _col_tile(k, n)
    return pl.pallas_call(
        _norm_mm_kernel,
        grid=(m // tm, n // tn),
        in_specs=[pl.BlockSpec((tm, k), lambda i, j: (i, 0)),
                  pl.BlockSpec((1, k), lambda i, j: (0, 0)),
                  pl.BlockSpec((k, tn), lambda i, j: (0, j))],
        out_specs=pl.BlockSpec((tm, tn), lambda i, j: (i, j)),
        out_shape=jax.ShapeDtypeStruct((m, n), F32),
        scratch_shapes=[pltpu.VMEM((tm, k), BF16)],
        compiler_params=_cparams("parallel", "arbitrary"),
    )(x, g.reshape(1, k), w)


def _mm_res_kernel(*refs, n_in):
    a_refs, w_refs = refs[:n_in], refs[n_in:2 * n_in]
    res_ref, o_ref = refs[2 * n_in], refs[2 * n_in + 1]
    acc = res_ref[...]
    for a_ref, w_ref in zip(a_refs, w_refs):
        acc = acc + _dot(a_ref[...].astype(BF16), w_ref[...])
    o_ref[...] = acc


def matmul_residual(a_list, w_list, res):
    m, n = res.shape
    tm = _pick(m, (1024, 512, 256, 128))
    tn = _col_tile(sum(w.shape[0] for w in w_list), n)
    n_in = len(a_list)
    in_specs = ([pl.BlockSpec((tm, a.shape[1]), lambda i, j: (i, 0)) for a in a_list]
                + [pl.BlockSpec((w.shape[0], tn), lambda i, j: (0, j)) for w in w_list]
                + [pl.BlockSpec((tm, tn), lambda i, j: (i, j))])
    return pl.pallas_call(
        functools.partial(_mm_res_kernel, n_in=n_in),
        grid=(m // tm, n // tn),
        in_specs=in_specs,
        out_specs=pl.BlockSpec((tm, tn), lambda i, j: (i, j)),
        out_shape=jax.ShapeDtypeStruct((m, n), F32),
        compiler_params=_cparams("parallel", "arbitrary"),
    )(*a_list, *w_list, res)


def _token_shift(p, carry_ref, mu):
    rows = p.shape[0]
    prev = pltpu.roll(p, 1, axis=0)
    rid = lax.broadcasted_iota(jnp.int32, p.shape, 0)
    prev = jnp.where(rid == 0, carry_ref[...], prev)
    carry_ref[...] = p[rows - 1:rows, :]
    return p + (prev - p) * mu


def _rwkv_prep_kernel(*refs, width, has_vres):
    if has_vres:
        (p_ref, pv_ref, vf_ref, mu_ref, w0_ref, a0_ref, wwa_ref, gup_ref, vmu_ref, vb_ref, vup_ref,
         r_ref, lw_ref, k_ref, v_ref, a_ref, g_ref, carry_ref, carry_v_ref) = refs
    else:
        (p_ref, mu_ref, w0_ref, a0_ref, wwa_ref, gup_ref,
         r_ref, lw_ref, k_ref, v_ref, a_ref, g_ref, carry_ref) = refs

    @pl.when(pl.program_id(1) == 0)
    def _():
        carry_ref[...] = jnp.zeros_like(carry_ref)
        if has_vres:
            carry_v_ref[...] = jnp.zeros_like(carry_v_ref)

    sh = _token_shift(p_ref[0], carry_ref, mu_ref[...])
    r_ref[0] = sh[:, :width]
    k_ref[0] = sh[:, width:2 * width]
    v = sh[:, 2 * width:3 * width]
    dwa = sh[:, 3 * width:3 * width + LANES]
    lane = lax.broadcasted_iota(jnp.int32, dwa.shape, 1)
    dwa = jnp.where(lane < LANES // 2, jnp.tanh(dwa), dwa)
    wa = _dot(dwa.astype(BF16), wwa_ref[...])
    z = -(w0_ref[...] + wa[:, :width])
    softplus = jnp.maximum(z, 0.0) + jnp.log(1.0 + jnp.exp(-jnp.abs(z)))
    lw_ref[0] = -jnp.exp(-softplus - 0.5)
    a_ref[0] = jax.nn.sigmoid(a0_ref[...] + wa[:, width:])
    dg = jax.nn.sigmoid(sh[:, 3 * width + LANES:3 * width + 2 * LANES])
    g_ref[0] = _dot(dg.astype(BF16), gup_ref[...])
    if has_vres:
        shv = _token_shift(pv_ref[0], carry_v_ref, vmu_ref[...])
        mix = jax.nn.sigmoid(vb_ref[...] + _dot(shv.astype(BF16), vup_ref[...]))
        v = v + (vf_ref[0] - v) * mix
    v_ref[0] = v


def rwkv_prep(proj, width, mu, w0, a0, wwa, gup, vres=None):
    b, s, _ = proj.shape
    ts = _pick(s, (512, 256, 128))
    ncol = 3 * width + 2 * LANES
    assert mu.shape[-1] == ncol
    row = lambda a: a.reshape(1, -1)
    full = lambda a: pl.BlockSpec(a.shape, lambda i, j: (0,) * a.ndim)
    seq = lambda w: pl.BlockSpec((1, ts, w), lambda i, j: (i, j, 0))
    params = [row(mu), row(w0), row(a0), wwa, gup]
    inputs = [proj]
    in_specs = [seq(ncol)]
    scratch = [pltpu.VMEM((1, ncol), F32)]
    if vres is not None:
        v_first, vcol_block, vmu, vb, vup = vres
        inputs += [proj, v_first]
        in_specs += [pl.BlockSpec((1, ts, LANES), lambda i, j: (i, j, vcol_block)), seq(width)]
        params += [row(vmu), row(vb), vup]
        scratch.append(pltpu.VMEM((1, LANES), F32))
    in_specs += [full(p) for p in params]
    out = jax.ShapeDtypeStruct((b, s, width), F32)
    return pl.pallas_call(
        functools.partial(_rwkv_prep_kernel, width=width, has_vres=vres is not None),
        grid=(b, s // ts),
        in_specs=in_specs,
        out_specs=[seq(width)] * 6,
        out_shape=[out] * 6,
        scratch_shapes=scratch,
        compiler_params=_cparams("parallel", "arbitrary"),
    )(*inputs, *params)


def _rwkv_chunk_kernel(r_ref, lw_ref, k_ref, v_ref, a_ref, g_ref, kk_ref, ka_ref, rk_ref, gw_ref, gb_ref,
                       y_ref, state_ref, *, n_pairs):
    c = CHUNK
    half = LANES // 2

    @pl.when(pl.program_id(1) == 0)
    def _():
        state_ref[...] = jnp.zeros_like(state_ref)

    row = lax.broadcasted_iota(jnp.int32, (c, c), 0)
    col = lax.broadcasted_iota(jnp.int32, (c, c), 1)
    ltri = (row >= col).astype(BF16)
    same_head = (row < half) == (col < half)
    lo = col < half
    row4 = lax.broadcasted_iota(jnp.int32, (c, 4 * c), 0)
    col4 = lax.broadcasted_iota(jnp.int32, (c, 4 * c), 1)
    incl4 = row4 >= (col4 & (c - 1))
    row2 = lax.broadcasted_iota(jnp.int32, (c, 2 * c), 0)
    col2 = lax.broadcasted_iota(jnp.int32, (c, 2 * c), 1)
    strict2 = row2 > (col2 & (c - 1))
    first_block = col2 < c

    def gsum(x):
        s_lo = jnp.sum(jnp.where(lo, x, 0.0), axis=-1, keepdims=True)
        s_hi = jnp.sum(jnp.where(lo, 0.0, x), axis=-1, keepdims=True)
        return jnp.where(lo, s_lo, s_hi)

    def by_head(x):
        return jnp.concatenate([jnp.where(lo, x, 0.0), jnp.where(lo, 0.0, x)], axis=0).astype(BF16)

    def split_bf16(x):
        hi = x.astype(BF16)
        rest = x - hi.astype(F32)
        mid = rest.astype(BF16)
        return hi, mid, (rest - mid.astype(F32)).astype(BF16)

    def stage_factors(p):
        sl = slice(p * LANES, (p + 1) * LANES)
        r, lw, k, v, a = (ref[0, :, sl] for ref in (r_ref, lw_ref, k_ref, v_ref, a_ref))
        cum3 = _dot(ltri, jnp.concatenate(split_bf16(lw), axis=1))
        cum = cum3[:, :LANES] + cum3[:, LANES:2 * LANES] + cum3[:, 2 * LANES:]
        cmid = cum[c // 2 - 1:c // 2, :]
        clast = cum[c - 1:c, :]
        ci = cum - cmid
        e_neg = jnp.exp(-ci)
        p_mid = jnp.exp(cmid)
        kk = k * kk_ref[:, sl]
        kkn = kk * lax.rsqrt(jnp.maximum(gsum(kk * kk), 1e-24))
        k2 = k * (1.0 + (a - 1.0) * ka_ref[:, sl])
        beta = kkn * a
        al_m = -kkn * jnp.exp(ci - lw)
        r_m = r * jnp.exp(ci)
        e_end = jnp.exp(clast - cum)
        ends = jnp.concatenate([beta * e_end, k2 * e_end], axis=0).astype(BF16)
        al2, r2 = by_head(al_m), by_head(r_m)
        lhs = jnp.concatenate([al2[:c], r2[:c], al2[c:], r2[c:]], axis=0)
        rhs = jnp.concatenate([beta * e_neg, k2 * e_neg], axis=0).astype(BF16)
        big = _dot_nt(lhs, rhs)
        s0 = state_ref[p]
        base = _dot_nt(jnp.concatenate([al_m * p_mid, r_m * p_mid], axis=0).astype(BF16), s0.astype(BF16))
        return dict(big=big, base=base, v2=by_head(v), ends=ends, decayed=s0 * jnp.exp(clast),
                    bonus=gsum(r * k2 * rk_ref[:, sl]) * v)

    def stage_masks(d):
        big = d.pop("big")
        blk = lambda i, j: big[i * c:(i + 1) * c, j * c:(j + 1) * c]
        zero2 = jnp.zeros((c, 2 * c), F32)
        d["pw"] = jnp.where(strict2, jnp.concatenate([blk(0, 0), blk(2, 0)], axis=1), zero2).astype(BF16)
        a_ak = jnp.where(strict2, jnp.concatenate([blk(0, 1), blk(2, 1)], axis=1), zero2).astype(BF16)
        d["a_y"] = jnp.where(incl4, jnp.concatenate([blk(1, 0), blk(3, 0), blk(1, 1), blk(3, 1)], axis=1),
                             0.0).astype(BF16)
        d["x"] = d["base"][:c] + _dot(a_ak, d["v2"])

    def stage_square(d, last):
        pw, x2 = d["pw"], by_head(d["x"])
        if last:
            d["x"] = d["x"] + _dot(pw, x2)
            return
        zero_b = jnp.zeros_like(pw)
        bd = jnp.concatenate([jnp.where(first_block, pw, zero_b), jnp.where(first_block, zero_b, pw)], axis=0)
        res = _dot(pw, jnp.concatenate([bd, x2], axis=1))
        d["pw"] = res[:, :2 * c].astype(BF16)
        d["x"] = d["x"] + res[:, 2 * c:]

    def stage_output(p, d):
        sl = slice(p * LANES, (p + 1) * LANES)
        x = d["x"]
        y = d["base"][c:] + _dot(d["a_y"], jnp.concatenate([by_head(x), d["v2"]], axis=0))
        upd = _dot(jnp.concatenate([x.T, v_ref[0, :, sl].T], axis=1).astype(BF16), d["ends"])
        state_ref[p] = d["decayed"] + jnp.where(same_head, upd, 0.0)
        mean = gsum(y) * (1.0 / half)
        yc = y - mean
        var = gsum(yc * yc) * (1.0 / half)
        out = yc * lax.rsqrt(var + GN_EPS) * gw_ref[:, sl] + gb_ref[:, sl]
        y_ref[0, :, sl] = (out + d["bonus"]) * g_ref[0, :, sl]

    pairs = [stage_factors(p) for p in range(n_pairs)]
    for d in pairs:
        stage_masks(d)
    n_steps = c.bit_length() - 1
    for step in range(n_steps):
        for d in pairs:
            stage_square(d, last=step + 1 == n_steps)
    for p, d in enumerate(pairs):
        stage_output(p, d)


def rwkv_chunked(r, lw, k, v, a, g, k_k, k_a, r_k, gn_w, gn_b):
    b, s, width = r.shape
    assert RWKV_HEAD * 2 == LANES and width % LANES == 0 and s % CHUNK == 0
    n_pairs = width // LANES
    seq = pl.BlockSpec((1, CHUNK, width), lambda i, j: (i, j, 0))
    par = pl.BlockSpec((1, width), lambda i, j: (0, 0))
    row = lambda t: t.reshape(1, width)
    return pl.pallas_call(
        functools.partial(_rwkv_chunk_kernel, n_pairs=n_pairs),
        grid=(b, s // CHUNK),
        in_specs=[seq] * 6 + [par] * 5,
        out_specs=seq,
        out_shape=jax.ShapeDtypeStruct((b, s, width), F32),
        scratch_shapes=[pltpu.VMEM((n_pairs, LANES, LANES), F32)],
        compiler_params=_cparams("parallel", "arbitrary"),
    )(r, lw, k, v, a, g, row(k_k), row(k_a), row(r_k), row(gn_w), row(gn_b))


def _diff_prep_kernel(p_ref, cf_ref, sa_ref, sb_ref, qg_ref, kg_ref, qk_ref, v_ref, *, heads):
    half = LANES // 2
    lane = lax.broadcasted_iota(jnp.int32, (p_ref.shape[1], LANES), 1)
    lo = lane < half
    cf, sa, sb = cf_ref[0], sa_ref[0], sb_ref[0]

    def norm_rope(x, g):
        sq = x * x
        ms = jnp.where(lo, jnp.sum(jnp.where(lo, sq, 0.0), axis=-1, keepdims=True),
                       jnp.sum(jnp.where(lo, 0.0, sq), axis=-1, keepdims=True)) * (1.0 / half)
        x = x * lax.rsqrt(ms + EPS) * g
        return x * cf + pltpu.roll(x, LANES - 8, axis=1) * sa + pltpu.roll(x, 8, axis=1) * sb

    ones_col = jnp.where(lane == 0, 1.0, 0.0).astype(BF16)
    for h in range(heads):
        tile = lambda i: p_ref[0, :, (i * heads + h) * LANES:(i * heads + h + 1) * LANES]
        q = norm_rope(tile(0), qg_ref[...]) * (half ** -0.5 * math.log2(math.e))
        qk_ref[0, :, h * LANES:(h + 1) * LANES] = q.astype(BF16)
        qk_ref[0, :, (heads + h) * LANES:(heads + h + 1) * LANES] = norm_rope(tile(1), kg_ref[...]).astype(BF16)
        v_ref[0, :, 2 * h * LANES:(2 * h + 1) * LANES] = tile(2).astype(BF16)
        v_ref[0, :, (2 * h + 1) * LANES:(2 * h + 2) * LANES] = ones_col


def _diff_attn_kernel(q_ref, k_ref, v_ref, lam_ref, sub_ref, o_ref, *, tq, lam_init):
    qi = pl.program_id(2)
    half = LANES // 2
    q = q_ref[0]
    lane = lax.broadcasted_iota(jnp.int32, q.shape, 1)
    zero = jnp.zeros_like(q)
    qs = [jnp.where(lane < half, q, zero), jnp.where(lane < half, zero, q)]
    n_parts = ATTN_ROW_PARTS
    rp = tq // n_parts
    chains = [(m, part) for part in range(n_parts) for m in range(2)]

    def block(j, carry, masked):
        kv_rows = pl.ds(pl.multiple_of(j * tq, tq), tq)
        kb = k_ref[0, kv_rows, :]
        vb = v_ref[0, kv_rows, :]
        scores, probs, maxes, out = {}, {}, {}, {}
        for i in range(len(chains) + 2):
            if i < len(chains):
                m, part = chains[i]
                s = _dot_nt(qs[m][part * rp:(part + 1) * rp], kb)
                if masked:
                    r_id = lax.broadcasted_iota(jnp.int32, s.shape, 0) + part * rp
                    c_id = lax.broadcasted_iota(jnp.int32, s.shape, 1)
                    s = jnp.where(c_id <= r_id, s, -1e30)
                scores[i] = s
            if 0 <= i - 1 < len(chains):
                s = scores.pop(i - 1)
                maxes[i - 1] = jnp.maximum(carry[2 * (i - 1)], jnp.max(s, axis=-1, keepdims=True))
                probs[i - 1] = jnp.exp2((s - maxes[i - 1]).astype(BF16))
            if 0 <= i - 2 < len(chains):
                c = i - 2
                out[c] = jnp.exp2(carry[2 * c] - maxes[c]) * carry[2 * c + 1] + _dot(probs.pop(c), vb)
        return tuple(t for c in range(len(chains)) for t in (maxes[c], out[c]))

    init = (jnp.full((rp, 1), -1e30, F32), jnp.zeros((rp, 2 * LANES), F32)) * len(chains)
    carry = lax.fori_loop(0, qi, lambda j, cr: block(j, cr, False), init)
    carry = block(qi, carry, True)
    acc1 = jnp.concatenate([carry[2 * c + 1] for c, (m, _) in enumerate(chains) if m == 0], axis=0)
    acc2 = jnp.concatenate([carry[2 * c + 1] for c, (m, _) in enumerate(chains) if m == 1], axis=0)

    lv = lam_ref[...]
    lam = (jnp.exp(jnp.sum(lv[0:1] * lv[1:2], axis=-1, keepdims=True))
           - jnp.exp(jnp.sum(lv[2:3] * lv[3:4], axis=-1, keepdims=True)) + lam_init)
    o = acc1[:, :LANES] / acc1[:, LANES:LANES + 1] - lam * (acc2[:, :LANES] / acc2[:, LANES:LANES + 1])
    o_ref[0] = _rms(o, sub_ref[...]) * (1.0 - lam_init)


def diff_attention(proj, tables, q_g, k_g, lam_vecs, subln_g, lam_init):
    b, s, cols = proj.shape
    h = DIFF_HEADS
    assert cols == 3 * h * LANES
    ts = _pick(s, (512, 256, 128))
    tile2 = lambda g: jnp.tile(g, 2).reshape(1, LANES)
    seq = lambda w: pl.BlockSpec((1, ts, w), lambda bi, si: (bi, si, 0))
    gspec2 = pl.BlockSpec((1, LANES), lambda bi, si: (0, 0))
    qk, vext = pl.pallas_call(
        functools.partial(_diff_prep_kernel, heads=h),
        grid=(b, s // ts),
        in_specs=[seq(cols), seq(LANES), seq(LANES), seq(LANES), gspec2, gspec2],
        out_specs=[seq(2 * h * LANES), seq(2 * h * LANES)],
        out_shape=[jax.ShapeDtypeStruct((b, s, 2 * h * LANES), BF16)] * 2,
        compiler_params=_cparams("parallel", "parallel"),
    )(proj, *tables, tile2(q_g), tile2(k_g))

    tq = _pick(s, (512, 256, 128))
    gspec = pl.BlockSpec((1, LANES), lambda bi, hi, qi: (0, 0))
    return pl.pallas_call(
        functools.partial(_diff_attn_kernel, tq=tq, lam_init=lam_init),
        grid=(b, h, s // tq),
        in_specs=[pl.BlockSpec((1, tq, LANES), lambda bi, hi, qi: (bi, qi, hi)),
                  pl.BlockSpec((1, s, LANES), lambda bi, hi, qi: (bi, 0, h + hi)),
                  pl.BlockSpec((1, s, 2 * LANES), lambda bi, hi, qi: (bi, 0, hi)),
                  pl.BlockSpec(lam_vecs.shape, lambda bi, hi, qi: (0, 0)),
                  gspec],
        out_specs=pl.BlockSpec((1, tq, LANES), lambda bi, hi, qi: (bi, qi, hi)),
        out_shape=jax.ShapeDtypeStruct((b, s, h * LANES), F32),
        compiler_params=_cparams("parallel", "parallel", "arbitrary"),
    )(qk, qk, vext, lam_vecs, subln_g.reshape(1, LANES))


def rope_tables(positions):
    half = LANES // 2
    rope_dims = half // 4
    inv_freq = ROPE_THETA ** (-jnp.arange(0, rope_dims, 2, dtype=F32) / rope_dims)
    ang = positions.astype(F32)[..., None] * inv_freq
    cos, sin = jnp.cos(ang), jnp.sin(ang)
    pad = jnp.zeros(cos.shape[:-1] + (half - rope_dims,), F32)
    zero = jnp.zeros_like(cos)
    cf = jnp.concatenate([cos, cos, pad + 1.0], axis=-1)
    sa = jnp.concatenate([-sin, zero, pad], axis=-1)
    sb = jnp.concatenate([zero, sin, pad], axis=-1)
    return tuple(jnp.concatenate([t, t], axis=-1) for t in (cf, sa, sb))


def _xattn_kernel(q_ref, kv_ref, qg_ref, kg_ref, o_ref, *, heads):
    d = q_ref.shape[-1] // heads
    for h in range(heads):
        q = _rms(q_ref[0, :, h * d:(h + 1) * d], qg_ref[...]) * (d ** -0.5)
        k = _rms(kv_ref[0, :, h * d:(h + 1) * d], kg_ref[...])
        v = kv_ref[0, :, (heads + h) * d:(heads + h + 1) * d]
        s = _dot_nt(q.astype(BF16), k.astype(BF16))
        s = s - jnp.max(s, axis=-1, keepdims=True)
        pr = jnp.exp(s)
        o = _dot(pr.astype(BF16), v.astype(BF16)) / jnp.sum(pr, axis=-1, keepdims=True)
        o_ref[0, :, h * d:(h + 1) * d] = o


def cross_attention(q, kv, q_g, k_g):
    b, s, dm = q.shape
    mlen = kv.shape[1]
    ts = _pick(s, (512, 256, 128))
    return pl.pallas_call(
        functools.partial(_xattn_kernel, heads=XATTN_HEADS),
        grid=(b, s // ts),
        in_specs=[pl.BlockSpec((1, ts, dm), lambda i, j: (i, j, 0)),
                  pl.BlockSpec((1, mlen, 2 * dm), lambda i, j: (i, 0, 0)),
                  pl.BlockSpec((1, q_g.shape[0]), lambda i, j: (0, 0)),
                  pl.BlockSpec((1, k_g.shape[0]), lambda i, j: (0, 0))],
        out_specs=pl.BlockSpec((1, ts, dm), lambda i, j: (i, j, 0)),
        out_shape=jax.ShapeDtypeStruct((b, s, dm), F32),
        compiler_params=_cparams("parallel", "arbitrary"),
    )(q, kv, q_g.reshape(1, -1), k_g.reshape(1, -1))


def _ffn_kernel(x_ref, g_ref, wg_ref, wu_ref, wd_ref, o_ref, h_ref):
    j = pl.program_id(1)

    @pl.when(j == 0)
    def _():
        h_ref[...] = _rms(x_ref[...], g_ref[...]).astype(BF16)
        o_ref[...] = x_ref[...]

    h = h_ref[...]
    gate = _dot(h, wg_ref[...])
    up = _dot(h, wu_ref[...])
    act = gate * jax.nn.sigmoid(gate) * up
    o_ref[...] += _dot(act.astype(BF16), wd_ref[...])


def ffn_dense(x, g, w_gu, w_down):
    m, d = x.shape
    f = w_down.shape[0]
    tm = _pick(m, (512, 256, 128))
    tf = _col_tile(d, f)
    nf = f // tf
    return pl.pallas_call(
        _ffn_kernel,
        grid=(m // tm, nf),
        in_specs=[pl.BlockSpec((tm, d), lambda i, j: (i, 0)),
                  pl.BlockSpec((1, d), lambda i, j: (0, 0)),
                  pl.BlockSpec((d, tf), lambda i, j: (0, j)),
                  pl.BlockSpec((d, tf), lambda i, j: (0, j + nf)),
                  pl.BlockSpec((tf, d), lambda i, j: (j, 0))],
        out_specs=pl.BlockSpec((tm, d), lambda i, j: (i, 0)),
        out_shape=jax.ShapeDtypeStruct((m, d), F32),
        scratch_shapes=[pltpu.VMEM((tm, d), BF16)],
        compiler_params=_cparams("parallel", "arbitrary"),
    )(x, g.reshape(1, d), w_gu, w_gu, w_down)


META_I1, META_I2, META_W1, META_W2, META_R1, META_R2 = range(6)


def _lane_pick(x, lane, idx):
    return jnp.sum(jnp.where(lane == idx, x, 0.0), axis=-1, keepdims=True)


def _router_kernel(x_ref, g_ref, wr_ref, meta_ref, cnt_ref, tri_ref, *, n_experts):
    tm = x_ref.shape[0]

    @pl.when(pl.program_id(0) == 0)
    def _():
        cnt_ref[...] = jnp.zeros_like(cnt_ref)
        r_id = lax.broadcasted_iota(jnp.int32, (tm, tm), 0)
        c_id = lax.broadcasted_iota(jnp.int32, (tm, tm), 1)
        tri_ref[...] = (r_id > c_id).astype(BF16)

    h = _rms(x_ref[...], g_ref[...])
    logits = _dot(h, wr_ref[...], HIGHEST)
    lane = lax.broadcasted_iota(jnp.int32, logits.shape, 1).astype(F32)
    neg = -jnp.inf
    l1 = jnp.where(lane < n_experts, logits, neg)
    m1 = jnp.max(l1, axis=-1, keepdims=True)
    i1 = jnp.min(jnp.where(l1 == m1, lane, float(LANES)), axis=-1, keepdims=True)
    l2 = jnp.where(lane == i1, neg, l1)
    m2 = jnp.max(l2, axis=-1, keepdims=True)
    i2 = jnp.min(jnp.where(l2 == m2, lane, float(LANES)), axis=-1, keepdims=True)
    e2 = jnp.exp(m2 - m1)
    w1 = 1.0 / (1.0 + e2)
    onehot = jnp.where((lane == i1) | (lane == i2), 1.0, 0.0)
    before = _dot(tri_ref[...], onehot.astype(BF16)) + cnt_ref[0:1, :]
    cnt_ref[0:1, :] += jnp.sum(onehot, axis=0, keepdims=True)
    meta = jnp.zeros_like(logits)
    for slot, val in ((META_I1, i1), (META_I2, i2), (META_W1, w1), (META_W2, e2 * w1),
                      (META_R1, _lane_pick(before, lane, i1)), (META_R2, _lane_pick(before, lane, i2))):
        meta = jnp.where(lane == slot, val, meta)
    meta_ref[...] = meta


def router_top2(x, g, w_router):
    m, d = x.shape
    n_experts = w_router.shape[1]
    wr = jnp.zeros((d, LANES), F32).at[:, :n_experts].set(w_router)
    tm = _pick(m, (1024, 512, 256, 128))
    return pl.pallas_call(
        functools.partial(_router_kernel, n_experts=n_experts),
        grid=(m // tm,),
        in_specs=[pl.BlockSpec((tm, d), lambda i: (i, 0)),
                  pl.BlockSpec((1, d), lambda i: (0, 0)),
                  pl.BlockSpec((d, LANES), lambda i: (0, 0))],
        out_specs=[pl.BlockSpec((tm, LANES), lambda i: (i, 0)),
                   pl.BlockSpec((8, LANES), lambda i: (0, 0))],
        out_shape=[jax.ShapeDtypeStruct((m, LANES), F32), jax.ShapeDtypeStruct((8, LANES), F32)],
        scratch_shapes=[pltpu.VMEM((tm, tm), BF16)],
        compiler_params=_cparams("arbitrary"),
    )(x, g.reshape(1, d), wr)


def _idx_copy(dest_hbm, idx_smem, isem, tile, slot):
    return pltpu.make_async_copy(dest_hbm.at[tile], idx_smem.at[slot], isem.at[slot])


def _dispatch_kernel(dest_hbm, x_ref, zeros_hbm, xs_hbm, idx_smem, isem, sem, *, tm):
    del zeros_hbm
    i = pl.program_id(0)
    slot = i % 2

    @pl.when(i == 0)
    def _():
        _idx_copy(dest_hbm, idx_smem, isem, 0, 0).start()

    @pl.when(i + 1 < pl.num_programs(0))
    def _():
        _idx_copy(dest_hbm, idx_smem, isem, i + 1, 1 - slot).start()

    _idx_copy(dest_hbm, idx_smem, isem, i, slot).wait()

    def row_copy(t, k):
        d = idx_smem[slot, TOP_K * t + k]
        return pltpu.make_async_copy(x_ref.at[pl.ds(t, 1)], xs_hbm.at[pl.ds(d, 1)], sem)

    def issue(t, carry):
        for k in range(TOP_K):
            row_copy(t, k).start()
        return carry

    lax.fori_loop(0, tm, issue, 0, unroll=8)
    for _ in range(TOP_K):
        pltpu.make_async_copy(x_ref, xs_hbm.at[pl.ds(0, tm)], sem).wait()


def moe_dispatch(x, dest, n_rows):
    m, d = x.shape
    n_tiles, per_tile = dest.shape
    tm = per_tile // TOP_K
    zeros = jnp.zeros((n_rows, d), F32)
    return pl.pallas_call(
        functools.partial(_dispatch_kernel, tm=tm),
        grid=(n_tiles,),
        in_specs=[pl.BlockSpec(memory_space=pl.ANY),
                  pl.BlockSpec((tm, d), lambda i: (i, 0)),
                  pl.BlockSpec(memory_space=pl.ANY)],
        out_specs=pl.BlockSpec(memory_space=pl.ANY),
        out_shape=jax.ShapeDtypeStruct((n_rows, d), F32),
        scratch_shapes=[pltpu.SMEM((2, per_tile), jnp.int32), pltpu.SemaphoreType.DMA((2,)),
                        pltpu.SemaphoreType.DMA(())],
        input_output_aliases={2: 0},
        compiler_params=_cparams("arbitrary"),
    )(dest, x, zeros)


def _moe_ffn_kernel(te_ref, nu_ref, x_ref, g_ref, wg_ref, wu_ref, wd_ref, o_ref, h_ref):
    del te_ref
    j = pl.program_id(1)

    @pl.when(pl.program_id(0) < nu_ref[0])
    def _():
        @pl.when(j == 0)
        def _():
            h_ref[...] = _rms(x_ref[...], g_ref[...]).astype(BF16)

        h = h_ref[...]
        gate = _dot(h, wg_ref[0])
        up = _dot(h, wu_ref[0])
        act = gate * jax.nn.sigmoid(gate) * up
        y = _dot(act.astype(BF16), wd_ref[0])

        @pl.when(j == 0)
        def _():
            o_ref[...] = y

        @pl.when(j > 0)
        def _():
            o_ref[...] += y

    @pl.when((pl.program_id(0) >= nu_ref[0]) & (j == 0))
    def _():
        o_ref[...] = jnp.zeros_like(o_ref)


def moe_grouped_ffn(xs, g, tile_expert, n_used, w_gu, w_down, tm):
    p, d = xs.shape
    f = w_down.shape[1]
    tf = _col_tile(d, f)
    nf = f // tf
    last = lambda n, nu: jnp.minimum(n, nu[0] - 1)
    col = lambda n, j, nu: jnp.where(n < nu[0], j, nf - 1)
    grid_spec = pltpu.PrefetchScalarGridSpec(
        num_scalar_prefetch=2,
        grid=(p // tm, nf),
        in_specs=[pl.BlockSpec((tm, d), lambda n, j, te, nu: (last(n, nu), 0)),
                  pl.BlockSpec((1, d), lambda n, j, te, nu: (0, 0)),
                  pl.BlockSpec((1, d, tf), lambda n, j, te, nu: (te[n], 0, col(n, j, nu))),
                  pl.BlockSpec((1, d, tf), lambda n, j, te, nu: (te[n], 0, col(n, j, nu) + nf)),
                  pl.BlockSpec((1, tf, d), lambda n, j, te, nu: (te[n], col(n, j, nu), 0))],
        out_specs=pl.BlockSpec((tm, d), lambda n, j, te, nu: (n, 0)),
        scratch_shapes=[pltpu.VMEM((tm, d), BF16)],
    )
    return pl.pallas_call(
        _moe_ffn_kernel,
        grid_spec=grid_spec,
        out_shape=jax.ShapeDtypeStruct((p, d), F32),
        compiler_params=_cparams("arbitrary", "arbitrary"),
    )(tile_expert, n_used, xs, g.reshape(1, d), w_gu, w_gu, w_down)


def _combine_kernel(dest_hbm, x_ref, meta_ref, ys_hbm, o_ref, idx_smem, ybuf, isem, sem, *, tm):
    i = pl.program_id(0)
    slot = i % 2

    def row_copy(sl, t, k):
        d = idx_smem[sl, TOP_K * t + k]
        return pltpu.make_async_copy(ys_hbm.at[pl.ds(d, 1)], ybuf.at[sl, k, pl.ds(t, 1)], sem.at[sl])

    def gather_tile(tile, sl):
        _idx_copy(dest_hbm, idx_smem, isem, tile, sl).start()
        _idx_copy(dest_hbm, idx_smem, isem, tile, sl).wait()

        def issue(t, carry):
            for k in range(TOP_K):
                row_copy(sl, t, k).start()
            return carry

        lax.fori_loop(0, tm, issue, 0, unroll=8)

    @pl.when(i == 0)
    def _():
        gather_tile(0, 0)

    @pl.when(i + 1 < pl.num_programs(0))
    def _():
        gather_tile(i + 1, 1 - slot)

    for k in range(TOP_K):
        pltpu.make_async_copy(ys_hbm.at[pl.ds(0, tm)], ybuf.at[slot, k], sem.at[slot]).wait()
    meta = meta_ref[...]
    lane = lax.broadcasted_iota(jnp.int32, meta.shape, 1)
    w1 = _lane_pick(meta, lane, META_W1)
    w2 = _lane_pick(meta, lane, META_W2)
    o_ref[...] = x_ref[...] + w1 * ybuf[slot, 0] + w2 * ybuf[slot, 1]


def moe_combine(x, meta, dest, ys):
    m, d = x.shape
    n_tiles, per_tile = dest.shape
    tm = per_tile // TOP_K
    return pl.pallas_call(
        functools.partial(_combine_kernel, tm=tm),
        grid=(n_tiles,),
        in_specs=[pl.BlockSpec(memory_space=pl.ANY),
                  pl.BlockSpec((tm, d), lambda i: (i, 0)),
                  pl.BlockSpec((tm, LANES), lambda i: (i, 0)),
                  pl.BlockSpec(memory_space=pl.ANY)],
        out_specs=pl.BlockSpec((tm, d), lambda i: (i, 0)),
        out_shape=jax.ShapeDtypeStruct((m, d), F32),
        scratch_shapes=[pltpu.SMEM((2, per_tile), jnp.int32), pltpu.VMEM((2, TOP_K, tm, d), F32),
                        pltpu.SemaphoreType.DMA((2,)), pltpu.SemaphoreType.DMA((2,))],
        compiler_params=_cparams("arbitrary"),
    )(dest, x, meta, ys)


def ffn_moe(x, g, w_router, w_gu, w_down):
    m, d = x.shape
    n_experts = w_down.shape[0]
    tg = _pick(m, (MOE_GROUP_TILE, 256, 128))
    td = _pick(m, (MOE_TOKEN_TILE, 256, 128))
    meta, counts = router_top2(x, g, w_router)
    counts = counts[0, :n_experts].astype(jnp.int32)
    padded = (counts + tg - 1) // tg * tg
    ends = jnp.cumsum(padded)
    offsets = ends - padded
    n_tiles = (TOP_K * m) // tg + n_experts
    n_used = (ends[-1] // tg).astype(jnp.int32)
    tile_id = jnp.minimum(jnp.arange(n_tiles, dtype=jnp.int32), n_used - 1)
    tile_expert = jnp.sum(ends[None, :] <= (tile_id * tg)[:, None], axis=1).astype(jnp.int32)
    picks = meta[:, META_I1:META_I2 + 1].astype(jnp.int32)
    ranks = meta[:, META_R1:META_R2 + 1].astype(jnp.int32)
    dest = (offsets[picks] + ranks).reshape(m // td, TOP_K * td)
    xs = moe_dispatch(x, dest, n_tiles * tg)
    ys = moe_grouped_ffn(xs, g, tile_expert, n_used.reshape(1), w_gu, w_down, tg)
    return moe_combine(x, meta, dest, ys)


def _block_diag(w_a, w_b):
    za = jnp.zeros_like(w_a)
    zb = jnp.zeros_like(w_b)
    return jnp.concatenate([jnp.concatenate([w_a, za], axis=1), jnp.concatenate([zb, w_b], axis=1)], axis=0)


def kernel(x, mem, positions, mix_norm, in_proj, shift_mu, decay_bias, decay_up, iclr_bias, iclr_up, gate_up,
           key_kk_scale, key_iclr_scale, bonus_rk, rwkv_gn_w, rwkv_gn_b, vres_down, vres_shift_mu, vres_bias,
           vres_up, q_norm, k_norm, lambda_q1, lambda_k1, lambda_q2, lambda_k2, diff_subln, out_proj, xattn_norm,
           mem_norm, xattn_wq, xattn_wkv, xattn_wo, xattn_q_norm, xattn_k_norm, ffn_norm, dense_w_gu,
           dense_w_down, router, expert_w_gu, expert_w_down):
    b, s, d = x.shape
    depth = mix_norm.shape[0]
    width = decay_bias.shape[1]
    rwkv_cols = shift_mu.shape[1]
    in_cols = in_proj.shape[2]
    vres_rank = vres_down.shape[2] if depth > 1 else 0
    assert decay_up.shape[1] == LANES // 2 and iclr_up.shape[1] == LANES // 2 and gate_up.shape[1] == LANES
    assert rwkv_cols == 3 * width + 2 * LANES and rwkv_cols % LANES == 0 and in_cols % LANES == 0
    assert vres_rank <= LANES
    tables = rope_tables(positions)
    xf = x.reshape(b * s, d)
    memf = mem.reshape(-1, d)
    v_first = None
    for l in range(depth):
        w_rwkv = in_proj[l][:, :rwkv_cols]
        vres = None
        if l > 0:
            pad = jnp.zeros((d, LANES - vres_rank), F32)
            w_rwkv = jnp.concatenate([w_rwkv, vres_down[l - 1], pad], axis=1)
            vmu = jnp.zeros((LANES,), F32).at[:vres_rank].set(vres_shift_mu[l - 1])
            vup = jnp.zeros((LANES, width), F32).at[:vres_rank].set(vres_up[l - 1]).astype(BF16)
            vres = (v_first, rwkv_cols // LANES, vmu, vres_bias[l - 1], vup)
        proj_rwkv = norm_matmul(xf, mix_norm[l], w_rwkv.astype(BF16)).reshape(b, s, -1)
        proj_diff = norm_matmul(xf, mix_norm[l], in_proj[l][:, rwkv_cols:].astype(BF16)).reshape(b, s, -1)
        wwa = _block_diag(decay_up[l], iclr_up[l]).astype(BF16)
        r_, lw_, k_, v_, a_, g_ = rwkv_prep(proj_rwkv, width, shift_mu[l], decay_bias[l], iclr_bias[l], wwa,
                                            gate_up[l].astype(BF16), vres)
        if l == 0:
            v_first = v_
        y_rwkv = rwkv_chunked(r_, lw_, k_, v_, a_, g_, key_kk_scale[l], key_iclr_scale[l],
                              bonus_rk[l].reshape(-1), rwkv_gn_w[l], rwkv_gn_b[l])
        lam_init = 0.8 - 0.6 * math.exp(-0.3 * l)
        lam_vecs = jnp.stack([lambda_q1[l], lambda_k1[l], lambda_q2[l], lambda_k2[l]])
        y_diff = diff_attention(proj_diff, tables, q_norm[l], k_norm[l], lam_vecs, diff_subln[l], lam_init)
        w_out = out_proj[l].astype(BF16)
        xf = matmul_residual([y_rwkv.reshape(b * s, -1), y_diff.reshape(b * s, -1)],
                             [w_out[:width], w_out[width:]], xf)
        q = norm_matmul(xf, xattn_norm[l], xattn_wq[l].astype(BF16)).reshape(b, s, d)
        kv = norm_matmul(memf, mem_norm[l], xattn_wkv[l].astype(BF16)).reshape(b, -1, 2 * d)
        o = cross_attention(q, kv, xattn_q_norm[l], xattn_k_norm[l])
        xf = matmul_residual([o.reshape(b * s, d)], [xattn_wo[l].astype(BF16)], xf)
        if l % 2 == 0:
            xf = ffn_dense(xf, ffn_norm[l], dense_w_gu[l // 2].astype(BF16), dense_w_down[l // 2].astype(BF16))
        else:
            xf = ffn_moe(xf, ffn_norm[l], router[l // 2], expert_w_gu[l // 2].astype(BF16),
                         expert_w_down[l // 2].astype(BF16))
    return xf.reshape(b, s, d)
```

```python
import functools
import math

import jax
import jax.numpy as jnp
from jax import lax
from jax.experimental import pallas as pl
from jax.experimental.pallas import tpu as pltpu

F32 = jnp.float32
BF16 = jnp.bfloat16
HIGHEST = lax.Precision.HIGHEST

EPS = 1e-6
GN_EPS = 64e-5
ROPE_THETA = 500000.0
RWKV_HEAD = 64
DIFF_HEADS = 4
XATTN_HEADS = 4
TOP_K = 2
LANES = 128
CHUNK = 128
VMEM_LIMIT = 56 * 1024 * 1024
RWKV_ROWS_PER_STEP = 2
ATTN_ROW_PARTS = 2
WEIGHT_TILE_BYTES = 4 * 1024 * 1024
MOE_GROUP_TILE = 512
MOE_TOKEN_TILE = 512


def _cparams(*sem):
    return pltpu.CompilerParams(dimension_semantics=sem, vmem_limit_bytes=VMEM_LIMIT)


def _pick(n, prefs):
    for p in prefs:
        if n % p == 0:
            return p
    return n


def _col_tile(k, n):
    best = LANES
    for t in range(LANES, n + 1, LANES):
        if n % t == 0 and k * t * 2 <= WEIGHT_TILE_BYTES:
            best = t
    return best


def _dot(a, b, prec=None):
    return jnp.dot(a, b, preferred_element_type=F32, precision=prec)


def _dot_nt(a, b, prec=None):
    return lax.dot_general(a, b, (((1,), (1,)), ((), ())), preferred_element_type=F32, precision=prec)


def _rms(x, g):
    return x * lax.rsqrt(jnp.mean(x * x, axis=-1, keepdims=True) + EPS) * g


def _norm_mm_kernel(x_ref, g_ref, w_ref, o_ref, h_ref):
    @pl.when(pl.program_id(1) == 0)
    def _():
        h_ref[...] = _rms(x_ref[...], g_ref[...]).astype(BF16)

    o_ref[...] = _dot(h_ref[...], w_ref[...])


def norm_matmul(x, g, w):
    m, k = x.shape
    n = w.shape[1]
    tm = _pick(m, (1024, 512, 256, 128))
    tn = _col_tile(k, n)
    return pl.pallas_call(
        _norm_mm_kernel,
        grid=(m // tm, n // tn),
        in_specs=[pl.BlockSpec((tm, k), lambda i, j: (i, 0)),
                  pl.BlockSpec((1, k), lambda i, j: (0, 0)),
                  pl.BlockSpec((k, tn), lambda i, j: (0, j))],
        out_specs=pl.BlockSpec((tm, tn), lambda i, j: (i, j)),
        out_shape=jax.ShapeDtypeStruct((m, n), F32),
        scratch_shapes=[pltpu.VMEM((tm, k), BF16)],
        compiler_params=_cparams("parallel", "arbitrary"),
    )(x, g.reshape(1, k), w)


def _mm_res_kernel(*refs, n_in):
    a_refs, w_refs = refs[:n_in], refs[n_in:2 * n_in]
    res_ref, o_ref = refs[2 * n_in], refs[2 * n_in + 1]
    acc = res_ref[...]
    for a_ref, w_ref in zip(a_refs, w_refs):
        acc = acc + _dot(a_ref[...].astype(BF16), w_ref[...])
    o_ref[...] = acc


def matmul_residual(a_list, w_list, res):
    m, n = res.shape
    tm = _pick(m, (1024, 512, 256, 128))
    tn = _col_tile(sum(w.shape[0] for w in w_list), n)
    n_in = len(a_list)
    in_specs = ([pl.BlockSpec((tm, a.shape[1]), lambda i, j: (i, 0)) for a in a_list]
                + [pl.BlockSpec((w.shape[0], tn), lambda i, j: (0, j)) for w in w_list]
                + [pl.BlockSpec((tm, tn), lambda i, j: (i, j))])
    return pl.pallas_call(
        functools.partial(_mm_res_kernel, n_in=n_in),
        grid=(m // tm, n // tn),
        in_specs=in_specs,
        out_specs=pl.BlockSpec((tm, tn), lambda i, j: (i, j)),
        out_shape=jax.ShapeDtypeStruct((m, n), F32),
        compiler_params=_cparams("parallel", "arbitrary"),
    )(*a_list, *w_list, res)


def _token_shift(p, carry_ref, mu):
    rows = p.shape[0]
    prev = pltpu.roll(p, 1, axis=0)
    rid = lax.broadcasted_iota(jnp.int32, p.shape, 0)
    prev = jnp.where(rid == 0, carry_ref[...], prev)
    carry_ref[...] = p[rows - 1:rows, :]
    return p + (prev - p) * mu


def _rwkv_in_kernel(*refs, width, has_vres, tiles_per_seq):
    if has_vres:
        (x_ref, gn_ref, w_ref, vf_ref, mu_ref, w0_ref, a0_ref, wwa_ref, gup_ref, vmu_ref, vb_ref, vup_ref,
         r_ref, lw_ref, k_ref, v_ref, a_ref, g_ref, carry_ref, carry_v_ref) = refs
    else:
        (x_ref, gn_ref, w_ref, mu_ref, w0_ref, a0_ref, wwa_ref, gup_ref,
         r_ref, lw_ref, k_ref, v_ref, a_ref, g_ref, carry_ref) = refs
    ncol = 3 * width + 2 * LANES

    @pl.when(pl.program_id(0) % tiles_per_seq == 0)
    def _():
        carry_ref[...] = jnp.zeros_like(carry_ref)
        if has_vres:
            carry_v_ref[...] = jnp.zeros_like(carry_v_ref)

    proj = _dot(_rms(x_ref[...], gn_ref[...]).astype(BF16), w_ref[...])
    sh = _token_shift(proj[:, :ncol], carry_ref, mu_ref[...])
    r_ref[...] = sh[:, :width]
    k_ref[...] = sh[:, width:2 * width]
    v = sh[:, 2 * width:3 * width]
    dwa = sh[:, 3 * width:3 * width + LANES]
    lane = lax.broadcasted_iota(jnp.int32, dwa.shape, 1)
    dwa = jnp.where(lane < LANES // 2, jnp.tanh(dwa), dwa)
    wa = _dot(dwa.astype(BF16), wwa_ref[...])
    z = -(w0_ref[...] + wa[:, :width])
    softplus = jnp.maximum(z, 0.0) + jnp.log(1.0 + jnp.exp(-jnp.abs(z)))
    lw_ref[...] = -jnp.exp(-softplus - 0.5)
    a_ref[...] = jax.nn.sigmoid(a0_ref[...] + wa[:, width:])
    dg = jax.nn.sigmoid(sh[:, 3 * width + LANES:3 * width + 2 * LANES])
    g_ref[...] = _dot(dg.astype(BF16), gup_ref[...])
    if has_vres:
        shv = _token_shift(proj[:, ncol:ncol + LANES], carry_v_ref, vmu_ref[...])
        mix = jax.nn.sigmoid(vb_ref[...] + _dot(shv.astype(BF16), vup_ref[...]))
        v = v + (vf_ref[...] - v) * mix
    v_ref[...] = v


def rwkv_in_proj(x, seq_len, gn, w, width, mu, w0, a0, wwa, gup, vres=None):
    m, d = x.shape
    tm = _pick(seq_len, (512, 256, 128))
    ncol = 3 * width + 2 * LANES
    assert mu.shape[-1] == ncol and m % seq_len == 0
    row = lambda a: a.reshape(1, -1)
    full = lambda a: pl.BlockSpec(a.shape, lambda i: (0,) * a.ndim)
    rows = lambda c: pl.BlockSpec((tm, c), lambda i: (i, 0))
    params = [row(mu), row(w0), row(a0), wwa, gup]
    inputs = [x, row(gn), w]
    in_specs = [rows(d), full(row(gn)), full(w)]
    scratch = [pltpu.VMEM((1, ncol), F32)]
    if vres is not None:
        v_first, vmu, vb, vup = vres
        inputs.append(v_first)
        in_specs.append(rows(width))
        params += [row(vmu), row(vb), vup]
        scratch.append(pltpu.VMEM((1, LANES), F32))
    in_specs += [full(p) for p in params]
    out = jax.ShapeDtypeStruct((m, width), F32)
    return pl.pallas_call(
        functools.partial(_rwkv_in_kernel, width=width, has_vres=vres is not None, tiles_per_seq=seq_len // tm),
        grid=(m // tm,),
        in_specs=in_specs,
        out_specs=[rows(width)] * 6,
        out_shape=[out] * 6,
        scratch_shapes=scratch,
        compiler_params=_cparams("arbitrary"),
    )(*inputs, *params)


def _rwkv_chunk_kernel(r_ref, lw_ref, k_ref, v_ref, a_ref, g_ref, kk_ref, ka_ref, rk_ref, gw_ref, gb_ref,
                       y_ref, state_ref, *, n_pairs):
    c = CHUNK
    half = LANES // 2

    @pl.when(pl.program_id(1) == 0)
    def _():
        state_ref[...] = jnp.zeros_like(state_ref)

    row = lax.broadcasted_iota(jnp.int32, (c, c), 0)
    col = lax.broadcasted_iota(jnp.int32, (c, c), 1)
    ltri = (row >= col).astype(BF16)
    same_head = (row < half) == (col < half)
    lo = col < half
    row4 = lax.broadcasted_iota(jnp.int32, (c, 4 * c), 0)
    col4 = lax.broadcasted_iota(jnp.int32, (c, 4 * c), 1)
    incl4 = row4 >= (col4 & (c - 1))
    row2 = lax.broadcasted_iota(jnp.int32, (c, 2 * c), 0)
    col2 = lax.broadcasted_iota(jnp.int32, (c, 2 * c), 1)
    strict2 = row2 > (col2 & (c - 1))
    first_block = col2 < c

    def gsum(x):
        s_lo = jnp.sum(jnp.where(lo, x, 0.0), axis=-1, keepdims=True)
        s_hi = jnp.sum(jnp.where(lo, 0.0, x), axis=-1, keepdims=True)
        return jnp.where(lo, s_lo, s_hi)

    def by_head(x):
        return jnp.concatenate([jnp.where(lo, x, 0.0), jnp.where(lo, 0.0, x)], axis=0).astype(BF16)

    def split_bf16(x):
        hi = x.astype(BF16)
        rest = x - hi.astype(F32)
        mid = rest.astype(BF16)
        return hi, mid, (rest - mid.astype(F32)).astype(BF16)

    def stage_factors(bi, p):
        sl = slice(p * LANES, (p + 1) * LANES)
        r, lw, k, v, a = (ref[bi, :, sl] for ref in (r_ref, lw_ref, k_ref, v_ref, a_ref))
        cum3 = _dot(ltri, jnp.concatenate(split_bf16(lw), axis=1))
        cum = cum3[:, :LANES] + cum3[:, LANES:2 * LANES] + cum3[:, 2 * LANES:]
        cmid = cum[c // 2 - 1:c // 2, :]
        clast = cum[c - 1:c, :]
        ci = cum - cmid
        e_neg = jnp.exp(-ci)
        p_mid = jnp.exp(cmid)
        kk = k * kk_ref[:, sl]
        kkn = kk * lax.rsqrt(jnp.maximum(gsum(kk * kk), 1e-24))
        k2 = k * (1.0 + (a - 1.0) * ka_ref[:, sl])
        beta = kkn * a
        al_m = -kkn * jnp.exp(ci - lw)
        r_m = r * jnp.exp(ci)
        e_end = jnp.exp(clast - cum)
        ends = jnp.concatenate([beta * e_end, k2 * e_end], axis=0).astype(BF16)
        al2, r2 = by_head(al_m), by_head(r_m)
        lhs = jnp.concatenate([al2[:c], r2[:c], al2[c:], r2[c:]], axis=0)
        rhs = jnp.concatenate([beta * e_neg, k2 * e_neg], axis=0).astype(BF16)
        big = _dot_nt(lhs, rhs)
        s0 = state_ref[bi * n_pairs + p]
        base = _dot_nt(jnp.concatenate([al_m * p_mid, r_m * p_mid], axis=0).astype(BF16), s0.astype(BF16))
        return dict(big=big, base=base, v2=by_head(v), ends=ends, decayed=s0 * jnp.exp(clast),
                    bonus=gsum(r * k2 * rk_ref[:, sl]) * v)

    def stage_masks(d):
        big = d.pop("big")
        blk = lambda i, j: big[i * c:(i + 1) * c, j * c:(j + 1) * c]
        zero2 = jnp.zeros((c, 2 * c), F32)
        d["pw"] = jnp.where(strict2, jnp.concatenate([blk(0, 0), blk(2, 0)], axis=1), zero2).astype(BF16)
        a_ak = jnp.where(strict2, jnp.concatenate([blk(0, 1), blk(2, 1)], axis=1), zero2).astype(BF16)
        d["a_y"] = jnp.where(incl4, jnp.concatenate([blk(1, 0), blk(3, 0), blk(1, 1), blk(3, 1)], axis=1),
                             0.0).astype(BF16)
        d["x"] = d["base"][:c] + _dot(a_ak, d["v2"])

    def stage_square(d, last):
        pw, x2 = d["pw"], by_head(d["x"])
        if last:
            d["x"] = d["x"] + _dot(pw, x2)
            return
        zero_b = jnp.zeros_like(pw)
        bd = jnp.concatenate([jnp.where(first_block, pw, zero_b), jnp.where(first_block, zero_b, pw)], axis=0)
        res = _dot(pw, jnp.concatenate([bd, x2], axis=1))
        d["pw"] = res[:, :2 * c].astype(BF16)
        d["x"] = d["x"] + res[:, 2 * c:]

    def stage_output(bi, p, d):
        sl = slice(p * LANES, (p + 1) * LANES)
        x = d["x"]
        y = d["base"][c:] + _dot(d["a_y"], jnp.concatenate([by_head(x), d["v2"]], axis=0))
        upd = _dot(jnp.concatenate([x.T, v_ref[bi, :, sl].T], axis=1).astype(BF16), d["ends"])
        state_ref[bi * n_pairs + p] = d["decayed"] + jnp.where(same_head, upd, 0.0)
        mean = gsum(y) * (1.0 / half)
        yc = y - mean
        var = gsum(yc * yc) * (1.0 / half)
        out = yc * lax.rsqrt(var + GN_EPS) * gw_ref[:, sl] + gb_ref[:, sl]
        y_ref[bi, :, sl] = (out + d["bonus"]) * g_ref[bi, :, sl]

    ids = [(bi, p) for bi in range(r_ref.shape[0]) for p in range(n_pairs)]
    pairs = [stage_factors(bi, p) for bi, p in ids]
    for d in pairs:
        stage_masks(d)
    n_steps = c.bit_length() - 1
    for step in range(n_steps):
        for d in pairs:
            stage_square(d, last=step + 1 == n_steps)
    for (bi, p), d in zip(ids, pairs):
        stage_output(bi, p, d)


def rwkv_chunked(r, lw, k, v, a, g, k_k, k_a, r_k, gn_w, gn_b):
    b, s, width = r.shape
    assert RWKV_HEAD * 2 == LANES and width % LANES == 0 and s % CHUNK == 0
    n_pairs = width // LANES
    nb = _pick(b, (RWKV_ROWS_PER_STEP, 1))
    seq = pl.BlockSpec((nb, CHUNK, width), lambda i, j: (i, j, 0))
    par = pl.BlockSpec((1, width), lambda i, j: (0, 0))
    row = lambda t: t.reshape(1, width)
    return pl.pallas_call(
        functools.partial(_rwkv_chunk_kernel, n_pairs=n_pairs),
        grid=(b // nb, s // CHUNK),
        in_specs=[seq] * 6 + [par] * 5,
        out_specs=seq,
        out_shape=jax.ShapeDtypeStruct((b, s, width), F32),
        scratch_shapes=[pltpu.VMEM((nb * n_pairs, LANES, LANES), F32)],
        compiler_params=_cparams("parallel", "arbitrary"),
    )(r, lw, k, v, a, g, row(k_k), row(k_a), row(r_k), row(gn_w), row(gn_b))


def _diff_in_kernel(x_ref, gn_ref, w_ref, rot_ref, qg_ref, kg_ref, qk_ref, v_ref, *, heads):
    half = LANES // 2
    rope_half = half // 8
    proj = _dot(_rms(x_ref[...], gn_ref[...]).astype(BF16), w_ref[...])
    lane = lax.broadcasted_iota(jnp.int32, (x_ref.shape[0], LANES), 1)
    lo = lane < half
    pos = lane & (half - 1)
    rot = rot_ref[...]
    first, second = pos < rope_half, (pos >= rope_half) & (pos < 2 * rope_half)
    cf = jnp.where(first, rot, jnp.where(second, pltpu.roll(rot, rope_half, axis=1), 1.0))
    sa = jnp.where(first, -pltpu.roll(rot, LANES - rope_half, axis=1), 0.0)
    sb = jnp.where(second, rot, 0.0)

    def norm_rope(x, g):
        sq = x * x
        ms = jnp.where(lo, jnp.sum(jnp.where(lo, sq, 0.0), axis=-1, keepdims=True),
                       jnp.sum(jnp.where(lo, 0.0, sq), axis=-1, keepdims=True)) * (1.0 / half)
        x = x * lax.rsqrt(ms + EPS) * g
        return x * cf + pltpu.roll(x, LANES - rope_half, axis=1) * sa + pltpu.roll(x, rope_half, axis=1) * sb

    ones_col = jnp.where(lane == 0, 1.0, 0.0).astype(BF16)
    for h in range(heads):
        tile = lambda i: proj[:, (i * heads + h) * LANES:(i * heads + h + 1) * LANES]
        q = norm_rope(tile(0), qg_ref[...]) * (half ** -0.5 * math.log2(math.e))
        qk_ref[:, h * LANES:(h + 1) * LANES] = q.astype(BF16)
        qk_ref[:, (heads + h) * LANES:(heads + h + 1) * LANES] = norm_rope(tile(1), kg_ref[...]).astype(BF16)
        v_ref[:, 2 * h * LANES:(2 * h + 1) * LANES] = tile(2).astype(BF16)
        v_ref[:, (2 * h + 1) * LANES:(2 * h + 2) * LANES] = ones_col


def _diff_attn_kernel(q_ref, k_ref, v_ref, lam_ref, sub_ref, o_ref, *, tq, lam_init):
    qi = pl.program_id(2)
    half = LANES // 2
    q = q_ref[0]
    lane = lax.broadcasted_iota(jnp.int32, q.shape, 1)
    zero = jnp.zeros_like(q)
    qs = [jnp.where(lane < half, q, zero), jnp.where(lane < half, zero, q)]
    n_parts = ATTN_ROW_PARTS
    rp = tq // n_parts
    chains = [(m, part) for part in range(n_parts) for m in range(2)]

    def block(j, carry, masked):
        kv_rows = pl.ds(pl.multiple_of(j * tq, tq), tq)
        kb = k_ref[0, kv_rows, :]
        vb = v_ref[0, kv_rows, :]
        scores, probs, maxes, out = {}, {}, {}, {}
        for i in range(len(chains) + 2):
            if i < len(chains):
                m, part = chains[i]
                s = _dot_nt(qs[m][part * rp:(part + 1) * rp], kb)
                if masked:
                    r_id = lax.broadcasted_iota(jnp.int32, s.shape, 0) + part * rp
                    c_id = lax.broadcasted_iota(jnp.int32, s.shape, 1)
                    s = jnp.where(c_id <= r_id, s, -1e30)
                scores[i] = s
            if 0 <= i - 1 < len(chains):
                s = scores.pop(i - 1)
                maxes[i - 1] = jnp.maximum(carry[2 * (i - 1)], jnp.max(s, axis=-1, keepdims=True))
                probs[i - 1] = jnp.exp2((s - maxes[i - 1]).astype(BF16))
            if 0 <= i - 2 < len(chains):
                c = i - 2
                out[c] = jnp.exp2(carry[2 * c] - maxes[c]) * carry[2 * c + 1] + _dot(probs.pop(c), vb)
        return tuple(t for c in range(len(chains)) for t in (maxes[c], out[c]))

    init = (jnp.full((rp, 1), -1e30, F32), jnp.zeros((rp, 2 * LANES), F32)) * len(chains)
    carry = lax.fori_loop(0, qi, lambda j, cr: block(j, cr, False), init)
    carry = block(qi, carry, True)
    acc1 = jnp.concatenate([carry[2 * c + 1] for c, (m, _) in enumerate(chains) if m == 0], axis=0)
    acc2 = jnp.concatenate([carry[2 * c + 1] for c, (m, _) in enumerate(chains) if m == 1], axis=0)

    lv = lam_ref[...]
    lam = (jnp.exp(jnp.sum(lv[0:1] * lv[1:2], axis=-1, keepdims=True))
           - jnp.exp(jnp.sum(lv[2:3] * lv[3:4], axis=-1, keepdims=True)) + lam_init)
    o = acc1[:, :LANES] / acc1[:, LANES:LANES + 1] - lam * (acc2[:, :LANES] / acc2[:, LANES:LANES + 1])
    o_ref[0] = _rms(o, sub_ref[...]) * (1.0 - lam_init)


def diff_attention(x, b, gn, w, rot, q_g, k_g, lam_vecs, subln_g, lam_init):
    m, d = x.shape
    s = m // b
    h = DIFF_HEADS
    assert w.shape[1] == 3 * h * LANES
    tm = _pick(m, (512, 256, 128))
    tile2 = lambda g: jnp.tile(g, 2).reshape(1, LANES)
    rows = lambda c: pl.BlockSpec((tm, c), lambda i: (i, 0))
    full = lambda a: pl.BlockSpec(a.shape, lambda i: (0,) * a.ndim)
    gspec2 = pl.BlockSpec((1, LANES), lambda i: (0, 0))
    qk, vext = pl.pallas_call(
        functools.partial(_diff_in_kernel, heads=h),
        grid=(m // tm,),
        in_specs=[rows(d), pl.BlockSpec((1, d), lambda i: (0, 0)), full(w), rows(LANES), gspec2, gspec2],
        out_specs=[rows(2 * h * LANES), rows(2 * h * LANES)],
        out_shape=[jax.ShapeDtypeStruct((m, 2 * h * LANES), BF16)] * 2,
        compiler_params=_cparams("parallel"),
    )(x, gn.reshape(1, d), w, rot, tile2(q_g), tile2(k_g))
    qk = qk.reshape(b, s, 2 * h * LANES)
    vext = vext.reshape(b, s, 2 * h * LANES)

    tq = _pick(s, (512, 256, 128))
    gspec = pl.BlockSpec((1, LANES), lambda bi, hi, qi: (0, 0))
    return pl.pallas_call(
        functools.partial(_diff_attn_kernel, tq=tq, lam_init=lam_init),
        grid=(b, h, s // tq),
        in_specs=[pl.BlockSpec((1, tq, LANES), lambda bi, hi, qi: (bi, qi, hi)),
                  pl.BlockSpec((1, s, LANES), lambda bi, hi, qi: (bi, 0, h + hi)),
                  pl.BlockSpec((1, s, 2 * LANES), lambda bi, hi, qi: (bi, 0, hi)),
                  pl.BlockSpec(lam_vecs.shape, lambda bi, hi, qi: (0, 0)),
                  gspec],
        out_specs=pl.BlockSpec((1, tq, LANES), lambda bi, hi, qi: (bi, qi, hi)),
        out_shape=jax.ShapeDtypeStruct((b, s, h * LANES), F32),
        compiler_params=_cparams("parallel", "parallel", "arbitrary"),
    )(qk, qk, vext, lam_vecs, subln_g.reshape(1, LANES))


def rope_table(positions):
    half = LANES // 2
    rope_dims = half // 4
    inv_freq = ROPE_THETA ** (-jnp.arange(0, rope_dims, 2, dtype=F32) / rope_dims)
    ang = positions.astype(F32).reshape(-1, 1) * inv_freq
    one_half = jnp.concatenate([jnp.cos(ang), jnp.sin(ang), jnp.zeros((ang.shape[0], half - rope_dims), F32)], axis=-1)
    return jnp.concatenate([one_half, one_half], axis=-1)


def _xattn_kernel(x_ref, gn_ref, wq_ref, kv_ref, qg_ref, kg_ref, wo_ref, o_ref, *, heads):
    x = x_ref[0]
    d = x.shape[-1] // heads
    q_all = _dot(_rms(x, gn_ref[...]).astype(BF16), wq_ref[...])
    outs = []
    for h in range(heads):
        q = _rms(q_all[:, h * d:(h + 1) * d], qg_ref[...]) * (d ** -0.5)
        k = _rms(kv_ref[0, :, h * d:(h + 1) * d], kg_ref[...])
        v = kv_ref[0, :, (heads + h) * d:(heads + h + 1) * d]
        s = _dot_nt(q.astype(BF16), k.astype(BF16))
        s = s - jnp.max(s, axis=-1, keepdims=True)
        pr = jnp.exp(s)
        o = _dot(pr.astype(BF16), v.astype(BF16)) / jnp.sum(pr, axis=-1, keepdims=True)
        outs.append(o.astype(BF16))
    o_ref[0] = x + _dot(jnp.concatenate(outs, axis=1), wo_ref[...])


def cross_attention(x, gn, wq, kv, q_g, k_g, wo):
    b, s, dm = x.shape
    mlen = kv.shape[1]
    ts = _pick(s, (512, 256, 128))
    const = lambda a: pl.BlockSpec(a.shape, lambda i, j: (0,) * a.ndim)
    row = lambda a: a.reshape(1, -1)
    return pl.pallas_call(
        functools.partial(_xattn_kernel, heads=XATTN_HEADS),
        grid=(b, s // ts),
        in_specs=[pl.BlockSpec((1, ts, dm), lambda i, j: (i, j, 0)),
                  const(row(gn)), const(wq),
                  pl.BlockSpec((1, mlen, 2 * dm), lambda i, j: (i, 0, 0)),
                  const(row(q_g)), const(row(k_g)), const(wo)],
        out_specs=pl.BlockSpec((1, ts, dm), lambda i, j: (i, j, 0)),
        out_shape=jax.ShapeDtypeStruct((b, s, dm), F32),
        compiler_params=_cparams("parallel", "arbitrary"),
    )(x, row(gn), wq, kv, row(q_g), row(k_g), wo)


def _ffn_kernel(x_ref, g_ref, wg_ref, wu_ref, wd_ref, o_ref, h_ref):
    j = pl.program_id(1)

    @pl.when(j == 0)
    def _():
        h_ref[...] = _rms(x_ref[...], g_ref[...]).astype(BF16)
        o_ref[...] = x_ref[...]

    h = h_ref[...]
    gate = _dot(h, wg_ref[...])
    up = _dot(h, wu_ref[...])
    act = gate * jax.nn.sigmoid(gate) * up
    o_ref[...] += _dot(act.astype(BF16), wd_ref[...])


def ffn_dense(x, g, w_gu, w_down):
    m, d = x.shape
    f = w_down.shape[0]
    tm = _pick(m, (512, 256, 128))
    tf = _col_tile(d, f)
    nf = f // tf
    return pl.pallas_call(
        _ffn_kernel,
        grid=(m // tm, nf),
        in_specs=[pl.BlockSpec((tm, d), lambda i, j: (i, 0)),
                  pl.BlockSpec((1, d), lambda i, j: (0, 0)),
                  pl.BlockSpec((d, tf), lambda i, j: (0, j)),
                  pl.BlockSpec((d, tf), lambda i, j: (0, j + nf)),
                  pl.BlockSpec((tf, d), lambda i, j: (j, 0))],
        out_specs=pl.BlockSpec((tm, d), lambda i, j: (i, 0)),
        out_shape=jax.ShapeDtypeStruct((m, d), F32),
        scratch_shapes=[pltpu.VMEM((tm, d), BF16)],
        compiler_params=_cparams("parallel", "arbitrary"),
    )(x, g.reshape(1, d), w_gu, w_gu, w_down)


META_I1, META_I2, META_W1, META_W2, META_R1, META_R2 = range(6)


def _lane_pick(x, lane, idx):
    return jnp.sum(jnp.where(lane == idx, x, 0.0), axis=-1, keepdims=True)


def _router_kernel(x_ref, g_ref, wr_ref, meta_ref, cnt_ref, tri_ref, *, n_experts):
    tm = x_ref.shape[0]

    @pl.when(pl.program_id(0) == 0)
    def _():
        cnt_ref[...] = jnp.zeros_like(cnt_ref)
        r_id = lax.broadcasted_iota(jnp.int32, (tm, tm), 0)
        c_id = lax.broadcasted_iota(jnp.int32, (tm, tm), 1)
        tri_ref[...] = (r_id > c_id).astype(BF16)

    h = _rms(x_ref[...], g_ref[...])
    logits = _dot(h, wr_ref[...], HIGHEST)
    lane = lax.broadcasted_iota(jnp.int32, logits.shape, 1).astype(F32)
    neg = -jnp.inf
    l1 = jnp.where(lane < n_experts, logits, neg)
    m1 = jnp.max(l1, axis=-1, keepdims=True)
    i1 = jnp.min(jnp.where(l1 == m1, lane, float(LANES)), axis=-1, keepdims=True)
    l2 = jnp.where(lane == i1, neg, l1)
    m2 = jnp.max(l2, axis=-1, keepdims=True)
    i2 = jnp.min(jnp.where(l2 == m2, lane, float(LANES)), axis=-1, keepdims=True)
    e2 = jnp.exp(m2 - m1)
    w1 = 1.0 / (1.0 + e2)
    onehot = jnp.where((lane == i1) | (lane == i2), 1.0, 0.0)
    before = _dot(tri_ref[...], onehot.astype(BF16)) + cnt_ref[0:1, :]
    cnt_ref[0:1, :] += jnp.sum(onehot, axis=0, keepdims=True)
    meta = jnp.zeros_like(logits)
    for slot, val in ((META_I1, i1), (META_I2, i2), (META_W1, w1), (META_W2, e2 * w1),
                      (META_R1, _lane_pick(before, lane, i1)), (META_R2, _lane_pick(before, lane, i2))):
        meta = jnp.where(lane == slot, val, meta)
    meta_ref[...] = meta


def router_top2(x, g, w_router):
    m, d = x.shape
    n_experts = w_router.shape[1]
    wr = jnp.zeros((d, LANES), F32).at[:, :n_experts].set(w_router)
    tm = _pick(m, (1024, 512, 256, 128))
    return pl.pallas_call(
        functools.partial(_router_kernel, n_experts=n_experts),
        grid=(m // tm,),
        in_specs=[pl.BlockSpec((tm, d), lambda i: (i, 0)),
                  pl.BlockSpec((1, d), lambda i: (0, 0)),
                  pl.BlockSpec((d, LANES), lambda i: (0, 0))],
        out_specs=[pl.BlockSpec((tm, LANES), lambda i: (i, 0)),
                   pl.BlockSpec((8, LANES), lambda i: (0, 0))],
        out_shape=[jax.ShapeDtypeStruct((m, LANES), F32), jax.ShapeDtypeStruct((8, LANES), F32)],
        scratch_shapes=[pltpu.VMEM((tm, tm), BF16)],
        compiler_params=_cparams("arbitrary"),
    )(x, g.reshape(1, d), wr)


def _idx_copy(dest_hbm, idx_smem, isem, tile, slot):
    return pltpu.make_async_copy(dest_hbm.at[tile], idx_smem.at[slot], isem.at[slot])


def _dispatch_kernel(dest_hbm, x_ref, zeros_hbm, xs_hbm, idx_smem, isem, sem, *, tm):
    del zeros_hbm
    i = pl.program_id(0)
    slot = i % 2

    @pl.when(i == 0)
    def _():
        _idx_copy(dest_hbm, idx_smem, isem, 0, 0).start()

    @pl.when(i + 1 < pl.num_programs(0))
    def _():
        _idx_copy(dest_hbm, idx_smem, isem, i + 1, 1 - slot).start()

    _idx_copy(dest_hbm, idx_smem, isem, i, slot).wait()

    def row_copy(t, k):
        d = idx_smem[slot, TOP_K * t + k]
        return pltpu.make_async_copy(x_ref.at[pl.ds(t, 1)], xs_hbm.at[pl.ds(d, 1)], sem)

    def issue(t, carry):
        for k in range(TOP_K):
            row_copy(t, k).start()
        return carry

    lax.fori_loop(0, tm, issue, 0, unroll=8)
    for _ in range(TOP_K):
        pltpu.make_async_copy(x_ref, xs_hbm.at[pl.ds(0, tm)], sem).wait()


def moe_dispatch(x, dest, n_rows):
    m, d = x.shape
    n_tiles, per_tile = dest.shape
    tm = per_tile // TOP_K
    zeros = jnp.zeros((n_rows, d), F32)
    return pl.pallas_call(
        functools.partial(_dispatch_kernel, tm=tm),
        grid=(n_tiles,),
        in_specs=[pl.BlockSpec(memory_space=pl.ANY),
                  pl.BlockSpec((tm, d), lambda i: (i, 0)),
                  pl.BlockSpec(memory_space=pl.ANY)],
        out_specs=pl.BlockSpec(memory_space=pl.ANY),
        out_shape=jax.ShapeDtypeStruct((n_rows, d), F32),
        scratch_shapes=[pltpu.SMEM((2, per_tile), jnp.int32), pltpu.SemaphoreType.DMA((2,)),
                        pltpu.SemaphoreType.DMA(())],
        input_output_aliases={2: 0},
        compiler_params=_cparams("arbitrary"),
    )(dest, x, zeros)


def _moe_ffn_kernel(te_ref, nu_ref, x_ref, g_ref, wg_ref, wu_ref, wd_ref, o_ref, h_ref):
    del te_ref
    j = pl.program_id(1)

    @pl.when(pl.program_id(0) < nu_ref[0])
    def _():
        @pl.when(j == 0)
        def _():
            h_ref[...] = _rms(x_ref[...], g_ref[...]).astype(BF16)

        h = h_ref[...]
        gate = _dot(h, wg_ref[0])
        up = _dot(h, wu_ref[0])
        act = gate * jax.nn.sigmoid(gate) * up
        y = _dot(act.astype(BF16), wd_ref[0])

        @pl.when(j == 0)
        def _():
            o_ref[...] = y

        @pl.when(j > 0)
        def _():
            o_ref[...] += y

    @pl.when((pl.program_id(0) >= nu_ref[0]) & (j == 0))
    def _():
        o_ref[...] = jnp.zeros_like(o_ref)


def moe_grouped_ffn(xs, g, tile_expert, n_used, w_gu, w_down, tm):
    p, d = xs.shape
    f = w_down.shape[1]
    tf = _col_tile(d, f)
    nf = f // tf
    last = lambda n, nu: jnp.minimum(n, nu[0] - 1)
    col = lambda n, j, nu: jnp.where(n < nu[0], j, nf - 1)
    grid_spec = pltpu.PrefetchScalarGridSpec(
        num_scalar_prefetch=2,
        grid=(p // tm, nf),
        in_specs=[pl.BlockSpec((tm, d), lambda n, j, te, nu: (last(n, nu), 0)),
                  pl.BlockSpec((1, d), lambda n, j, te, nu: (0, 0)),
                  pl.BlockSpec((1, d, tf), lambda n, j, te, nu: (te[n], 0, col(n, j, nu))),
                  pl.BlockSpec((1, d, tf), lambda n, j, te, nu: (te[n], 0, col(n, j, nu) + nf)),
                  pl.BlockSpec((1, tf, d), lambda n, j, te, nu: (te[n], col(n, j, nu), 0))],
        out_specs=pl.BlockSpec((tm, d), lambda n, j, te, nu: (n, 0)),
        scratch_shapes=[pltpu.VMEM((tm, d), BF16)],
    )
    return pl.pallas_call(
        _moe_ffn_kernel,
        grid_spec=grid_spec,
        out_shape=jax.ShapeDtypeStruct((p, d), F32),
        compiler_params=_cparams("arbitrary", "arbitrary"),
    )(tile_expert, n_used, xs, g.reshape(1, d), w_gu, w_gu, w_down)


def _combine_kernel(dest_hbm, x_ref, meta_ref, ys_hbm, o_ref, idx_smem, ybuf, isem, sem, *, tm):
    i = pl.program_id(0)
    slot = i % 2

    def row_copy(sl, t, k):
        d = idx_smem[sl, TOP_K * t + k]
        return pltpu.make_async_copy(ys_hbm.at[pl.ds(d, 1)], ybuf.at[sl, k, pl.ds(t, 1)], sem.at[sl])

    def gather_tile(tile, sl):
        _idx_copy(dest_hbm, idx_smem, isem, tile, sl).start()
        _idx_copy(dest_hbm, idx_smem, isem, tile, sl).wait()

        def issue(t, carry):
            for k in range(TOP_K):
                row_copy(sl, t, k).start()
            return carry

        lax.fori_loop(0, tm, issue, 0, unroll=8)

    @pl.when(i == 0)
    def _():
        gather_tile(0, 0)

    @pl.when(i + 1 < pl.num_programs(0))
    def _():
        gather_tile(i + 1, 1 - slot)

    for k in range(TOP_K):
        pltpu.make_async_copy(ys_hbm.at[pl.ds(0, tm)], ybuf.at[slot, k], sem.at[slot]).wait()
    meta = meta_ref[...]
    lane = lax.broadcasted_iota(jnp.int32, meta.shape, 1)
    w1 = _lane_pick(meta, lane, META_W1)
    w2 = _lane_pick(meta, lane, META_W2)
    o_ref[...] = x_ref[...] + w1 * ybuf[slot, 0] + w2 * ybuf[slot, 1]


def moe_combine(x, meta, dest, ys):
    m, d = x.shape
    n_tiles, per_tile = dest.shape
    tm = per_tile // TOP_K
    return pl.pallas_call(
        functools.partial(_combine_kernel, tm=tm),
        grid=(n_tiles,),
        in_specs=[pl.BlockSpec(memory_space=pl.ANY),
                  pl.BlockSpec((tm, d), lambda i: (i, 0)),
                  pl.BlockSpec((tm, LANES), lambda i: (i, 0)),
                  pl.BlockSpec(memory_space=pl.ANY)],
        out_specs=pl.BlockSpec((tm, d), lambda i: (i, 0)),
        out_shape=jax.ShapeDtypeStruct((m, d), F32),
        scratch_shapes=[pltpu.SMEM((2, per_tile), jnp.int32), pltpu.VMEM((2, TOP_K, tm, d), F32),
                        pltpu.SemaphoreType.DMA((2,)), pltpu.SemaphoreType.DMA((2,))],
        compiler_params=_cparams("arbitrary"),
    )(dest, x, meta, ys)


def ffn_moe(x, g, w_router, w_gu, w_down):
    m, d = x.shape
    n_experts = w_down.shape[0]
    tg = _pick(m, (MOE_GROUP_TILE, 256, 128))
    td = _pick(m, (MOE_TOKEN_TILE, 256, 128))
    meta, counts = router_top2(x, g, w_router)
    counts = counts[0, :n_experts].astype(jnp.int32)
    padded = (counts + tg - 1) // tg * tg
    ends = jnp.cumsum(padded)
    offsets = ends - padded
    n_tiles = (TOP_K * m) // tg + n_experts
    n_used = (ends[-1] // tg).astype(jnp.int32)
    tile_id = jnp.minimum(jnp.arange(n_tiles, dtype=jnp.int32), n_used - 1)
    tile_expert = jnp.sum(ends[None, :] <= (tile_id * tg)[:, None], axis=1).astype(jnp.int32)
    picks = meta[:, META_I1:META_I2 + 1].astype(jnp.int32)
    ranks = meta[:, META_R1:META_R2 + 1].astype(jnp.int32)
    dest = (offsets[picks] + ranks).reshape(m // td, TOP_K * td)
    xs = moe_dispatch(x, dest, n_tiles * tg)
    ys = moe_grouped_ffn(xs, g, tile_expert, n_used.reshape(1), w_gu, w_down, tg)
    return moe_combine(x, meta, dest, ys)


def _block_diag(w_a, w_b):
    za = jnp.zeros_like(w_a)
    zb = jnp.zeros_like(w_b)
    return jnp.concatenate([jnp.concatenate([w_a, za], axis=1), jnp.concatenate([zb, w_b], axis=1)], axis=0)


def kernel(x, mem, positions, mix_norm, in_proj, shift_mu, decay_bias, decay_up, iclr_bias, iclr_up, gate_up,
           key_kk_scale, key_iclr_scale, bonus_rk, rwkv_gn_w, rwkv_gn_b, vres_down, vres_shift_mu, vres_bias,
           vres_up, q_norm, k_norm, lambda_q1, lambda_k1, lambda_q2, lambda_k2, diff_subln, out_proj, xattn_norm,
           mem_norm, xattn_wq, xattn_wkv, xattn_wo, xattn_q_norm, xattn_k_norm, ffn_norm, dense_w_gu,
           dense_w_down, router, expert_w_gu, expert_w_down):
    b, s, d = x.shape
    depth = mix_norm.shape[0]
    width = decay_bias.shape[1]
    rwkv_cols = shift_mu.shape[1]
    in_cols = in_proj.shape[2]
    vres_rank = vres_down.shape[2] if depth > 1 else 0
    assert decay_up.shape[1] == LANES // 2 and iclr_up.shape[1] == LANES // 2 and gate_up.shape[1] == LANES
    assert rwkv_cols == 3 * width + 2 * LANES and rwkv_cols % LANES == 0 and in_cols % LANES == 0
    assert vres_rank <= LANES
    rot = rope_table(positions)
    xf = x.reshape(b * s, d)
    memf = mem.reshape(-1, d)
    v_first = None
    for l in range(depth):
        w_rwkv = in_proj[l][:, :rwkv_cols]
        vres = None
        if l > 0:
            pad = jnp.zeros((d, LANES - vres_rank), F32)
            w_rwkv = jnp.concatenate([w_rwkv, vres_down[l - 1], pad], axis=1)
            vmu = jnp.zeros((LANES,), F32).at[:vres_rank].set(vres_shift_mu[l - 1])
            vup = jnp.zeros((LANES, width), F32).at[:vres_rank].set(vres_up[l - 1]).astype(BF16)
            vres = (v_first, vmu, vres_bias[l - 1], vup)
        wwa = _block_diag(decay_up[l], iclr_up[l]).astype(BF16)
        r_, lw_, k_, v_, a_, g_ = rwkv_in_proj(xf, s, mix_norm[l], w_rwkv.astype(BF16), width, shift_mu[l],
                                               decay_bias[l], iclr_bias[l], wwa, gate_up[l].astype(BF16), vres)
        if l == 0:
            v_first = v_
        seq3 = lambda t: t.reshape(b, s, width)
        y_rwkv = rwkv_chunked(seq3(r_), seq3(lw_), seq3(k_), seq3(v_), seq3(a_), seq3(g_), key_kk_scale[l],
                              key_iclr_scale[l], bonus_rk[l].reshape(-1), rwkv_gn_w[l], rwkv_gn_b[l])
        lam_init = 0.8 - 0.6 * math.exp(-0.3 * l)
        lam_vecs = jnp.stack([lambda_q1[l], lambda_k1[l], lambda_q2[l], lambda_k2[l]])
        y_diff = diff_attention(xf, b, mix_norm[l], in_proj[l][:, rwkv_cols:].astype(BF16), rot, q_norm[l],
                                k_norm[l], lam_vecs, diff_subln[l], lam_init)
        w_out = out_proj[l].astype(BF16)
        xf = matmul_residual([y_rwkv.reshape(b * s, -1), y_diff.reshape(b * s, -1)],
                             [w_out[:width], w_out[width:]], xf)
        kv = norm_matmul(memf, mem_norm[l], xattn_wkv[l].astype(BF16)).reshape(b, -1, 2 * d)
        xf = cross_attention(xf.reshape(b, s, d), xattn_norm[l], xattn_wq[l].astype(BF16), kv, xattn_q_norm[l],
                             xattn_k_norm[l], xattn_wo[l].astype(BF16)).reshape(b * s, d)
        if l % 2 == 0:
            xf = ffn_dense(xf, ffn_norm[l], dense_w_gu[l // 2].astype(BF16), dense_w_down[l // 2].astype(BF16))
        else:
            xf = ffn_moe(xf, ffn_norm[l], router[l // 2], expert_w_gu[l // 2].astype(BF16),
                         expert_w_down[l // 2].astype(BF16))
    return xf.reshape(b, s, d)
```

```python
import functools
import math

import jax
import jax.numpy as jnp
from jax import lax
from jax.experimental import pallas as pl
from jax.experimental.pallas import tpu as pltpu

F32 = jnp.float32
BF16 = jnp.bfloat16
HIGHEST = lax.Precision.HIGHEST

EPS = 1e-6
GN_EPS = 64e-5
ROPE_THETA = 500000.0
RWKV_HEAD = 64
DIFF_HEADS = 4
XATTN_HEADS = 4
TOP_K = 2
LANES = 128
CHUNK = 128
VMEM_LIMIT = 56 * 1024 * 1024
RWKV_ROWS_PER_STEP = 2
ATTN_ROW_PARTS = 2
WEIGHT_TILE_BYTES = 4 * 1024 * 1024
MOE_GROUP_TILE = 512
MOE_TOKEN_TILE = 512


def _cparams(*sem):
    return pltpu.CompilerParams(dimension_semantics=sem, vmem_limit_bytes=VMEM_LIMIT)


def _pick(n, prefs):
    for p in prefs:
        if n % p == 0:
            return p
    return n


def _col_tile(k, n):
    best = LANES
    for t in range(LANES, n + 1, LANES):
        if n % t == 0 and k * t * 2 <= WEIGHT_TILE_BYTES:
            best = t
    return best


def _dot(a, b, prec=None):
    return jnp.dot(a, b, preferred_element_type=F32, precision=prec)


def _dot_nt(a, b, prec=None):
    return lax.dot_general(a, b, (((1,), (1,)), ((), ())), preferred_element_type=F32, precision=prec)


def _rms(x, g):
    return x * lax.rsqrt(jnp.mean(x * x, axis=-1, keepdims=True) + EPS) * g


def _norm_mm_kernel(x_ref, g_ref, w_ref, o_ref, h_ref):
    @pl.when(pl.program_id(1) == 0)
    def _():
        h_ref[...] = _rms(x_ref[...], g_ref[...]).astype(BF16)

    o_ref[...] = _dot(h_ref[...], w_ref[...])


def norm_matmul(x, g, w):
    m, k = x.shape
    n = w.shape[1]
    tm = _pick(m, (1024, 512, 256, 128))
    tn = _col_tile(k, n)
    return pl.pallas_call(
        _norm_mm_kernel,
        grid=(m // tm, n // tn),
        in_specs=[pl.BlockSpec((tm, k), lambda i, j: (i, 0)),
                  pl.BlockSpec((1, k), lambda i, j: (0, 0)),
                  pl.BlockSpec((k, tn), lambda i, j: (0, j))],
        out_specs=pl.BlockSpec((tm, tn), lambda i, j: (i, j)),
        out_shape=jax.ShapeDtypeStruct((m, n), F32),
        scratch_shapes=[pltpu.VMEM((tm, k), BF16)],
        compiler_params=_cparams("parallel", "arbitrary"),
    )(x, g.reshape(1, k), w)


def _token_shift(p, carry_ref, mu):
    rows = p.shape[0]
    prev = pltpu.roll(p, 1, axis=0)
    rid = lax.broadcasted_iota(jnp.int32, p.shape, 0)
    prev = jnp.where(rid == 0, carry_ref[...], prev)
    carry_ref[...] = p[rows - 1:rows, :]
    return p + (prev - p) * mu


def _rwkv_in_kernel(*refs, width, has_vres, tiles_per_seq):
    if has_vres:
        (x_ref, gn_ref, w_ref, vf_ref, mu_ref, w0_ref, a0_ref, wwa_ref, gup_ref, vmu_ref, vb_ref, vup_ref,
         r_ref, lw_ref, k_ref, v_ref, a_ref, g_ref, carry_ref, carry_v_ref) = refs
    else:
        (x_ref, gn_ref, w_ref, mu_ref, w0_ref, a0_ref, wwa_ref, gup_ref,
         r_ref, lw_ref, k_ref, v_ref, a_ref, g_ref, carry_ref) = refs
    ncol = 3 * width + 2 * LANES

    @pl.when(pl.program_id(0) % tiles_per_seq == 0)
    def _():
        carry_ref[...] = jnp.zeros_like(carry_ref)
        if has_vres:
            carry_v_ref[...] = jnp.zeros_like(carry_v_ref)

    proj = _dot(_rms(x_ref[...], gn_ref[...]).astype(BF16), w_ref[...])
    sh = _token_shift(proj[:, :ncol], carry_ref, mu_ref[...])
    r_ref[...] = sh[:, :width]
    k_ref[...] = sh[:, width:2 * width]
    v = sh[:, 2 * width:3 * width]
    dwa = sh[:, 3 * width:3 * width + LANES]
    lane = lax.broadcasted_iota(jnp.int32, dwa.shape, 1)
    dwa = jnp.where(lane < LANES // 2, jnp.tanh(dwa), dwa)
    wa = _dot(dwa.astype(BF16), wwa_ref[...])
    z = -(w0_ref[...] + wa[:, :width])
    softplus = jnp.maximum(z, 0.0) + jnp.log(1.0 + jnp.exp(-jnp.abs(z)))
    lw_ref[...] = -jnp.exp(-softplus - 0.5)
    a_ref[...] = jax.nn.sigmoid(a0_ref[...] + wa[:, width:])
    dg = jax.nn.sigmoid(sh[:, 3 * width + LANES:3 * width + 2 * LANES])
    g_ref[...] = _dot(dg.astype(BF16), gup_ref[...])
    if has_vres:
        shv = _token_shift(proj[:, ncol:ncol + LANES], carry_v_ref, vmu_ref[...])
        mix = jax.nn.sigmoid(vb_ref[...] + _dot(shv.astype(BF16), vup_ref[...]))
        v = v + (vf_ref[...] - v) * mix
    v_ref[...] = v


def rwkv_in_proj(x, seq_len, gn, w, width, mu, w0, a0, wwa, gup, vres=None):
    m, d = x.shape
    tm = _pick(seq_len, (512, 256, 128))
    ncol = 3 * width + 2 * LANES
    assert mu.shape[-1] == ncol and m % seq_len == 0
    row = lambda a: a.reshape(1, -1)
    full = lambda a: pl.BlockSpec(a.shape, lambda i: (0,) * a.ndim)
    rows = lambda c: pl.BlockSpec((tm, c), lambda i: (i, 0))
    params = [row(mu), row(w0), row(a0), wwa, gup]
    inputs = [x, row(gn), w]
    in_specs = [rows(d), full(row(gn)), full(w)]
    scratch = [pltpu.VMEM((1, ncol), F32)]
    if vres is not None:
        v_first, vmu, vb, vup = vres
        inputs.append(v_first)
        in_specs.append(rows(width))
        params += [row(vmu), row(vb), vup]
        scratch.append(pltpu.VMEM((1, LANES), F32))
    in_specs += [full(p) for p in params]
    out = jax.ShapeDtypeStruct((m, width), F32)
    return pl.pallas_call(
        functools.partial(_rwkv_in_kernel, width=width, has_vres=vres is not None, tiles_per_seq=seq_len // tm),
        grid=(m // tm,),
        in_specs=in_specs,
        out_specs=[rows(width)] * 6,
        out_shape=[out] * 6,
        scratch_shapes=scratch,
        compiler_params=_cparams("arbitrary"),
    )(*inputs, *params)


def _rwkv_chunk_kernel(r_ref, lw_ref, k_ref, v_ref, a_ref, g_ref, kk_ref, ka_ref, rk_ref, gw_ref, gb_ref,
                       y_ref, state_ref, *, n_pairs):
    c = CHUNK
    half = LANES // 2

    @pl.when(pl.program_id(1) == 0)
    def _():
        state_ref[...] = jnp.zeros_like(state_ref)

    row = lax.broadcasted_iota(jnp.int32, (c, c), 0)
    col = lax.broadcasted_iota(jnp.int32, (c, c), 1)
    ltri = (row >= col).astype(BF16)
    same_head = (row < half) == (col < half)
    lo = col < half
    row4 = lax.broadcasted_iota(jnp.int32, (c, 4 * c), 0)
    col4 = lax.broadcasted_iota(jnp.int32, (c, 4 * c), 1)
    incl4 = row4 >= (col4 & (c - 1))
    row2 = lax.broadcasted_iota(jnp.int32, (c, 2 * c), 0)
    col2 = lax.broadcasted_iota(jnp.int32, (c, 2 * c), 1)
    strict2 = row2 > (col2 & (c - 1))
    first_block = col2 < c

    def gsum(x):
        s_lo = jnp.sum(jnp.where(lo, x, 0.0), axis=-1, keepdims=True)
        s_hi = jnp.sum(jnp.where(lo, 0.0, x), axis=-1, keepdims=True)
        return jnp.where(lo, s_lo, s_hi)

    def by_head(x):
        return jnp.concatenate([jnp.where(lo, x, 0.0), jnp.where(lo, 0.0, x)], axis=0).astype(BF16)

    def split_bf16(x):
        hi = x.astype(BF16)
        rest = x - hi.astype(F32)
        mid = rest.astype(BF16)
        return hi, mid, (rest - mid.astype(F32)).astype(BF16)

    def stage_factors(bi, p):
        sl = slice(p * LANES, (p + 1) * LANES)
        r, lw, k, v, a = (ref[bi, :, sl] for ref in (r_ref, lw_ref, k_ref, v_ref, a_ref))
        cum3 = _dot(ltri, jnp.concatenate(split_bf16(lw), axis=1))
        cum = cum3[:, :LANES] + cum3[:, LANES:2 * LANES] + cum3[:, 2 * LANES:]
        cmid = cum[c // 2 - 1:c // 2, :]
        clast = cum[c - 1:c, :]
        ci = cum - cmid
        e_neg = jnp.exp(-ci)
        p_mid = jnp.exp(cmid)
        kk = k * kk_ref[:, sl]
        kkn = kk * lax.rsqrt(jnp.maximum(gsum(kk * kk), 1e-24))
        k2 = k * (1.0 + (a - 1.0) * ka_ref[:, sl])
        beta = kkn * a
        al_m = -kkn * jnp.exp(ci - lw)
        r_m = r * jnp.exp(ci)
        e_end = jnp.exp(clast - cum)
        ends = jnp.concatenate([beta * e_end, k2 * e_end], axis=0).astype(BF16)
        al2, r2 = by_head(al_m), by_head(r_m)
        lhs = jnp.concatenate([al2[:c], r2[:c], al2[c:], r2[c:]], axis=0)
        rhs = jnp.concatenate([beta * e_neg, k2 * e_neg], axis=0).astype(BF16)
        big = _dot_nt(lhs, rhs)
        s0 = state_ref[bi * n_pairs + p]
        base = _dot_nt(jnp.concatenate([al_m * p_mid, r_m * p_mid], axis=0).astype(BF16), s0.astype(BF16))
        return dict(big=big, base=base, v2=by_head(v), ends=ends, decayed=s0 * jnp.exp(clast),
                    bonus=gsum(r * k2 * rk_ref[:, sl]) * v)

    def stage_masks(d):
        big = d.pop("big")
        blk = lambda i, j: big[i * c:(i + 1) * c, j * c:(j + 1) * c]
        zero2 = jnp.zeros((c, 2 * c), F32)
        d["pw"] = jnp.where(strict2, jnp.concatenate([blk(0, 0), blk(2, 0)], axis=1), zero2).astype(BF16)
        a_ak = jnp.where(strict2, jnp.concatenate([blk(0, 1), blk(2, 1)], axis=1), zero2).astype(BF16)
        d["a_y"] = jnp.where(incl4, jnp.concatenate([blk(1, 0), blk(3, 0), blk(1, 1), blk(3, 1)], axis=1),
                             0.0).astype(BF16)
        d["x"] = d["base"][:c] + _dot(a_ak, d["v2"])

    def stage_square(d, last):
        pw, x2 = d["pw"], by_head(d["x"])
        if last:
            d["x"] = d["x"] + _dot(pw, x2)
            return
        zero_b = jnp.zeros_like(pw)
        bd = jnp.concatenate([jnp.where(first_block, pw, zero_b), jnp.where(first_block, zero_b, pw)], axis=0)
        res = _dot(pw, jnp.concatenate([bd, x2], axis=1))
        d["pw"] = res[:, :2 * c].astype(BF16)
        d["x"] = d["x"] + res[:, 2 * c:]

    def stage_output(bi, p, d):
        sl = slice(p * LANES, (p + 1) * LANES)
        x = d["x"]
        y = d["base"][c:] + _dot(d["a_y"], jnp.concatenate([by_head(x), d["v2"]], axis=0))
        upd = _dot(jnp.concatenate([x.T, v_ref[bi, :, sl].T], axis=1).astype(BF16), d["ends"])
        state_ref[bi * n_pairs + p] = d["decayed"] + jnp.where(same_head, upd, 0.0)
        mean = gsum(y) * (1.0 / half)
        yc = y - mean
        var = gsum(yc * yc) * (1.0 / half)
        out = yc * lax.rsqrt(var + GN_EPS) * gw_ref[:, sl] + gb_ref[:, sl]
        y_ref[bi, :, sl] = ((out + d["bonus"]) * g_ref[bi, :, sl]).astype(y_ref.dtype)

    ids = [(bi, p) for bi in range(r_ref.shape[0]) for p in range(n_pairs)]
    pairs = [stage_factors(bi, p) for bi, p in ids]
    for d in pairs:
        stage_masks(d)
    n_steps = c.bit_length() - 1
    for step in range(n_steps):
        for d in pairs:
            stage_square(d, last=step + 1 == n_steps)
    for (bi, p), d in zip(ids, pairs):
        stage_output(bi, p, d)


def rwkv_chunked(r, lw, k, v, a, g, k_k, k_a, r_k, gn_w, gn_b):
    b, s, width = r.shape
    assert RWKV_HEAD * 2 == LANES and width % LANES == 0 and s % CHUNK == 0
    n_pairs = width // LANES
    nb = _pick(b, (RWKV_ROWS_PER_STEP, 1))
    seq = pl.BlockSpec((nb, CHUNK, width), lambda i, j: (i, j, 0))
    par = pl.BlockSpec((1, width), lambda i, j: (0, 0))
    row = lambda t: t.reshape(1, width)
    return pl.pallas_call(
        functools.partial(_rwkv_chunk_kernel, n_pairs=n_pairs),
        grid=(b // nb, s // CHUNK),
        in_specs=[seq] * 6 + [par] * 5,
        out_specs=seq,
        out_shape=jax.ShapeDtypeStruct((b, s, width), BF16),
        scratch_shapes=[pltpu.VMEM((nb * n_pairs, LANES, LANES), F32)],
        compiler_params=_cparams("parallel", "arbitrary"),
    )(r, lw, k, v, a, g, row(k_k), row(k_a), row(r_k), row(gn_w), row(gn_b))


def _diff_in_kernel(x_ref, gn_ref, w_ref, rot_ref, qg_ref, kg_ref, qk_ref, v_ref, *, heads):
    half = LANES // 2
    rope_half = half // 8
    proj = _dot(_rms(x_ref[...], gn_ref[...]).astype(BF16), w_ref[...])
    lane = lax.broadcasted_iota(jnp.int32, (x_ref.shape[0], LANES), 1)
    pos = lane & (half - 1)
    rot = rot_ref[...]
    first, second = pos < rope_half, (pos >= rope_half) & (pos < 2 * rope_half)
    cf = jnp.where(first, rot, jnp.where(second, pltpu.roll(rot, rope_half, axis=1), 1.0))
    sa = jnp.where(first, -pltpu.roll(rot, LANES - rope_half, axis=1), 0.0)
    sb = jnp.where(second, rot, 0.0)

    g_row = lax.broadcasted_iota(jnp.int32, (LANES, LANES), 0)
    g_col = lax.broadcasted_iota(jnp.int32, (LANES, LANES), 1)
    same_half = ((g_row < half) == (g_col < half)).astype(BF16)

    def norm_rope(x, g):
        ms = _dot((x * x).astype(BF16), same_half) * (1.0 / half)
        x = x * lax.rsqrt(ms + EPS) * g
        return x * cf + pltpu.roll(x, LANES - rope_half, axis=1) * sa + pltpu.roll(x, rope_half, axis=1) * sb

    ones = jnp.ones((x_ref.shape[0], LANES), BF16)
    for h in range(heads):
        tile = lambda i: proj[:, (i * heads + h) * LANES:(i * heads + h + 1) * LANES]
        q = norm_rope(tile(0), qg_ref[...]) * (half ** -0.5 * math.log2(math.e))
        qk_ref[:, h * LANES:(h + 1) * LANES] = q.astype(BF16)
        qk_ref[:, (heads + h) * LANES:(heads + h + 1) * LANES] = norm_rope(tile(1), kg_ref[...]).astype(BF16)
        v_ref[:, 2 * h * LANES:(2 * h + 1) * LANES] = tile(2).astype(BF16)
        v_ref[:, (2 * h + 1) * LANES:(2 * h + 2) * LANES] = ones


def _diff_attn_kernel(q_ref, k_ref, v_ref, lam_ref, sub_ref, o_ref, *, tq, lam_init):
    half = LANES // 2
    s_len = q_ref.shape[1]
    lv = lam_ref[...]
    lam = (jnp.exp(jnp.sum(lv[0:1] * lv[1:2], axis=-1, keepdims=True))
           - jnp.exp(jnp.sum(lv[2:3] * lv[3:4], axis=-1, keepdims=True)) + lam_init)
    lane = lax.broadcasted_iota(jnp.int32, (tq, LANES), 1)
    n_parts = ATTN_ROW_PARTS
    rp = tq // n_parts
    chains = [(m, part) for part in range(n_parts) for m in range(2)]

    def block(qs, j, carry, masked):
        kb = k_ref[0, j * tq:(j + 1) * tq, :]
        vb = v_ref[0, j * tq:(j + 1) * tq, :]
        scores, probs, maxes, out = {}, {}, {}, {}
        for i in range(len(chains) + 2):
            if i < len(chains):
                m, part = chains[i]
                keys = (part + 1) * rp if masked else tq
                s = _dot_nt(qs[m][part * rp:(part + 1) * rp], kb[:keys])
                if masked:
                    r_id = lax.broadcasted_iota(jnp.int32, s.shape, 0) + part * rp
                    c_id = lax.broadcasted_iota(jnp.int32, s.shape, 1)
                    s = jnp.where(c_id <= r_id, s, -1e30)
                scores[i] = s
            if 0 <= i - 1 < len(chains):
                c = i - 1
                s = scores.pop(c)
                mx = jnp.max(s, axis=-1, keepdims=True)
                maxes[c] = mx if carry is None else jnp.maximum(carry[2 * c], mx)
                probs[c] = jnp.exp2((s - maxes[c]).astype(BF16))
            if 0 <= i - 2 < len(chains):
                c = i - 2
                pr = probs.pop(c)
                pv = _dot(pr, vb[:pr.shape[1]])
                out[c] = pv if carry is None else jnp.exp2(carry[2 * c] - maxes[c]) * carry[2 * c + 1] + pv
        return tuple(t for c in range(len(chains)) for t in (maxes[c], out[c]))

    for qi in range(s_len // tq):
        q = q_ref[0, qi * tq:(qi + 1) * tq, :]
        zero = jnp.zeros_like(q)
        qs = [jnp.where(lane < half, q, zero), jnp.where(lane < half, zero, q)]
        carry = block(qs, qi, None, True)
        for j in range(qi):
            carry = block(qs, j, carry, False)
        acc1 = jnp.concatenate([carry[2 * c + 1] for c, (m, _) in enumerate(chains) if m == 0], axis=0)
        acc2 = jnp.concatenate([carry[2 * c + 1] for c, (m, _) in enumerate(chains) if m == 1], axis=0)
        o = acc1[:, :LANES] / acc1[:, LANES:] - lam * (acc2[:, :LANES] / acc2[:, LANES:])
        o_ref[0, qi * tq:(qi + 1) * tq, :] = (_rms(o, sub_ref[...]) * (1.0 - lam_init)).astype(o_ref.dtype)


def diff_attention(x, b, gn, w, rot, q_g, k_g, lam_vecs, subln_g, lam_init):
    m, d = x.shape
    s = m // b
    h = DIFF_HEADS
    assert w.shape[1] == 3 * h * LANES
    tm = _pick(m, (512, 256, 128))
    tile2 = lambda g: jnp.tile(g, 2).reshape(1, LANES)
    rows = lambda c: pl.BlockSpec((tm, c), lambda i: (i, 0))
    full = lambda a: pl.BlockSpec(a.shape, lambda i: (0,) * a.ndim)
    gspec2 = pl.BlockSpec((1, LANES), lambda i: (0, 0))
    qk, vext = pl.pallas_call(
        functools.partial(_diff_in_kernel, heads=h),
        grid=(m // tm,),
        in_specs=[rows(d), pl.BlockSpec((1, d), lambda i: (0, 0)), full(w), rows(LANES), gspec2, gspec2],
        out_specs=[rows(2 * h * LANES), rows(2 * h * LANES)],
        out_shape=[jax.ShapeDtypeStruct((m, 2 * h * LANES), BF16)] * 2,
        compiler_params=_cparams("parallel"),
    )(x, gn.reshape(1, d), w, rot, tile2(q_g), tile2(k_g))
    qk = qk.reshape(b, s, 2 * h * LANES)
    vext = vext.reshape(b, s, 2 * h * LANES)

    tq = _pick(s, (512, 256, 128))
    gspec = pl.BlockSpec((1, LANES), lambda bi, hi: (0, 0))
    return pl.pallas_call(
        functools.partial(_diff_attn_kernel, tq=tq, lam_init=lam_init),
        grid=(b, h),
        in_specs=[pl.BlockSpec((1, s, LANES), lambda bi, hi: (bi, 0, hi)),
                  pl.BlockSpec((1, s, LANES), lambda bi, hi: (bi, 0, h + hi)),
                  pl.BlockSpec((1, s, 2 * LANES), lambda bi, hi: (bi, 0, hi)),
                  pl.BlockSpec(lam_vecs.shape, lambda bi, hi: (0, 0)),
                  gspec],
        out_specs=pl.BlockSpec((1, s, LANES), lambda bi, hi: (bi, 0, hi)),
        out_shape=jax.ShapeDtypeStruct((b, s, h * LANES), BF16),
        compiler_params=_cparams("parallel", "parallel"),
    )(qk, qk, vext, lam_vecs, subln_g.reshape(1, LANES))


def rope_table(positions):
    half = LANES // 2
    rope_dims = half // 4
    inv_freq = ROPE_THETA ** (-jnp.arange(0, rope_dims, 2, dtype=F32) / rope_dims)
    ang = positions.astype(F32).reshape(-1, 1) * inv_freq
    one_half = jnp.concatenate([jnp.cos(ang), jnp.sin(ang), jnp.zeros((ang.shape[0], half - rope_dims), F32)], axis=-1)
    return jnp.concatenate([one_half, one_half], axis=-1)


def _xattn_kernel(x_ref, ya_ref, yb_ref, wa_ref, wb_ref, gn_ref, wq_ref, kv_ref, qg_ref, kg_ref, wo_ref, o_ref,
                  *, heads):
    x = x_ref[0] + _dot(ya_ref[0], wa_ref[...]) + _dot(yb_ref[0], wb_ref[...])
    d = x.shape[-1] // heads
    q_all = _dot(_rms(x, gn_ref[...]).astype(BF16), wq_ref[...])
    outs = []
    for h in range(heads):
        q = _rms(q_all[:, h * d:(h + 1) * d], qg_ref[...]) * (d ** -0.5)
        k = _rms(kv_ref[0, :, h * d:(h + 1) * d], kg_ref[...])
        v = kv_ref[0, :, (heads + h) * d:(heads + h + 1) * d]
        s = _dot_nt(q.astype(BF16), k.astype(BF16))
        s = s - jnp.max(s, axis=-1, keepdims=True)
        pr = jnp.exp(s)
        o = _dot(pr.astype(BF16), v.astype(BF16)) / jnp.sum(pr, axis=-1, keepdims=True)
        outs.append(o.astype(BF16))
    o_ref[0] = x + _dot(jnp.concatenate(outs, axis=1), wo_ref[...])


def mixer_out_cross_attention(x, ya, yb, wa, wb, gn, wq, kv, q_g, k_g, wo):
    b, s, dm = x.shape
    mlen = kv.shape[1]
    ts = _pick(s, (512, 256, 128))
    const = lambda a: pl.BlockSpec(a.shape, lambda i, j: (0,) * a.ndim)
    seq = lambda a: pl.BlockSpec((1, ts, a.shape[2]), lambda i, j: (i, j, 0))
    row = lambda a: a.reshape(1, -1)
    return pl.pallas_call(
        functools.partial(_xattn_kernel, heads=XATTN_HEADS),
        grid=(b, s // ts),
        in_specs=[seq(x), seq(ya), seq(yb), const(wa), const(wb),
                  const(row(gn)), const(wq),
                  pl.BlockSpec((1, mlen, 2 * dm), lambda i, j: (i, 0, 0)),
                  const(row(q_g)), const(row(k_g)), const(wo)],
        out_specs=seq(x),
        out_shape=jax.ShapeDtypeStruct((b, s, dm), F32),
        compiler_params=_cparams("parallel", "arbitrary"),
    )(x, ya, yb, wa, wb, row(gn), wq, kv, row(q_g), row(k_g), wo)


def _ffn_kernel(x_ref, g_ref, wg_ref, wu_ref, wd_ref, o_ref, h_ref):
    j = pl.program_id(1)

    @pl.when(j == 0)
    def _():
        h_ref[...] = _rms(x_ref[...], g_ref[...]).astype(BF16)
        o_ref[...] = x_ref[...]

    h = h_ref[...]
    gate = _dot(h, wg_ref[...])
    up = _dot(h, wu_ref[...])
    act = gate * jax.nn.sigmoid(gate) * up
    o_ref[...] += _dot(act.astype(BF16), wd_ref[...])


def ffn_dense(x, g, w_gu, w_down):
    m, d = x.shape
    f = w_down.shape[0]
    tm = _pick(m, (512, 256, 128))
    tf = _col_tile(d, f)
    nf = f // tf
    return pl.pallas_call(
        _ffn_kernel,
        grid=(m // tm, nf),
        in_specs=[pl.BlockSpec((tm, d), lambda i, j: (i, 0)),
                  pl.BlockSpec((1, d), lambda i, j: (0, 0)),
                  pl.BlockSpec((d, tf), lambda i, j: (0, j)),
                  pl.BlockSpec((d, tf), lambda i, j: (0, j + nf)),
                  pl.BlockSpec((tf, d), lambda i, j: (j, 0))],
        out_specs=pl.BlockSpec((tm, d), lambda i, j: (i, 0)),
        out_shape=jax.ShapeDtypeStruct((m, d), F32),
        scratch_shapes=[pltpu.VMEM((tm, d), BF16)],
        compiler_params=_cparams("parallel", "arbitrary"),
    )(x, g.reshape(1, d), w_gu, w_gu, w_down)


META_I1, META_I2, META_W1, META_W2, META_R1, META_R2 = range(6)


def _lane_pick(x, lane, idx):
    return jnp.sum(jnp.where(lane == idx, x, 0.0), axis=-1, keepdims=True)


def _router_kernel(x_ref, g_ref, wr_ref, meta_ref, cnt_ref, tri_ref, *, n_experts):
    tm = x_ref.shape[0]

    @pl.when(pl.program_id(0) == 0)
    def _():
        cnt_ref[...] = jnp.zeros_like(cnt_ref)
        r_id = lax.broadcasted_iota(jnp.int32, (tm, tm), 0)
        c_id = lax.broadcasted_iota(jnp.int32, (tm, tm), 1)
        tri_ref[...] = (r_id > c_id).astype(BF16)

    h = _rms(x_ref[...], g_ref[...])
    logits = _dot(h, wr_ref[...], HIGHEST)
    lane = lax.broadcasted_iota(jnp.int32, logits.shape, 1).astype(F32)
    neg = -jnp.inf
    l1 = jnp.where(lane < n_experts, logits, neg)
    m1 = jnp.max(l1, axis=-1, keepdims=True)
    i1 = jnp.min(jnp.where(l1 == m1, lane, float(LANES)), axis=-1, keepdims=True)
    l2 = jnp.where(lane == i1, neg, l1)
    m2 = jnp.max(l2, axis=-1, keepdims=True)
    i2 = jnp.min(jnp.where(l2 == m2, lane, float(LANES)), axis=-1, keepdims=True)
    e2 = jnp.exp(m2 - m1)
    w1 = 1.0 / (1.0 + e2)
    onehot = jnp.where((lane == i1) | (lane == i2), 1.0, 0.0)
    before = _dot(tri_ref[...], onehot.astype(BF16)) + cnt_ref[0:1, :]
    cnt_ref[0:1, :] += jnp.sum(onehot, axis=0, keepdims=True)
    meta = jnp.zeros_like(logits)
    for slot, val in ((META_I1, i1), (META_I2, i2), (META_W1, w1), (META_W2, e2 * w1),
                      (META_R1, _lane_pick(before, lane, i1)), (META_R2, _lane_pick(before, lane, i2))):
        meta = jnp.where(lane == slot, val, meta)
    meta_ref[...] = meta


def router_top2(x, g, w_router):
    m, d = x.shape
    n_experts = w_router.shape[1]
    wr = jnp.zeros((d, LANES), F32).at[:, :n_experts].set(w_router)
    tm = _pick(m, (1024, 512, 256, 128))
    return pl.pallas_call(
        functools.partial(_router_kernel, n_experts=n_experts),
        grid=(m // tm,),
        in_specs=[pl.BlockSpec((tm, d), lambda i: (i, 0)),
                  pl.BlockSpec((1, d), lambda i: (0, 0)),
                  pl.BlockSpec((d, LANES), lambda i: (0, 0))],
        out_specs=[pl.BlockSpec((tm, LANES), lambda i: (i, 0)),
                   pl.BlockSpec((8, LANES), lambda i: (0, 0))],
        out_shape=[jax.ShapeDtypeStruct((m, LANES), F32), jax.ShapeDtypeStruct((8, LANES), F32)],
        scratch_shapes=[pltpu.VMEM((tm, tm), BF16)],
        compiler_params=_cparams("arbitrary"),
    )(x, g.reshape(1, d), wr)


def _idx_copy(dest_hbm, idx_smem, isem, tile, slot):
    return pltpu.make_async_copy(dest_hbm.at[tile], idx_smem.at[slot], isem.at[slot])


def _dispatch_kernel(ends_ref, dest_hbm, x_ref, xs_hbm, idx_smem, zbuf, isem, sem, zsem, *, tm, tg):
    i = pl.program_id(0)
    slot = i % 2
    n_experts = ends_ref.shape[0]
    n_out_tiles = xs_hbm.shape[0] // tg

    @pl.when(i == 0)
    def _():
        _idx_copy(dest_hbm, idx_smem, isem, 0, 0).start()
        zbuf[...] = jnp.zeros_like(zbuf)
        fills = []
        for e in range(n_experts):
            group_start = ends_ref[e - 1] if e > 0 else 0
            fills.append((ends_ref[e] > group_start, ends_ref[e] - tg))
            tail = ends_ref[n_experts - 1] + e * tg
            fills.append((tail < n_out_tiles * tg, tail))
        zero_fill = lambda start: pltpu.make_async_copy(zbuf, xs_hbm.at[pl.ds(pl.multiple_of(start, tg), tg)], zsem)
        for cond, start in fills:
            @pl.when(cond)
            def _():
                zero_fill(start).start()
        for cond, start in fills:
            @pl.when(cond)
            def _():
                zero_fill(start).wait()

    @pl.when(i + 1 < pl.num_programs(0))
    def _():
        _idx_copy(dest_hbm, idx_smem, isem, i + 1, 1 - slot).start()

    _idx_copy(dest_hbm, idx_smem, isem, i, slot).wait()

    def row_copy(t, k):
        d = idx_smem[slot, TOP_K * t + k]
        return pltpu.make_async_copy(x_ref.at[pl.ds(t, 1)], xs_hbm.at[pl.ds(d, 1)], sem)

    def issue(t, carry):
        for k in range(TOP_K):
            row_copy(t, k).start()
        return carry

    lax.fori_loop(0, tm, issue, 0, unroll=8)
    for _ in range(TOP_K):
        pltpu.make_async_copy(x_ref, xs_hbm.at[pl.ds(0, tm)], sem).wait()


def moe_dispatch(x, dest, ends, n_rows, tg):
    m, d = x.shape
    n_tiles, per_tile = dest.shape
    tm = per_tile // TOP_K
    grid_spec = pltpu.PrefetchScalarGridSpec(
        num_scalar_prefetch=1,
        grid=(n_tiles,),
        in_specs=[pl.BlockSpec(memory_space=pl.ANY),
                  pl.BlockSpec((tm, d), lambda i, ends: (i, 0))],
        out_specs=pl.BlockSpec(memory_space=pl.ANY),
        scratch_shapes=[pltpu.SMEM((2, per_tile), jnp.int32), pltpu.VMEM((tg, d), F32),
                        pltpu.SemaphoreType.DMA((2,)), pltpu.SemaphoreType.DMA(()), pltpu.SemaphoreType.DMA(())],
    )
    return pl.pallas_call(
        functools.partial(_dispatch_kernel, tm=tm, tg=tg),
        grid_spec=grid_spec,
        out_shape=jax.ShapeDtypeStruct((n_rows, d), F32),
        compiler_params=_cparams("arbitrary"),
    )(ends, dest, x)


def _moe_ffn_kernel(te_ref, nu_ref, x_ref, g_ref, wg_ref, wu_ref, wd_ref, o_ref, h_ref):
    del te_ref
    j = pl.program_id(1)

    @pl.when(pl.program_id(0) < nu_ref[0])
    def _():
        @pl.when(j == 0)
        def _():
            h_ref[...] = _rms(x_ref[...], g_ref[...]).astype(BF16)

        h = h_ref[...]
        gate = _dot(h, wg_ref[0])
        up = _dot(h, wu_ref[0])
        act = gate * jax.nn.sigmoid(gate) * up
        y = _dot(act.astype(BF16), wd_ref[0])

        @pl.when(j == 0)
        def _():
            o_ref[...] = y

        @pl.when(j > 0)
        def _():
            o_ref[...] += y

    @pl.when((pl.program_id(0) >= nu_ref[0]) & (j == 0))
    def _():
        o_ref[...] = jnp.zeros_like(o_ref)


def moe_grouped_ffn(xs, g, tile_expert, n_used, w_gu, w_down, tm):
    p, d = xs.shape
    f = w_down.shape[1]
    tf = _col_tile(d, f)
    nf = f // tf
    last = lambda n, nu: jnp.minimum(n, nu[0] - 1)
    col = lambda n, j, nu: jnp.where(n < nu[0], j, nf - 1)
    grid_spec = pltpu.PrefetchScalarGridSpec(
        num_scalar_prefetch=2,
        grid=(p // tm, nf),
        in_specs=[pl.BlockSpec((tm, d), lambda n, j, te, nu: (last(n, nu), 0)),
                  pl.BlockSpec((1, d), lambda n, j, te, nu: (0, 0)),
                  pl.BlockSpec((1, d, tf), lambda n, j, te, nu: (te[n], 0, col(n, j, nu))),
                  pl.BlockSpec((1, d, tf), lambda n, j, te, nu: (te[n], 0, col(n, j, nu) + nf)),
                  pl.BlockSpec((1, tf, d), lambda n, j, te, nu: (te[n], col(n, j, nu), 0))],
        out_specs=pl.BlockSpec((tm, d), lambda n, j, te, nu: (n, 0)),
        scratch_shapes=[pltpu.VMEM((tm, d), BF16)],
    )
    return pl.pallas_call(
        _moe_ffn_kernel,
        grid_spec=grid_spec,
        out_shape=jax.ShapeDtypeStruct((p, d), F32),
        compiler_params=_cparams("arbitrary", "arbitrary"),
    )(tile_expert, n_used, xs, g.reshape(1, d), w_gu, w_gu, w_down)


def _combine_kernel(dest_hbm, x_ref, meta_ref, ys_hbm, o_ref, idx_smem, ybuf, isem, sem, *, tm):
    i = pl.program_id(0)
    slot = i % 2

    def row_copy(sl, t, k):
        d = idx_smem[sl, TOP_K * t + k]
        return pltpu.make_async_copy(ys_hbm.at[pl.ds(d, 1)], ybuf.at[sl, k, pl.ds(t, 1)], sem.at[sl])

    def gather_tile(tile, sl):
        _idx_copy(dest_hbm, idx_smem, isem, tile, sl).start()
        _idx_copy(dest_hbm, idx_smem, isem, tile, sl).wait()

        def issue(t, carry):
            for k in range(TOP_K):
                row_copy(sl, t, k).start()
            return carry

        lax.fori_loop(0, tm, issue, 0, unroll=8)

    @pl.when(i == 0)
    def _():
        gather_tile(0, 0)

    @pl.when(i + 1 < pl.num_programs(0))
    def _():
        gather_tile(i + 1, 1 - slot)

    for k in range(TOP_K):
        pltpu.make_async_copy(ys_hbm.at[pl.ds(0, tm)], ybuf.at[slot, k], sem.at[slot]).wait()
    meta = meta_ref[...]
    lane = lax.broadcasted_iota(jnp.int32, meta.shape, 1)
    w1 = _lane_pick(meta, lane, META_W1)
    w2 = _lane_pick(meta, lane, META_W2)
    o_ref[...] = x_ref[...] + w1 * ybuf[slot, 0] + w2 * ybuf[slot, 1]


def moe_combine(x, meta, dest, ys):
    m, d = x.shape
    n_tiles, per_tile = dest.shape
    tm = per_tile // TOP_K
    return pl.pallas_call(
        functools.partial(_combine_kernel, tm=tm),
        grid=(n_tiles,),
        in_specs=[pl.BlockSpec(memory_space=pl.ANY),
                  pl.BlockSpec((tm, d), lambda i: (i, 0)),
                  pl.BlockSpec((tm, LANES), lambda i: (i, 0)),
                  pl.BlockSpec(memory_space=pl.ANY)],
        out_specs=pl.BlockSpec((tm, d), lambda i: (i, 0)),
        out_shape=jax.ShapeDtypeStruct((m, d), F32),
        scratch_shapes=[pltpu.SMEM((2, per_tile), jnp.int32), pltpu.VMEM((2, TOP_K, tm, d), F32),
                        pltpu.SemaphoreType.DMA((2,)), pltpu.SemaphoreType.DMA((2,))],
        compiler_params=_cparams("arbitrary"),
    )(dest, x, meta, ys)


def ffn_moe(x, g, w_router, w_gu, w_down):
    m, d = x.shape
    n_experts = w_down.shape[0]
    tg = _pick(m, (MOE_GROUP_TILE, 256, 128))
    td = _pick(m, (MOE_TOKEN_TILE, 256, 128))
    meta, counts = router_top2(x, g, w_router)
    counts = counts[0, :n_experts].astype(jnp.int32)
    padded = (counts + tg - 1) // tg * tg
    ends = jnp.cumsum(padded)
    offsets = ends - padded
    n_tiles = (TOP_K * m) // tg + n_experts
    n_used = (ends[-1] // tg).astype(jnp.int32)
    tile_id = jnp.minimum(jnp.arange(n_tiles, dtype=jnp.int32), n_used - 1)
    tile_expert = jnp.sum(ends[None, :] <= (tile_id * tg)[:, None], axis=1).astype(jnp.int32)
    picks = meta[:, META_I1:META_I2 + 1].astype(jnp.int32)
    ranks = meta[:, META_R1:META_R2 + 1].astype(jnp.int32)
    dest = (offsets[picks] + ranks).reshape(m // td, TOP_K * td)
    xs = moe_dispatch(x, dest, ends.astype(jnp.int32), n_tiles * tg, tg)
    ys = moe_grouped_ffn(xs, g, tile_expert, n_used.reshape(1), w_gu, w_down, tg)
    return moe_combine(x, meta, dest, ys)


def _block_diag(w_a, w_b):
    za = jnp.zeros_like(w_a)
    zb = jnp.zeros_like(w_b)
    return jnp.concatenate([jnp.concatenate([w_a, za], axis=1), jnp.concatenate([zb, w_b], axis=1)], axis=0)


def kernel(x, mem, positions, mix_norm, in_proj, shift_mu, decay_bias, decay_up, iclr_bias, iclr_up, gate_up,
           key_kk_scale, key_iclr_scale, bonus_rk, rwkv_gn_w, rwkv_gn_b, vres_down, vres_shift_mu, vres_bias,
           vres_up, q_norm, k_norm, lambda_q1, lambda_k1, lambda_q2, lambda_k2, diff_subln, out_proj, xattn_norm,
           mem_norm, xattn_wq, xattn_wkv, xattn_wo, xattn_q_norm, xattn_k_norm, ffn_norm, dense_w_gu,
           dense_w_down, router, expert_w_gu, expert_w_down):
    b, s, d = x.shape
    depth = mix_norm.shape[0]
    width = decay_bias.shape[1]
    rwkv_cols = shift_mu.shape[1]
    in_cols = in_proj.shape[2]
    vres_rank = vres_down.shape[2] if depth > 1 else 0
    assert decay_up.shape[1] == LANES // 2 and iclr_up.shape[1] == LANES // 2 and gate_up.shape[1] == LANES
    assert rwkv_cols == 3 * width + 2 * LANES and rwkv_cols % LANES == 0 and in_cols % LANES == 0
    assert vres_rank <= LANES
    rot = rope_table(positions)
    xf = x.reshape(b * s, d)
    memf = mem.reshape(-1, d)
    v_first = None
    for l in range(depth):
        w_rwkv = in_proj[l][:, :rwkv_cols]
        vres = None
        if l > 0:
            pad = jnp.zeros((d, LANES - vres_rank), F32)
            w_rwkv = jnp.concatenate([w_rwkv, vres_down[l - 1], pad], axis=1)
            vmu = jnp.zeros((LANES,), F32).at[:vres_rank].set(vres_shift_mu[l - 1])
            vup = jnp.zeros((LANES, width), F32).at[:vres_rank].set(vres_up[l - 1]).astype(BF16)
            vres = (v_first, vmu, vres_bias[l - 1], vup)
        wwa = _block_diag(decay_up[l], iclr_up[l]).astype(BF16)
        r_, lw_, k_, v_, a_, g_ = rwkv_in_proj(xf, s, mix_norm[l], w_rwkv.astype(BF16), width, shift_mu[l],
                                               decay_bias[l], iclr_bias[l], wwa, gate_up[l].astype(BF16), vres)
        if l == 0:
            v_first = v_
        seq3 = lambda t: t.reshape(b, s, width)
        y_rwkv = rwkv_chunked(seq3(r_), seq3(lw_), seq3(k_), seq3(v_), seq3(a_), seq3(g_), key_kk_scale[l],
                              key_iclr_scale[l], bonus_rk[l].reshape(-1), rwkv_gn_w[l], rwkv_gn_b[l])
        lam_init = 0.8 - 0.6 * math.exp(-0.3 * l)
        lam_vecs = jnp.stack([lambda_q1[l], lambda_k1[l], lambda_q2[l], lambda_k2[l]])
        y_diff = diff_attention(xf, b, mix_norm[l], in_proj[l][:, rwkv_cols:].astype(BF16), rot, q_norm[l],
                                k_norm[l], lam_vecs, diff_subln[l], lam_init)
        w_out = out_proj[l].astype(BF16)
        kv = norm_matmul(memf, mem_norm[l], xattn_wkv[l].astype(BF16)).reshape(b, -1, 2 * d)
        xf = mixer_out_cross_attention(xf.reshape(b, s, d), y_rwkv, y_diff, w_out[:width], w_out[width:],
                                       xattn_norm[l], xattn_wq[l].astype(BF16), kv, xattn_q_norm[l],
                                       xattn_k_norm[l], xattn_wo[l].astype(BF16)).reshape(b * s, d)
        if l % 2 == 0:
            xf = ffn_dense(xf, ffn_norm[l], dense_w_gu[l // 2].astype(BF16), dense_w_down[l // 2].astype(BF16))
        else:
            xf = ffn_moe(xf, ffn_norm[l], router[l // 2], expert_w_gu[l // 2].astype(BF16),
                         expert_w_down[l // 2].astype(BF16))
    return xf.reshape(b, s, d)
```

```python
import functools
import math

import jax
import jax.numpy as jnp
from jax import lax
from jax.experimental import pallas as pl
from jax.experimental.pallas import tpu as pltpu

F32 = jnp.float32
BF16 = jnp.bfloat16

EPS = 1e-6
GN_EPS = 64e-5
ROPE_THETA = 500000.0
RWKV_HEAD = 64
DIFF_HEADS = 4
XATTN_HEADS = 4
TOP_K = 2
LANES = 128
CHUNK = 128
VMEM_LIMIT = 56 * 1024 * 1024
RWKV_ROWS_PER_STEP = 2
ATTN_ROW_PARTS = 2
WEIGHT_TILE_BYTES = 4 * 1024 * 1024
MOE_GROUP_TILE = 512
MOE_TOKEN_TILE = 512


def _cparams(*sem):
    return pltpu.CompilerParams(dimension_semantics=sem, vmem_limit_bytes=VMEM_LIMIT)


def _pick(n, prefs):
    for p in prefs:
        if n % p == 0:
            return p
    return n


def _col_tile(k, n):
    best = LANES
    for t in range(LANES, n + 1, LANES):
        if n % t == 0 and k * t * 2 <= WEIGHT_TILE_BYTES:
            best = t
    return best


def _dot(a, b, prec=None):
    return jnp.dot(a, b, preferred_element_type=F32, precision=prec)


def _dot_nt(a, b, prec=None):
    return lax.dot_general(a, b, (((1,), (1,)), ((), ())), preferred_element_type=F32, precision=prec)


def _rms(x, g):
    return x * lax.rsqrt(jnp.mean(x * x, axis=-1, keepdims=True) + EPS) * g


def _norm_mm_kernel(x_ref, g_ref, w_ref, o_ref, h_ref):
    @pl.when(pl.program_id(1) == 0)
    def _():
        h_ref[...] = _rms(x_ref[...], g_ref[...]).astype(BF16)

    o_ref[...] = _dot(h_ref[...], w_ref[...])


def norm_matmul(x, g, w):
    m, k = x.shape
    n = w.shape[1]
    tm = _pick(m, (1024, 512, 256, 128))
    tn = _col_tile(k, n)
    return pl.pallas_call(
        _norm_mm_kernel,
        grid=(m // tm, n // tn),
        in_specs=[pl.BlockSpec((tm, k), lambda i, j: (i, 0)),
                  pl.BlockSpec((1, k), lambda i, j: (0, 0)),
                  pl.BlockSpec((k, tn), lambda i, j: (0, j))],
        out_specs=pl.BlockSpec((tm, tn), lambda i, j: (i, j)),
        out_shape=jax.ShapeDtypeStruct((m, n), F32),
        scratch_shapes=[pltpu.VMEM((tm, k), BF16)],
        compiler_params=_cparams("parallel", "arbitrary"),
    )(x, g.reshape(1, k), w)


def _token_shift(p, carry_ref, mu):
    rows = p.shape[0]
    prev = pltpu.roll(p, 1, axis=0)
    rid = lax.broadcasted_iota(jnp.int32, p.shape, 0)
    prev = jnp.where(rid == 0, carry_ref[...], prev)
    carry_ref[...] = p[rows - 1:rows, :]
    return p + (prev - p) * mu


def _rwkv_in_kernel(*refs, width, has_vres, tiles_per_seq):
    if has_vres:
        (x_ref, gn_ref, w_ref, vf_ref, mu_ref, w0_ref, a0_ref, wwa_ref, gup_ref, vmu_ref, vb_ref, vup_ref,
         r_ref, lw_ref, k_ref, v_ref, a_ref, g_ref, carry_ref, carry_v_ref) = refs
    else:
        (x_ref, gn_ref, w_ref, mu_ref, w0_ref, a0_ref, wwa_ref, gup_ref,
         r_ref, lw_ref, k_ref, v_ref, a_ref, g_ref, carry_ref) = refs
    ncol = 3 * width + 2 * LANES

    @pl.when(pl.program_id(0) % tiles_per_seq == 0)
    def _():
        carry_ref[...] = jnp.zeros_like(carry_ref)
        if has_vres:
            carry_v_ref[...] = jnp.zeros_like(carry_v_ref)

    proj = _dot(_rms(x_ref[...], gn_ref[...]).astype(BF16), w_ref[...])
    sh = _token_shift(proj[:, :ncol], carry_ref, mu_ref[...])
    r_ref[...] = sh[:, :width].astype(r_ref.dtype)
    k_ref[...] = sh[:, width:2 * width].astype(k_ref.dtype)
    v = sh[:, 2 * width:3 * width]
    dwa = sh[:, 3 * width:3 * width + LANES]
    lane = lax.broadcasted_iota(jnp.int32, dwa.shape, 1)
    dwa = jnp.where(lane < LANES // 2, jnp.tanh(dwa), dwa)
    wa = _dot(dwa.astype(BF16), wwa_ref[...])
    z = -(w0_ref[...] + wa[:, :width])
    softplus = jnp.maximum(z, 0.0) + jnp.log(1.0 + jnp.exp(-jnp.abs(z)))
    lw_ref[...] = -jnp.exp(-softplus - 0.5)
    a_ref[...] = jax.nn.sigmoid(a0_ref[...] + wa[:, width:]).astype(a_ref.dtype)
    dg = jax.nn.sigmoid(sh[:, 3 * width + LANES:3 * width + 2 * LANES])
    g_ref[...] = _dot(dg.astype(BF16), gup_ref[...]).astype(g_ref.dtype)
    if has_vres:
        shv = _token_shift(proj[:, ncol:ncol + LANES], carry_v_ref, vmu_ref[...])
        mix = jax.nn.sigmoid(vb_ref[...] + _dot(shv.astype(BF16), vup_ref[...]))
        v = v + (vf_ref[...].astype(F32) - v) * mix
    v_ref[...] = v.astype(v_ref.dtype)


def rwkv_in_proj(x, seq_len, gn, w, width, mu, w0, a0, wwa, gup, vres=None):
    m, d = x.shape
    tm = _pick(seq_len, (512, 256, 128))
    ncol = 3 * width + 2 * LANES
    assert mu.shape[-1] == ncol and m % seq_len == 0
    row = lambda a: a.reshape(1, -1)
    full = lambda a: pl.BlockSpec(a.shape, lambda i: (0,) * a.ndim)
    rows = lambda c: pl.BlockSpec((tm, c), lambda i: (i, 0))
    params = [row(mu), row(w0), row(a0), wwa, gup]
    inputs = [x, row(gn), w]
    in_specs = [rows(d), full(row(gn)), full(w)]
    scratch = [pltpu.VMEM((1, ncol), F32)]
    if vres is not None:
        v_first, vmu, vb, vup = vres
        inputs.append(v_first)
        in_specs.append(rows(width))
        params += [row(vmu), row(vb), vup]
        scratch.append(pltpu.VMEM((1, LANES), F32))
    in_specs += [full(p) for p in params]
    out = lambda dt: jax.ShapeDtypeStruct((m, width), dt)
    return pl.pallas_call(
        functools.partial(_rwkv_in_kernel, width=width, has_vres=vres is not None, tiles_per_seq=seq_len // tm),
        grid=(m // tm,),
        in_specs=in_specs,
        out_specs=[rows(width)] * 6,
        out_shape=[out(BF16), out(F32), out(BF16), out(BF16), out(BF16), out(BF16)],
        scratch_shapes=scratch,
        compiler_params=_cparams("arbitrary"),
    )(*inputs, *params)


def _rwkv_chunk_kernel(r_ref, lw_ref, k_ref, v_ref, a_ref, g_ref, kk_ref, ka_ref, rk_ref, gw_ref, gb_ref,
                       y_ref, state_ref, *, n_pairs):
    c = CHUNK
    half = LANES // 2

    @pl.when(pl.program_id(1) == 0)
    def _():
        state_ref[...] = jnp.zeros_like(state_ref)

    row = lax.broadcasted_iota(jnp.int32, (c, c), 0)
    col = lax.broadcasted_iota(jnp.int32, (c, c), 1)
    ltri = (row >= col).astype(BF16)
    same_head = (row < half) == (col < half)
    lo = col < half
    row4 = lax.broadcasted_iota(jnp.int32, (c, 4 * c), 0)
    col4 = lax.broadcasted_iota(jnp.int32, (c, 4 * c), 1)
    incl4 = row4 >= (col4 & (c - 1))
    row2 = lax.broadcasted_iota(jnp.int32, (c, 2 * c), 0)
    col2 = lax.broadcasted_iota(jnp.int32, (c, 2 * c), 1)
    strict2 = row2 > (col2 & (c - 1))
    first_block = col2 < c

    def gsum(x):
        s_lo = jnp.sum(jnp.where(lo, x, 0.0), axis=-1, keepdims=True)
        s_hi = jnp.sum(jnp.where(lo, 0.0, x), axis=-1, keepdims=True)
        return jnp.where(lo, s_lo, s_hi)

    def by_head(x):
        return jnp.concatenate([jnp.where(lo, x, 0.0), jnp.where(lo, 0.0, x)], axis=0).astype(BF16)

    def split_bf16(x):
        hi = x.astype(BF16)
        rest = x - hi.astype(F32)
        mid = rest.astype(BF16)
        return hi, mid, (rest - mid.astype(F32)).astype(BF16)

    def stage_factors(bi, p):
        sl = slice(p * LANES, (p + 1) * LANES)
        r, lw, k, v, a = (ref[bi, :, sl].astype(F32) for ref in (r_ref, lw_ref, k_ref, v_ref, a_ref))
        cum3 = _dot(ltri, jnp.concatenate(split_bf16(lw), axis=1))
        cum = cum3[:, :LANES] + cum3[:, LANES:2 * LANES] + cum3[:, 2 * LANES:]
        cmid = cum[c // 2 - 1:c // 2, :]
        clast = cum[c - 1:c, :]
        ci = cum - cmid
        e_neg = jnp.exp(-ci)
        p_mid = jnp.exp(cmid)
        kk = k * kk_ref[:, sl]
        kkn = kk * lax.rsqrt(jnp.maximum(gsum(kk * kk), 1e-24))
        k2 = k * (1.0 + (a - 1.0) * ka_ref[:, sl])
        beta = kkn * a
        al_m = -kkn * jnp.exp(ci - lw)
        r_m = r * jnp.exp(ci)
        e_end = jnp.exp(clast - cum)
        ends = jnp.concatenate([beta * e_end, k2 * e_end], axis=0).astype(BF16)
        al2, r2 = by_head(al_m), by_head(r_m)
        lhs = jnp.concatenate([al2[:c], r2[:c], al2[c:], r2[c:]], axis=0)
        rhs = jnp.concatenate([beta * e_neg, k2 * e_neg], axis=0).astype(BF16)
        big = _dot_nt(lhs, rhs)
        s0 = state_ref[bi * n_pairs + p]
        base = _dot_nt(jnp.concatenate([al_m * p_mid, r_m * p_mid], axis=0).astype(BF16), s0.astype(BF16))
        return dict(big=big, base=base, v2=by_head(v), ends=ends, decayed=s0 * jnp.exp(clast),
                    bonus=gsum(r * k2 * rk_ref[:, sl]) * v)

    def stage_masks(d):
        big = d.pop("big")
        blk = lambda i, j: big[i * c:(i + 1) * c, j * c:(j + 1) * c]
        zero2 = jnp.zeros((c, 2 * c), F32)
        d["pw"] = jnp.where(strict2, jnp.concatenate([blk(0, 0), blk(2, 0)], axis=1), zero2).astype(BF16)
        a_ak = jnp.where(strict2, jnp.concatenate([blk(0, 1), blk(2, 1)], axis=1), zero2).astype(BF16)
        d["a_y"] = jnp.where(incl4, jnp.concatenate([blk(1, 0), blk(3, 0), blk(1, 1), blk(3, 1)], axis=1),
                             0.0).astype(BF16)
        d["x"] = d["base"][:c] + _dot(a_ak, d["v2"])

    def stage_square(d, last):
        pw, x2 = d["pw"], by_head(d["x"])
        if last:
            d["x"] = d["x"] + _dot(pw, x2)
            return
        zero_b = jnp.zeros_like(pw)
        bd = jnp.concatenate([jnp.where(first_block, pw, zero_b), jnp.where(first_block, zero_b, pw)], axis=0)
        res = _dot(pw, jnp.concatenate([bd, x2], axis=1))
        d["pw"] = res[:, :2 * c].astype(BF16)
        d["x"] = d["x"] + res[:, 2 * c:]

    def stage_output(bi, p, d):
        sl = slice(p * LANES, (p + 1) * LANES)
        x = d["x"]
        y = d["base"][c:] + _dot(d["a_y"], jnp.concatenate([by_head(x), d["v2"]], axis=0))
        upd = _dot(jnp.concatenate([x.T, v_ref[bi, :, sl].astype(F32).T], axis=1).astype(BF16), d["ends"])
        state_ref[bi * n_pairs + p] = d["decayed"] + jnp.where(same_head, upd, 0.0)
        mean = gsum(y) * (1.0 / half)
        yc = y - mean
        var = gsum(yc * yc) * (1.0 / half)
        out = yc * lax.rsqrt(var + GN_EPS) * gw_ref[:, sl] + gb_ref[:, sl]
        y_ref[bi, :, sl] = ((out + d["bonus"]) * g_ref[bi, :, sl].astype(F32)).astype(y_ref.dtype)

    ids = [(bi, p) for bi in range(r_ref.shape[0]) for p in range(n_pairs)]
    pairs = [stage_factors(bi, p) for bi, p in ids]
    for d in pairs:
        stage_masks(d)
    n_steps = c.bit_length() - 1
    for step in range(n_steps):
        for d in pairs:
            stage_square(d, last=step + 1 == n_steps)
    for (bi, p), d in zip(ids, pairs):
        stage_output(bi, p, d)


def rwkv_chunked(r, lw, k, v, a, g, k_k, k_a, r_k, gn_w, gn_b):
    b, s, width = r.shape
    assert RWKV_HEAD * 2 == LANES and width % LANES == 0 and s % CHUNK == 0
    n_pairs = width // LANES
    nb = _pick(b, (RWKV_ROWS_PER_STEP, 1))
    seq = pl.BlockSpec((nb, CHUNK, width), lambda i, j: (i, j, 0))
    par = pl.BlockSpec((1, width), lambda i, j: (0, 0))
    row = lambda t: t.reshape(1, width)
    return pl.pallas_call(
        functools.partial(_rwkv_chunk_kernel, n_pairs=n_pairs),
        grid=(b // nb, s // CHUNK),
        in_specs=[seq] * 6 + [par] * 5,
        out_specs=seq,
        out_shape=jax.ShapeDtypeStruct((b, s, width), BF16),
        scratch_shapes=[pltpu.VMEM((nb * n_pairs, LANES, LANES), F32)],
        compiler_params=_cparams("parallel", "arbitrary"),
    )(r, lw, k, v, a, g, row(k_k), row(k_a), row(r_k), row(gn_w), row(gn_b))


def _diff_in_kernel(x_ref, gn_ref, w_ref, rot_ref, qg_ref, kg_ref, qk_ref, v_ref, *, heads):
    half = LANES // 2
    rope_half = half // 8
    proj = _dot(_rms(x_ref[...], gn_ref[...]).astype(BF16), w_ref[...])
    lane = lax.broadcasted_iota(jnp.int32, (x_ref.shape[0], LANES), 1)
    pos = lane & (half - 1)
    rot = rot_ref[...]
    first, second = pos < rope_half, (pos >= rope_half) & (pos < 2 * rope_half)
    cf = jnp.where(first, rot, jnp.where(second, pltpu.roll(rot, rope_half, axis=1), 1.0))
    sa = jnp.where(first, -pltpu.roll(rot, LANES - rope_half, axis=1), 0.0)
    sb = jnp.where(second, rot, 0.0)

    g_row = lax.broadcasted_iota(jnp.int32, (LANES, LANES), 0)
    g_col = lax.broadcasted_iota(jnp.int32, (LANES, LANES), 1)
    same_half = ((g_row < half) == (g_col < half)).astype(BF16)

    def norm_rope(x, g):
        ms = _dot((x * x).astype(BF16), same_half) * (1.0 / half)
        x = x * lax.rsqrt(ms + EPS) * g
        return x * cf + pltpu.roll(x, LANES - rope_half, axis=1) * sa + pltpu.roll(x, rope_half, axis=1) * sb

    ones = jnp.ones((x_ref.shape[0], LANES), BF16)
    for h in range(heads):
        tile = lambda i: proj[:, (i * heads + h) * LANES:(i * heads + h + 1) * LANES]
        q = norm_rope(tile(0), qg_ref[...]) * (half ** -0.5 * math.log2(math.e))
        qk_ref[:, h * LANES:(h + 1) * LANES] = q.astype(BF16)
        qk_ref[:, (heads + h) * LANES:(heads + h + 1) * LANES] = norm_rope(tile(1), kg_ref[...]).astype(BF16)
        v_ref[:, 2 * h * LANES:(2 * h + 1) * LANES] = tile(2).astype(BF16)
        v_ref[:, (2 * h + 1) * LANES:(2 * h + 2) * LANES] = ones


def _diff_attn_kernel(q_ref, k_ref, v_ref, lam_ref, sub_ref, o_ref, *, tq, lam_init):
    half = LANES // 2
    s_len = q_ref.shape[1]
    lv = lam_ref[...]
    lam = (jnp.exp(jnp.sum(lv[0:1] * lv[1:2], axis=-1, keepdims=True))
           - jnp.exp(jnp.sum(lv[2:3] * lv[3:4], axis=-1, keepdims=True)) + lam_init)
    lane = lax.broadcasted_iota(jnp.int32, (tq, LANES), 1)
    n_parts = ATTN_ROW_PARTS
    rp = tq // n_parts
    chains = [(m, part) for part in range(n_parts) for m in range(2)]

    def block(qs, j, carry, masked):
        kb = k_ref[0, j * tq:(j + 1) * tq, :]
        vb = v_ref[0, j * tq:(j + 1) * tq, :]
        scores, probs, maxes, out = {}, {}, {}, {}
        for i in range(len(chains) + 2):
            if i < len(chains):
                m, part = chains[i]
                keys = (part + 1) * rp if masked else tq
                s = _dot_nt(qs[m][part * rp:(part + 1) * rp], kb[:keys])
                if masked:
                    r_id = lax.broadcasted_iota(jnp.int32, s.shape, 0) + part * rp
                    c_id = lax.broadcasted_iota(jnp.int32, s.shape, 1)
                    s = jnp.where(c_id <= r_id, s, -1e30)
                scores[i] = s
            if 0 <= i - 1 < len(chains):
                c = i - 1
                s = scores.pop(c)
                mx = jnp.max(s, axis=-1, keepdims=True)
                maxes[c] = mx if carry is None else jnp.maximum(carry[2 * c], mx)
                probs[c] = jnp.exp2((s - maxes[c]).astype(BF16))
            if 0 <= i - 2 < len(chains):
                c = i - 2
                pr = probs.pop(c)
                pv = _dot(pr, vb[:pr.shape[1]])
                out[c] = pv if carry is None else jnp.exp2(carry[2 * c] - maxes[c]) * carry[2 * c + 1] + pv
        return tuple(t for c in range(len(chains)) for t in (maxes[c], out[c]))

    for qi in range(s_len // tq):
        q = q_ref[0, qi * tq:(qi + 1) * tq, :]
        zero = jnp.zeros_like(q)
        qs = [jnp.where(lane < half, q, zero), jnp.where(lane < half, zero, q)]
        carry = block(qs, qi, None, True)
        for j in range(qi):
            carry = block(qs, j, carry, False)
        acc1 = jnp.concatenate([carry[2 * c + 1] for c, (m, _) in enumerate(chains) if m == 0], axis=0)
        acc2 = jnp.concatenate([carry[2 * c + 1] for c, (m, _) in enumerate(chains) if m == 1], axis=0)
        o = acc1[:, :LANES] / acc1[:, LANES:] - lam * (acc2[:, :LANES] / acc2[:, LANES:])
        o_ref[0, qi * tq:(qi + 1) * tq, :] = (_rms(o, sub_ref[...]) * (1.0 - lam_init)).astype(o_ref.dtype)


def diff_attention(x, b, gn, w, rot, q_g, k_g, lam_vecs, subln_g, lam_init):
    m, d = x.shape
    s = m // b
    h = DIFF_HEADS
    assert w.shape[1] == 3 * h * LANES
    tm = _pick(m, (512, 256, 128))
    tile2 = lambda g: jnp.tile(g, 2).reshape(1, LANES)
    rows = lambda c: pl.BlockSpec((tm, c), lambda i: (i, 0))
    full = lambda a: pl.BlockSpec(a.shape, lambda i: (0,) * a.ndim)
    gspec2 = pl.BlockSpec((1, LANES), lambda i: (0, 0))
    qk, vext = pl.pallas_call(
        functools.partial(_diff_in_kernel, heads=h),
        grid=(m // tm,),
        in_specs=[rows(d), pl.BlockSpec((1, d), lambda i: (0, 0)), full(w), rows(LANES), gspec2, gspec2],
        out_specs=[rows(2 * h * LANES), rows(2 * h * LANES)],
        out_shape=[jax.ShapeDtypeStruct((m, 2 * h * LANES), BF16)] * 2,
        compiler_params=_cparams("parallel"),
    )(x, gn.reshape(1, d), w, rot, tile2(q_g), tile2(k_g))
    qk = qk.reshape(b, s, 2 * h * LANES)
    vext = vext.reshape(b, s, 2 * h * LANES)

    tq = _pick(s, (512, 256, 128))
    gspec = pl.BlockSpec((1, LANES), lambda bi, hi: (0, 0))
    return pl.pallas_call(
        functools.partial(_diff_attn_kernel, tq=tq, lam_init=lam_init),
        grid=(b, h),
        in_specs=[pl.BlockSpec((1, s, LANES), lambda bi, hi: (bi, 0, hi)),
                  pl.BlockSpec((1, s, LANES), lambda bi, hi: (bi, 0, h + hi)),
                  pl.BlockSpec((1, s, 2 * LANES), lambda bi, hi: (bi, 0, hi)),
                  pl.BlockSpec(lam_vecs.shape, lambda bi, hi: (0, 0)),
                  gspec],
        out_specs=pl.BlockSpec((1, s, LANES), lambda bi, hi: (bi, 0, hi)),
        out_shape=jax.ShapeDtypeStruct((b, s, h * LANES), BF16),
        compiler_params=_cparams("parallel", "parallel"),
    )(qk, qk, vext, lam_vecs, subln_g.reshape(1, LANES))


def rope_table(positions):
    half = LANES // 2
    rope_dims = half // 4
    inv_freq = ROPE_THETA ** (-jnp.arange(0, rope_dims, 2, dtype=F32) / rope_dims)
    ang = positions.astype(F32).reshape(-1, 1) * inv_freq
    one_half = jnp.concatenate([jnp.cos(ang), jnp.sin(ang), jnp.zeros((ang.shape[0], half - rope_dims), F32)], axis=-1)
    return jnp.concatenate([one_half, one_half], axis=-1)


def _xattn_kernel(x_ref, ya_ref, yb_ref, wa_ref, wb_ref, gn_ref, wq_ref, kv_ref, qg_ref, kg_ref, wo_ref, o_ref,
                  *, heads):
    x = x_ref[0] + _dot(ya_ref[0], wa_ref[...]) + _dot(yb_ref[0], wb_ref[...])
    d = x.shape[-1] // heads
    q_all = _dot(_rms(x, gn_ref[...]).astype(BF16), wq_ref[...])
    outs = []
    for h in range(heads):
        q = _rms(q_all[:, h * d:(h + 1) * d], qg_ref[...]) * (d ** -0.5)
        k = _rms(kv_ref[0, :, h * d:(h + 1) * d], kg_ref[...])
        v = kv_ref[0, :, (heads + h) * d:(heads + h + 1) * d]
        s = _dot_nt(q.astype(BF16), k.astype(BF16))
        s = s - jnp.max(s, axis=-1, keepdims=True)
        pr = jnp.exp(s)
        o = _dot(pr.astype(BF16), v.astype(BF16)) / jnp.sum(pr, axis=-1, keepdims=True)
        outs.append(o.astype(BF16))
    o_ref[0] = x + _dot(jnp.concatenate(outs, axis=1), wo_ref[...])


def mixer_out_cross_attention(x, ya, yb, wa, wb, gn, wq, kv, q_g, k_g, wo):
    b, s, dm = x.shape
    mlen = kv.shape[1]
    ts = _pick(s, (512, 256, 128))
    const = lambda a: pl.BlockSpec(a.shape, lambda i, j: (0,) * a.ndim)
    seq = lambda a: pl.BlockSpec((1, ts, a.shape[2]), lambda i, j: (i, j, 0))
    row = lambda a: a.reshape(1, -1)
    return pl.pallas_call(
        functools.partial(_xattn_kernel, heads=XATTN_HEADS),
        grid=(b, s // ts),
        in_specs=[seq(x), seq(ya), seq(yb), const(wa), const(wb),
                  const(row(gn)), const(wq),
                  pl.BlockSpec((1, mlen, 2 * dm), lambda i, j: (i, 0, 0)),
                  const(row(q_g)), const(row(k_g)), const(wo)],
        out_specs=seq(x),
        out_shape=jax.ShapeDtypeStruct((b, s, dm), F32),
        compiler_params=_cparams("parallel", "arbitrary"),
    )(x, ya, yb, wa, wb, row(gn), wq, kv, row(q_g), row(k_g), wo)


def _ffn_kernel(x_ref, g_ref, wg_ref, wu_ref, wd_ref, o_ref, h_ref):
    j = pl.program_id(1)

    @pl.when(j == 0)
    def _():
        h_ref[...] = _rms(x_ref[...], g_ref[...]).astype(BF16)
        o_ref[...] = x_ref[...]

    h = h_ref[...]
    gate = _dot(h, wg_ref[...])
    up = _dot(h, wu_ref[...])
    act = gate * jax.nn.sigmoid(gate) * up
    o_ref[...] += _dot(act.astype(BF16), wd_ref[...])


def ffn_dense(x, g, w_gu, w_down):
    m, d = x.shape
    f = w_down.shape[0]
    tm = _pick(m, (1024, 512, 256, 128))
    tf = _col_tile(d, f)
    nf = f // tf
    return pl.pallas_call(
        _ffn_kernel,
        grid=(m // tm, nf),
        in_specs=[pl.BlockSpec((tm, d), lambda i, j: (i, 0)),
                  pl.BlockSpec((1, d), lambda i, j: (0, 0)),
                  pl.BlockSpec((d, tf), lambda i, j: (0, j)),
                  pl.BlockSpec((d, tf), lambda i, j: (0, j + nf)),
                  pl.BlockSpec((tf, d), lambda i, j: (j, 0))],
        out_specs=pl.BlockSpec((tm, d), lambda i, j: (i, 0)),
        out_shape=jax.ShapeDtypeStruct((m, d), F32),
        scratch_shapes=[pltpu.VMEM((tm, d), BF16)],
        compiler_params=_cparams("parallel", "arbitrary"),
    )(x, g.reshape(1, d), w_gu, w_gu, w_down)


META_I1, META_I2, META_W1, META_W2, META_R1, META_R2 = range(6)


def _lane_pick(x, lane, idx):
    return jnp.sum(jnp.where(lane == idx, x, 0.0), axis=-1, keepdims=True)


def _router_kernel(x_ref, g_ref, wr_ref, meta_ref, cnt_ref, tri_ref, *, n_experts):
    tm = x_ref.shape[0]

    @pl.when(pl.program_id(0) == 0)
    def _():
        cnt_ref[...] = jnp.zeros_like(cnt_ref)
        r_id = lax.broadcasted_iota(jnp.int32, (tm, tm), 0)
        c_id = lax.broadcasted_iota(jnp.int32, (tm, tm), 1)
        tri_ref[...] = (r_id > c_id).astype(BF16)

    split = lambda t: (t.astype(BF16), (t - t.astype(BF16).astype(F32)).astype(BF16))
    h_hi, h_lo = split(_rms(x_ref[...], g_ref[...]))
    w_hi, w_lo = split(wr_ref[...])
    logits = _dot(h_hi, w_hi) + (_dot(h_hi, w_lo) + _dot(h_lo, w_hi))
    lane = lax.broadcasted_iota(jnp.int32, logits.shape, 1).astype(F32)
    neg = -jnp.inf
    l1 = jnp.where(lane < n_experts, logits, neg)
    m1 = jnp.max(l1, axis=-1, keepdims=True)
    i1 = jnp.min(jnp.where(l1 == m1, lane, float(LANES)), axis=-1, keepdims=True)
    l2 = jnp.where(lane == i1, neg, l1)
    m2 = jnp.max(l2, axis=-1, keepdims=True)
    i2 = jnp.min(jnp.where(l2 == m2, lane, float(LANES)), axis=-1, keepdims=True)
    e2 = jnp.exp(m2 - m1)
    w1 = 1.0 / (1.0 + e2)
    onehot = jnp.where((lane == i1) | (lane == i2), 1.0, 0.0)
    before = _dot(tri_ref[...], onehot.astype(BF16)) + cnt_ref[0:1, :]
    cnt_ref[0:1, :] += jnp.sum(onehot, axis=0, keepdims=True)
    meta = jnp.zeros_like(logits)
    for slot, val in ((META_I1, i1), (META_I2, i2), (META_W1, w1), (META_W2, e2 * w1),
                      (META_R1, _lane_pick(before, lane, i1)), (META_R2, _lane_pick(before, lane, i2))):
        meta = jnp.where(lane == slot, val, meta)
    meta_ref[...] = meta


def router_top2(x, g, w_router):
    m, d = x.shape
    n_experts = w_router.shape[1]
    wr = jnp.zeros((d, LANES), F32).at[:, :n_experts].set(w_router)
    tm = _pick(m, (1024, 512, 256, 128))
    return pl.pallas_call(
        functools.partial(_router_kernel, n_experts=n_experts),
        grid=(m // tm,),
        in_specs=[pl.BlockSpec((tm, d), lambda i: (i, 0)),
                  pl.BlockSpec((1, d), lambda i: (0, 0)),
                  pl.BlockSpec((d, LANES), lambda i: (0, 0))],
        out_specs=[pl.BlockSpec((tm, LANES), lambda i: (i, 0)),
                   pl.BlockSpec((8, LANES), lambda i: (0, 0))],
        out_shape=[jax.ShapeDtypeStruct((m, LANES), F32), jax.ShapeDtypeStruct((8, LANES), F32)],
        scratch_shapes=[pltpu.VMEM((tm, tm), BF16)],
        compiler_params=_cparams("arbitrary"),
    )(x, g.reshape(1, d), wr)


def _idx_copy(dest_hbm, idx_smem, isem, tile, slot):
    return pltpu.make_async_copy(dest_hbm.at[tile], idx_smem.at[slot], isem.at[slot])


def _dispatch_kernel(ends_ref, dest_hbm, x_ref, xs_hbm, idx_smem, zbuf, isem, sem, zsem, *, tm, tg):
    i = pl.program_id(0)
    slot = i % 2
    n_experts = ends_ref.shape[0]
    n_out_tiles = xs_hbm.shape[0] // tg

    @pl.when(i == 0)
    def _():
        _idx_copy(dest_hbm, idx_smem, isem, 0, 0).start()
        zbuf[...] = jnp.zeros_like(zbuf)
        fills = []
        for e in range(n_experts):
            group_start = ends_ref[e - 1] if e > 0 else 0
            fills.append((ends_ref[e] > group_start, ends_ref[e] - tg))
            tail = ends_ref[n_experts - 1] + e * tg
            fills.append((tail < n_out_tiles * tg, tail))
        zero_fill = lambda start: pltpu.make_async_copy(zbuf, xs_hbm.at[pl.ds(pl.multiple_of(start, tg), tg)], zsem)
        for cond, start in fills:
            @pl.when(cond)
            def _():
                zero_fill(start).start()
        for cond, start in fills:
            @pl.when(cond)
            def _():
                zero_fill(start).wait()

    @pl.when(i + 1 < pl.num_programs(0))
    def _():
        _idx_copy(dest_hbm, idx_smem, isem, i + 1, 1 - slot).start()

    _idx_copy(dest_hbm, idx_smem, isem, i, slot).wait()

    def row_copy(t, k):
        d = idx_smem[slot, TOP_K * t + k]
        return pltpu.make_async_copy(x_ref.at[pl.ds(t, 1)], xs_hbm.at[pl.ds(d, 1)], sem)

    def issue(t, carry):
        for k in range(TOP_K):
            row_copy(t, k).start()
        return carry

    lax.fori_loop(0, tm, issue, 0, unroll=8)
    for _ in range(TOP_K):
        pltpu.make_async_copy(x_ref, xs_hbm.at[pl.ds(0, tm)], sem).wait()


def moe_dispatch(x, dest, ends, n_rows, tg):
    m, d = x.shape
    n_tiles, per_tile = dest.shape
    tm = per_tile // TOP_K
    grid_spec = pltpu.PrefetchScalarGridSpec(
        num_scalar_prefetch=1,
        grid=(n_tiles,),
        in_specs=[pl.BlockSpec(memory_space=pl.ANY),
                  pl.BlockSpec((tm, d), lambda i, ends: (i, 0))],
        out_specs=pl.BlockSpec(memory_space=pl.ANY),
        scratch_shapes=[pltpu.SMEM((2, per_tile), jnp.int32), pltpu.VMEM((tg, d), F32),
                        pltpu.SemaphoreType.DMA((2,)), pltpu.SemaphoreType.DMA(()), pltpu.SemaphoreType.DMA(())],
    )
    return pl.pallas_call(
        functools.partial(_dispatch_kernel, tm=tm, tg=tg),
        grid_spec=grid_spec,
        out_shape=jax.ShapeDtypeStruct((n_rows, d), F32),
        compiler_params=_cparams("arbitrary"),
    )(ends, dest, x)


def _moe_ffn_kernel(te_ref, nu_ref, x_ref, g_ref, wg_ref, wu_ref, wd_ref, o_ref, h_ref):
    del te_ref
    j = pl.program_id(1)

    @pl.when(pl.program_id(0) < nu_ref[0])
    def _():
        @pl.when(j == 0)
        def _():
            h_ref[...] = _rms(x_ref[...], g_ref[...]).astype(BF16)

        h = h_ref[...]
        gate = _dot(h, wg_ref[0])
        up = _dot(h, wu_ref[0])
        act = gate * jax.nn.sigmoid(gate) * up
        y = _dot(act.astype(BF16), wd_ref[0])

        @pl.when(j == 0)
        def _():
            o_ref[...] = y

        @pl.when(j > 0)
        def _():
            o_ref[...] += y

    @pl.when((pl.program_id(0) >= nu_ref[0]) & (j == 0))
    def _():
        o_ref[...] = jnp.zeros_like(o_ref)


def moe_grouped_ffn(xs, g, tile_expert, n_used, w_gu, w_down, tm):
    p, d = xs.shape
    f = w_down.shape[1]
    tf = _col_tile(d, f)
    nf = f // tf
    last = lambda n, nu: jnp.minimum(n, nu[0] - 1)
    col = lambda n, j, nu: jnp.where(n < nu[0], j, nf - 1)
    grid_spec = pltpu.PrefetchScalarGridSpec(
        num_scalar_prefetch=2,
        grid=(p // tm, nf),
        in_specs=[pl.BlockSpec((tm, d), lambda n, j, te, nu: (last(n, nu), 0)),
                  pl.BlockSpec((1, d), lambda n, j, te, nu: (0, 0)),
                  pl.BlockSpec((1, d, tf), lambda n, j, te, nu: (te[n], 0, col(n, j, nu))),
                  pl.BlockSpec((1, d, tf), lambda n, j, te, nu: (te[n], 0, col(n, j, nu) + nf)),
                  pl.BlockSpec((1, tf, d), lambda n, j, te, nu: (te[n], col(n, j, nu), 0))],
        out_specs=pl.BlockSpec((tm, d), lambda n, j, te, nu: (n, 0)),
        scratch_shapes=[pltpu.VMEM((tm, d), BF16)],
    )
    return pl.pallas_call(
        _moe_ffn_kernel,
        grid_spec=grid_spec,
        out_shape=jax.ShapeDtypeStruct((p, d), F32),
        compiler_params=_cparams("arbitrary", "arbitrary"),
    )(tile_expert, n_used, xs, g.reshape(1, d), w_gu, w_gu, w_down)


def _combine_kernel(dest_hbm, x_ref, meta_ref, ys_hbm, o_ref, idx_smem, ybuf, isem, sem, *, tm, n_tiles):
    i = pl.program_id(0)
    slot = i % 2

    def row_copy(sl, t, k):
        d = idx_smem[sl, TOP_K * t + k]
        return pltpu.make_async_copy(ys_hbm.at[pl.ds(d, 1)], ybuf.at[sl, k, pl.ds(t, 1)], sem.at[sl])

    def gather_tile(tile, sl):
        _idx_copy(dest_hbm, idx_smem, isem, tile, sl).wait()

        def issue(t, carry):
            for k in range(TOP_K):
                row_copy(sl, t, k).start()
            return carry

        lax.fori_loop(0, tm, issue, 0, unroll=8)

    @pl.when(i == 0)
    def _():
        _idx_copy(dest_hbm, idx_smem, isem, 0, 0).start()
        gather_tile(0, 0)
        if n_tiles > 1:
            _idx_copy(dest_hbm, idx_smem, isem, 1, 1).start()

    @pl.when(i + 1 < pl.num_programs(0))
    def _():
        gather_tile(i + 1, 1 - slot)

    @pl.when(i + 2 < pl.num_programs(0))
    def _():
        _idx_copy(dest_hbm, idx_smem, isem, i + 2, slot).start()

    for k in range(TOP_K):
        pltpu.make_async_copy(ys_hbm.at[pl.ds(0, tm)], ybuf.at[slot, k], sem.at[slot]).wait()
    meta = meta_ref[...]
    lane = lax.broadcasted_iota(jnp.int32, meta.shape, 1)
    w1 = _lane_pick(meta, lane, META_W1)
    w2 = _lane_pick(meta, lane, META_W2)
    o_ref[...] = x_ref[...] + w1 * ybuf[slot, 0] + w2 * ybuf[slot, 1]


def moe_combine(x, meta, dest, ys):
    m, d = x.shape
    n_tiles, per_tile = dest.shape
    tm = per_tile // TOP_K
    return pl.pallas_call(
        functools.partial(_combine_kernel, tm=tm, n_tiles=n_tiles),
        grid=(n_tiles,),
        in_specs=[pl.BlockSpec(memory_space=pl.ANY),
                  pl.BlockSpec((tm, d), lambda i: (i, 0)),
                  pl.BlockSpec((tm, LANES), lambda i: (i, 0)),
                  pl.BlockSpec(memory_space=pl.ANY)],
        out_specs=pl.BlockSpec((tm, d), lambda i: (i, 0)),
        out_shape=jax.ShapeDtypeStruct((m, d), F32),
        scratch_shapes=[pltpu.SMEM((2, per_tile), jnp.int32), pltpu.VMEM((2, TOP_K, tm, d), F32),
                        pltpu.SemaphoreType.DMA((2,)), pltpu.SemaphoreType.DMA((2,))],
        compiler_params=_cparams("arbitrary"),
    )(dest, x, meta, ys)


def ffn_moe(x, g, w_router, w_gu, w_down):
    m, d = x.shape
    n_experts = w_down.shape[0]
    tg = _pick(m, (MOE_GROUP_TILE, 256, 128))
    td = _pick(m, (MOE_TOKEN_TILE, 256, 128))
    meta, counts = router_top2(x, g, w_router)
    counts = counts[0, :n_experts].astype(jnp.int32)
    padded = (counts + tg - 1) // tg * tg
    ends = jnp.cumsum(padded)
    offsets = ends - padded
    n_tiles = (TOP_K * m) // tg + n_experts
    n_used = (ends[-1] // tg).astype(jnp.int32)
    tile_id = jnp.minimum(jnp.arange(n_tiles, dtype=jnp.int32), n_used - 1)
    tile_expert = jnp.sum(ends[None, :] <= (tile_id * tg)[:, None], axis=1).astype(jnp.int32)
    picks = meta[:, META_I1:META_I2 + 1].astype(jnp.int32)
    ranks = meta[:, META_R1:META_R2 + 1].astype(jnp.int32)
    dest = (offsets[picks] + ranks).reshape(m // td, TOP_K * td)
    xs = moe_dispatch(x, dest, ends.astype(jnp.int32), n_tiles * tg, tg)
    ys = moe_grouped_ffn(xs, g, tile_expert, n_used.reshape(1), w_gu, w_down, tg)
    return moe_combine(x, meta, dest, ys)


def _block_diag(w_a, w_b):
    za = jnp.zeros_like(w_a)
    zb = jnp.zeros_like(w_b)
    return jnp.concatenate([jnp.concatenate([w_a, za], axis=1), jnp.concatenate([zb, w_b], axis=1)], axis=0)


def kernel(x, mem, positions, mix_norm, in_proj, shift_mu, decay_bias, decay_up, iclr_bias, iclr_up, gate_up,
           key_kk_scale, key_iclr_scale, bonus_rk, rwkv_gn_w, rwkv_gn_b, vres_down, vres_shift_mu, vres_bias,
           vres_up, q_norm, k_norm, lambda_q1, lambda_k1, lambda_q2, lambda_k2, diff_subln, out_proj, xattn_norm,
           mem_norm, xattn_wq, xattn_wkv, xattn_wo, xattn_q_norm, xattn_k_norm, ffn_norm, dense_w_gu,
           dense_w_down, router, expert_w_gu, expert_w_down):
    b, s, d = x.shape
    depth = mix_norm.shape[0]
    width = decay_bias.shape[1]
    rwkv_cols = shift_mu.shape[1]
    in_cols = in_proj.shape[2]
    vres_rank = vres_down.shape[2] if depth > 1 else 0
    assert decay_up.shape[1] == LANES // 2 and iclr_up.shape[1] == LANES // 2 and gate_up.shape[1] == LANES
    assert rwkv_cols == 3 * width + 2 * LANES and rwkv_cols % LANES == 0 and in_cols % LANES == 0
    assert vres_rank <= LANES
    rot = rope_table(positions)
    xf = x.reshape(b * s, d)
    memf = mem.reshape(-1, d)
    v_first = None
    for l in range(depth):
        w_rwkv = in_proj[l][:, :rwkv_cols]
        vres = None
        if l > 0:
            pad = jnp.zeros((d, LANES - vres_rank), F32)
            w_rwkv = jnp.concatenate([w_rwkv, vres_down[l - 1], pad], axis=1)
            vmu = jnp.zeros((LANES,), F32).at[:vres_rank].set(vres_shift_mu[l - 1])
            vup = jnp.zeros((LANES, width), F32).at[:vres_rank].set(vres_up[l - 1]).astype(BF16)
            vres = (v_first, vmu, vres_bias[l - 1], vup)
        wwa = _block_diag(decay_up[l], iclr_up[l]).astype(BF16)
        r_, lw_, k_, v_, a_, g_ = rwkv_in_proj(xf, s, mix_norm[l], w_rwkv.astype(BF16), width, shift_mu[l],
                                               decay_bias[l], iclr_bias[l], wwa, gate_up[l].astype(BF16), vres)
        if l == 0:
            v_first = v_
        seq3 = lambda t: t.reshape(b, s, width)
        y_rwkv = rwkv_chunked(seq3(r_), seq3(lw_), seq3(k_), seq3(v_), seq3(a_), seq3(g_), key_kk_scale[l],
                              key_iclr_scale[l], bonus_rk[l].reshape(-1), rwkv_gn_w[l], rwkv_gn_b[l])
        lam_init = 0.8 - 0.6 * math.exp(-0.3 * l)
        lam_vecs = jnp.stack([lambda_q1[l], lambda_k1[l], lambda_q2[l], lambda_k2[l]])
        y_diff = diff_attention(xf, b, mix_norm[l], in_proj[l][:, rwkv_cols:].astype(BF16), rot, q_norm[l],
                                k_norm[l], lam_vecs, diff_subln[l], lam_init)
        w_out = out_proj[l].astype(BF16)
        kv = norm_matmul(memf, mem_norm[l], xattn_wkv[l].astype(BF16)).reshape(b, -1, 2 * d)
        xf = mixer_out_cross_attention(xf.reshape(b, s, d), y_rwkv, y_diff, w_out[:width], w_out[width:],
                                       xattn_norm[l], xattn_wq[l].astype(BF16), kv, xattn_q_norm[l],
                                       xattn_k_norm[l], xattn_wo[l].astype(BF16)).reshape(b * s, d)
        if l % 2 == 0:
            xf = ffn_dense(xf, ffn_norm[l], dense_w_gu[l // 2].astype(BF16), dense_w_down[l // 2].astype(BF16))
        else:
            xf = ffn_moe(xf, ffn_norm[l], router[l // 2], expert_w_gu[l // 2].astype(BF16),
                         expert_w_down[l // 2].astype(BF16))
    return xf.reshape(b, s, d)
```

```python
import functools
import math

import jax
import jax.numpy as jnp
from jax import lax
from jax.experimental import pallas as pl
from jax.experimental.pallas import tpu as pltpu

F32 = jnp.float32
BF16 = jnp.bfloat16

EPS = 1e-6
GN_EPS = 64e-5
ROPE_THETA = 500000.0
RWKV_HEAD = 64
DIFF_HEADS = 4
XATTN_HEADS = 4
TOP_K = 2
LANES = 128
CHUNK = 128
VMEM_LIMIT = 56 * 1024 * 1024
RWKV_ROWS_PER_STEP = 2
RWKV_IN_ROW_PARTS = 2
ATTN_ROW_PARTS = 2
WEIGHT_TILE_BYTES = 4 * 1024 * 1024
MOE_GROUP_TILE = 512
MOE_TOKEN_TILE = 512


def _cparams(*sem):
    return pltpu.CompilerParams(dimension_semantics=sem, vmem_limit_bytes=VMEM_LIMIT)


def _pick(n, prefs):
    for p in prefs:
        if n % p == 0:
            return p
    return n


def _col_tile(k, n):
    best = LANES
    for t in range(LANES, n + 1, LANES):
        if n % t == 0 and k * t * 2 <= WEIGHT_TILE_BYTES:
            best = t
    return best


def _dot(a, b, prec=None):
    return jnp.dot(a, b, preferred_element_type=F32, precision=prec)


def _dot_nt(a, b, prec=None):
    return lax.dot_general(a, b, (((1,), (1,)), ((), ())), preferred_element_type=F32, precision=prec)


def _rms(x, g):
    return x * lax.rsqrt(jnp.mean(x * x, axis=-1, keepdims=True) + EPS) * g


def _norm_mm_kernel(x_ref, g_ref, w_ref, o_ref, h_ref):
    @pl.when(pl.program_id(1) == 0)
    def _():
        h_ref[...] = _rms(x_ref[...], g_ref[...]).astype(BF16)

    o_ref[...] = _dot(h_ref[...], w_ref[...])


def norm_matmul(x, g, w):
    m, k = x.shape
    n = w.shape[1]
    tm = _pick(m, (1024, 512, 256, 128))
    tn = _col_tile(k, n)
    return pl.pallas_call(
        _norm_mm_kernel,
        grid=(m // tm, n // tn),
        in_specs=[pl.BlockSpec((tm, k), lambda i, j: (i, 0)),
                  pl.BlockSpec((1, k), lambda i, j: (0, 0)),
                  pl.BlockSpec((k, tn), lambda i, j: (0, j))],
        out_specs=pl.BlockSpec((tm, tn), lambda i, j: (i, j)),
        out_shape=jax.ShapeDtypeStruct((m, n), F32),
        scratch_shapes=[pltpu.VMEM((tm, k), BF16)],
        compiler_params=_cparams("parallel", "arbitrary"),
    )(x, g.reshape(1, k), w)


def _token_shift(p, prev_row, mu):
    prev = pltpu.roll(p, 1, axis=0)
    rid = lax.broadcasted_iota(jnp.int32, p.shape, 0)
    prev = jnp.where(rid == 0, prev_row, prev)
    return p + (prev - p) * mu


def _rwkv_in_kernel(*refs, width, has_vres, tiles_per_seq):
    if has_vres:
        (x_ref, gn_ref, w_ref, vf_ref, mu_ref, w0_ref, a0_ref, wwa_ref, gup_ref, vmu_ref, vb_ref, vup_ref,
         r_ref, lw_ref, k_ref, v_ref, a_ref, g_ref, carry_ref, carry_v_ref) = refs
    else:
        (x_ref, gn_ref, w_ref, mu_ref, w0_ref, a0_ref, wwa_ref, gup_ref,
         r_ref, lw_ref, k_ref, v_ref, a_ref, g_ref, carry_ref) = refs
    ncol = 3 * width + 2 * LANES
    tm = x_ref.shape[0]
    n_parts = RWKV_IN_ROW_PARTS
    rp = tm // n_parts

    @pl.when(pl.program_id(0) % tiles_per_seq == 0)
    def _():
        carry_ref[...] = jnp.zeros_like(carry_ref)
        if has_vres:
            carry_v_ref[...] = jnp.zeros_like(carry_v_ref)

    projs = [_dot(_rms(x_ref[i * rp:(i + 1) * rp, :], gn_ref[...]).astype(BF16), w_ref[...]) for i in range(n_parts)]
    prev = carry_ref[...]
    prev_v = carry_v_ref[...] if has_vres else None
    for i, proj in enumerate(projs):
        rows = slice(i * rp, (i + 1) * rp)
        sh = _token_shift(proj[:, :ncol], prev, mu_ref[...])
        prev = proj[rp - 1:rp, :ncol]
        r_ref[rows, :] = sh[:, :width].astype(r_ref.dtype)
        k_ref[rows, :] = sh[:, width:2 * width].astype(k_ref.dtype)
        v = sh[:, 2 * width:3 * width]
        dwa = sh[:, 3 * width:3 * width + LANES]
        lane = lax.broadcasted_iota(jnp.int32, dwa.shape, 1)
        dwa = jnp.where(lane < LANES // 2, jnp.tanh(dwa), dwa)
        wa = _dot(dwa.astype(BF16), wwa_ref[...])
        z = -(w0_ref[...] + wa[:, :width])
        softplus = jnp.maximum(z, 0.0) + jnp.log(1.0 + jnp.exp(-jnp.abs(z)))
        lw_ref[rows, :] = -jnp.exp(-softplus - 0.5)
        a_ref[rows, :] = jax.nn.sigmoid(a0_ref[...] + wa[:, width:]).astype(a_ref.dtype)
        dg = jax.nn.sigmoid(sh[:, 3 * width + LANES:3 * width + 2 * LANES])
        g_ref[rows, :] = _dot(dg.astype(BF16), gup_ref[...]).astype(g_ref.dtype)
        if has_vres:
            shv = _token_shift(proj[:, ncol:ncol + LANES], prev_v, vmu_ref[...])
            prev_v = proj[rp - 1:rp, ncol:ncol + LANES]
            mix = jax.nn.sigmoid(vb_ref[...] + _dot(shv.astype(BF16), vup_ref[...]))
            v = v + (vf_ref[rows, :].astype(F32) - v) * mix
        v_ref[rows, :] = v.astype(v_ref.dtype)
    carry_ref[...] = prev
    if has_vres:
        carry_v_ref[...] = prev_v


def rwkv_in_proj(x, seq_len, gn, w, width, mu, w0, a0, wwa, gup, vres=None):
    m, d = x.shape
    tm = _pick(seq_len, (512, 256, 128))
    ncol = 3 * width + 2 * LANES
    assert mu.shape[-1] == ncol and m % seq_len == 0
    row = lambda a: a.reshape(1, -1)
    full = lambda a: pl.BlockSpec(a.shape, lambda i: (0,) * a.ndim)
    rows = lambda c: pl.BlockSpec((tm, c), lambda i: (i, 0))
    params = [row(mu), row(w0), row(a0), wwa, gup]
    inputs = [x, row(gn), w]
    in_specs = [rows(d), full(row(gn)), full(w)]
    scratch = [pltpu.VMEM((1, ncol), F32)]
    if vres is not None:
        v_first, vmu, vb, vup = vres
        inputs.append(v_first)
        in_specs.append(rows(width))
        params += [row(vmu), row(vb), vup]
        scratch.append(pltpu.VMEM((1, LANES), F32))
    in_specs += [full(p) for p in params]
    out = lambda dt: jax.ShapeDtypeStruct((m, width), dt)
    return pl.pallas_call(
        functools.partial(_rwkv_in_kernel, width=width, has_vres=vres is not None, tiles_per_seq=seq_len // tm),
        grid=(m // tm,),
        in_specs=in_specs,
        out_specs=[rows(width)] * 6,
        out_shape=[out(BF16), out(F32), out(BF16), out(BF16), out(BF16), out(BF16)],
        scratch_shapes=scratch,
        compiler_params=_cparams("arbitrary"),
    )(*inputs, *params)


def _rwkv_chunk_kernel(r_ref, lw_ref, k_ref, v_ref, a_ref, g_ref, kk_ref, ka_ref, rk_ref, gw_ref, gb_ref,
                       y_ref, state_ref, *, n_pairs):
    c = CHUNK
    half = LANES // 2

    @pl.when(pl.program_id(1) == 0)
    def _():
        state_ref[...] = jnp.zeros_like(state_ref)

    row = lax.broadcasted_iota(jnp.int32, (c, c), 0)
    col = lax.broadcasted_iota(jnp.int32, (c, c), 1)
    ltri = (row >= col).astype(BF16)
    same_head = (row < half) == (col < half)
    lo = col < half
    row4 = lax.broadcasted_iota(jnp.int32, (c, 4 * c), 0)
    col4 = lax.broadcasted_iota(jnp.int32, (c, 4 * c), 1)
    incl4 = row4 >= (col4 & (c - 1))
    row2 = lax.broadcasted_iota(jnp.int32, (c, 2 * c), 0)
    col2 = lax.broadcasted_iota(jnp.int32, (c, 2 * c), 1)
    strict2 = row2 > (col2 & (c - 1))
    first_block = col2 < c

    def gsum(x):
        s_lo = jnp.sum(jnp.where(lo, x, 0.0), axis=-1, keepdims=True)
        s_hi = jnp.sum(jnp.where(lo, 0.0, x), axis=-1, keepdims=True)
        return jnp.where(lo, s_lo, s_hi)

    def by_head(x):
        return jnp.concatenate([jnp.where(lo, x, 0.0), jnp.where(lo, 0.0, x)], axis=0).astype(BF16)

    def split_bf16(x):
        hi = x.astype(BF16)
        rest = x - hi.astype(F32)
        mid = rest.astype(BF16)
        return hi, mid, (rest - mid.astype(F32)).astype(BF16)

    def stage_factors(bi, p):
        sl = slice(p * LANES, (p + 1) * LANES)
        r, lw, k, v, a = (ref[bi, :, sl].astype(F32) for ref in (r_ref, lw_ref, k_ref, v_ref, a_ref))
        cum3 = _dot(ltri, jnp.concatenate(split_bf16(lw), axis=1))
        cum = cum3[:, :LANES] + cum3[:, LANES:2 * LANES] + cum3[:, 2 * LANES:]
        cmid = cum[c // 2 - 1:c // 2, :]
        clast = cum[c - 1:c, :]
        ci = cum - cmid
        e_neg = jnp.exp(-ci)
        p_mid = jnp.exp(cmid)
        kk = k * kk_ref[:, sl]
        kkn = kk * lax.rsqrt(jnp.maximum(gsum(kk * kk), 1e-24))
        k2 = k * (1.0 + (a - 1.0) * ka_ref[:, sl])
        beta = kkn * a
        al_m = -kkn * jnp.exp(ci - lw)
        r_m = r * jnp.exp(ci)
        e_end = jnp.exp(clast - cum)
        ends = jnp.concatenate([beta * e_end, k2 * e_end], axis=0).astype(BF16)
        al2, r2 = by_head(al_m), by_head(r_m)
        lhs = jnp.concatenate([al2[:c], r2[:c], al2[c:], r2[c:]], axis=0)
        rhs = jnp.concatenate([beta * e_neg, k2 * e_neg], axis=0).astype(BF16)
        big = _dot_nt(lhs, rhs)
        s0 = state_ref[bi * n_pairs + p]
        base = _dot_nt(jnp.concatenate([al_m * p_mid, r_m * p_mid], axis=0).astype(BF16), s0.astype(BF16))
        return dict(big=big, base=base, v2=by_head(v), ends=ends, decayed=s0 * jnp.exp(clast),
                    bonus=gsum(r * k2 * rk_ref[:, sl]) * v)

    def stage_masks(d):
        big = d.pop("big")
        blk = lambda i, j: big[i * c:(i + 1) * c, j * c:(j + 1) * c]
        zero2 = jnp.zeros((c, 2 * c), F32)
        d["pw"] = jnp.where(strict2, jnp.concatenate([blk(0, 0), blk(2, 0)], axis=1), zero2).astype(BF16)
        a_ak = jnp.where(strict2, jnp.concatenate([blk(0, 1), blk(2, 1)], axis=1), zero2).astype(BF16)
        d["a_y"] = jnp.where(incl4, jnp.concatenate([blk(1, 0), blk(3, 0), blk(1, 1), blk(3, 1)], axis=1),
                             0.0).astype(BF16)
        d["x"] = d["base"][:c] + _dot(a_ak, d["v2"])

    def stage_square(d, last):
        pw, x2 = d["pw"], by_head(d["x"])
        if last:
            d["x"] = d["x"] + _dot(pw, x2)
            return
        zero_b = jnp.zeros_like(pw)
        bd = jnp.concatenate([jnp.where(first_block, pw, zero_b), jnp.where(first_block, zero_b, pw)], axis=0)
        res = _dot(pw, jnp.concatenate([bd, x2], axis=1))
        d["pw"] = res[:, :2 * c].astype(BF16)
        d["x"] = d["x"] + res[:, 2 * c:]

    def stage_output(bi, p, d):
        sl = slice(p * LANES, (p + 1) * LANES)
        x = d["x"]
        y = d["base"][c:] + _dot(d["a_y"], jnp.concatenate([by_head(x), d["v2"]], axis=0))
        upd = _dot(jnp.concatenate([x.T, v_ref[bi, :, sl].astype(F32).T], axis=1).astype(BF16), d["ends"])
        state_ref[bi * n_pairs + p] = d["decayed"] + jnp.where(same_head, upd, 0.0)
        mean = gsum(y) * (1.0 / half)
        yc = y - mean
        var = gsum(yc * yc) * (1.0 / half)
        out = yc * lax.rsqrt(var + GN_EPS) * gw_ref[:, sl] + gb_ref[:, sl]
        y_ref[bi, :, sl] = ((out + d["bonus"]) * g_ref[bi, :, sl].astype(F32)).astype(y_ref.dtype)

    ids = [(bi, p) for bi in range(r_ref.shape[0]) for p in range(n_pairs)]
    pairs = [stage_factors(bi, p) for bi, p in ids]
    for d in pairs:
        stage_masks(d)
    n_steps = c.bit_length() - 1
    for step in range(n_steps):
        for d in pairs:
            stage_square(d, last=step + 1 == n_steps)
    for (bi, p), d in zip(ids, pairs):
        stage_output(bi, p, d)


def rwkv_chunked(r, lw, k, v, a, g, k_k, k_a, r_k, gn_w, gn_b):
    b, s, width = r.shape
    assert RWKV_HEAD * 2 == LANES and width % LANES == 0 and s % CHUNK == 0
    n_pairs = width // LANES
    nb = _pick(b, (RWKV_ROWS_PER_STEP, 1))
    seq = pl.BlockSpec((nb, CHUNK, width), lambda i, j: (i, j, 0))
    par = pl.BlockSpec((1, width), lambda i, j: (0, 0))
    row = lambda t: t.reshape(1, width)
    return pl.pallas_call(
        functools.partial(_rwkv_chunk_kernel, n_pairs=n_pairs),
        grid=(b // nb, s // CHUNK),
        in_specs=[seq] * 6 + [par] * 5,
        out_specs=seq,
        out_shape=jax.ShapeDtypeStruct((b, s, width), BF16),
        scratch_shapes=[pltpu.VMEM((nb * n_pairs, LANES, LANES), F32)],
        compiler_params=_cparams("parallel", "arbitrary"),
    )(r, lw, k, v, a, g, row(k_k), row(k_a), row(r_k), row(gn_w), row(gn_b))


def _diff_in_kernel(x_ref, gn_ref, w_ref, rot_ref, qg_ref, kg_ref, qk_ref, v_ref, *, heads):
    half = LANES // 2
    rope_half = half // 8
    proj = _dot(_rms(x_ref[...], gn_ref[...]).astype(BF16), w_ref[...])
    lane = lax.broadcasted_iota(jnp.int32, (x_ref.shape[0], LANES), 1)
    pos = lane & (half - 1)
    rot = rot_ref[...]
    first, second = pos < rope_half, (pos >= rope_half) & (pos < 2 * rope_half)
    cf = jnp.where(first, rot, jnp.where(second, pltpu.roll(rot, rope_half, axis=1), 1.0))
    sa = jnp.where(first, -pltpu.roll(rot, LANES - rope_half, axis=1), 0.0)
    sb = jnp.where(second, rot, 0.0)

    g_row = lax.broadcasted_iota(jnp.int32, (LANES, LANES), 0)
    g_col = lax.broadcasted_iota(jnp.int32, (LANES, LANES), 1)
    same_half = ((g_row < half) == (g_col < half)).astype(BF16)

    def norm_rope(x, g):
        ms = _dot((x * x).astype(BF16), same_half) * (1.0 / half)
        x = x * lax.rsqrt(ms + EPS) * g
        return x * cf + pltpu.roll(x, LANES - rope_half, axis=1) * sa + pltpu.roll(x, rope_half, axis=1) * sb

    ones = jnp.ones((x_ref.shape[0], LANES), BF16)
    for h in range(heads):
        tile = lambda i: proj[:, (i * heads + h) * LANES:(i * heads + h + 1) * LANES]
        q = norm_rope(tile(0), qg_ref[...]) * (half ** -0.5 * math.log2(math.e))
        qk_ref[:, h * LANES:(h + 1) * LANES] = q.astype(BF16)
        qk_ref[:, (heads + h) * LANES:(heads + h + 1) * LANES] = norm_rope(tile(1), kg_ref[...]).astype(BF16)
        v_ref[:, 2 * h * LANES:(2 * h + 1) * LANES] = tile(2).astype(BF16)
        v_ref[:, (2 * h + 1) * LANES:(2 * h + 2) * LANES] = ones


def _diff_attn_kernel(q_ref, k_ref, v_ref, lam_ref, sub_ref, o_ref, *, tq, lam_init):
    half = LANES // 2
    s_len = q_ref.shape[1]
    lv = lam_ref[...]
    lam = (jnp.exp(jnp.sum(lv[0:1] * lv[1:2], axis=-1, keepdims=True))
           - jnp.exp(jnp.sum(lv[2:3] * lv[3:4], axis=-1, keepdims=True)) + lam_init)
    lane = lax.broadcasted_iota(jnp.int32, (tq, LANES), 1)
    n_parts = ATTN_ROW_PARTS
    rp = tq // n_parts
    chains = [(m, part) for part in range(n_parts) for m in range(2)]

    def block(qs, j, carry, masked):
        kb = k_ref[0, j * tq:(j + 1) * tq, :]
        vb = v_ref[0, j * tq:(j + 1) * tq, :]
        scores, probs, maxes, out = {}, {}, {}, {}
        for i in range(len(chains) + 2):
            if i < len(chains):
                m, part = chains[i]
                keys = (part + 1) * rp if masked else tq
                s = _dot_nt(qs[m][part * rp:(part + 1) * rp], kb[:keys])
                if masked:
                    r_id = lax.broadcasted_iota(jnp.int32, s.shape, 0) + part * rp
                    c_id = lax.broadcasted_iota(jnp.int32, s.shape, 1)
                    s = jnp.where(c_id <= r_id, s, -1e30)
                scores[i] = s
            if 0 <= i - 1 < len(chains):
                c = i - 1
                s = scores.pop(c)
                mx = jnp.max(s, axis=-1, keepdims=True)
                maxes[c] = mx if carry is None else jnp.maximum(carry[2 * c], mx)
                probs[c] = jnp.exp2((s - maxes[c]).astype(BF16))
            if 0 <= i - 2 < len(chains):
                c = i - 2
                pr = probs.pop(c)
                pv = _dot(pr, vb[:pr.shape[1]])
                out[c] = pv if carry is None else jnp.exp2(carry[2 * c] - maxes[c]) * carry[2 * c + 1] + pv
        return tuple(t for c in range(len(chains)) for t in (maxes[c], out[c]))

    for qi in range(s_len // tq):
        q = q_ref[0, qi * tq:(qi + 1) * tq, :]
        zero = jnp.zeros_like(q)
        qs = [jnp.where(lane < half, q, zero), jnp.where(lane < half, zero, q)]
        carry = block(qs, qi, None, True)
        for j in range(qi):
            carry = block(qs, j, carry, False)
        acc1 = jnp.concatenate([carry[2 * c + 1] for c, (m, _) in enumerate(chains) if m == 0], axis=0)
        acc2 = jnp.concatenate([carry[2 * c + 1] for c, (m, _) in enumerate(chains) if m == 1], axis=0)
        o = acc1[:, :LANES] / acc1[:, LANES:] - lam * (acc2[:, :LANES] / acc2[:, LANES:])
        o_ref[0, qi * tq:(qi + 1) * tq, :] = (_rms(o, sub_ref[...]) * (1.0 - lam_init)).astype(o_ref.dtype)


def diff_attention(x, b, gn, w, rot, q_g, k_g, lam_vecs, subln_g, lam_init):
    m, d = x.shape
    s = m // b
    h = DIFF_HEADS
    assert w.shape[1] == 3 * h * LANES
    tm = _pick(m, (512, 256, 128))
    tile2 = lambda g: jnp.tile(g, 2).reshape(1, LANES)
    rows = lambda c: pl.BlockSpec((tm, c), lambda i: (i, 0))
    full = lambda a: pl.BlockSpec(a.shape, lambda i: (0,) * a.ndim)
    gspec2 = pl.BlockSpec((1, LANES), lambda i: (0, 0))
    qk, vext = pl.pallas_call(
        functools.partial(_diff_in_kernel, heads=h),
        grid=(m // tm,),
        in_specs=[rows(d), pl.BlockSpec((1, d), lambda i: (0, 0)), full(w), rows(LANES), gspec2, gspec2],
        out_specs=[rows(2 * h * LANES), rows(2 * h * LANES)],
        out_shape=[jax.ShapeDtypeStruct((m, 2 * h * LANES), BF16)] * 2,
        compiler_params=_cparams("parallel"),
    )(x, gn.reshape(1, d), w, rot, tile2(q_g), tile2(k_g))
    qk = qk.reshape(b, s, 2 * h * LANES)
    vext = vext.reshape(b, s, 2 * h * LANES)

    tq = _pick(s, (512, 256, 128))
    gspec = pl.BlockSpec((1, LANES), lambda bi, hi: (0, 0))
    return pl.pallas_call(
        functools.partial(_diff_attn_kernel, tq=tq, lam_init=lam_init),
        grid=(b, h),
        in_specs=[pl.BlockSpec((1, s, LANES), lambda bi, hi: (bi, 0, hi)),
                  pl.BlockSpec((1, s, LANES), lambda bi, hi: (bi, 0, h + hi)),
                  pl.BlockSpec((1, s, 2 * LANES), lambda bi, hi: (bi, 0, hi)),
                  pl.BlockSpec(lam_vecs.shape, lambda bi, hi: (0, 0)),
                  gspec],
        out_specs=pl.BlockSpec((1, s, LANES), lambda bi, hi: (bi, 0, hi)),
        out_shape=jax.ShapeDtypeStruct((b, s, h * LANES), BF16),
        compiler_params=_cparams("parallel", "parallel"),
    )(qk, qk, vext, lam_vecs, subln_g.reshape(1, LANES))


def rope_table(positions):
    half = LANES // 2
    rope_dims = half // 4
    inv_freq = ROPE_THETA ** (-jnp.arange(0, rope_dims, 2, dtype=F32) / rope_dims)
    ang = positions.astype(F32).reshape(-1, 1) * inv_freq
    one_half = jnp.concatenate([jnp.cos(ang), jnp.sin(ang), jnp.zeros((ang.shape[0], half - rope_dims), F32)], axis=-1)
    return jnp.concatenate([one_half, one_half], axis=-1)


def _xattn_kernel(x_ref, ya_ref, yb_ref, wa_ref, wb_ref, gn_ref, wq_ref, kv_ref, qg_ref, kg_ref, wo_ref, o_ref,
                  *, heads):
    x = x_ref[0] + _dot(ya_ref[0], wa_ref[...]) + _dot(yb_ref[0], wb_ref[...])
    d = x.shape[-1] // heads
    q_all = _dot(_rms(x, gn_ref[...]).astype(BF16), wq_ref[...])
    outs = []
    for h in range(heads):
        q = _rms(q_all[:, h * d:(h + 1) * d], qg_ref[...]) * (d ** -0.5)
        k = _rms(kv_ref[0, :, h * d:(h + 1) * d], kg_ref[...])
        v = kv_ref[0, :, (heads + h) * d:(heads + h + 1) * d]
        s = _dot_nt(q.astype(BF16), k.astype(BF16))
        s = s - jnp.max(s, axis=-1, keepdims=True)
        pr = jnp.exp(s)
        o = _dot(pr.astype(BF16), v.astype(BF16)) / jnp.sum(pr, axis=-1, keepdims=True)
        outs.append(o.astype(BF16))
    o_ref[0] = x + _dot(jnp.concatenate(outs, axis=1), wo_ref[...])


def mixer_out_cross_attention(x, ya, yb, wa, wb, gn, wq, kv, q_g, k_g, wo):
    b, s, dm = x.shape
    mlen = kv.shape[1]
    ts = _pick(s, (512, 256, 128))
    const = lambda a: pl.BlockSpec(a.shape, lambda i, j: (0,) * a.ndim)
    seq = lambda a: pl.BlockSpec((1, ts, a.shape[2]), lambda i, j: (i, j, 0))
    row = lambda a: a.reshape(1, -1)
    return pl.pallas_call(
        functools.partial(_xattn_kernel, heads=XATTN_HEADS),
        grid=(b, s // ts),
        in_specs=[seq(x), seq(ya), seq(yb), const(wa), const(wb),
                  const(row(gn)), const(wq),
                  pl.BlockSpec((1, mlen, 2 * dm), lambda i, j: (i, 0, 0)),
                  const(row(q_g)), const(row(k_g)), const(wo)],
        out_specs=seq(x),
        out_shape=jax.ShapeDtypeStruct((b, s, dm), F32),
        compiler_params=_cparams("parallel", "arbitrary"),
    )(x, ya, yb, wa, wb, row(gn), wq, kv, row(q_g), row(k_g), wo)


def _swiglu(h, wgu, wd):
    f = wd.shape[0]
    tf = _col_tile(h.shape[1], f)
    y = None
    for lo in range(0, f, tf):
        gate = _dot(h, wgu[:, lo:lo + tf])
        up = _dot(h, wgu[:, f + lo:f + lo + tf])
        part = _dot((gate * jax.nn.sigmoid(gate) * up).astype(BF16), wd[lo:lo + tf, :])
        y = part if y is None else y + part
    return y


def _ffn_kernel(x_ref, g_ref, wgu_ref, wd_ref, o_ref):
    x = x_ref[...]
    o_ref[...] = x + _swiglu(_rms(x, g_ref[...]).astype(BF16), wgu_ref, wd_ref)


def ffn_dense(x, g, w_gu, w_down):
    m, d = x.shape
    tm = _pick(m, (1024, 512, 256, 128))
    resident = lambda a: pl.BlockSpec(a.shape, lambda i: (0,) * a.ndim, pipeline_mode=pl.Buffered(1))
    return pl.pallas_call(
        _ffn_kernel,
        grid=(m // tm,),
        in_specs=[pl.BlockSpec((tm, d), lambda i: (i, 0)),
                  pl.BlockSpec((1, d), lambda i: (0, 0)),
                  resident(w_gu), resident(w_down)],
        out_specs=pl.BlockSpec((tm, d), lambda i: (i, 0)),
        out_shape=jax.ShapeDtypeStruct((m, d), F32),
        compiler_params=_cparams("parallel"),
    )(x, g.reshape(1, d), w_gu, w_down)


META_I1, META_I2, META_W1, META_W2, META_R1, META_R2 = range(6)


def _lane_pick(x, lane, idx):
    return jnp.sum(jnp.where(lane == idx, x, 0.0), axis=-1, keepdims=True)


def _router_kernel(x_ref, g_ref, wr_ref, meta_ref, cnt_ref, tri_ref, *, n_experts):
    tm = x_ref.shape[0]

    @pl.when(pl.program_id(0) == 0)
    def _():
        cnt_ref[...] = jnp.zeros_like(cnt_ref)
        r_id = lax.broadcasted_iota(jnp.int32, (tm, tm), 0)
        c_id = lax.broadcasted_iota(jnp.int32, (tm, tm), 1)
        tri_ref[...] = (r_id > c_id).astype(BF16)

    split = lambda t: (t.astype(BF16), (t - t.astype(BF16).astype(F32)).astype(BF16))
    h_hi, h_lo = split(_rms(x_ref[...], g_ref[...]))
    w_hi, w_lo = split(wr_ref[...])
    logits = _dot(h_hi, w_hi) + (_dot(h_hi, w_lo) + _dot(h_lo, w_hi))
    lane = lax.broadcasted_iota(jnp.int32, logits.shape, 1).astype(F32)
    neg = -jnp.inf
    l1 = jnp.where(lane < n_experts, logits, neg)
    m1 = jnp.max(l1, axis=-1, keepdims=True)
    i1 = jnp.min(jnp.where(l1 == m1, lane, float(LANES)), axis=-1, keepdims=True)
    l2 = jnp.where(lane == i1, neg, l1)
    m2 = jnp.max(l2, axis=-1, keepdims=True)
    i2 = jnp.min(jnp.where(l2 == m2, lane, float(LANES)), axis=-1, keepdims=True)
    e2 = jnp.exp(m2 - m1)
    w1 = 1.0 / (1.0 + e2)
    onehot = jnp.where((lane == i1) | (lane == i2), 1.0, 0.0)
    before = _dot(tri_ref[...], onehot.astype(BF16)) + cnt_ref[0:1, :]
    cnt_ref[0:1, :] += jnp.sum(onehot, axis=0, keepdims=True)
    meta = jnp.zeros_like(logits)
    for slot, val in ((META_I1, i1), (META_I2, i2), (META_W1, w1), (META_W2, e2 * w1),
                      (META_R1, _lane_pick(before, lane, i1)), (META_R2, _lane_pick(before, lane, i2))):
        meta = jnp.where(lane == slot, val, meta)
    meta_ref[...] = meta


def router_top2(x, g, w_router):
    m, d = x.shape
    n_experts = w_router.shape[1]
    wr = jnp.zeros((d, LANES), F32).at[:, :n_experts].set(w_router)
    tm = _pick(m, (1024, 512, 256, 128))
    return pl.pallas_call(
        functools.partial(_router_kernel, n_experts=n_experts),
        grid=(m // tm,),
        in_specs=[pl.BlockSpec((tm, d), lambda i: (i, 0)),
                  pl.BlockSpec((1, d), lambda i: (0, 0)),
                  pl.BlockSpec((d, LANES), lambda i: (0, 0))],
        out_specs=[pl.BlockSpec((tm, LANES), lambda i: (i, 0)),
                   pl.BlockSpec((8, LANES), lambda i: (0, 0))],
        out_shape=[jax.ShapeDtypeStruct((m, LANES), F32), jax.ShapeDtypeStruct((8, LANES), F32)],
        scratch_shapes=[pltpu.VMEM((tm, tm), BF16)],
        compiler_params=_cparams("arbitrary"),
    )(x, g.reshape(1, d), wr)


def _idx_copy(dest_hbm, idx_smem, isem, tile, slot):
    return pltpu.make_async_copy(dest_hbm.at[tile], idx_smem.at[slot], isem.at[slot])


def _dispatch_kernel(ends_ref, dest_hbm, x_ref, xs_hbm, idx_smem, zbuf, isem, sem, zsem, *, tm, tg):
    i = pl.program_id(0)
    slot = i % 2
    n_experts = ends_ref.shape[0]
    n_out_tiles = xs_hbm.shape[0] // tg

    @pl.when(i == 0)
    def _():
        _idx_copy(dest_hbm, idx_smem, isem, 0, 0).start()
        zbuf[...] = jnp.zeros_like(zbuf)
        fills = []
        for e in range(n_experts):
            group_start = ends_ref[e - 1] if e > 0 else 0
            fills.append((ends_ref[e] > group_start, ends_ref[e] - tg))
            tail = ends_ref[n_experts - 1] + e * tg
            fills.append((tail < n_out_tiles * tg, tail))
        zero_fill = lambda start: pltpu.make_async_copy(zbuf, xs_hbm.at[pl.ds(pl.multiple_of(start, tg), tg)], zsem)
        for cond, start in fills:
            @pl.when(cond)
            def _():
                zero_fill(start).start()
        for cond, start in fills:
            @pl.when(cond)
            def _():
                zero_fill(start).wait()

    @pl.when(i + 1 < pl.num_programs(0))
    def _():
        _idx_copy(dest_hbm, idx_smem, isem, i + 1, 1 - slot).start()

    _idx_copy(dest_hbm, idx_smem, isem, i, slot).wait()

    def row_copy(t, k):
        d = idx_smem[slot, TOP_K * t + k]
        return pltpu.make_async_copy(x_ref.at[pl.ds(t, 1)], xs_hbm.at[pl.ds(d, 1)], sem)

    def issue(t, carry):
        for k in range(TOP_K):
            row_copy(t, k).start()
        return carry

    lax.fori_loop(0, tm, issue, 0, unroll=8)
    for _ in range(TOP_K):
        pltpu.make_async_copy(x_ref, xs_hbm.at[pl.ds(0, tm)], sem).wait()


def moe_dispatch(x, dest, ends, n_rows, tg):
    m, d = x.shape
    n_tiles, per_tile = dest.shape
    tm = per_tile // TOP_K
    grid_spec = pltpu.PrefetchScalarGridSpec(
        num_scalar_prefetch=1,
        grid=(n_tiles,),
        in_specs=[pl.BlockSpec(memory_space=pl.ANY),
                  pl.BlockSpec((tm, d), lambda i, ends: (i, 0))],
        out_specs=pl.BlockSpec(memory_space=pl.ANY),
        scratch_shapes=[pltpu.SMEM((2, per_tile), jnp.int32), pltpu.VMEM((tg, d), F32),
                        pltpu.SemaphoreType.DMA((2,)), pltpu.SemaphoreType.DMA(()), pltpu.SemaphoreType.DMA(())],
    )
    return pl.pallas_call(
        functools.partial(_dispatch_kernel, tm=tm, tg=tg),
        grid_spec=grid_spec,
        out_shape=jax.ShapeDtypeStruct((n_rows, d), F32),
        compiler_params=_cparams("arbitrary"),
    )(ends, dest, x)


def _moe_ffn_kernel(te_ref, nu_ref, x_ref, g_ref, wgu_ref, wd_ref, o_ref):
    del te_ref

    @pl.when(pl.program_id(0) < nu_ref[0])
    def _():
        o_ref[...] = _swiglu(_rms(x_ref[...], g_ref[...]).astype(BF16), wgu_ref.at[0], wd_ref.at[0])

    @pl.when(pl.program_id(0) >= nu_ref[0])
    def _():
        o_ref[...] = jnp.zeros_like(o_ref)


def moe_grouped_ffn(xs, g, tile_expert, n_used, w_gu, w_down, tm):
    p, d = xs.shape
    f = w_down.shape[1]
    last = lambda n, nu: jnp.minimum(n, nu[0] - 1)
    grid_spec = pltpu.PrefetchScalarGridSpec(
        num_scalar_prefetch=2,
        grid=(p // tm,),
        in_specs=[pl.BlockSpec((tm, d), lambda n, te, nu: (last(n, nu), 0)),
                  pl.BlockSpec((1, d), lambda n, te, nu: (0, 0)),
                  pl.BlockSpec((1, d, 2 * f), lambda n, te, nu: (te[n], 0, 0), pipeline_mode=pl.Buffered(1)),
                  pl.BlockSpec((1, f, d), lambda n, te, nu: (te[n], 0, 0), pipeline_mode=pl.Buffered(1))],
        out_specs=pl.BlockSpec((tm, d), lambda n, te, nu: (n, 0)),
    )
    return pl.pallas_call(
        _moe_ffn_kernel,
        grid_spec=grid_spec,
        out_shape=jax.ShapeDtypeStruct((p, d), F32),
        compiler_params=_cparams("arbitrary"),
    )(tile_expert, n_used, xs, g.reshape(1, d), w_gu, w_down)


def _combine_kernel(dest_hbm, x_ref, meta_ref, ys_hbm, o_ref, idx_smem, ybuf, isem, sem, *, tm, n_tiles):
    i = pl.program_id(0)
    slot = i % 2

    def row_copy(sl, t, k):
        d = idx_smem[sl, TOP_K * t + k]
        return pltpu.make_async_copy(ys_hbm.at[pl.ds(d, 1)], ybuf.at[sl, k, pl.ds(t, 1)], sem.at[sl])

    def gather_tile(tile, sl):
        _idx_copy(dest_hbm, idx_smem, isem, tile, sl).wait()

        def issue(t, carry):
            for k in range(TOP_K):
                row_copy(sl, t, k).start()
            return carry

        lax.fori_loop(0, tm, issue, 0, unroll=8)

    @pl.when(i == 0)
    def _():
        _idx_copy(dest_hbm, idx_smem, isem, 0, 0).start()
        gather_tile(0, 0)
        if n_tiles > 1:
            _idx_copy(dest_hbm, idx_smem, isem, 1, 1).start()

    @pl.when(i + 1 < pl.num_programs(0))
    def _():
        gather_tile(i + 1, 1 - slot)

    @pl.when(i + 2 < pl.num_programs(0))
    def _():
        _idx_copy(dest_hbm, idx_smem, isem, i + 2, slot).start()

    for k in range(TOP_K):
        pltpu.make_async_copy(ys_hbm.at[pl.ds(0, tm)], ybuf.at[slot, k], sem.at[slot]).wait()
    meta = meta_ref[...]
    lane = lax.broadcasted_iota(jnp.int32, meta.shape, 1)
    w1 = _lane_pick(meta, lane, META_W1)
    w2 = _lane_pick(meta, lane, META_W2)
    o_ref[...] = x_ref[...] + w1 * ybuf[slot, 0] + w2 * ybuf[slot, 1]


def moe_combine(x, meta, dest, ys):
    m, d = x.shape
    n_tiles, per_tile = dest.shape
    tm = per_tile // TOP_K
    return pl.pallas_call(
        functools.partial(_combine_kernel, tm=tm, n_tiles=n_tiles),
        grid=(n_tiles,),
        in_specs=[pl.BlockSpec(memory_space=pl.ANY),
                  pl.BlockSpec((tm, d), lambda i: (i, 0)),
                  pl.BlockSpec((tm, LANES), lambda i: (i, 0)),
                  pl.BlockSpec(memory_space=pl.ANY)],
        out_specs=pl.BlockSpec((tm, d), lambda i: (i, 0)),
        out_shape=jax.ShapeDtypeStruct((m, d), F32),
        scratch_shapes=[pltpu.SMEM((2, per_tile), jnp.int32), pltpu.VMEM((2, TOP_K, tm, d), F32),
                        pltpu.SemaphoreType.DMA((2,)), pltpu.SemaphoreType.DMA((2,))],
        compiler_params=_cparams("arbitrary"),
    )(dest, x, meta, ys)


def ffn_moe(x, g, w_router, w_gu, w_down):
    m, d = x.shape
    n_experts = w_down.shape[0]
    tg = _pick(m, (MOE_GROUP_TILE, 256, 128))
    td = _pick(m, (MOE_TOKEN_TILE, 256, 128))
    meta, counts = router_top2(x, g, w_router)
    counts = counts[0, :n_experts].astype(jnp.int32)
    padded = (counts + tg - 1) // tg * tg
    ends = jnp.cumsum(padded)
    offsets = ends - padded
    n_tiles = (TOP_K * m) // tg + n_experts
    n_used = (ends[-1] // tg).astype(jnp.int32)
    tile_id = jnp.minimum(jnp.arange(n_tiles, dtype=jnp.int32), n_used - 1)
    tile_expert = jnp.sum(ends[None, :] <= (tile_id * tg)[:, None], axis=1).astype(jnp.int32)
    picks = meta[:, META_I1:META_I2 + 1].astype(jnp.int32)
    ranks = meta[:, META_R1:META_R2 + 1].astype(jnp.int32)
    dest = (offsets[picks] + ranks).reshape(m // td, TOP_K * td)
    xs = moe_dispatch(x, dest, ends.astype(jnp.int32), n_tiles * tg, tg)
    ys = moe_grouped_ffn(xs, g, tile_expert, n_used.reshape(1), w_gu, w_down, tg)
    return moe_combine(x, meta, dest, ys)


def _block_diag(w_a, w_b):
    za = jnp.zeros_like(w_a)
    zb = jnp.zeros_like(w_b)
    return jnp.concatenate([jnp.concatenate([w_a, za], axis=1), jnp.concatenate([zb, w_b], axis=1)], axis=0)


def kernel(x, mem, positions, mix_norm, in_proj, shift_mu, decay_bias, decay_up, iclr_bias, iclr_up, gate_up,
           key_kk_scale, key_iclr_scale, bonus_rk, rwkv_gn_w, rwkv_gn_b, vres_down, vres_shift_mu, vres_bias,
           vres_up, q_norm, k_norm, lambda_q1, lambda_k1, lambda_q2, lambda_k2, diff_subln, out_proj, xattn_norm,
           mem_norm, xattn_wq, xattn_wkv, xattn_wo, xattn_q_norm, xattn_k_norm, ffn_norm, dense_w_gu,
           dense_w_down, router, expert_w_gu, expert_w_down):
    b, s, d = x.shape
    depth = mix_norm.shape[0]
    width = decay_bias.shape[1]
    rwkv_cols = shift_mu.shape[1]
    in_cols = in_proj.shape[2]
    vres_rank = vres_down.shape[2] if depth > 1 else 0
    assert decay_up.shape[1] == LANES // 2 and iclr_up.shape[1] == LANES // 2 and gate_up.shape[1] == LANES
    assert rwkv_cols == 3 * width + 2 * LANES and rwkv_cols % LANES == 0 and in_cols % LANES == 0
    assert vres_rank <= LANES
    rot = rope_table(positions)
    xf = x.reshape(b * s, d)
    memf = mem.reshape(-1, d)
    v_first = None
    for l in range(depth):
        w_rwkv = in_proj[l][:, :rwkv_cols]
        vres = None
        if l > 0:
            pad = jnp.zeros((d, LANES - vres_rank), F32)
            w_rwkv = jnp.concatenate([w_rwkv, vres_down[l - 1], pad], axis=1)
            vmu = jnp.zeros((LANES,), F32).at[:vres_rank].set(vres_shift_mu[l - 1])
            vup = jnp.zeros((LANES, width), F32).at[:vres_rank].set(vres_up[l - 1]).astype(BF16)
            vres = (v_first, vmu, vres_bias[l - 1], vup)
        wwa = _block_diag(decay_up[l], iclr_up[l]).astype(BF16)
        r_, lw_, k_, v_, a_, g_ = rwkv_in_proj(xf, s, mix_norm[l], w_rwkv.astype(BF16), width, shift_mu[l],
                                               decay_bias[l], iclr_bias[l], wwa, gate_up[l].astype(BF16), vres)
        if l == 0:
            v_first = v_
        seq3 = lambda t: t.reshape(b, s, width)
        y_rwkv = rwkv_chunked(seq3(r_), seq3(lw_), seq3(k_), seq3(v_), seq3(a_), seq3(g_), key_kk_scale[l],
                              key_iclr_scale[l], bonus_rk[l].reshape(-1), rwkv_gn_w[l], rwkv_gn_b[l])
        lam_init = 0.8 - 0.6 * math.exp(-0.3 * l)
        lam_vecs = jnp.stack([lambda_q1[l], lambda_k1[l], lambda_q2[l], lambda_k2[l]])
        y_diff = diff_attention(xf, b, mix_norm[l], in_proj[l][:, rwkv_cols:].astype(BF16), rot, q_norm[l],
                                k_norm[l], lam_vecs, diff_subln[l], lam_init)
        w_out = out_proj[l].astype(BF16)
        kv = norm_matmul(memf, mem_norm[l], xattn_wkv[l].astype(BF16)).reshape(b, -1, 2 * d)
        xf = mixer_out_cross_attention(xf.reshape(b, s, d), y_rwkv, y_diff, w_out[:width], w_out[width:],
                                       xattn_norm[l], xattn_wq[l].astype(BF16), kv, xattn_q_norm[l],
                                       xattn_k_norm[l], xattn_wo[l].astype(BF16)).reshape(b * s, d)
        if l % 2 == 0:
            xf = ffn_dense(xf, ffn_norm[l], dense_w_gu[l // 2].astype(BF16), dense_w_down[l // 2].astype(BF16))
        else:
            xf = ffn_moe(xf, ffn_norm[l], router[l // 2], expert_w_gu[l // 2].astype(BF16),
                         expert_w_down[l // 2].astype(BF16))
    return xf.reshape(b, s, d)
```

```python
import functools
import math

import jax
import jax.numpy as jnp
from jax import lax
from jax.experimental import pallas as pl
from jax.experimental.pallas import tpu as pltpu

F32 = jnp.float32
BF16 = jnp.bfloat16

EPS = 1e-6
GN_EPS = 64e-5
ROPE_THETA = 500000.0
RWKV_HEAD = 64
DIFF_HEADS = 4
XATTN_HEADS = 4
TOP_K = 2
LANES = 128
CHUNK = 128
VMEM_LIMIT = 56 * 1024 * 1024
RWKV_ROWS_PER_STEP = 2
RWKV_IN_ROW_PARTS = 2
ATTN_ROW_PARTS = 2
WEIGHT_TILE_BYTES = 4 * 1024 * 1024
MOE_GROUP_TILE = 512
MOE_TOKEN_TILE = 512


def _cparams(*sem):
    return pltpu.CompilerParams(dimension_semantics=sem, vmem_limit_bytes=VMEM_LIMIT)


def _pick(n, prefs):
    for p in prefs:
        if n % p == 0:
            return p
    return n


def _col_tile(k, n):
    best = LANES
    for t in range(LANES, n + 1, LANES):
        if n % t == 0 and k * t * 2 <= WEIGHT_TILE_BYTES:
            best = t
    return best


def _dot(a, b, prec=None):
    return jnp.dot(a, b, preferred_element_type=F32, precision=prec)


def _dot_nt(a, b, prec=None):
    return lax.dot_general(a, b, (((1,), (1,)), ((), ())), preferred_element_type=F32, precision=prec)


def _rms(x, g):
    return x * lax.rsqrt(jnp.mean(x * x, axis=-1, keepdims=True) + EPS) * g


def _norm_mm_kernel(x_ref, g_ref, w_ref, o_ref, h_ref):
    @pl.when(pl.program_id(1) == 0)
    def _():
        h_ref[...] = _rms(x_ref[...], g_ref[...]).astype(BF16)

    o_ref[...] = _dot(h_ref[...], w_ref[...])


def norm_matmul(x, g, w):
    m, k = x.shape
    n = w.shape[1]
    tm = _pick(m, (1024, 512, 256, 128))
    tn = _col_tile(k, n)
    return pl.pallas_call(
        _norm_mm_kernel,
        grid=(m // tm, n // tn),
        in_specs=[pl.BlockSpec((tm, k), lambda i, j: (i, 0)),
                  pl.BlockSpec((1, k), lambda i, j: (0, 0)),
                  pl.BlockSpec((k, tn), lambda i, j: (0, j))],
        out_specs=pl.BlockSpec((tm, tn), lambda i, j: (i, j)),
        out_shape=jax.ShapeDtypeStruct((m, n), F32),
        scratch_shapes=[pltpu.VMEM((tm, k), BF16)],
        compiler_params=_cparams("parallel", "arbitrary"),
    )(x, g.reshape(1, k), w)


def _token_shift(p, prev_row, mu):
    prev = pltpu.roll(p, 1, axis=0)
    rid = lax.broadcasted_iota(jnp.int32, p.shape, 0)
    prev = jnp.where(rid == 0, prev_row, prev)
    return p + (prev - p) * mu


def _rwkv_in_kernel(*refs, width, has_vres, tiles_per_seq):
    if has_vres:
        (x_ref, gn_ref, w_ref, vf_ref, mu_ref, w0_ref, a0_ref, wwa_ref, gup_ref, vmu_ref, vb_ref, vup_ref,
         r_ref, lw_ref, k_ref, v_ref, a_ref, g_ref, carry_ref, carry_v_ref) = refs
    else:
        (x_ref, gn_ref, w_ref, mu_ref, w0_ref, a0_ref, wwa_ref, gup_ref,
         r_ref, lw_ref, k_ref, v_ref, a_ref, g_ref, carry_ref) = refs
    ncol = 3 * width + 2 * LANES
    tm = x_ref.shape[0]
    n_parts = RWKV_IN_ROW_PARTS
    rp = tm // n_parts

    @pl.when(pl.program_id(0) % tiles_per_seq == 0)
    def _():
        carry_ref[...] = jnp.zeros_like(carry_ref)
        if has_vres:
            carry_v_ref[...] = jnp.zeros_like(carry_v_ref)

    projs = [_dot(_rms(x_ref[i * rp:(i + 1) * rp, :], gn_ref[...]).astype(BF16), w_ref[...]) for i in range(n_parts)]
    prev = carry_ref[...]
    prev_v = carry_v_ref[...] if has_vres else None
    for i, proj in enumerate(projs):
        rows = slice(i * rp, (i + 1) * rp)
        sh = _token_shift(proj[:, :ncol], prev, mu_ref[...])
        prev = proj[rp - 1:rp, :ncol]
        r_ref[rows, :] = sh[:, :width].astype(r_ref.dtype)
        k_ref[rows, :] = sh[:, width:2 * width].astype(k_ref.dtype)
        v = sh[:, 2 * width:3 * width]
        dwa = sh[:, 3 * width:3 * width + LANES]
        lane = lax.broadcasted_iota(jnp.int32, dwa.shape, 1)
        dwa = jnp.where(lane < LANES // 2, jnp.tanh(dwa), dwa)
        wa = _dot(dwa.astype(BF16), wwa_ref[...])
        z = -(w0_ref[...] + wa[:, :width])
        softplus = jnp.maximum(z, 0.0) + jnp.log(1.0 + jnp.exp(-jnp.abs(z)))
        lw_ref[rows, :] = -jnp.exp(-softplus - 0.5)
        a_ref[rows, :] = jax.nn.sigmoid(a0_ref[...] + wa[:, width:]).astype(a_ref.dtype)
        dg = jax.nn.sigmoid(sh[:, 3 * width + LANES:3 * width + 2 * LANES])
        g_ref[rows, :] = _dot(dg.astype(BF16), gup_ref[...]).astype(g_ref.dtype)
        if has_vres:
            shv = _token_shift(proj[:, ncol:ncol + LANES], prev_v, vmu_ref[...])
            prev_v = proj[rp - 1:rp, ncol:ncol + LANES]
            mix = jax.nn.sigmoid(vb_ref[...] + _dot(shv.astype(BF16), vup_ref[...]))
            v = v + (vf_ref[rows, :].astype(F32) - v) * mix
        v_ref[rows, :] = v.astype(v_ref.dtype)
    carry_ref[...] = prev
    if has_vres:
        carry_v_ref[...] = prev_v


def rwkv_in_proj(x, seq_len, gn, w, width, mu, w0, a0, wwa, gup, vres=None):
    m, d = x.shape
    tm = _pick(seq_len, (512, 256, 128))
    ncol = 3 * width + 2 * LANES
    assert mu.shape[-1] == ncol and m % seq_len == 0
    row = lambda a: a.reshape(1, -1)
    full = lambda a: pl.BlockSpec(a.shape, lambda i: (0,) * a.ndim)
    rows = lambda c: pl.BlockSpec((tm, c), lambda i: (i, 0))
    params = [row(mu), row(w0), row(a0), wwa, gup]
    inputs = [x, row(gn), w]
    in_specs = [rows(d), full(row(gn)), full(w)]
    scratch = [pltpu.VMEM((1, ncol), F32)]
    if vres is not None:
        v_first, vmu, vb, vup = vres
        inputs.append(v_first)
        in_specs.append(rows(width))
        params += [row(vmu), row(vb), vup]
        scratch.append(pltpu.VMEM((1, LANES), F32))
    in_specs += [full(p) for p in params]
    out = lambda dt: jax.ShapeDtypeStruct((m, width), dt)
    return pl.pallas_call(
        functools.partial(_rwkv_in_kernel, width=width, has_vres=vres is not None, tiles_per_seq=seq_len // tm),
        grid=(m // tm,),
        in_specs=in_specs,
        out_specs=[rows(width)] * 6,
        out_shape=[out(BF16), out(F32), out(BF16), out(BF16), out(BF16), out(BF16)],
        scratch_shapes=scratch,
        compiler_params=_cparams("arbitrary"),
    )(*inputs, *params)


def _rwkv_chunk_kernel(r_ref, lw_ref, k_ref, v_ref, a_ref, g_ref, kk_ref, ka_ref, rk_ref, gw_ref, gb_ref,
                       y_ref, state_ref, *, n_pairs):
    c = CHUNK
    half = LANES // 2

    @pl.when(pl.program_id(1) == 0)
    def _():
        state_ref[...] = jnp.zeros_like(state_ref)

    row = lax.broadcasted_iota(jnp.int32, (c, c), 0)
    col = lax.broadcasted_iota(jnp.int32, (c, c), 1)
    ltri = (row >= col).astype(BF16)
    same_head = (row < half) == (col < half)
    lo = col < half
    row4 = lax.broadcasted_iota(jnp.int32, (c, 4 * c), 0)
    col4 = lax.broadcasted_iota(jnp.int32, (c, 4 * c), 1)
    incl4 = row4 >= (col4 & (c - 1))
    row2 = lax.broadcasted_iota(jnp.int32, (c, 2 * c), 0)
    col2 = lax.broadcasted_iota(jnp.int32, (c, 2 * c), 1)
    strict2 = row2 > (col2 & (c - 1))
    first_block = col2 < c

    def gsum(x):
        s_lo = jnp.sum(jnp.where(lo, x, 0.0), axis=-1, keepdims=True)
        s_hi = jnp.sum(jnp.where(lo, 0.0, x), axis=-1, keepdims=True)
        return jnp.where(lo, s_lo, s_hi)

    def by_head(x):
        return jnp.concatenate([jnp.where(lo, x, 0.0), jnp.where(lo, 0.0, x)], axis=0).astype(BF16)

    def split_bf16(x):
        hi = x.astype(BF16)
        rest = x - hi.astype(F32)
        mid = rest.astype(BF16)
        return hi, mid, (rest - mid.astype(F32)).astype(BF16)

    def stage_factors(bi, p):
        sl = slice(p * LANES, (p + 1) * LANES)
        r, lw, k, v, a = (ref[bi, :, sl].astype(F32) for ref in (r_ref, lw_ref, k_ref, v_ref, a_ref))
        cum3 = _dot(ltri, jnp.concatenate(split_bf16(lw), axis=1))
        cum = cum3[:, :LANES] + cum3[:, LANES:2 * LANES] + cum3[:, 2 * LANES:]
        cmid = cum[c // 2 - 1:c // 2, :]
        clast = cum[c - 1:c, :]
        ci = cum - cmid
        e_neg = jnp.exp(-ci)
        p_mid = jnp.exp(cmid)
        kk = k * kk_ref[:, sl]
        kkn = kk * lax.rsqrt(jnp.maximum(gsum(kk * kk), 1e-24))
        k2 = k * (1.0 + (a - 1.0) * ka_ref[:, sl])
        beta = kkn * a
        al_m = -kkn * jnp.exp(ci - lw)
        r_m = r * jnp.exp(ci)
        e_end = jnp.exp(clast - cum)
        ends = jnp.concatenate([beta * e_end, k2 * e_end], axis=0).astype(BF16)
        al2, r2 = by_head(al_m), by_head(r_m)
        lhs = jnp.concatenate([al2[:c], r2[:c], al2[c:], r2[c:]], axis=0)
        rhs = jnp.concatenate([beta * e_neg, k2 * e_neg], axis=0).astype(BF16)
        big = _dot_nt(lhs, rhs)
        s0 = state_ref[bi * n_pairs + p]
        base = _dot_nt(jnp.concatenate([al_m * p_mid, r_m * p_mid], axis=0).astype(BF16), s0.astype(BF16))
        return dict(big=big, base=base, v2=by_head(v), ends=ends, decayed=s0 * jnp.exp(clast),
                    bonus=gsum(r * k2 * rk_ref[:, sl]) * v)

    def stage_masks(d):
        big = d.pop("big")
        blk = lambda i, j: big[i * c:(i + 1) * c, j * c:(j + 1) * c]
        zero2 = jnp.zeros((c, 2 * c), F32)
        d["pw"] = jnp.where(strict2, jnp.concatenate([blk(0, 0), blk(2, 0)], axis=1), zero2).astype(BF16)
        a_ak = jnp.where(strict2, jnp.concatenate([blk(0, 1), blk(2, 1)], axis=1), zero2).astype(BF16)
        d["a_y"] = jnp.where(incl4, jnp.concatenate([blk(1, 0), blk(3, 0), blk(1, 1), blk(3, 1)], axis=1),
                             0.0).astype(BF16)
        d["x"] = d["base"][:c] + _dot(a_ak, d["v2"])

    def stage_square(d, last):
        pw, x2 = d["pw"], by_head(d["x"])
        if last:
            d["x"] = d["x"] + _dot(pw, x2)
            return
        zero_b = jnp.zeros_like(pw)
        bd = jnp.concatenate([jnp.where(first_block, pw, zero_b), jnp.where(first_block, zero_b, pw)], axis=0)
        res = _dot(pw, jnp.concatenate([bd, x2], axis=1))
        d["pw"] = res[:, :2 * c].astype(BF16)
        d["x"] = d["x"] + res[:, 2 * c:]

    def stage_output(bi, p, d):
        sl = slice(p * LANES, (p + 1) * LANES)
        x = d["x"]
        y = d["base"][c:] + _dot(d["a_y"], jnp.concatenate([by_head(x), d["v2"]], axis=0))
        upd = _dot(jnp.concatenate([x.T, v_ref[bi, :, sl].astype(F32).T], axis=1).astype(BF16), d["ends"])
        state_ref[bi * n_pairs + p] = d["decayed"] + jnp.where(same_head, upd, 0.0)
        mean = gsum(y) * (1.0 / half)
        yc = y - mean
        var = gsum(yc * yc) * (1.0 / half)
        out = yc * lax.rsqrt(var + GN_EPS) * gw_ref[:, sl] + gb_ref[:, sl]
        y_ref[bi, :, sl] = ((out + d["bonus"]) * g_ref[bi, :, sl].astype(F32)).astype(y_ref.dtype)

    ids = [(bi, p) for bi in range(r_ref.shape[0]) for p in range(n_pairs)]
    pairs = [stage_factors(bi, p) for bi, p in ids]
    for d in pairs:
        stage_masks(d)
    n_steps = c.bit_length() - 1
    for step in range(n_steps):
        for d in pairs:
            stage_square(d, last=step + 1 == n_steps)
    for (bi, p), d in zip(ids, pairs):
        stage_output(bi, p, d)


def rwkv_chunked(r, lw, k, v, a, g, k_k, k_a, r_k, gn_w, gn_b):
    b, s, width = r.shape
    assert RWKV_HEAD * 2 == LANES and width % LANES == 0 and s % CHUNK == 0
    n_pairs = width // LANES
    nb = _pick(b, (RWKV_ROWS_PER_STEP, 1))
    seq = pl.BlockSpec((nb, CHUNK, width), lambda i, j: (i, j, 0))
    par = pl.BlockSpec((1, width), lambda i, j: (0, 0))
    row = lambda t: t.reshape(1, width)
    return pl.pallas_call(
        functools.partial(_rwkv_chunk_kernel, n_pairs=n_pairs),
        grid=(b // nb, s // CHUNK),
        in_specs=[seq] * 6 + [par] * 5,
        out_specs=seq,
        out_shape=jax.ShapeDtypeStruct((b, s, width), BF16),
        scratch_shapes=[pltpu.VMEM((nb * n_pairs, LANES, LANES), F32)],
        compiler_params=_cparams("parallel", "arbitrary"),
    )(r, lw, k, v, a, g, row(k_k), row(k_a), row(r_k), row(gn_w), row(gn_b))


def _diff_in_kernel(x_ref, gn_ref, w_ref, rot_ref, qg_ref, kg_ref, qk_ref, v_ref, *, heads):
    half = LANES // 2
    rope_half = half // 8
    proj = _dot(_rms(x_ref[...], gn_ref[...]).astype(BF16), w_ref[...])
    lane = lax.broadcasted_iota(jnp.int32, (x_ref.shape[0], LANES), 1)
    pos = lane & (half - 1)
    rot = rot_ref[...]
    first, second = pos < rope_half, (pos >= rope_half) & (pos < 2 * rope_half)
    cf = jnp.where(first, rot, jnp.where(second, pltpu.roll(rot, rope_half, axis=1), 1.0))
    sn = jnp.where(first, -pltpu.roll(rot, LANES - rope_half, axis=1), jnp.where(second, rot, 0.0))

    g_row = lax.broadcasted_iota(jnp.int32, (LANES, LANES), 0)
    g_col = lax.broadcasted_iota(jnp.int32, (LANES, LANES), 1)
    same_half = ((g_row < half) == (g_col < half)).astype(BF16)
    c_pos = g_col & (half - 1)
    partner = jnp.where(c_pos < rope_half, g_col + rope_half, jnp.where(c_pos < 2 * rope_half, g_col - rope_half, -1))
    swap = (g_row == partner).astype(BF16)

    def norm_rope(x, g):
        ms = _dot((x * x).astype(BF16), same_half) * (1.0 / half)
        x = x * lax.rsqrt(ms + EPS) * g
        return x * cf + _dot(x.astype(BF16), swap) * sn

    ones = jnp.ones((x_ref.shape[0], LANES), BF16)
    for h in range(heads):
        tile = lambda i: proj[:, (i * heads + h) * LANES:(i * heads + h + 1) * LANES]
        q = norm_rope(tile(0), qg_ref[...]) * (half ** -0.5 * math.log2(math.e))
        qk_ref[:, h * LANES:(h + 1) * LANES] = q.astype(BF16)
        qk_ref[:, (heads + h) * LANES:(heads + h + 1) * LANES] = norm_rope(tile(1), kg_ref[...]).astype(BF16)
        v_ref[:, 2 * h * LANES:(2 * h + 1) * LANES] = tile(2).astype(BF16)
        v_ref[:, (2 * h + 1) * LANES:(2 * h + 2) * LANES] = ones


def _diff_attn_kernel(q_ref, k_ref, v_ref, lam_ref, sub_ref, o_ref, *, tq, lam_init):
    half = LANES // 2
    s_len = q_ref.shape[1]
    lv = lam_ref[...]
    lam = (jnp.exp(jnp.sum(lv[0:1] * lv[1:2], axis=-1, keepdims=True))
           - jnp.exp(jnp.sum(lv[2:3] * lv[3:4], axis=-1, keepdims=True)) + lam_init)
    lane = lax.broadcasted_iota(jnp.int32, (tq, LANES), 1)
    n_parts = ATTN_ROW_PARTS
    rp = tq // n_parts
    chains = [(m, part) for part in range(n_parts) for m in range(2)]

    def block(qs, j, carry, masked):
        kb = k_ref[0, j * tq:(j + 1) * tq, :]
        vb = v_ref[0, j * tq:(j + 1) * tq, :]
        scores, probs, maxes, out = {}, {}, {}, {}
        for i in range(len(chains) + 2):
            if i < len(chains):
                m, part = chains[i]
                keys = (part + 1) * rp if masked else tq
                s = _dot_nt(qs[m][part * rp:(part + 1) * rp], kb[:keys])
                if masked:
                    r_id = lax.broadcasted_iota(jnp.int32, s.shape, 0) + part * rp
                    c_id = lax.broadcasted_iota(jnp.int32, s.shape, 1)
                    s = jnp.where(c_id <= r_id, s, -1e30)
                scores[i] = s
            if 0 <= i - 1 < len(chains):
                c = i - 1
                s = scores.pop(c)
                mx = jnp.max(s, axis=-1, keepdims=True)
                maxes[c] = mx if carry is None else jnp.maximum(carry[2 * c], mx)
                probs[c] = jnp.exp2((s - maxes[c]).astype(BF16))
            if 0 <= i - 2 < len(chains):
                c = i - 2
                pr = probs.pop(c)
                pv = _dot(pr, vb[:pr.shape[1]])
                out[c] = pv if carry is None else jnp.exp2(carry[2 * c] - maxes[c]) * carry[2 * c + 1] + pv
        return tuple(t for c in range(len(chains)) for t in (maxes[c], out[c]))

    for qi in range(s_len // tq):
        q = q_ref[0, qi * tq:(qi + 1) * tq, :]
        zero = jnp.zeros_like(q)
        qs = [jnp.where(lane < half, q, zero), jnp.where(lane < half, zero, q)]
        carry = block(qs, qi, None, True)
        for j in range(qi):
            carry = block(qs, j, carry, False)
        acc1 = jnp.concatenate([carry[2 * c + 1] for c, (m, _) in enumerate(chains) if m == 0], axis=0)
        acc2 = jnp.concatenate([carry[2 * c + 1] for c, (m, _) in enumerate(chains) if m == 1], axis=0)
        o = acc1[:, :LANES] / acc1[:, LANES:] - lam * (acc2[:, :LANES] / acc2[:, LANES:])
        o_ref[0, qi * tq:(qi + 1) * tq, :] = (_rms(o, sub_ref[...]) * (1.0 - lam_init)).astype(o_ref.dtype)


def diff_attention(x, b, gn, w, rot, q_g, k_g, lam_vecs, subln_g, lam_init):
    m, d = x.shape
    s = m // b
    h = DIFF_HEADS
    assert w.shape[1] == 3 * h * LANES
    tm = _pick(m, (512, 256, 128))
    tile2 = lambda g: jnp.tile(g, 2).reshape(1, LANES)
    rows = lambda c: pl.BlockSpec((tm, c), lambda i: (i, 0))
    full = lambda a: pl.BlockSpec(a.shape, lambda i: (0,) * a.ndim)
    gspec2 = pl.BlockSpec((1, LANES), lambda i: (0, 0))
    qk, vext = pl.pallas_call(
        functools.partial(_diff_in_kernel, heads=h),
        grid=(m // tm,),
        in_specs=[rows(d), pl.BlockSpec((1, d), lambda i: (0, 0)), full(w), rows(LANES), gspec2, gspec2],
        out_specs=[rows(2 * h * LANES), rows(2 * h * LANES)],
        out_shape=[jax.ShapeDtypeStruct((m, 2 * h * LANES), BF16)] * 2,
        compiler_params=_cparams("parallel"),
    )(x, gn.reshape(1, d), w, rot, tile2(q_g), tile2(k_g))
    qk = qk.reshape(b, s, 2 * h * LANES)
    vext = vext.reshape(b, s, 2 * h * LANES)

    tq = _pick(s, (512, 256, 128))
    gspec = pl.BlockSpec((1, LANES), lambda bi, hi: (0, 0))
    return pl.pallas_call(
        functools.partial(_diff_attn_kernel, tq=tq, lam_init=lam_init),
        grid=(b, h),
        in_specs=[pl.BlockSpec((1, s, LANES), lambda bi, hi: (bi, 0, hi)),
                  pl.BlockSpec((1, s, LANES), lambda bi, hi: (bi, 0, h + hi)),
                  pl.BlockSpec((1, s, 2 * LANES), lambda bi, hi: (bi, 0, hi)),
                  pl.BlockSpec(lam_vecs.shape, lambda bi, hi: (0, 0)),
                  gspec],
        out_specs=pl.BlockSpec((1, s, LANES), lambda bi, hi: (bi, 0, hi)),
        out_shape=jax.ShapeDtypeStruct((b, s, h * LANES), BF16),
        compiler_params=_cparams("parallel", "parallel"),
    )(qk, qk, vext, lam_vecs, subln_g.reshape(1, LANES))


def rope_table(positions):
    half = LANES // 2
    rope_dims = half // 4
    inv_freq = ROPE_THETA ** (-jnp.arange(0, rope_dims, 2, dtype=F32) / rope_dims)
    ang = positions.astype(F32).reshape(-1, 1) * inv_freq
    one_half = jnp.concatenate([jnp.cos(ang), jnp.sin(ang), jnp.zeros((ang.shape[0], half - rope_dims), F32)], axis=-1)
    return jnp.concatenate([one_half, one_half], axis=-1)


def _xattn_kernel(x_ref, ya_ref, yb_ref, wa_ref, wb_ref, gn_ref, wq_ref, kv_ref, qg_ref, kg_ref, wo_ref, o_ref,
                  *, heads):
    x = x_ref[0] + _dot(ya_ref[0], wa_ref[...]) + _dot(yb_ref[0], wb_ref[...])
    d = x.shape[-1] // heads
    q_all = _dot(_rms(x, gn_ref[...]).astype(BF16), wq_ref[...])
    outs = []
    for h in range(heads):
        q = _rms(q_all[:, h * d:(h + 1) * d], qg_ref[...]) * (d ** -0.5)
        k = _rms(kv_ref[0, :, h * d:(h + 1) * d], kg_ref[...])
        v = kv_ref[0, :, (heads + h) * d:(heads + h + 1) * d]
        s = _dot_nt(q.astype(BF16), k.astype(BF16))
        s = s - jnp.max(s, axis=-1, keepdims=True)
        pr = jnp.exp(s)
        o = _dot(pr.astype(BF16), v.astype(BF16)) / jnp.sum(pr, axis=-1, keepdims=True)
        outs.append(o.astype(BF16))
    o_ref[0] = x + _dot(jnp.concatenate(outs, axis=1), wo_ref[...])


def mixer_out_cross_attention(x, ya, yb, wa, wb, gn, wq, kv, q_g, k_g, wo):
    b, s, dm = x.shape
    mlen = kv.shape[1]
    ts = _pick(s, (512, 256, 128))
    const = lambda a: pl.BlockSpec(a.shape, lambda i, j: (0,) * a.ndim)
    seq = lambda a: pl.BlockSpec((1, ts, a.shape[2]), lambda i, j: (i, j, 0))
    row = lambda a: a.reshape(1, -1)
    return pl.pallas_call(
        functools.partial(_xattn_kernel, heads=XATTN_HEADS),
        grid=(b, s // ts),
        in_specs=[seq(x), seq(ya), seq(yb), const(wa), const(wb),
                  const(row(gn)), const(wq),
                  pl.BlockSpec((1, mlen, 2 * dm), lambda i, j: (i, 0, 0)),
                  const(row(q_g)), const(row(k_g)), const(wo)],
        out_specs=seq(x),
        out_shape=jax.ShapeDtypeStruct((b, s, dm), F32),
        compiler_params=_cparams("parallel", "arbitrary"),
    )(x, ya, yb, wa, wb, row(gn), wq, kv, row(q_g), row(k_g), wo)


def _swiglu(h, wgu, wd):
    f = wd.shape[0]
    tf = _col_tile(h.shape[1], f)
    y = None
    for lo in range(0, f, tf):
        gate = _dot(h, wgu[:, lo:lo + tf])
        up = _dot(h, wgu[:, f + lo:f + lo + tf])
        part = _dot((gate * jax.nn.sigmoid(gate) * up).astype(BF16), wd[lo:lo + tf, :])
        y = part if y is None else y + part
    return y


def _ffn_kernel(x_ref, g_ref, wgu_ref, wd_ref, o_ref):
    x = x_ref[...]
    o_ref[...] = x + _swiglu(_rms(x, g_ref[...]).astype(BF16), wgu_ref, wd_ref)


def ffn_dense(x, g, w_gu, w_down):
    m, d = x.shape
    tm = _pick(m, (1024, 512, 256, 128))
    resident = lambda a: pl.BlockSpec(a.shape, lambda i: (0,) * a.ndim, pipeline_mode=pl.Buffered(1))
    return pl.pallas_call(
        _ffn_kernel,
        grid=(m // tm,),
        in_specs=[pl.BlockSpec((tm, d), lambda i: (i, 0)),
                  pl.BlockSpec((1, d), lambda i: (0, 0)),
                  resident(w_gu), resident(w_down)],
        out_specs=pl.BlockSpec((tm, d), lambda i: (i, 0)),
        out_shape=jax.ShapeDtypeStruct((m, d), F32),
        compiler_params=_cparams("parallel"),
    )(x, g.reshape(1, d), w_gu, w_down)


META_I1, META_I2, META_W1, META_W2, META_R1, META_R2 = range(6)


def _lane_pick(x, lane, idx):
    return jnp.sum(jnp.where(lane == idx, x, 0.0), axis=-1, keepdims=True)


def _router_kernel(x_ref, g_ref, wr_ref, meta_ref, cnt_ref, tri_ref, *, n_experts):
    tm = x_ref.shape[0]

    @pl.when(pl.program_id(0) == 0)
    def _():
        cnt_ref[...] = jnp.zeros_like(cnt_ref)
        r_id = lax.broadcasted_iota(jnp.int32, (tm, tm), 0)
        c_id = lax.broadcasted_iota(jnp.int32, (tm, tm), 1)
        tri_ref[...] = (r_id > c_id).astype(BF16)

    split = lambda t: (t.astype(BF16), (t - t.astype(BF16).astype(F32)).astype(BF16))
    h_hi, h_lo = split(_rms(x_ref[...], g_ref[...]))
    w_hi, w_lo = split(wr_ref[...])
    logits = _dot(h_hi, w_hi) + (_dot(h_hi, w_lo) + _dot(h_lo, w_hi))
    lane = lax.broadcasted_iota(jnp.int32, logits.shape, 1).astype(F32)
    neg = -jnp.inf
    l1 = jnp.where(lane < n_experts, logits, neg)
    m1 = jnp.max(l1, axis=-1, keepdims=True)
    i1 = jnp.min(jnp.where(l1 == m1, lane, float(LANES)), axis=-1, keepdims=True)
    l2 = jnp.where(lane == i1, neg, l1)
    m2 = jnp.max(l2, axis=-1, keepdims=True)
    i2 = jnp.min(jnp.where(l2 == m2, lane, float(LANES)), axis=-1, keepdims=True)
    e2 = jnp.exp(m2 - m1)
    w1 = 1.0 / (1.0 + e2)
    onehot = jnp.where((lane == i1) | (lane == i2), 1.0, 0.0)
    before = _dot(tri_ref[...], onehot.astype(BF16)) + cnt_ref[0:1, :]
    cnt_ref[0:1, :] += jnp.sum(onehot, axis=0, keepdims=True)
    meta = jnp.zeros_like(logits)
    for slot, val in ((META_I1, i1), (META_I2, i2), (META_W1, w1), (META_W2, e2 * w1),
                      (META_R1, _lane_pick(before, lane, i1)), (META_R2, _lane_pick(before, lane, i2))):
        meta = jnp.where(lane == slot, val, meta)
    meta_ref[...] = meta


def router_top2(x, g, w_router):
    m, d = x.shape
    n_experts = w_router.shape[1]
    wr = jnp.zeros((d, LANES), F32).at[:, :n_experts].set(w_router)
    tm = _pick(m, (1024, 512, 256, 128))
    return pl.pallas_call(
        functools.partial(_router_kernel, n_experts=n_experts),
        grid=(m // tm,),
        in_specs=[pl.BlockSpec((tm, d), lambda i: (i, 0)),
                  pl.BlockSpec((1, d), lambda i: (0, 0)),
                  pl.BlockSpec((d, LANES), lambda i: (0, 0))],
        out_specs=[pl.BlockSpec((tm, LANES), lambda i: (i, 0)),
                   pl.BlockSpec((8, LANES), lambda i: (0, 0))],
        out_shape=[jax.ShapeDtypeStruct((m, LANES), F32), jax.ShapeDtypeStruct((8, LANES), F32)],
        scratch_shapes=[pltpu.VMEM((tm, tm), BF16)],
        compiler_params=_cparams("arbitrary"),
    )(x, g.reshape(1, d), wr)


def _idx_copy(dest_hbm, idx_smem, isem, tile, slot):
    return pltpu.make_async_copy(dest_hbm.at[tile], idx_smem.at[slot], isem.at[slot])


def _dispatch_kernel(ends_ref, dest_hbm, x_ref, xs_hbm, idx_smem, zbuf, isem, sem, zsem, *, tm, tg):
    i = pl.program_id(0)
    slot = i % 2
    n_experts = ends_ref.shape[0]
    n_out_tiles = xs_hbm.shape[0] // tg

    @pl.when(i == 0)
    def _():
        _idx_copy(dest_hbm, idx_smem, isem, 0, 0).start()
        zbuf[...] = jnp.zeros_like(zbuf)
        fills = []
        for e in range(n_experts):
            group_start = ends_ref[e - 1] if e > 0 else 0
            fills.append((ends_ref[e] > group_start, ends_ref[e] - tg))
            tail = ends_ref[n_experts - 1] + e * tg
            fills.append((tail < n_out_tiles * tg, tail))
        zero_fill = lambda start: pltpu.make_async_copy(zbuf, xs_hbm.at[pl.ds(pl.multiple_of(start, tg), tg)], zsem)
        for cond, start in fills:
            @pl.when(cond)
            def _():
                zero_fill(start).start()
        for cond, start in fills:
            @pl.when(cond)
            def _():
                zero_fill(start).wait()

    @pl.when(i + 1 < pl.num_programs(0))
    def _():
        _idx_copy(dest_hbm, idx_smem, isem, i + 1, 1 - slot).start()

    _idx_copy(dest_hbm, idx_smem, isem, i, slot).wait()

    def row_copy(t, k):
        d = idx_smem[slot, TOP_K * t + k]
        return pltpu.make_async_copy(x_ref.at[pl.ds(t, 1)], xs_hbm.at[pl.ds(d, 1)], sem)

    def issue(t, carry):
        for k in range(TOP_K):
            row_copy(t, k).start()
        return carry

    lax.fori_loop(0, tm, issue, 0, unroll=8)
    for _ in range(TOP_K):
        pltpu.make_async_copy(x_ref, xs_hbm.at[pl.ds(0, tm)], sem).wait()


def moe_dispatch(x, dest, ends, n_rows, tg):
    m, d = x.shape
    n_tiles, per_tile = dest.shape
    tm = per_tile // TOP_K
    grid_spec = pltpu.PrefetchScalarGridSpec(
        num_scalar_prefetch=1,
        grid=(n_tiles,),
        in_specs=[pl.BlockSpec(memory_space=pl.ANY),
                  pl.BlockSpec((tm, d), lambda i, ends: (i, 0))],
        out_specs=pl.BlockSpec(memory_space=pl.ANY),
        scratch_shapes=[pltpu.SMEM((2, per_tile), jnp.int32), pltpu.VMEM((tg, d), F32),
                        pltpu.SemaphoreType.DMA((2,)), pltpu.SemaphoreType.DMA(()), pltpu.SemaphoreType.DMA(())],
    )
    return pl.pallas_call(
        functools.partial(_dispatch_kernel, tm=tm, tg=tg),
        grid_spec=grid_spec,
        out_shape=jax.ShapeDtypeStruct((n_rows, d), F32),
        compiler_params=_cparams("arbitrary"),
    )(ends, dest, x)


def _moe_ffn_kernel(te_ref, nu_ref, x_ref, g_ref, wgu_ref, wd_ref, o_ref):
    del te_ref

    @pl.when(pl.program_id(0) < nu_ref[0])
    def _():
        o_ref[...] = _swiglu(_rms(x_ref[...], g_ref[...]).astype(BF16), wgu_ref.at[0], wd_ref.at[0])

    @pl.when(pl.program_id(0) >= nu_ref[0])
    def _():
        o_ref[...] = jnp.zeros_like(o_ref)


def moe_grouped_ffn(xs, g, tile_expert, n_used, w_gu, w_down, tm):
    p, d = xs.shape
    f = w_down.shape[1]
    last = lambda n, nu: jnp.minimum(n, nu[0] - 1)
    grid_spec = pltpu.PrefetchScalarGridSpec(
        num_scalar_prefetch=2,
        grid=(p // tm,),
        in_specs=[pl.BlockSpec((tm, d), lambda n, te, nu: (last(n, nu), 0)),
                  pl.BlockSpec((1, d), lambda n, te, nu: (0, 0)),
                  pl.BlockSpec((1, d, 2 * f), lambda n, te, nu: (te[n], 0, 0), pipeline_mode=pl.Buffered(1)),
                  pl.BlockSpec((1, f, d), lambda n, te, nu: (te[n], 0, 0), pipeline_mode=pl.Buffered(1))],
        out_specs=pl.BlockSpec((tm, d), lambda n, te, nu: (n, 0)),
    )
    return pl.pallas_call(
        _moe_ffn_kernel,
        grid_spec=grid_spec,
        out_shape=jax.ShapeDtypeStruct((p, d), F32),
        compiler_params=_cparams("arbitrary"),
    )(tile_expert, n_used, xs, g.reshape(1, d), w_gu, w_down)


def _combine_kernel(dest_hbm, x_ref, meta_ref, ys_hbm, o_ref, idx_smem, ybuf, isem, sem, *, tm, n_tiles):
    i = pl.program_id(0)
    slot = i % 2

    def row_copy(sl, t, k):
        d = idx_smem[sl, TOP_K * t + k]
        return pltpu.make_async_copy(ys_hbm.at[pl.ds(d, 1)], ybuf.at[sl, k, pl.ds(t, 1)], sem.at[sl])

    def gather_tile(tile, sl):
        _idx_copy(dest_hbm, idx_smem, isem, tile, sl).wait()

        def issue(t, carry):
            for k in range(TOP_K):
                row_copy(sl, t, k).start()
            return carry

        lax.fori_loop(0, tm, issue, 0, unroll=8)

    @pl.when(i == 0)
    def _():
        _idx_copy(dest_hbm, idx_smem, isem, 0, 0).start()
        gather_tile(0, 0)
        if n_tiles > 1:
            _idx_copy(dest_hbm, idx_smem, isem, 1, 1).start()

    @pl.when(i + 1 < pl.num_programs(0))
    def _():
        gather_tile(i + 1, 1 - slot)

    @pl.when(i + 2 < pl.num_programs(0))
    def _():
        _idx_copy(dest_hbm, idx_smem, isem, i + 2, slot).start()

    for k in range(TOP_K):
        pltpu.make_async_copy(ys_hbm.at[pl.ds(0, tm)], ybuf.at[slot, k], sem.at[slot]).wait()
    meta = meta_ref[...]
    lane = lax.broadcasted_iota(jnp.int32, meta.shape, 1)
    w1 = _lane_pick(meta, lane, META_W1)
    w2 = _lane_pick(meta, lane, META_W2)
    o_ref[...] = x_ref[...] + w1 * ybuf[slot, 0] + w2 * ybuf[slot, 1]


def moe_combine(x, meta, dest, ys):
    m, d = x.shape
    n_tiles, per_tile = dest.shape
    tm = per_tile // TOP_K
    return pl.pallas_call(
        functools.partial(_combine_kernel, tm=tm, n_tiles=n_tiles),
        grid=(n_tiles,),
        in_specs=[pl.BlockSpec(memory_space=pl.ANY),
                  pl.BlockSpec((tm, d), lambda i: (i, 0)),
                  pl.BlockSpec((tm, LANES), lambda i: (i, 0)),
                  pl.BlockSpec(memory_space=pl.ANY)],
        out_specs=pl.BlockSpec((tm, d), lambda i: (i, 0)),
        out_shape=jax.ShapeDtypeStruct((m, d), F32),
        scratch_shapes=[pltpu.SMEM((2, per_tile), jnp.int32), pltpu.VMEM((2, TOP_K, tm, d), F32),
                        pltpu.SemaphoreType.DMA((2,)), pltpu.SemaphoreType.DMA((2,))],
        compiler_params=_cparams("arbitrary"),
    )(dest, x, meta, ys)


def ffn_moe(x, g, w_router, w_gu, w_down):
    m, d = x.shape
    n_experts = w_down.shape[0]
    tg = _pick(m, (MOE_GROUP_TILE, 256, 128))
    td = _pick(m, (MOE_TOKEN_TILE, 256, 128))
    meta, counts = router_top2(x, g, w_router)
    counts = counts[0, :n_experts].astype(jnp.int32)
    padded = (counts + tg - 1) // tg * tg
    ends = jnp.cumsum(padded)
    offsets = ends - padded
    n_tiles = (TOP_K * m) // tg + n_experts
    n_used = (ends[-1] // tg).astype(jnp.int32)
    tile_id = jnp.minimum(jnp.arange(n_tiles, dtype=jnp.int32), n_used - 1)
    tile_expert = jnp.sum(ends[None, :] <= (tile_id * tg)[:, None], axis=1).astype(jnp.int32)
    picks = meta[:, META_I1:META_I2 + 1].astype(jnp.int32)
    ranks = meta[:, META_R1:META_R2 + 1].astype(jnp.int32)
    dest = (offsets[picks] + ranks).reshape(m // td, TOP_K * td)
    xs = moe_dispatch(x, dest, ends.astype(jnp.int32), n_tiles * tg, tg)
    ys = moe_grouped_ffn(xs, g, tile_expert, n_used.reshape(1), w_gu, w_down, tg)
    return moe_combine(x, meta, dest, ys)


def _block_diag(w_a, w_b):
    za = jnp.zeros_like(w_a)
    zb = jnp.zeros_like(w_b)
    return jnp.concatenate([jnp.concatenate([w_a, za], axis=1), jnp.concatenate([zb, w_b], axis=1)], axis=0)


def kernel(x, mem, positions, mix_norm, in_proj, shift_mu, decay_bias, decay_up, iclr_bias, iclr_up, gate_up,
           key_kk_scale, key_iclr_scale, bonus_rk, rwkv_gn_w, rwkv_gn_b, vres_down, vres_shift_mu, vres_bias,
           vres_up, q_norm, k_norm, lambda_q1, lambda_k1, lambda_q2, lambda_k2, diff_subln, out_proj, xattn_norm,
           mem_norm, xattn_wq, xattn_wkv, xattn_wo, xattn_q_norm, xattn_k_norm, ffn_norm, dense_w_gu,
           dense_w_down, router, expert_w_gu, expert_w_down):
    b, s, d = x.shape
    depth = mix_norm.shape[0]
    width = decay_bias.shape[1]
    rwkv_cols = shift_mu.shape[1]
    in_cols = in_proj.shape[2]
    vres_rank = vres_down.shape[2] if depth > 1 else 0
    assert decay_up.shape[1] == LANES // 2 and iclr_up.shape[1] == LANES // 2 and gate_up.shape[1] == LANES
    assert rwkv_cols == 3 * width + 2 * LANES and rwkv_cols % LANES == 0 and in_cols % LANES == 0
    assert vres_rank <= LANES
    rot = rope_table(positions)
    xf = x.reshape(b * s, d)
    memf = mem.reshape(-1, d)
    v_first = None
    for l in range(depth):
        w_rwkv = in_proj[l][:, :rwkv_cols]
        vres = None
        if l > 0:
            pad = jnp.zeros((d, LANES - vres_rank), F32)
            w_rwkv = jnp.concatenate([w_rwkv, vres_down[l - 1], pad], axis=1)
            vmu = jnp.zeros((LANES,), F32).at[:vres_rank].set(vres_shift_mu[l - 1])
            vup = jnp.zeros((LANES, width), F32).at[:vres_rank].set(vres_up[l - 1]).astype(BF16)
            vres = (v_first, vmu, vres_bias[l - 1], vup)
        wwa = _block_diag(decay_up[l], iclr_up[l]).astype(BF16)
        r_, lw_, k_, v_, a_, g_ = rwkv_in_proj(xf, s, mix_norm[l], w_rwkv.astype(BF16), width, shift_mu[l],
                                               decay_bias[l], iclr_bias[l], wwa, gate_up[l].astype(BF16), vres)
        if l == 0:
            v_first = v_
        seq3 = lambda t: t.reshape(b, s, width)
        y_rwkv = rwkv_chunked(seq3(r_), seq3(lw_), seq3(k_), seq3(v_), seq3(a_), seq3(g_), key_kk_scale[l],
                              key_iclr_scale[l], bonus_rk[l].reshape(-1), rwkv_gn_w[l], rwkv_gn_b[l])
        lam_init = 0.8 - 0.6 * math.exp(-0.3 * l)
        lam_vecs = jnp.stack([lambda_q1[l], lambda_k1[l], lambda_q2[l], lambda_k2[l]])
        y_diff = diff_attention(xf, b, mix_norm[l], in_proj[l][:, rwkv_cols:].astype(BF16), rot, q_norm[l],
                                k_norm[l], lam_vecs, diff_subln[l], lam_init)
        w_out = out_proj[l].astype(BF16)
        kv = norm_matmul(memf, mem_norm[l], xattn_wkv[l].astype(BF16)).reshape(b, -1, 2 * d)
        xf = mixer_out_cross_attention(xf.reshape(b, s, d), y_rwkv, y_diff, w_out[:width], w_out[width:],
                                       xattn_norm[l], xattn_wq[l].astype(BF16), kv, xattn_q_norm[l],
                                       xattn_k_norm[l], xattn_wo[l].astype(BF16)).reshape(b * s, d)
        if l % 2 == 0:
            xf = ffn_dense(xf, ffn_norm[l], dense_w_gu[l // 2].astype(BF16), dense_w_down[l // 2].astype(BF16))
        else:
            xf = ffn_moe(xf, ffn_norm[l], router[l // 2], expert_w_gu[l // 2].astype(BF16),
                         expert_w_down[l // 2].astype(BF16))
    return xf.reshape(b, s, d)
```

```python
import functools
import math

import jax
import jax.numpy as jnp
from jax import lax
from jax.experimental import pallas as pl
from jax.experimental.pallas import tpu as pltpu

F32 = jnp.float32
BF16 = jnp.bfloat16

EPS = 1e-6
GN_EPS = 64e-5
ROPE_THETA = 500000.0
RWKV_HEAD = 64
DIFF_HEADS = 4
XATTN_HEADS = 4
TOP_K = 2
LANES = 128
CHUNK = 128
VMEM_LIMIT = 56 * 1024 * 1024
RWKV_ROWS_PER_STEP = 2
RWKV_IN_ROW_PARTS = 2
ATTN_ROW_PARTS = 4
WEIGHT_TILE_BYTES = 4 * 1024 * 1024
MOE_GROUP_TILE = 512
MOE_TOKEN_TILE = 1024


def _cparams(*sem):
    return pltpu.CompilerParams(dimension_semantics=sem, vmem_limit_bytes=VMEM_LIMIT)


def _pick(n, prefs):
    for p in prefs:
        if n % p == 0:
            return p
    return n


def _col_tile(k, n):
    best = LANES
    for t in range(LANES, n + 1, LANES):
        if n % t == 0 and k * t * 2 <= WEIGHT_TILE_BYTES:
            best = t
    return best


def _dot(a, b, prec=None):
    return jnp.dot(a, b, preferred_element_type=F32, precision=prec)


def _dot_nt(a, b, prec=None):
    return lax.dot_general(a, b, (((1,), (1,)), ((), ())), preferred_element_type=F32, precision=prec)


def _rms(x, g):
    return x * lax.rsqrt(jnp.mean(x * x, axis=-1, keepdims=True) + EPS) * g


def _norm_mm_kernel(x_ref, g_ref, w_ref, o_ref, h_ref):
    @pl.when(pl.program_id(1) == 0)
    def _():
        h_ref[...] = _rms(x_ref[...], g_ref[...]).astype(BF16)

    o_ref[...] = _dot(h_ref[...], w_ref[...])


def norm_matmul(x, g, w):
    m, k = x.shape
    n = w.shape[1]
    tm = _pick(m, (1024, 512, 256, 128))
    tn = _col_tile(k, n)
    return pl.pallas_call(
        _norm_mm_kernel,
        grid=(m // tm, n // tn),
        in_specs=[pl.BlockSpec((tm, k), lambda i, j: (i, 0)),
                  pl.BlockSpec((1, k), lambda i, j: (0, 0)),
                  pl.BlockSpec((k, tn), lambda i, j: (0, j))],
        out_specs=pl.BlockSpec((tm, tn), lambda i, j: (i, j)),
        out_shape=jax.ShapeDtypeStruct((m, n), F32),
        scratch_shapes=[pltpu.VMEM((tm, k), BF16)],
        compiler_params=_cparams("parallel", "arbitrary"),
    )(x, g.reshape(1, k), w)


def _token_shift(p, prev_row, mu):
    prev = pltpu.roll(p, 1, axis=0)
    rid = lax.broadcasted_iota(jnp.int32, p.shape, 0)
    prev = jnp.where(rid == 0, prev_row, prev)
    return p + (prev - p) * mu


def _rwkv_in_kernel(*refs, width, has_vres, tiles_per_seq):
    if has_vres:
        (x_ref, gn_ref, w_ref, vf_ref, mu_ref, w0_ref, a0_ref, wwa_ref, gup_ref, vmu_ref, vb_ref, vup_ref,
         r_ref, lw_ref, k_ref, v_ref, a_ref, g_ref, carry_ref, carry_v_ref) = refs
    else:
        (x_ref, gn_ref, w_ref, mu_ref, w0_ref, a0_ref, wwa_ref, gup_ref,
         r_ref, lw_ref, k_ref, v_ref, a_ref, g_ref, carry_ref) = refs
    ncol = 3 * width + 2 * LANES
    tm = x_ref.shape[0]
    n_parts = RWKV_IN_ROW_PARTS
    rp = tm // n_parts

    @pl.when(pl.program_id(0) % tiles_per_seq == 0)
    def _():
        carry_ref[...] = jnp.zeros_like(carry_ref)
        if has_vres:
            carry_v_ref[...] = jnp.zeros_like(carry_v_ref)

    projs = [_dot(_rms(x_ref[i * rp:(i + 1) * rp, :], gn_ref[...]).astype(BF16), w_ref[...]) for i in range(n_parts)]
    prev = carry_ref[...]
    prev_v = carry_v_ref[...] if has_vres else None
    for i, proj in enumerate(projs):
        rows = slice(i * rp, (i + 1) * rp)
        sh = _token_shift(proj[:, :ncol], prev, mu_ref[...])
        prev = proj[rp - 1:rp, :ncol]
        r_ref[rows, :] = sh[:, :width].astype(r_ref.dtype)
        k_ref[rows, :] = sh[:, width:2 * width].astype(k_ref.dtype)
        v = sh[:, 2 * width:3 * width]
        dwa = sh[:, 3 * width:3 * width + LANES]
        lane = lax.broadcasted_iota(jnp.int32, dwa.shape, 1)
        dwa = jnp.where(lane < LANES // 2, jnp.tanh(dwa), dwa)
        wa = _dot(dwa.astype(BF16), wwa_ref[...])
        z = -(w0_ref[...] + wa[:, :width])
        softplus = jnp.maximum(z, 0.0) + jnp.log(1.0 + jnp.exp(-jnp.abs(z)))
        lw_ref[rows, :] = -jnp.exp(-softplus - 0.5)
        a_ref[rows, :] = jax.nn.sigmoid(a0_ref[...] + wa[:, width:]).astype(a_ref.dtype)
        dg = jax.nn.sigmoid(sh[:, 3 * width + LANES:3 * width + 2 * LANES])
        g_ref[rows, :] = _dot(dg.astype(BF16), gup_ref[...]).astype(g_ref.dtype)
        if has_vres:
            shv = _token_shift(proj[:, ncol:ncol + LANES], prev_v, vmu_ref[...])
            prev_v = proj[rp - 1:rp, ncol:ncol + LANES]
            mix = jax.nn.sigmoid(vb_ref[...] + _dot(shv.astype(BF16), vup_ref[...]))
            v = v + (vf_ref[rows, :].astype(F32) - v) * mix
        v_ref[rows, :] = v.astype(v_ref.dtype)
    carry_ref[...] = prev
    if has_vres:
        carry_v_ref[...] = prev_v


def rwkv_in_proj(x, seq_len, gn, w, width, mu, w0, a0, wwa, gup, vres=None):
    m, d = x.shape
    tm = _pick(seq_len, (512, 256, 128))
    ncol = 3 * width + 2 * LANES
    assert mu.shape[-1] == ncol and m % seq_len == 0
    row = lambda a: a.reshape(1, -1)
    full = lambda a: pl.BlockSpec(a.shape, lambda i: (0,) * a.ndim)
    rows = lambda c: pl.BlockSpec((tm, c), lambda i: (i, 0))
    params = [row(mu), row(w0), row(a0), wwa, gup]
    inputs = [x, row(gn), w]
    in_specs = [rows(d), full(row(gn)), full(w)]
    scratch = [pltpu.VMEM((1, ncol), F32)]
    if vres is not None:
        v_first, vmu, vb, vup = vres
        inputs.append(v_first)
        in_specs.append(rows(width))
        params += [row(vmu), row(vb), vup]
        scratch.append(pltpu.VMEM((1, LANES), F32))
    in_specs += [full(p) for p in params]
    out = lambda dt: jax.ShapeDtypeStruct((m, width), dt)
    return pl.pallas_call(
        functools.partial(_rwkv_in_kernel, width=width, has_vres=vres is not None, tiles_per_seq=seq_len // tm),
        grid=(m // tm,),
        in_specs=in_specs,
        out_specs=[rows(width)] * 6,
        out_shape=[out(BF16), out(F32), out(BF16), out(BF16), out(BF16), out(BF16)],
        scratch_shapes=scratch,
        compiler_params=_cparams("arbitrary"),
    )(*inputs, *params)


def _rwkv_chunk_kernel(r_ref, lw_ref, k_ref, v_ref, a_ref, g_ref, kk_ref, ka_ref, rk_ref, gw_ref, gb_ref,
                       y_ref, state_ref, *, n_pairs):
    c = CHUNK
    half = LANES // 2

    @pl.when(pl.program_id(1) == 0)
    def _():
        state_ref[...] = jnp.zeros_like(state_ref)

    row = lax.broadcasted_iota(jnp.int32, (c, c), 0)
    col = lax.broadcasted_iota(jnp.int32, (c, c), 1)
    ltri = (row >= col).astype(BF16)
    same_head = (row < half) == (col < half)
    lo = col < half
    row4 = lax.broadcasted_iota(jnp.int32, (c, 4 * c), 0)
    col4 = lax.broadcasted_iota(jnp.int32, (c, 4 * c), 1)
    incl4 = row4 >= (col4 & (c - 1))
    row2 = lax.broadcasted_iota(jnp.int32, (c, 2 * c), 0)
    col2 = lax.broadcasted_iota(jnp.int32, (c, 2 * c), 1)
    strict2 = row2 > (col2 & (c - 1))
    first_block = col2 < c

    def gsum(x):
        s_lo = jnp.sum(jnp.where(lo, x, 0.0), axis=-1, keepdims=True)
        s_hi = jnp.sum(jnp.where(lo, 0.0, x), axis=-1, keepdims=True)
        return jnp.where(lo, s_lo, s_hi)

    def by_head(x):
        return jnp.concatenate([jnp.where(lo, x, 0.0), jnp.where(lo, 0.0, x)], axis=0).astype(BF16)

    def split_bf16(x):
        hi = x.astype(BF16)
        rest = x - hi.astype(F32)
        mid = rest.astype(BF16)
        return hi, mid, (rest - mid.astype(F32)).astype(BF16)

    def stage_factors(bi, p):
        sl = slice(p * LANES, (p + 1) * LANES)
        r, lw, k, v, a = (ref[bi, :, sl].astype(F32) for ref in (r_ref, lw_ref, k_ref, v_ref, a_ref))
        cum3 = _dot(ltri, jnp.concatenate(split_bf16(lw), axis=1))
        cum = cum3[:, :LANES] + cum3[:, LANES:2 * LANES] + cum3[:, 2 * LANES:]
        cmid = cum[c // 2 - 1:c // 2, :]
        clast = cum[c - 1:c, :]
        ci = cum - cmid
        e_neg = jnp.exp(-ci)
        p_mid = jnp.exp(cmid)
        kk = k * kk_ref[:, sl]
        kkn = kk * lax.rsqrt(jnp.maximum(gsum(kk * kk), 1e-24))
        k2 = k * (1.0 + (a - 1.0) * ka_ref[:, sl])
        beta = kkn * a
        al_m = -kkn * jnp.exp(ci - lw)
        r_m = r * jnp.exp(ci)
        e_end = jnp.exp(clast - cum)
        ends = jnp.concatenate([beta * e_end, k2 * e_end], axis=0).astype(BF16)
        al2, r2 = by_head(al_m), by_head(r_m)
        lhs = jnp.concatenate([al2[:c], r2[:c], al2[c:], r2[c:]], axis=0)
        rhs = jnp.concatenate([beta * e_neg, k2 * e_neg], axis=0).astype(BF16)
        big = _dot_nt(lhs, rhs)
        s0 = state_ref[bi * n_pairs + p]
        base = _dot_nt(jnp.concatenate([al_m * p_mid, r_m * p_mid], axis=0).astype(BF16), s0.astype(BF16))
        return dict(big=big, base=base, v2=by_head(v), ends=ends, decayed=s0 * jnp.exp(clast),
                    bonus=gsum(r * k2 * rk_ref[:, sl]) * v)

    def stage_masks(d):
        big = d.pop("big")
        blk = lambda i, j: big[i * c:(i + 1) * c, j * c:(j + 1) * c]
        zero2 = jnp.zeros((c, 2 * c), F32)
        d["pw"] = jnp.where(strict2, jnp.concatenate([blk(0, 0), blk(2, 0)], axis=1), zero2).astype(BF16)
        a_ak = jnp.where(strict2, jnp.concatenate([blk(0, 1), blk(2, 1)], axis=1), zero2).astype(BF16)
        d["a_y"] = jnp.where(incl4, jnp.concatenate([blk(1, 0), blk(3, 0), blk(1, 1), blk(3, 1)], axis=1),
                             0.0).astype(BF16)
        d["x"] = d["base"][:c] + _dot(a_ak, d["v2"])

    def stage_square(d, last):
        pw, x2 = d["pw"], by_head(d["x"])
        if last:
            d["x"] = d["x"] + _dot(pw, x2)
            return
        zero_b = jnp.zeros_like(pw)
        bd = jnp.concatenate([jnp.where(first_block, pw, zero_b), jnp.where(first_block, zero_b, pw)], axis=0)
        res = _dot(pw, jnp.concatenate([bd, x2], axis=1))
        d["pw"] = res[:, :2 * c].astype(BF16)
        d["x"] = d["x"] + res[:, 2 * c:]

    def stage_output(bi, p, d):
        sl = slice(p * LANES, (p + 1) * LANES)
        x = d["x"]
        y = d["base"][c:] + _dot(d["a_y"], jnp.concatenate([by_head(x), d["v2"]], axis=0))
        upd = _dot(jnp.concatenate([x.T, v_ref[bi, :, sl].astype(F32).T], axis=1).astype(BF16), d["ends"])
        state_ref[bi * n_pairs + p] = d["decayed"] + jnp.where(same_head, upd, 0.0)
        mean = gsum(y) * (1.0 / half)
        yc = y - mean
        var = gsum(yc * yc) * (1.0 / half)
        out = yc * lax.rsqrt(var + GN_EPS) * gw_ref[:, sl] + gb_ref[:, sl]
        y_ref[bi, :, sl] = ((out + d["bonus"]) * g_ref[bi, :, sl].astype(F32)).astype(y_ref.dtype)

    ids = [(bi, p) for bi in range(r_ref.shape[0]) for p in range(n_pairs)]
    pairs = [stage_factors(bi, p) for bi, p in ids]
    for d in pairs:
        stage_masks(d)
    n_steps = c.bit_length() - 1
    for step in range(n_steps):
        for d in pairs:
            stage_square(d, last=step + 1 == n_steps)
    for (bi, p), d in zip(ids, pairs):
        stage_output(bi, p, d)


def rwkv_chunked(r, lw, k, v, a, g, k_k, k_a, r_k, gn_w, gn_b):
    b, s, width = r.shape
    assert RWKV_HEAD * 2 == LANES and width % LANES == 0 and s % CHUNK == 0
    n_pairs = width // LANES
    nb = _pick(b, (RWKV_ROWS_PER_STEP, 1))
    seq = pl.BlockSpec((nb, CHUNK, width), lambda i, j: (i, j, 0))
    par = pl.BlockSpec((1, width), lambda i, j: (0, 0))
    row = lambda t: t.reshape(1, width)
    return pl.pallas_call(
        functools.partial(_rwkv_chunk_kernel, n_pairs=n_pairs),
        grid=(b // nb, s // CHUNK),
        in_specs=[seq] * 6 + [par] * 5,
        out_specs=seq,
        out_shape=jax.ShapeDtypeStruct((b, s, width), BF16),
        scratch_shapes=[pltpu.VMEM((nb * n_pairs, LANES, LANES), F32)],
        compiler_params=_cparams("parallel", "arbitrary"),
    )(r, lw, k, v, a, g, row(k_k), row(k_a), row(r_k), row(gn_w), row(gn_b))


def _diff_in_kernel(x_ref, gn_ref, w_ref, rot_ref, qg_ref, kg_ref, qk_ref, v_ref, *, heads):
    half = LANES // 2
    rope_half = half // 8
    proj = _dot(_rms(x_ref[...], gn_ref[...]).astype(BF16), w_ref[...])
    lane = lax.broadcasted_iota(jnp.int32, (x_ref.shape[0], LANES), 1)
    pos = lane & (half - 1)
    rot = rot_ref[...]
    first, second = pos < rope_half, (pos >= rope_half) & (pos < 2 * rope_half)
    cf = jnp.where(first, rot, jnp.where(second, pltpu.roll(rot, rope_half, axis=1), 1.0))
    sn = jnp.where(first, -pltpu.roll(rot, LANES - rope_half, axis=1), jnp.where(second, rot, 0.0))

    g_row = lax.broadcasted_iota(jnp.int32, (LANES, LANES), 0)
    g_col = lax.broadcasted_iota(jnp.int32, (LANES, LANES), 1)
    same_half = ((g_row < half) == (g_col < half)).astype(BF16)
    c_pos = g_col & (half - 1)
    partner = jnp.where(c_pos < rope_half, g_col + rope_half, jnp.where(c_pos < 2 * rope_half, g_col - rope_half, -1))
    swap = (g_row == partner).astype(BF16)

    def norm_rope(x, g):
        ms = _dot((x * x).astype(BF16), same_half) * (1.0 / half)
        x = x * lax.rsqrt(ms + EPS) * g
        return x * cf + _dot(x.astype(BF16), swap) * sn

    ones = jnp.ones((x_ref.shape[0], LANES), BF16)
    for h in range(heads):
        tile = lambda i: proj[:, (i * heads + h) * LANES:(i * heads + h + 1) * LANES]
        q = norm_rope(tile(0), qg_ref[...]) * (half ** -0.5 * math.log2(math.e))
        qk_ref[:, h * LANES:(h + 1) * LANES] = q.astype(BF16)
        qk_ref[:, (heads + h) * LANES:(heads + h + 1) * LANES] = norm_rope(tile(1), kg_ref[...]).astype(BF16)
        v_ref[:, 2 * h * LANES:(2 * h + 1) * LANES] = tile(2).astype(BF16)
        v_ref[:, (2 * h + 1) * LANES:(2 * h + 2) * LANES] = ones


def _diff_attn_kernel(q_ref, k_ref, v_ref, lam_ref, sub_ref, o_ref, *, tq, lam_init):
    half = LANES // 2
    s_len = q_ref.shape[1]
    lv = lam_ref[...]
    lam = (jnp.exp(jnp.sum(lv[0:1] * lv[1:2], axis=-1, keepdims=True))
           - jnp.exp(jnp.sum(lv[2:3] * lv[3:4], axis=-1, keepdims=True)) + lam_init)
    lane = lax.broadcasted_iota(jnp.int32, (tq, LANES), 1)
    n_parts = ATTN_ROW_PARTS
    rp = tq // n_parts
    chains = [(m, part) for part in range(n_parts) for m in range(2)]

    def block(qs, j, carry, masked):
        kb = k_ref[0, j * tq:(j + 1) * tq, :]
        vb = v_ref[0, j * tq:(j + 1) * tq, :]
        scores, probs, maxes, out = {}, {}, {}, {}
        for i in range(len(chains) + 2):
            if i < len(chains):
                m, part = chains[i]
                keys = (part + 1) * rp if masked else tq
                s = _dot_nt(qs[m][part * rp:(part + 1) * rp], kb[:keys])
                if masked:
                    r_id = lax.broadcasted_iota(jnp.int32, s.shape, 0) + part * rp
                    c_id = lax.broadcasted_iota(jnp.int32, s.shape, 1)
                    s = jnp.where(c_id <= r_id, s, -1e30)
                scores[i] = s
            if 0 <= i - 1 < len(chains):
                c = i - 1
                s = scores.pop(c)
                mx = jnp.max(s, axis=-1, keepdims=True)
                maxes[c] = mx if carry is None else jnp.maximum(carry[2 * c], mx)
                probs[c] = jnp.exp2((s - maxes[c]).astype(BF16))
            if 0 <= i - 2 < len(chains):
                c = i - 2
                pr = probs.pop(c)
                pv = _dot(pr, vb[:pr.shape[1]])
                out[c] = pv if carry is None else jnp.exp2(carry[2 * c] - maxes[c]) * carry[2 * c + 1] + pv
        return tuple(t for c in range(len(chains)) for t in (maxes[c], out[c]))

    for qi in range(s_len // tq):
        q = q_ref[0, qi * tq:(qi + 1) * tq, :]
        zero = jnp.zeros_like(q)
        qs = [jnp.where(lane < half, q, zero), jnp.where(lane < half, zero, q)]
        carry = block(qs, qi, None, True)
        for j in range(qi):
            carry = block(qs, j, carry, False)
        acc1 = jnp.concatenate([carry[2 * c + 1] for c, (m, _) in enumerate(chains) if m == 0], axis=0)
        acc2 = jnp.concatenate([carry[2 * c + 1] for c, (m, _) in enumerate(chains) if m == 1], axis=0)
        o = acc1[:, :LANES] / acc1[:, LANES:] - lam * (acc2[:, :LANES] / acc2[:, LANES:])
        o_ref[0, qi * tq:(qi + 1) * tq, :] = (_rms(o, sub_ref[...]) * (1.0 - lam_init)).astype(o_ref.dtype)


def diff_attention(x, b, gn, w, rot, q_g, k_g, lam_vecs, subln_g, lam_init):
    m, d = x.shape
    s = m // b
    h = DIFF_HEADS
    assert w.shape[1] == 3 * h * LANES
    tm = _pick(m, (512, 256, 128))
    tile2 = lambda g: jnp.tile(g, 2).reshape(1, LANES)
    rows = lambda c: pl.BlockSpec((tm, c), lambda i: (i, 0))
    full = lambda a: pl.BlockSpec(a.shape, lambda i: (0,) * a.ndim)
    gspec2 = pl.BlockSpec((1, LANES), lambda i: (0, 0))
    qk, vext = pl.pallas_call(
        functools.partial(_diff_in_kernel, heads=h),
        grid=(m // tm,),
        in_specs=[rows(d), pl.BlockSpec((1, d), lambda i: (0, 0)), full(w), rows(LANES), gspec2, gspec2],
        out_specs=[rows(2 * h * LANES), rows(2 * h * LANES)],
        out_shape=[jax.ShapeDtypeStruct((m, 2 * h * LANES), BF16)] * 2,
        compiler_params=_cparams("parallel"),
    )(x, gn.reshape(1, d), w, rot, tile2(q_g), tile2(k_g))
    qk = qk.reshape(b, s, 2 * h * LANES)
    vext = vext.reshape(b, s, 2 * h * LANES)

    tq = _pick(s, (1024, 512, 256, 128))
    gspec = pl.BlockSpec((1, LANES), lambda bi, hi: (0, 0))
    return pl.pallas_call(
        functools.partial(_diff_attn_kernel, tq=tq, lam_init=lam_init),
        grid=(b, h),
        in_specs=[pl.BlockSpec((1, s, LANES), lambda bi, hi: (bi, 0, hi)),
                  pl.BlockSpec((1, s, LANES), lambda bi, hi: (bi, 0, h + hi)),
                  pl.BlockSpec((1, s, 2 * LANES), lambda bi, hi: (bi, 0, hi)),
                  pl.BlockSpec(lam_vecs.shape, lambda bi, hi: (0, 0)),
                  gspec],
        out_specs=pl.BlockSpec((1, s, LANES), lambda bi, hi: (bi, 0, hi)),
        out_shape=jax.ShapeDtypeStruct((b, s, h * LANES), BF16),
        compiler_params=_cparams("parallel", "parallel"),
    )(qk, qk, vext, lam_vecs, subln_g.reshape(1, LANES))


def rope_table(positions):
    half = LANES // 2
    rope_dims = half // 4
    inv_freq = ROPE_THETA ** (-jnp.arange(0, rope_dims, 2, dtype=F32) / rope_dims)
    ang = positions.astype(F32).reshape(-1, 1) * inv_freq
    one_half = jnp.concatenate([jnp.cos(ang), jnp.sin(ang), jnp.zeros((ang.shape[0], half - rope_dims), F32)], axis=-1)
    return jnp.concatenate([one_half, one_half], axis=-1)


def _xattn_kernel(x_ref, ya_ref, yb_ref, wa_ref, wb_ref, gn_ref, wq_ref, kv_ref, qg_ref, kg_ref, wo_ref, o_ref,
                  *, heads):
    x = x_ref[0] + _dot(ya_ref[0], wa_ref[...]) + _dot(yb_ref[0], wb_ref[...])
    d = x.shape[-1] // heads
    q_all = _dot(_rms(x, gn_ref[...]).astype(BF16), wq_ref[...])
    outs = []
    for h in range(heads):
        q = _rms(q_all[:, h * d:(h + 1) * d], qg_ref[...]) * (d ** -0.5)
        k = _rms(kv_ref[0, :, h * d:(h + 1) * d], kg_ref[...])
        v = kv_ref[0, :, (heads + h) * d:(heads + h + 1) * d]
        s = _dot_nt(q.astype(BF16), k.astype(BF16))
        s = s - jnp.max(s, axis=-1, keepdims=True)
        pr = jnp.exp(s)
        o = _dot(pr.astype(BF16), v.astype(BF16)) / jnp.sum(pr, axis=-1, keepdims=True)
        outs.append(o.astype(BF16))
    o_ref[0] = x + _dot(jnp.concatenate(outs, axis=1), wo_ref[...])


def mixer_out_cross_attention(x, ya, yb, wa, wb, gn, wq, kv, q_g, k_g, wo):
    b, s, dm = x.shape
    mlen = kv.shape[1]
    ts = _pick(s, (1024, 512, 256, 128))
    const = lambda a: pl.BlockSpec(a.shape, lambda i, j: (0,) * a.ndim)
    seq = lambda a: pl.BlockSpec((1, ts, a.shape[2]), lambda i, j: (i, j, 0))
    row = lambda a: a.reshape(1, -1)
    return pl.pallas_call(
        functools.partial(_xattn_kernel, heads=XATTN_HEADS),
        grid=(b, s // ts),
        in_specs=[seq(x), seq(ya), seq(yb), const(wa), const(wb),
                  const(row(gn)), const(wq),
                  pl.BlockSpec((1, mlen, 2 * dm), lambda i, j: (i, 0, 0)),
                  const(row(q_g)), const(row(k_g)), const(wo)],
        out_specs=seq(x),
        out_shape=jax.ShapeDtypeStruct((b, s, dm), F32),
        compiler_params=_cparams("parallel", "arbitrary"),
    )(x, ya, yb, wa, wb, row(gn), wq, kv, row(q_g), row(k_g), wo)


def _swiglu(h, wgu, wd):
    f = wd.shape[0]
    tf = _col_tile(h.shape[1], f)
    y = None
    for lo in range(0, f, tf):
        gate = _dot(h, wgu[:, lo:lo + tf])
        up = _dot(h, wgu[:, f + lo:f + lo + tf])
        part = _dot((gate * jax.nn.sigmoid(gate) * up).astype(BF16), wd[lo:lo + tf, :])
        y = part if y is None else y + part
    return y


def _ffn_kernel(x_ref, g_ref, wgu_ref, wd_ref, o_ref):
    x = x_ref[...]
    o_ref[...] = x + _swiglu(_rms(x, g_ref[...]).astype(BF16), wgu_ref, wd_ref)


def ffn_dense(x, g, w_gu, w_down):
    m, d = x.shape
    tm = _pick(m, (1024, 512, 256, 128))
    resident = lambda a: pl.BlockSpec(a.shape, lambda i: (0,) * a.ndim, pipeline_mode=pl.Buffered(1))
    return pl.pallas_call(
        _ffn_kernel,
        grid=(m // tm,),
        in_specs=[pl.BlockSpec((tm, d), lambda i: (i, 0)),
                  pl.BlockSpec((1, d), lambda i: (0, 0)),
                  resident(w_gu), resident(w_down)],
        out_specs=pl.BlockSpec((tm, d), lambda i: (i, 0)),
        out_shape=jax.ShapeDtypeStruct((m, d), F32),
        compiler_params=_cparams("parallel"),
    )(x, g.reshape(1, d), w_gu, w_down)


META_I1, META_I2, META_W1, META_W2, META_R1, META_R2 = range(6)


def _lane_pick(x, lane, idx):
    return jnp.sum(jnp.where(lane == idx, x, 0.0), axis=-1, keepdims=True)


def _router_kernel(x_ref, g_ref, wr_ref, meta_ref, cnt_ref, tri_ref, *, n_experts):
    tm = x_ref.shape[0]

    @pl.when(pl.program_id(0) == 0)
    def _():
        cnt_ref[...] = jnp.zeros_like(cnt_ref)
        r_id = lax.broadcasted_iota(jnp.int32, (tm, tm), 0)
        c_id = lax.broadcasted_iota(jnp.int32, (tm, tm), 1)
        tri_ref[...] = (r_id > c_id).astype(BF16)

    split = lambda t: (t.astype(BF16), (t - t.astype(BF16).astype(F32)).astype(BF16))
    h_hi, h_lo = split(_rms(x_ref[...], g_ref[...]))
    w_hi, w_lo = split(wr_ref[...])
    logits = _dot(h_hi, w_hi) + (_dot(h_hi, w_lo) + _dot(h_lo, w_hi))
    lane = lax.broadcasted_iota(jnp.int32, logits.shape, 1).astype(F32)
    neg = -jnp.inf
    l1 = jnp.where(lane < n_experts, logits, neg)
    m1 = jnp.max(l1, axis=-1, keepdims=True)
    i1 = jnp.min(jnp.where(l1 == m1, lane, float(LANES)), axis=-1, keepdims=True)
    l2 = jnp.where(lane == i1, neg, l1)
    m2 = jnp.max(l2, axis=-1, keepdims=True)
    i2 = jnp.min(jnp.where(l2 == m2, lane, float(LANES)), axis=-1, keepdims=True)
    e2 = jnp.exp(m2 - m1)
    w1 = 1.0 / (1.0 + e2)
    onehot = jnp.where((lane == i1) | (lane == i2), 1.0, 0.0)
    before = _dot(tri_ref[...], onehot.astype(BF16)) + cnt_ref[0:1, :]
    cnt_ref[0:1, :] += jnp.sum(onehot, axis=0, keepdims=True)
    meta = jnp.zeros_like(logits)
    for slot, val in ((META_I1, i1), (META_I2, i2), (META_W1, w1), (META_W2, e2 * w1),
                      (META_R1, _lane_pick(before, lane, i1)), (META_R2, _lane_pick(before, lane, i2))):
        meta = jnp.where(lane == slot, val, meta)
    meta_ref[...] = meta


def router_top2(x, g, w_router):
    m, d = x.shape
    n_experts = w_router.shape[1]
    wr = jnp.zeros((d, LANES), F32).at[:, :n_experts].set(w_router)
    tm = _pick(m, (1024, 512, 256, 128))
    return pl.pallas_call(
        functools.partial(_router_kernel, n_experts=n_experts),
        grid=(m // tm,),
        in_specs=[pl.BlockSpec((tm, d), lambda i: (i, 0)),
                  pl.BlockSpec((1, d), lambda i: (0, 0)),
                  pl.BlockSpec((d, LANES), lambda i: (0, 0))],
        out_specs=[pl.BlockSpec((tm, LANES), lambda i: (i, 0)),
                   pl.BlockSpec((8, LANES), lambda i: (0, 0))],
        out_shape=[jax.ShapeDtypeStruct((m, LANES), F32), jax.ShapeDtypeStruct((8, LANES), F32)],
        scratch_shapes=[pltpu.VMEM((tm, tm), BF16)],
        compiler_params=_cparams("arbitrary"),
    )(x, g.reshape(1, d), wr)


def _idx_copy(dest_hbm, idx_smem, isem, tile, slot):
    return pltpu.make_async_copy(dest_hbm.at[tile], idx_smem.at[slot], isem.at[slot])


def _dispatch_kernel(ends_ref, dest_hbm, x_ref, xs_hbm, idx_smem, zbuf, isem, sem, zsem, *, tm, tg):
    i = pl.program_id(0)
    slot = i % 2
    n_experts = ends_ref.shape[0]
    n_out_tiles = xs_hbm.shape[0] // tg

    @pl.when(i == 0)
    def _():
        _idx_copy(dest_hbm, idx_smem, isem, 0, 0).start()
        zbuf[...] = jnp.zeros_like(zbuf)
        fills = []
        for e in range(n_experts):
            group_start = ends_ref[e - 1] if e > 0 else 0
            fills.append((ends_ref[e] > group_start, ends_ref[e] - tg))
            tail = ends_ref[n_experts - 1] + e * tg
            fills.append((tail < n_out_tiles * tg, tail))
        zero_fill = lambda start: pltpu.make_async_copy(zbuf, xs_hbm.at[pl.ds(pl.multiple_of(start, tg), tg)], zsem)
        for cond, start in fills:
            @pl.when(cond)
            def _():
                zero_fill(start).start()
        for cond, start in fills:
            @pl.when(cond)
            def _():
                zero_fill(start).wait()

    @pl.when(i + 1 < pl.num_programs(0))
    def _():
        _idx_copy(dest_hbm, idx_smem, isem, i + 1, 1 - slot).start()

    _idx_copy(dest_hbm, idx_smem, isem, i, slot).wait()

    def row_copy(t, k):
        d = idx_smem[slot, TOP_K * t + k]
        return pltpu.make_async_copy(x_ref.at[pl.ds(t, 1)], xs_hbm.at[pl.ds(d, 1)], sem)

    def issue(t, carry):
        for k in range(TOP_K):
            row_copy(t, k).start()
        return carry

    lax.fori_loop(0, tm, issue, 0, unroll=8)
    for _ in range(TOP_K):
        pltpu.make_async_copy(x_ref, xs_hbm.at[pl.ds(0, tm)], sem).wait()


def moe_dispatch(x, dest, ends, n_rows, tg):
    m, d = x.shape
    n_tiles, per_tile = dest.shape
    tm = per_tile // TOP_K
    grid_spec = pltpu.PrefetchScalarGridSpec(
        num_scalar_prefetch=1,
        grid=(n_tiles,),
        in_specs=[pl.BlockSpec(memory_space=pl.ANY),
                  pl.BlockSpec((tm, d), lambda i, ends: (i, 0))],
        out_specs=pl.BlockSpec(memory_space=pl.ANY),
        scratch_shapes=[pltpu.SMEM((2, per_tile), jnp.int32), pltpu.VMEM((tg, d), F32),
                        pltpu.SemaphoreType.DMA((2,)), pltpu.SemaphoreType.DMA(()), pltpu.SemaphoreType.DMA(())],
    )
    return pl.pallas_call(
        functools.partial(_dispatch_kernel, tm=tm, tg=tg),
        grid_spec=grid_spec,
        out_shape=jax.ShapeDtypeStruct((n_rows, d), F32),
        compiler_params=_cparams("arbitrary"),
    )(ends, dest, x)


def _moe_ffn_kernel(te_ref, nu_ref, x_ref, g_ref, wgu_ref, wd_ref, o_ref):
    del te_ref

    @pl.when(pl.program_id(0) < nu_ref[0])
    def _():
        o_ref[...] = _swiglu(_rms(x_ref[...], g_ref[...]).astype(BF16), wgu_ref.at[0], wd_ref.at[0])

    @pl.when(pl.program_id(0) >= nu_ref[0])
    def _():
        o_ref[...] = jnp.zeros_like(o_ref)


def moe_grouped_ffn(xs, g, tile_expert, n_used, w_gu, w_down, tm):
    p, d = xs.shape
    f = w_down.shape[1]
    last = lambda n, nu: jnp.minimum(n, nu[0] - 1)
    grid_spec = pltpu.PrefetchScalarGridSpec(
        num_scalar_prefetch=2,
        grid=(p // tm,),
        in_specs=[pl.BlockSpec((tm, d), lambda n, te, nu: (last(n, nu), 0)),
                  pl.BlockSpec((1, d), lambda n, te, nu: (0, 0)),
                  pl.BlockSpec((1, d, 2 * f), lambda n, te, nu: (te[n], 0, 0), pipeline_mode=pl.Buffered(1)),
                  pl.BlockSpec((1, f, d), lambda n, te, nu: (te[n], 0, 0), pipeline_mode=pl.Buffered(1))],
        out_specs=pl.BlockSpec((tm, d), lambda n, te, nu: (n, 0)),
    )
    return pl.pallas_call(
        _moe_ffn_kernel,
        grid_spec=grid_spec,
        out_shape=jax.ShapeDtypeStruct((p, d), F32),
        compiler_params=_cparams("arbitrary"),
    )(tile_expert, n_used, xs, g.reshape(1, d), w_gu, w_down)


def _combine_kernel(dest_hbm, x_ref, meta_ref, ys_hbm, o_ref, idx_smem, ybuf, isem, sem, *, tm, n_tiles):
    i = pl.program_id(0)
    slot = i % 2

    def row_copy(sl, t, k):
        d = idx_smem[sl, TOP_K * t + k]
        return pltpu.make_async_copy(ys_hbm.at[pl.ds(d, 1)], ybuf.at[sl, k, pl.ds(t, 1)], sem.at[sl])

    def gather_tile(tile, sl):
        _idx_copy(dest_hbm, idx_smem, isem, tile, sl).wait()

        def issue(t, carry):
            for k in range(TOP_K):
                row_copy(sl, t, k).start()
            return carry

        lax.fori_loop(0, tm, issue, 0, unroll=8)

    @pl.when(i == 0)
    def _():
        _idx_copy(dest_hbm, idx_smem, isem, 0, 0).start()
        gather_tile(0, 0)
        if n_tiles > 1:
            _idx_copy(dest_hbm, idx_smem, isem, 1, 1).start()

    @pl.when(i + 1 < pl.num_programs(0))
    def _():
        gather_tile(i + 1, 1 - slot)

    @pl.when(i + 2 < pl.num_programs(0))
    def _():
        _idx_copy(dest_hbm, idx_smem, isem, i + 2, slot).start()

    for k in range(TOP_K):
        pltpu.make_async_copy(ys_hbm.at[pl.ds(0, tm)], ybuf.at[slot, k], sem.at[slot]).wait()
    meta = meta_ref[...]
    lane = lax.broadcasted_iota(jnp.int32, meta.shape, 1)
    w1 = _lane_pick(meta, lane, META_W1)
    w2 = _lane_pick(meta, lane, META_W2)
    o_ref[...] = x_ref[...] + w1 * ybuf[slot, 0] + w2 * ybuf[slot, 1]


def moe_combine(x, meta, dest, ys):
    m, d = x.shape
    n_tiles, per_tile = dest.shape
    tm = per_tile // TOP_K
    return pl.pallas_call(
        functools.partial(_combine_kernel, tm=tm, n_tiles=n_tiles),
        grid=(n_tiles,),
        in_specs=[pl.BlockSpec(memory_space=pl.ANY),
                  pl.BlockSpec((tm, d), lambda i: (i, 0)),
                  pl.BlockSpec((tm, LANES), lambda i: (i, 0)),
                  pl.BlockSpec(memory_space=pl.ANY)],
        out_specs=pl.BlockSpec((tm, d), lambda i: (i, 0)),
        out_shape=jax.ShapeDtypeStruct((m, d), F32),
        scratch_shapes=[pltpu.SMEM((2, per_tile), jnp.int32), pltpu.VMEM((2, TOP_K, tm, d), F32),
                        pltpu.SemaphoreType.DMA((2,)), pltpu.SemaphoreType.DMA((2,))],
        compiler_params=_cparams("arbitrary"),
    )(dest, x, meta, ys)


def ffn_moe(x, g, w_router, w_gu, w_down):
    m, d = x.shape
    n_experts = w_down.shape[0]
    tg = _pick(m, (MOE_GROUP_TILE, 256, 128))
    td = _pick(m, (MOE_TOKEN_TILE, 256, 128))
    meta, counts = router_top2(x, g, w_router)
    counts = counts[0, :n_experts].astype(jnp.int32)
    padded = (counts + tg - 1) // tg * tg
    ends = jnp.cumsum(padded)
    offsets = ends - padded
    n_tiles = (TOP_K * m) // tg + n_experts
    n_used = (ends[-1] // tg).astype(jnp.int32)
    tile_id = jnp.minimum(jnp.arange(n_tiles, dtype=jnp.int32), n_used - 1)
    tile_expert = jnp.sum(ends[None, :] <= (tile_id * tg)[:, None], axis=1).astype(jnp.int32)
    picks = meta[:, META_I1:META_I2 + 1].astype(jnp.int32)
    ranks = meta[:, META_R1:META_R2 + 1].astype(jnp.int32)
    dest = (offsets[picks] + ranks).reshape(m // td, TOP_K * td)
    xs = moe_dispatch(x, dest, ends.astype(jnp.int32), n_tiles * tg, tg)
    ys = moe_grouped_ffn(xs, g, tile_expert, n_used.reshape(1), w_gu, w_down, tg)
    return moe_combine(x, meta, dest, ys)


def _block_diag(w_a, w_b):
    za = jnp.zeros_like(w_a)
    zb = jnp.zeros_like(w_b)
    return jnp.concatenate([jnp.concatenate([w_a, za], axis=1), jnp.concatenate([zb, w_b], axis=1)], axis=0)


def kernel(x, mem, positions, mix_norm, in_proj, shift_mu, decay_bias, decay_up, iclr_bias, iclr_up, gate_up,
           key_kk_scale, key_iclr_scale, bonus_rk, rwkv_gn_w, rwkv_gn_b, vres_down, vres_shift_mu, vres_bias,
           vres_up, q_norm, k_norm, lambda_q1, lambda_k1, lambda_q2, lambda_k2, diff_subln, out_proj, xattn_norm,
           mem_norm, xattn_wq, xattn_wkv, xattn_wo, xattn_q_norm, xattn_k_norm, ffn_norm, dense_w_gu,
           dense_w_down, router, expert_w_gu, expert_w_down):
    b, s, d = x.shape
    depth = mix_norm.shape[0]
    width = decay_bias.shape[1]
    rwkv_cols = shift_mu.shape[1]
    in_cols = in_proj.shape[2]
    vres_rank = vres_down.shape[2] if depth > 1 else 0
    assert decay_up.shape[1] == LANES // 2 and iclr_up.shape[1] == LANES // 2 and gate_up.shape[1] == LANES
    assert rwkv_cols == 3 * width + 2 * LANES and rwkv_cols % LANES == 0 and in_cols % LANES == 0
    assert vres_rank <= LANES
    rot = rope_table(positions)
    xf = x.reshape(b * s, d)
    memf = mem.reshape(-1, d)
    v_first = None
    for l in range(depth):
        w_rwkv = in_proj[l][:, :rwkv_cols]
        vres = None
        if l > 0:
            pad = jnp.zeros((d, LANES - vres_rank), F32)
            w_rwkv = jnp.concatenate([w_rwkv, vres_down[l - 1], pad], axis=1)
            vmu = jnp.zeros((LANES,), F32).at[:vres_rank].set(vres_shift_mu[l - 1])
            vup = jnp.zeros((LANES, width), F32).at[:vres_rank].set(vres_up[l - 1]).astype(BF16)
            vres = (v_first, vmu, vres_bias[l - 1], vup)
        wwa = _block_diag(decay_up[l], iclr_up[l]).astype(BF16)
        r_, lw_, k_, v_, a_, g_ = rwkv_in_proj(xf, s, mix_norm[l], w_rwkv.astype(BF16), width, shift_mu[l],
                                               decay_bias[l], iclr_bias[l], wwa, gate_up[l].astype(BF16), vres)
        if l == 0:
            v_first = v_
        seq3 = lambda t: t.reshape(b, s, width)
        y_rwkv = rwkv_chunked(seq3(r_), seq3(lw_), seq3(k_), seq3(v_), seq3(a_), seq3(g_), key_kk_scale[l],
                              key_iclr_scale[l], bonus_rk[l].reshape(-1), rwkv_gn_w[l], rwkv_gn_b[l])
        lam_init = 0.8 - 0.6 * math.exp(-0.3 * l)
        lam_vecs = jnp.stack([lambda_q1[l], lambda_k1[l], lambda_q2[l], lambda_k2[l]])
        y_diff = diff_attention(xf, b, mix_norm[l], in_proj[l][:, rwkv_cols:].astype(BF16), rot, q_norm[l],
                                k_norm[l], lam_vecs, diff_subln[l], lam_init)
        w_out = out_proj[l].astype(BF16)
        kv = norm_matmul(memf, mem_norm[l], xattn_wkv[l].astype(BF16)).reshape(b, -1, 2 * d)
        xf = mixer_out_cross_attention(xf.reshape(b, s, d), y_rwkv, y_diff, w_out[:width], w_out[width:],
                                       xattn_norm[l], xattn_wq[l].astype(BF16), kv, xattn_q_norm[l],
                                       xattn_k_norm[l], xattn_wo[l].astype(BF16)).reshape(b * s, d)
        if l % 2 == 0:
            xf = ffn_dense(xf, ffn_norm[l], dense_w_gu[l // 2].astype(BF16), dense_w_down[l // 2].astype(BF16))
        else:
            xf = ffn_moe(xf, ffn_norm[l], router[l // 2], expert_w_gu[l // 2].astype(BF16),
                         expert_w_down[l // 2].astype(BF16))
    return xf.reshape(b, s, d)
```

```python
import functools
import math

import jax
import jax.numpy as jnp
from jax import lax
from jax.experimental import pallas as pl
from jax.experimental.pallas import tpu as pltpu

F32 = jnp.float32
BF16 = jnp.bfloat16

EPS = 1e-6
GN_EPS = 64e-5
ROPE_THETA = 500000.0
RWKV_HEAD = 64
DIFF_HEADS = 4
XATTN_HEADS = 4
TOP_K = 2
LANES = 128
CHUNK = 128
VMEM_LIMIT = 56 * 1024 * 1024
RWKV_ROWS_PER_STEP = 4
RWKV_IN_ROW_PARTS = 2
ATTN_ROW_PARTS = 4
WEIGHT_TILE_BYTES = 4 * 1024 * 1024
MOE_GROUP_TILE = 512
MOE_TOKEN_TILE = 1024


def _cparams(*sem):
    return pltpu.CompilerParams(dimension_semantics=sem, vmem_limit_bytes=VMEM_LIMIT)


def _pick(n, prefs):
    for p in prefs:
        if n % p == 0:
            return p
    return n


def _col_tile(k, n):
    best = LANES
    for t in range(LANES, n + 1, LANES):
        if n % t == 0 and k * t * 2 <= WEIGHT_TILE_BYTES:
            best = t
    return best


def _dot(a, b, prec=None):
    return jnp.dot(a, b, preferred_element_type=F32, precision=prec)


def _dot_nt(a, b, prec=None):
    return lax.dot_general(a, b, (((1,), (1,)), ((), ())), preferred_element_type=F32, precision=prec)


def _rms(x, g):
    return x * lax.rsqrt(jnp.mean(x * x, axis=-1, keepdims=True) + EPS) * g


def _norm_mm_kernel(x_ref, g_ref, w_ref, o_ref, h_ref):
    @pl.when(pl.program_id(1) == 0)
    def _():
        h_ref[...] = _rms(x_ref[...], g_ref[...]).astype(BF16)

    o_ref[...] = _dot(h_ref[...], w_ref[...])


def norm_matmul(x, g, w):
    m, k = x.shape
    n = w.shape[1]
    tm = _pick(m, (1024, 512, 256, 128))
    tn = _col_tile(k, n)
    return pl.pallas_call(
        _norm_mm_kernel,
        grid=(m // tm, n // tn),
        in_specs=[pl.BlockSpec((tm, k), lambda i, j: (i, 0)),
                  pl.BlockSpec((1, k), lambda i, j: (0, 0)),
                  pl.BlockSpec((k, tn), lambda i, j: (0, j))],
        out_specs=pl.BlockSpec((tm, tn), lambda i, j: (i, j)),
        out_shape=jax.ShapeDtypeStruct((m, n), F32),
        scratch_shapes=[pltpu.VMEM((tm, k), BF16)],
        compiler_params=_cparams("parallel", "arbitrary"),
    )(x, g.reshape(1, k), w)


def _token_shift(p, prev_row, mu):
    prev = pltpu.roll(p, 1, axis=0)
    rid = lax.broadcasted_iota(jnp.int32, p.shape, 0)
    prev = jnp.where(rid == 0, prev_row, prev)
    return p + (prev - p) * mu


def _rwkv_in_kernel(*refs, width, has_vres, tiles_per_seq):
    if has_vres:
        (x_ref, gn_ref, w_ref, vf_ref, mu_ref, w0_ref, a0_ref, wwa_ref, gup_ref, vmu_ref, vb_ref, vup_ref,
         r_ref, lw_ref, k_ref, v_ref, a_ref, g_ref, carry_ref, carry_v_ref) = refs
    else:
        (x_ref, gn_ref, w_ref, mu_ref, w0_ref, a0_ref, wwa_ref, gup_ref,
         r_ref, lw_ref, k_ref, v_ref, a_ref, g_ref, carry_ref) = refs
    ncol = 3 * width + 2 * LANES
    tm = x_ref.shape[0]
    n_parts = RWKV_IN_ROW_PARTS
    rp = tm // n_parts

    @pl.when(pl.program_id(0) % tiles_per_seq == 0)
    def _():
        carry_ref[...] = jnp.zeros_like(carry_ref)
        if has_vres:
            carry_v_ref[...] = jnp.zeros_like(carry_v_ref)

    projs = [_dot(_rms(x_ref[i * rp:(i + 1) * rp, :], gn_ref[...]).astype(BF16), w_ref[...]) for i in range(n_parts)]
    prev = carry_ref[...]
    prev_v = carry_v_ref[...] if has_vres else None
    for i, proj in enumerate(projs):
        rows = slice(i * rp, (i + 1) * rp)
        sh = _token_shift(proj[:, :ncol], prev, mu_ref[...])
        prev = proj[rp - 1:rp, :ncol]
        r_ref[rows, :] = sh[:, :width].astype(r_ref.dtype)
        k_ref[rows, :] = sh[:, width:2 * width].astype(k_ref.dtype)
        v = sh[:, 2 * width:3 * width]
        dwa = sh[:, 3 * width:3 * width + LANES]
        lane = lax.broadcasted_iota(jnp.int32, dwa.shape, 1)
        dwa = jnp.where(lane < LANES // 2, jnp.tanh(dwa), dwa)
        wa = _dot(dwa.astype(BF16), wwa_ref[...])
        z = -(w0_ref[...] + wa[:, :width])
        softplus = jnp.maximum(z, 0.0) + jnp.log(1.0 + jnp.exp(-jnp.abs(z)))
        lw_ref[rows, :] = -jnp.exp(-softplus - 0.5)
        a_ref[rows, :] = jax.nn.sigmoid(a0_ref[...] + wa[:, width:]).astype(a_ref.dtype)
        dg = jax.nn.sigmoid(sh[:, 3 * width + LANES:3 * width + 2 * LANES])
        g_ref[rows, :] = _dot(dg.astype(BF16), gup_ref[...]).astype(g_ref.dtype)
        if has_vres:
            shv = _token_shift(proj[:, ncol:ncol + LANES], prev_v, vmu_ref[...])
            prev_v = proj[rp - 1:rp, ncol:ncol + LANES]
            mix = jax.nn.sigmoid(vb_ref[...] + _dot(shv.astype(BF16), vup_ref[...]))
            v = v + (vf_ref[rows, :].astype(F32) - v) * mix
        v_ref[rows, :] = v.astype(v_ref.dtype)
    carry_ref[...] = prev
    if has_vres:
        carry_v_ref[...] = prev_v


def rwkv_in_proj(x, seq_len, gn, w, width, mu, w0, a0, wwa, gup, vres=None):
    m, d = x.shape
    tm = _pick(seq_len, (512, 256, 128))
    ncol = 3 * width + 2 * LANES
    assert mu.shape[-1] == ncol and m % seq_len == 0
    row = lambda a: a.reshape(1, -1)
    full = lambda a: pl.BlockSpec(a.shape, lambda i: (0,) * a.ndim)
    rows = lambda c: pl.BlockSpec((tm, c), lambda i: (i, 0))
    params = [row(mu), row(w0), row(a0), wwa, gup]
    inputs = [x, row(gn), w]
    in_specs = [rows(d), full(row(gn)), full(w)]
    scratch = [pltpu.VMEM((1, ncol), F32)]
    if vres is not None:
        v_first, vmu, vb, vup = vres
        inputs.append(v_first)
        in_specs.append(rows(width))
        params += [row(vmu), row(vb), vup]
        scratch.append(pltpu.VMEM((1, LANES), F32))
    in_specs += [full(p) for p in params]
    out = lambda dt: jax.ShapeDtypeStruct((m, width), dt)
    return pl.pallas_call(
        functools.partial(_rwkv_in_kernel, width=width, has_vres=vres is not None, tiles_per_seq=seq_len // tm),
        grid=(m // tm,),
        in_specs=in_specs,
        out_specs=[rows(width)] * 6,
        out_shape=[out(BF16), out(F32), out(BF16), out(BF16), out(BF16), out(BF16)],
        scratch_shapes=scratch,
        compiler_params=_cparams("arbitrary"),
    )(*inputs, *params)


def _rwkv_chunk_kernel(r_ref, lw_ref, k_ref, v_ref, a_ref, g_ref, kk_ref, ka_ref, rk_ref, gw_ref, gb_ref,
                       y_ref, state_ref, *, n_pairs):
    c = CHUNK
    half = LANES // 2

    @pl.when(pl.program_id(1) == 0)
    def _():
        state_ref[...] = jnp.zeros_like(state_ref)

    row = lax.broadcasted_iota(jnp.int32, (c, c), 0)
    col = lax.broadcasted_iota(jnp.int32, (c, c), 1)
    ltri = (row >= col).astype(BF16)
    same_head = (row < half) == (col < half)
    lo = col < half
    row4 = lax.broadcasted_iota(jnp.int32, (c, 4 * c), 0)
    col4 = lax.broadcasted_iota(jnp.int32, (c, 4 * c), 1)
    incl4 = row4 >= (col4 & (c - 1))
    row2 = lax.broadcasted_iota(jnp.int32, (c, 2 * c), 0)
    col2 = lax.broadcasted_iota(jnp.int32, (c, 2 * c), 1)
    strict2 = row2 > (col2 & (c - 1))
    first_block = col2 < c

    def gsum(x):
        s_lo = jnp.sum(jnp.where(lo, x, 0.0), axis=-1, keepdims=True)
        s_hi = jnp.sum(jnp.where(lo, 0.0, x), axis=-1, keepdims=True)
        return jnp.where(lo, s_lo, s_hi)

    def by_head(x):
        return jnp.concatenate([jnp.where(lo, x, 0.0), jnp.where(lo, 0.0, x)], axis=0).astype(BF16)

    def split_bf16(x):
        hi = x.astype(BF16)
        rest = x - hi.astype(F32)
        mid = rest.astype(BF16)
        return hi, mid, (rest - mid.astype(F32)).astype(BF16)

    def stage_factors(bi, p):
        sl = slice(p * LANES, (p + 1) * LANES)
        r, lw, k, v, a = (ref[bi, :, sl].astype(F32) for ref in (r_ref, lw_ref, k_ref, v_ref, a_ref))
        cum3 = _dot(ltri, jnp.concatenate(split_bf16(lw), axis=1))
        cum = cum3[:, :LANES] + cum3[:, LANES:2 * LANES] + cum3[:, 2 * LANES:]
        cmid = cum[c // 2 - 1:c // 2, :]
        clast = cum[c - 1:c, :]
        ci = cum - cmid
        e_neg = jnp.exp(-ci)
        p_mid = jnp.exp(cmid)
        kk = k * kk_ref[:, sl]
        kkn = kk * lax.rsqrt(jnp.maximum(gsum(kk * kk), 1e-24))
        k2 = k * (1.0 + (a - 1.0) * ka_ref[:, sl])
        beta = kkn * a
        al_m = -kkn * jnp.exp(ci - lw)
        r_m = r * jnp.exp(ci)
        e_end = jnp.exp(clast - cum)
        ends = jnp.concatenate([beta * e_end, k2 * e_end], axis=0).astype(BF16)
        al2, r2 = by_head(al_m), by_head(r_m)
        lhs = jnp.concatenate([al2[:c], r2[:c], al2[c:], r2[c:]], axis=0)
        rhs = jnp.concatenate([beta * e_neg, k2 * e_neg], axis=0).astype(BF16)
        big = _dot_nt(lhs, rhs)
        s0 = state_ref[bi * n_pairs + p]
        base = _dot_nt(jnp.concatenate([al_m * p_mid, r_m * p_mid], axis=0).astype(BF16), s0.astype(BF16))
        return dict(big=big, base=base, v2=by_head(v), ends=ends, decayed=s0 * jnp.exp(clast),
                    bonus=gsum(r * k2 * rk_ref[:, sl]) * v)

    def stage_masks(d):
        big = d.pop("big")
        blk = lambda i, j: big[i * c:(i + 1) * c, j * c:(j + 1) * c]
        zero2 = jnp.zeros((c, 2 * c), F32)
        d["pw"] = jnp.where(strict2, jnp.concatenate([blk(0, 0), blk(2, 0)], axis=1), zero2).astype(BF16)
        a_ak = jnp.where(strict2, jnp.concatenate([blk(0, 1), blk(2, 1)], axis=1), zero2).astype(BF16)
        d["a_y"] = jnp.where(incl4, jnp.concatenate([blk(1, 0), blk(3, 0), blk(1, 1), blk(3, 1)], axis=1),
                             0.0).astype(BF16)
        d["x"] = d["base"][:c] + _dot(a_ak, d["v2"])

    def stage_square(d, last):
        pw, x2 = d["pw"], by_head(d["x"])
        if last:
            d["x"] = d["x"] + _dot(pw, x2)
            return
        zero_b = jnp.zeros_like(pw)
        bd = jnp.concatenate([jnp.where(first_block, pw, zero_b), jnp.where(first_block, zero_b, pw)], axis=0)
        res = _dot(pw, jnp.concatenate([bd, x2], axis=1))
        d["pw"] = res[:, :2 * c].astype(BF16)
        d["x"] = d["x"] + res[:, 2 * c:]

    def stage_output(bi, p, d):
        sl = slice(p * LANES, (p + 1) * LANES)
        x = d["x"]
        y = d["base"][c:] + _dot(d["a_y"], jnp.concatenate([by_head(x), d["v2"]], axis=0))
        upd = _dot(jnp.concatenate([x.T, v_ref[bi, :, sl].astype(F32).T], axis=1).astype(BF16), d["ends"])
        state_ref[bi * n_pairs + p] = d["decayed"] + jnp.where(same_head, upd, 0.0)
        mean = gsum(y) * (1.0 / half)
        yc = y - mean
        var = gsum(yc * yc) * (1.0 / half)
        out = yc * lax.rsqrt(var + GN_EPS) * gw_ref[:, sl] + gb_ref[:, sl]
        y_ref[bi, :, sl] = ((out + d["bonus"]) * g_ref[bi, :, sl].astype(F32)).astype(y_ref.dtype)

    ids = [(bi, p) for bi in range(r_ref.shape[0]) for p in range(n_pairs)]
    pairs = [stage_factors(bi, p) for bi, p in ids]
    for d in pairs:
        stage_masks(d)
    n_steps = c.bit_length() - 1
    for step in range(n_steps):
        for d in pairs:
            stage_square(d, last=step + 1 == n_steps)
    for (bi, p), d in zip(ids, pairs):
        stage_output(bi, p, d)


def rwkv_chunked(r, lw, k, v, a, g, k_k, k_a, r_k, gn_w, gn_b):
    b, s, width = r.shape
    assert RWKV_HEAD * 2 == LANES and width % LANES == 0 and s % CHUNK == 0
    n_pairs = width // LANES
    nb = _pick(b, (RWKV_ROWS_PER_STEP, 1))
    seq = pl.BlockSpec((nb, CHUNK, width), lambda i, j: (i, j, 0))
    par = pl.BlockSpec((1, width), lambda i, j: (0, 0))
    row = lambda t: t.reshape(1, width)
    return pl.pallas_call(
        functools.partial(_rwkv_chunk_kernel, n_pairs=n_pairs),
        grid=(b // nb, s // CHUNK),
        in_specs=[seq] * 6 + [par] * 5,
        out_specs=seq,
        out_shape=jax.ShapeDtypeStruct((b, s, width), BF16),
        scratch_shapes=[pltpu.VMEM((nb * n_pairs, LANES, LANES), F32)],
        compiler_params=_cparams("parallel", "arbitrary"),
    )(r, lw, k, v, a, g, row(k_k), row(k_a), row(r_k), row(gn_w), row(gn_b))


def _diff_in_kernel(x_ref, gn_ref, w_ref, rot_ref, qg_ref, kg_ref, qk_ref, v_ref, *, heads):
    half = LANES // 2
    rope_half = half // 8
    proj = _dot(_rms(x_ref[...], gn_ref[...]).astype(BF16), w_ref[...])
    lane = lax.broadcasted_iota(jnp.int32, (x_ref.shape[0], LANES), 1)
    pos = lane & (half - 1)
    e_row = lax.broadcasted_iota(jnp.int32, (2 * rope_half, LANES), 0)
    e_col = lax.broadcasted_iota(jnp.int32, (2 * rope_half, LANES), 1)
    expand = ((e_col & (half - 1)) == e_row).astype(BF16)
    cs = rot_ref[...]
    rot = None
    for _ in range(3):
        piece = cs.astype(BF16)
        term = lax.dot_general(piece, expand, (((0,), (0,)), ((), ())), preferred_element_type=F32)
        rot = term if rot is None else rot + term
        cs = cs - piece.astype(F32)
    first, second = pos < rope_half, (pos >= rope_half) & (pos < 2 * rope_half)
    cf = jnp.where(first, rot, jnp.where(second, pltpu.roll(rot, rope_half, axis=1), 1.0))
    sn = jnp.where(first, -pltpu.roll(rot, LANES - rope_half, axis=1), jnp.where(second, rot, 0.0))

    g_row = lax.broadcasted_iota(jnp.int32, (LANES, LANES), 0)
    g_col = lax.broadcasted_iota(jnp.int32, (LANES, LANES), 1)
    same_half = ((g_row < half) == (g_col < half)).astype(BF16)
    c_pos = g_col & (half - 1)
    partner = jnp.where(c_pos < rope_half, g_col + rope_half, jnp.where(c_pos < 2 * rope_half, g_col - rope_half, -1))
    swap = (g_row == partner).astype(BF16)

    def norm_rope(x, g):
        ms = _dot((x * x).astype(BF16), same_half) * (1.0 / half)
        x = x * lax.rsqrt(ms + EPS) * g
        return x * cf + _dot(x.astype(BF16), swap) * sn

    ones = jnp.ones((x_ref.shape[0], LANES), BF16)
    for h in range(heads):
        tile = lambda i: proj[:, (i * heads + h) * LANES:(i * heads + h + 1) * LANES]
        q = norm_rope(tile(0), qg_ref[...]) * (half ** -0.5 * math.log2(math.e))
        qk_ref[:, h * LANES:(h + 1) * LANES] = q.astype(BF16)
        qk_ref[:, (heads + h) * LANES:(heads + h + 1) * LANES] = norm_rope(tile(1), kg_ref[...]).astype(BF16)
        v_ref[:, 2 * h * LANES:(2 * h + 1) * LANES] = tile(2).astype(BF16)
        v_ref[:, (2 * h + 1) * LANES:(2 * h + 2) * LANES] = ones


def _diff_attn_kernel(q_ref, k_ref, v_ref, lam_ref, sub_ref, o_ref, *, tq, lam_init):
    half = LANES // 2
    s_len = q_ref.shape[1]
    lv = lam_ref[...]
    lam = (jnp.exp(jnp.sum(lv[0:1] * lv[1:2], axis=-1, keepdims=True))
           - jnp.exp(jnp.sum(lv[2:3] * lv[3:4], axis=-1, keepdims=True)) + lam_init)
    lane = lax.broadcasted_iota(jnp.int32, (tq, LANES), 1)
    n_parts = ATTN_ROW_PARTS
    rp = tq // n_parts
    chains = [(m, part) for part in range(n_parts) for m in range(2)]

    def block(qs, j, carry, masked):
        kb = k_ref[0, j * tq:(j + 1) * tq, :]
        vb = v_ref[0, j * tq:(j + 1) * tq, :]
        scores, probs, maxes, out = {}, {}, {}, {}
        for i in range(len(chains) + 2):
            if i < len(chains):
                m, part = chains[i]
                keys = (part + 1) * rp if masked else tq
                s = _dot_nt(qs[m][part * rp:(part + 1) * rp], kb[:keys])
                if masked:
                    r_id = lax.broadcasted_iota(jnp.int32, s.shape, 0) + part * rp
                    c_id = lax.broadcasted_iota(jnp.int32, s.shape, 1)
                    s = jnp.where(c_id <= r_id, s, -1e30)
                scores[i] = s
            if 0 <= i - 1 < len(chains):
                c = i - 1
                s = scores.pop(c)
                mx = jnp.max(s, axis=-1, keepdims=True)
                maxes[c] = mx if carry is None else jnp.maximum(carry[2 * c], mx)
                probs[c] = jnp.exp2((s - maxes[c]).astype(BF16))
            if 0 <= i - 2 < len(chains):
                c = i - 2
                pr = probs.pop(c)
                pv = _dot(pr, vb[:pr.shape[1]])
                out[c] = pv if carry is None else jnp.exp2(carry[2 * c] - maxes[c]) * carry[2 * c + 1] + pv
        return tuple(t for c in range(len(chains)) for t in (maxes[c], out[c]))

    for qi in range(s_len // tq):
        q = q_ref[0, qi * tq:(qi + 1) * tq, :]
        zero = jnp.zeros_like(q)
        qs = [jnp.where(lane < half, q, zero), jnp.where(lane < half, zero, q)]
        carry = block(qs, qi, None, True)
        for j in range(qi):
            carry = block(qs, j, carry, False)
        acc1 = jnp.concatenate([carry[2 * c + 1] for c, (m, _) in enumerate(chains) if m == 0], axis=0)
        acc2 = jnp.concatenate([carry[2 * c + 1] for c, (m, _) in enumerate(chains) if m == 1], axis=0)
        o = acc1[:, :LANES] / acc1[:, LANES:] - lam * (acc2[:, :LANES] / acc2[:, LANES:])
        o_ref[0, qi * tq:(qi + 1) * tq, :] = (_rms(o, sub_ref[...]) * (1.0 - lam_init)).astype(o_ref.dtype)


def diff_attention(x, b, gn, w, rot, q_g, k_g, lam_vecs, subln_g, lam_init):
    m, d = x.shape
    s = m // b
    h = DIFF_HEADS
    assert w.shape[1] == 3 * h * LANES
    tm = _pick(m, (512, 256, 128))
    tile2 = lambda g: jnp.tile(g, 2).reshape(1, LANES)
    rows = lambda c: pl.BlockSpec((tm, c), lambda i: (i, 0))
    full = lambda a: pl.BlockSpec(a.shape, lambda i: (0,) * a.ndim)
    gspec2 = pl.BlockSpec((1, LANES), lambda i: (0, 0))
    qk, vext = pl.pallas_call(
        functools.partial(_diff_in_kernel, heads=h),
        grid=(m // tm,),
        in_specs=[rows(d), pl.BlockSpec((1, d), lambda i: (0, 0)), full(w),
                  pl.BlockSpec((rot.shape[0], tm), lambda i: (0, i)), gspec2, gspec2],
        out_specs=[rows(2 * h * LANES), rows(2 * h * LANES)],
        out_shape=[jax.ShapeDtypeStruct((m, 2 * h * LANES), BF16)] * 2,
        compiler_params=_cparams("parallel"),
    )(x, gn.reshape(1, d), w, rot, tile2(q_g), tile2(k_g))
    qk = qk.reshape(b, s, 2 * h * LANES)
    vext = vext.reshape(b, s, 2 * h * LANES)

    tq = _pick(s, (1024, 512, 256, 128))
    gspec = pl.BlockSpec((1, LANES), lambda bi, hi: (0, 0))
    return pl.pallas_call(
        functools.partial(_diff_attn_kernel, tq=tq, lam_init=lam_init),
        grid=(b, h),
        in_specs=[pl.BlockSpec((1, s, LANES), lambda bi, hi: (bi, 0, hi)),
                  pl.BlockSpec((1, s, LANES), lambda bi, hi: (bi, 0, h + hi)),
                  pl.BlockSpec((1, s, 2 * LANES), lambda bi, hi: (bi, 0, hi)),
                  pl.BlockSpec(lam_vecs.shape, lambda bi, hi: (0, 0)),
                  gspec],
        out_specs=pl.BlockSpec((1, s, LANES), lambda bi, hi: (bi, 0, hi)),
        out_shape=jax.ShapeDtypeStruct((b, s, h * LANES), BF16),
        compiler_params=_cparams("parallel", "parallel"),
    )(qk, qk, vext, lam_vecs, subln_g.reshape(1, LANES))


def rope_table(positions):
    rope_dims = LANES // 8
    inv_freq = ROPE_THETA ** (-jnp.arange(0, rope_dims, 2, dtype=F32) / rope_dims)
    ang = inv_freq[:, None] * positions.astype(F32).reshape(1, -1)
    return jnp.concatenate([jnp.cos(ang), jnp.sin(ang)], axis=0)


def _xattn_kernel(x_ref, ya_ref, yb_ref, wa_ref, wb_ref, gn_ref, wq_ref, kv_ref, qg_ref, kg_ref, wo_ref, o_ref,
                  *, heads):
    x = x_ref[0] + _dot(ya_ref[0], wa_ref[...]) + _dot(yb_ref[0], wb_ref[...])
    d = x.shape[-1] // heads
    q_all = _dot(_rms(x, gn_ref[...]).astype(BF16), wq_ref[...])
    outs = []
    for h in range(heads):
        q = _rms(q_all[:, h * d:(h + 1) * d], qg_ref[...]) * (d ** -0.5)
        k = _rms(kv_ref[0, :, h * d:(h + 1) * d], kg_ref[...])
        v = kv_ref[0, :, (heads + h) * d:(heads + h + 1) * d]
        s = _dot_nt(q.astype(BF16), k.astype(BF16))
        s = s - jnp.max(s, axis=-1, keepdims=True)
        pr = jnp.exp(s)
        o = _dot(pr.astype(BF16), v.astype(BF16)) / jnp.sum(pr, axis=-1, keepdims=True)
        outs.append(o.astype(BF16))
    o_ref[0] = x + _dot(jnp.concatenate(outs, axis=1), wo_ref[...])


def mixer_out_cross_attention(x, ya, yb, wa, wb, gn, wq, kv, q_g, k_g, wo):
    b, s, dm = x.shape
    mlen = kv.shape[1]
    ts = _pick(s, (1024, 512, 256, 128))
    const = lambda a: pl.BlockSpec(a.shape, lambda i, j: (0,) * a.ndim)
    seq = lambda a: pl.BlockSpec((1, ts, a.shape[2]), lambda i, j: (i, j, 0))
    row = lambda a: a.reshape(1, -1)
    return pl.pallas_call(
        functools.partial(_xattn_kernel, heads=XATTN_HEADS),
        grid=(b, s // ts),
        in_specs=[seq(x), seq(ya), seq(yb), const(wa), const(wb),
                  const(row(gn)), const(wq),
                  pl.BlockSpec((1, mlen, 2 * dm), lambda i, j: (i, 0, 0)),
                  const(row(q_g)), const(row(k_g)), const(wo)],
        out_specs=seq(x),
        out_shape=jax.ShapeDtypeStruct((b, s, dm), F32),
        compiler_params=_cparams("parallel", "arbitrary"),
    )(x, ya, yb, wa, wb, row(gn), wq, kv, row(q_g), row(k_g), wo)


def _swiglu(h, wgu, wd):
    f = wd.shape[0]
    tf = _col_tile(h.shape[1], f)
    y = None
    for lo in range(0, f, tf):
        gate = _dot(h, wgu[:, lo:lo + tf])
        up = _dot(h, wgu[:, f + lo:f + lo + tf])
        part = _dot((gate * jax.nn.sigmoid(gate) * up).astype(BF16), wd[lo:lo + tf, :])
        y = part if y is None else y + part
    return y


def _ffn_kernel(x_ref, g_ref, wgu_ref, wd_ref, o_ref):
    x = x_ref[...]
    o_ref[...] = x + _swiglu(_rms(x, g_ref[...]).astype(BF16), wgu_ref, wd_ref)


def ffn_dense(x, g, w_gu, w_down):
    m, d = x.shape
    tm = _pick(m, (1024, 512, 256, 128))
    resident = lambda a: pl.BlockSpec(a.shape, lambda i: (0,) * a.ndim, pipeline_mode=pl.Buffered(1))
    return pl.pallas_call(
        _ffn_kernel,
        grid=(m // tm,),
        in_specs=[pl.BlockSpec((tm, d), lambda i: (i, 0)),
                  pl.BlockSpec((1, d), lambda i: (0, 0)),
                  resident(w_gu), resident(w_down)],
        out_specs=pl.BlockSpec((tm, d), lambda i: (i, 0)),
        out_shape=jax.ShapeDtypeStruct((m, d), F32),
        compiler_params=_cparams("parallel"),
    )(x, g.reshape(1, d), w_gu, w_down)


META_I1, META_I2, META_W1, META_W2, META_R1, META_R2 = range(6)


def _lane_pick(x, lane, idx):
    return jnp.sum(jnp.where(lane == idx, x, 0.0), axis=-1, keepdims=True)


def _router_kernel(x_ref, g_ref, wr_ref, meta_ref, cnt_ref, tri_ref, *, n_experts):
    tm = x_ref.shape[0]

    @pl.when(pl.program_id(0) == 0)
    def _():
        cnt_ref[...] = jnp.zeros_like(cnt_ref)
        r_id = lax.broadcasted_iota(jnp.int32, (tm, tm), 0)
        c_id = lax.broadcasted_iota(jnp.int32, (tm, tm), 1)
        tri_ref[...] = (r_id > c_id).astype(BF16)

    split = lambda t: (t.astype(BF16), (t - t.astype(BF16).astype(F32)).astype(BF16))
    h_hi, h_lo = split(_rms(x_ref[...], g_ref[...]))
    w_hi, w_lo = split(wr_ref[...])
    logits = _dot(h_hi, w_hi) + (_dot(h_hi, w_lo) + _dot(h_lo, w_hi))
    lane = lax.broadcasted_iota(jnp.int32, logits.shape, 1).astype(F32)
    neg = -jnp.inf
    l1 = jnp.where(lane < n_experts, logits, neg)
    m1 = jnp.max(l1, axis=-1, keepdims=True)
    i1 = jnp.min(jnp.where(l1 == m1, lane, float(LANES)), axis=-1, keepdims=True)
    l2 = jnp.where(lane == i1, neg, l1)
    m2 = jnp.max(l2, axis=-1, keepdims=True)
    i2 = jnp.min(jnp.where(l2 == m2, lane, float(LANES)), axis=-1, keepdims=True)
    e2 = jnp.exp(m2 - m1)
    w1 = 1.0 / (1.0 + e2)
    onehot = jnp.where((lane == i1) | (lane == i2), 1.0, 0.0)
    before = _dot(tri_ref[...], onehot.astype(BF16)) + cnt_ref[0:1, :]
    cnt_ref[0:1, :] += jnp.sum(onehot, axis=0, keepdims=True)
    meta = jnp.zeros_like(logits)
    for slot, val in ((META_I1, i1), (META_I2, i2), (META_W1, w1), (META_W2, e2 * w1),
                      (META_R1, _lane_pick(before, lane, i1)), (META_R2, _lane_pick(before, lane, i2))):
        meta = jnp.where(lane == slot, val, meta)
    meta_ref[...] = meta


def router_top2(x, g, w_router):
    m, d = x.shape
    n_experts = w_router.shape[1]
    wr = jnp.zeros((d, LANES), F32).at[:, :n_experts].set(w_router)
    tm = _pick(m, (1024, 512, 256, 128))
    return pl.pallas_call(
        functools.partial(_router_kernel, n_experts=n_experts),
        grid=(m // tm,),
        in_specs=[pl.BlockSpec((tm, d), lambda i: (i, 0)),
                  pl.BlockSpec((1, d), lambda i: (0, 0)),
                  pl.BlockSpec((d, LANES), lambda i: (0, 0))],
        out_specs=[pl.BlockSpec((tm, LANES), lambda i: (i, 0)),
                   pl.BlockSpec((8, LANES), lambda i: (0, 0))],
        out_shape=[jax.ShapeDtypeStruct((m, LANES), F32), jax.ShapeDtypeStruct((8, LANES), F32)],
        scratch_shapes=[pltpu.VMEM((tm, tm), BF16)],
        compiler_params=_cparams("arbitrary"),
    )(x, g.reshape(1, d), wr)


def _idx_copy(dest_hbm, idx_smem, isem, tile, slot):
    return pltpu.make_async_copy(dest_hbm.at[tile], idx_smem.at[slot], isem.at[slot])


def _dispatch_kernel(ends_ref, dest_hbm, x_ref, xs_hbm, idx_smem, zbuf, isem, sem, zsem, *, tm, tg):
    i = pl.program_id(0)
    slot = i % 2
    n_experts = ends_ref.shape[0]
    n_out_tiles = xs_hbm.shape[0] // tg

    @pl.when(i == 0)
    def _():
        _idx_copy(dest_hbm, idx_smem, isem, 0, 0).start()
        zbuf[...] = jnp.zeros_like(zbuf)
        fills = []
        for e in range(n_experts):
            group_start = ends_ref[e - 1] if e > 0 else 0
            fills.append((ends_ref[e] > group_start, ends_ref[e] - tg))
            tail = ends_ref[n_experts - 1] + e * tg
            fills.append((tail < n_out_tiles * tg, tail))
        zero_fill = lambda start: pltpu.make_async_copy(zbuf, xs_hbm.at[pl.ds(pl.multiple_of(start, tg), tg)], zsem)
        for cond, start in fills:
            @pl.when(cond)
            def _():
                zero_fill(start).start()
        for cond, start in fills:
            @pl.when(cond)
            def _():
                zero_fill(start).wait()

    @pl.when(i + 1 < pl.num_programs(0))
    def _():
        _idx_copy(dest_hbm, idx_smem, isem, i + 1, 1 - slot).start()

    _idx_copy(dest_hbm, idx_smem, isem, i, slot).wait()

    def row_copy(t, k):
        d = idx_smem[slot, TOP_K * t + k]
        return pltpu.make_async_copy(x_ref.at[pl.ds(t, 1)], xs_hbm.at[pl.ds(d, 1)], sem)

    def issue(t, carry):
        for k in range(TOP_K):
            row_copy(t, k).start()
        return carry

    lax.fori_loop(0, tm, issue, 0, unroll=8)
    for _ in range(TOP_K):
        pltpu.make_async_copy(x_ref, xs_hbm.at[pl.ds(0, tm)], sem).wait()


def moe_dispatch(x, dest, ends, n_rows, tg):
    m, d = x.shape
    n_tiles, per_tile = dest.shape
    tm = per_tile // TOP_K
    grid_spec = pltpu.PrefetchScalarGridSpec(
        num_scalar_prefetch=1,
        grid=(n_tiles,),
        in_specs=[pl.BlockSpec(memory_space=pl.ANY),
                  pl.BlockSpec((tm, d), lambda i, ends: (i, 0))],
        out_specs=pl.BlockSpec(memory_space=pl.ANY),
        scratch_shapes=[pltpu.SMEM((2, per_tile), jnp.int32), pltpu.VMEM((tg, d), F32),
                        pltpu.SemaphoreType.DMA((2,)), pltpu.SemaphoreType.DMA(()), pltpu.SemaphoreType.DMA(())],
    )
    return pl.pallas_call(
        functools.partial(_dispatch_kernel, tm=tm, tg=tg),
        grid_spec=grid_spec,
        out_shape=jax.ShapeDtypeStruct((n_rows, d), F32),
        compiler_params=_cparams("arbitrary"),
    )(ends, dest, x)


def _moe_ffn_kernel(te_ref, nu_ref, x_ref, g_ref, wgu_ref, wd_ref, o_ref):
    del te_ref

    @pl.when(pl.program_id(0) < nu_ref[0])
    def _():
        o_ref[...] = _swiglu(_rms(x_ref[...], g_ref[...]).astype(BF16), wgu_ref.at[0], wd_ref.at[0])

    @pl.when(pl.program_id(0) >= nu_ref[0])
    def _():
        o_ref[...] = jnp.zeros_like(o_ref)


def moe_grouped_ffn(xs, g, tile_expert, n_used, w_gu, w_down, tm):
    p, d = xs.shape
    f = w_down.shape[1]
    last = lambda n, nu: jnp.minimum(n, nu[0] - 1)
    grid_spec = pltpu.PrefetchScalarGridSpec(
        num_scalar_prefetch=2,
        grid=(p // tm,),
        in_specs=[pl.BlockSpec((tm, d), lambda n, te, nu: (last(n, nu), 0)),
                  pl.BlockSpec((1, d), lambda n, te, nu: (0, 0)),
                  pl.BlockSpec((1, d, 2 * f), lambda n, te, nu: (te[n], 0, 0), pipeline_mode=pl.Buffered(1)),
                  pl.BlockSpec((1, f, d), lambda n, te, nu: (te[n], 0, 0), pipeline_mode=pl.Buffered(1))],
        out_specs=pl.BlockSpec((tm, d), lambda n, te, nu: (n, 0)),
    )
    return pl.pallas_call(
        _moe_ffn_kernel,
        grid_spec=grid_spec,
        out_shape=jax.ShapeDtypeStruct((p, d), F32),
        compiler_params=_cparams("arbitrary"),
    )(tile_expert, n_used, xs, g.reshape(1, d), w_gu, w_down)


def _combine_kernel(dest_hbm, x_ref, meta_ref, ys_hbm, o_ref, idx_smem, ybuf, isem, sem, *, tm, n_tiles):
    i = pl.program_id(0)
    slot = i % 2

    def row_copy(sl, t, k):
        d = idx_smem[sl, TOP_K * t + k]
        return pltpu.make_async_copy(ys_hbm.at[pl.ds(d, 1)], ybuf.at[sl, k, pl.ds(t, 1)], sem.at[sl])

    def gather_tile(tile, sl):
        _idx_copy(dest_hbm, idx_smem, isem, tile, sl).wait()

        def issue(t, carry):
            for k in range(TOP_K):
                row_copy(sl, t, k).start()
            return carry

        lax.fori_loop(0, tm, issue, 0, unroll=8)

    @pl.when(i == 0)
    def _():
        _idx_copy(dest_hbm, idx_smem, isem, 0, 0).start()
        gather_tile(0, 0)
        if n_tiles > 1:
            _idx_copy(dest_hbm, idx_smem, isem, 1, 1).start()

    @pl.when(i + 1 < pl.num_programs(0))
    def _():
        gather_tile(i + 1, 1 - slot)

    @pl.when(i + 2 < pl.num_programs(0))
    def _():
        _idx_copy(dest_hbm, idx_smem, isem, i + 2, slot).start()

    for k in range(TOP_K):
        pltpu.make_async_copy(ys_hbm.at[pl.ds(0, tm)], ybuf.at[slot, k], sem.at[slot]).wait()
    meta = meta_ref[...]
    lane = lax.broadcasted_iota(jnp.int32, meta.shape, 1)
    w1 = _lane_pick(meta, lane, META_W1)
    w2 = _lane_pick(meta, lane, META_W2)
    o_ref[...] = x_ref[...] + w1 * ybuf[slot, 0] + w2 * ybuf[slot, 1]


def moe_combine(x, meta, dest, ys):
    m, d = x.shape
    n_tiles, per_tile = dest.shape
    tm = per_tile // TOP_K
    return pl.pallas_call(
        functools.partial(_combine_kernel, tm=tm, n_tiles=n_tiles),
        grid=(n_tiles,),
        in_specs=[pl.BlockSpec(memory_space=pl.ANY),
                  pl.BlockSpec((tm, d), lambda i: (i, 0)),
                  pl.BlockSpec((tm, LANES), lambda i: (i, 0)),
                  pl.BlockSpec(memory_space=pl.ANY)],
        out_specs=pl.BlockSpec((tm, d), lambda i: (i, 0)),
        out_shape=jax.ShapeDtypeStruct((m, d), F32),
        scratch_shapes=[pltpu.SMEM((2, per_tile), jnp.int32), pltpu.VMEM((2, TOP_K, tm, d), F32),
                        pltpu.SemaphoreType.DMA((2,)), pltpu.SemaphoreType.DMA((2,))],
        compiler_params=_cparams("arbitrary"),
    )(dest, x, meta, ys)


def ffn_moe(x, g, w_router, w_gu, w_down):
    m, d = x.shape
    n_experts = w_down.shape[0]
    tg = _pick(m, (MOE_GROUP_TILE, 256, 128))
    td = _pick(m, (MOE_TOKEN_TILE, 256, 128))
    meta, counts = router_top2(x, g, w_router)
    counts = counts[0, :n_experts].astype(jnp.int32)
    padded = (counts + tg - 1) // tg * tg
    ends = jnp.cumsum(padded)
    offsets = ends - padded
    n_tiles = (TOP_K * m) // tg + n_experts
    n_used = (ends[-1] // tg).astype(jnp.int32)
    tile_id = jnp.minimum(jnp.arange(n_tiles, dtype=jnp.int32), n_used - 1)
    tile_expert = jnp.sum(ends[None, :] <= (tile_id * tg)[:, None], axis=1).astype(jnp.int32)
    picks = meta[:, META_I1:META_I2 + 1].astype(jnp.int32)
    ranks = meta[:, META_R1:META_R2 + 1].astype(jnp.int32)
    dest = (offsets[picks] + ranks).reshape(m // td, TOP_K * td)
    xs = moe_dispatch(x, dest, ends.astype(jnp.int32), n_tiles * tg, tg)
    ys = moe_grouped_ffn(xs, g, tile_expert, n_used.reshape(1), w_gu, w_down, tg)
    return moe_combine(x, meta, dest, ys)


def _block_diag(w_a, w_b):
    za = jnp.zeros_like(w_a)
    zb = jnp.zeros_like(w_b)
    return jnp.concatenate([jnp.concatenate([w_a, za], axis=1), jnp.concatenate([zb, w_b], axis=1)], axis=0)


def kernel(x, mem, positions, mix_norm, in_proj, shift_mu, decay_bias, decay_up, iclr_bias, iclr_up, gate_up,
           key_kk_scale, key_iclr_scale, bonus_rk, rwkv_gn_w, rwkv_gn_b, vres_down, vres_shift_mu, vres_bias,
           vres_up, q_norm, k_norm, lambda_q1, lambda_k1, lambda_q2, lambda_k2, diff_subln, out_proj, xattn_norm,
           mem_norm, xattn_wq, xattn_wkv, xattn_wo, xattn_q_norm, xattn_k_norm, ffn_norm, dense_w_gu,
           dense_w_down, router, expert_w_gu, expert_w_down):
    b, s, d = x.shape
    depth = mix_norm.shape[0]
    width = decay_bias.shape[1]
    rwkv_cols = shift_mu.shape[1]
    in_cols = in_proj.shape[2]
    vres_rank = vres_down.shape[2] if depth > 1 else 0
    assert decay_up.shape[1] == LANES // 2 and iclr_up.shape[1] == LANES // 2 and gate_up.shape[1] == LANES
    assert rwkv_cols == 3 * width + 2 * LANES and rwkv_cols % LANES == 0 and in_cols % LANES == 0
    assert vres_rank <= LANES
    rot = rope_table(positions)
    xf = x.reshape(b * s, d)
    memf = mem.reshape(-1, d)
    v_first = None
    for l in range(depth):
        w_rwkv = in_proj[l][:, :rwkv_cols]
        vres = None
        if l > 0:
            pad = jnp.zeros((d, LANES - vres_rank), F32)
            w_rwkv = jnp.concatenate([w_rwkv, vres_down[l - 1], pad], axis=1)
            vmu = jnp.zeros((LANES,), F32).at[:vres_rank].set(vres_shift_mu[l - 1])
            vup = jnp.zeros((LANES, width), F32).at[:vres_rank].set(vres_up[l - 1]).astype(BF16)
            vres = (v_first, vmu, vres_bias[l - 1], vup)
        wwa = _block_diag(decay_up[l], iclr_up[l]).astype(BF16)
        r_, lw_, k_, v_, a_, g_ = rwkv_in_proj(xf, s, mix_norm[l], w_rwkv.astype(BF16), width, shift_mu[l],
                                               decay_bias[l], iclr_bias[l], wwa, gate_up[l].astype(BF16), vres)
        if l == 0:
            v_first = v_
        seq3 = lambda t: t.reshape(b, s, width)
        y_rwkv = rwkv_chunked(seq3(r_), seq3(lw_), seq3(k_), seq3(v_), seq3(a_), seq3(g_), key_kk_scale[l],
                              key_iclr_scale[l], bonus_rk[l].reshape(-1), rwkv_gn_w[l], rwkv_gn_b[l])
        lam_init = 0.8 - 0.6 * math.exp(-0.3 * l)
        lam_vecs = jnp.stack([lambda_q1[l], lambda_k1[l], lambda_q2[l], lambda_k2[l]])
        y_diff = diff_attention(xf, b, mix_norm[l], in_proj[l][:, rwkv_cols:].astype(BF16), rot, q_norm[l],
                                k_norm[l], lam_vecs, diff_subln[l], lam_init)
        w_out = out_proj[l].astype(BF16)
        kv = norm_matmul(memf, mem_norm[l], xattn_wkv[l].astype(BF16)).reshape(b, -1, 2 * d)
        xf = mixer_out_cross_attention(xf.reshape(b, s, d), y_rwkv, y_diff, w_out[:width], w_out[width:],
                                       xattn_norm[l], xattn_wq[l].astype(BF16), kv, xattn_q_norm[l],
                                       xattn_k_norm[l], xattn_wo[l].astype(BF16)).reshape(b * s, d)
        if l % 2 == 0:
            xf = ffn_dense(xf, ffn_norm[l], dense_w_gu[l // 2].astype(BF16), dense_w_down[l // 2].astype(BF16))
        else:
            xf = ffn_moe(xf, ffn_norm[l], router[l // 2], expert_w_gu[l // 2].astype(BF16),
                         expert_w_down[l // 2].astype(BF16))
    return xf.reshape(b, s, d)
```

```python
import functools
import math

import jax
import jax.numpy as jnp
from jax import lax
from jax.experimental import pallas as pl
from jax.experimental.pallas import tpu as pltpu

F32 = jnp.float32
BF16 = jnp.bfloat16

EPS = 1e-6
GN_EPS = 64e-5
ROPE_THETA = 500000.0
RWKV_HEAD = 64
DIFF_HEADS = 4
XATTN_HEADS = 4
TOP_K = 2
LANES = 128
CHUNK = 128
VMEM_LIMIT = 56 * 1024 * 1024
RWKV_ROWS_PER_STEP = 4
RWKV_IN_ROW_PARTS = 4
ATTN_ROW_PARTS = 4
WEIGHT_TILE_BYTES = 4 * 1024 * 1024
MOE_GROUP_TILE = 512
MOE_TOKEN_TILE = 1024


def _cparams(*sem):
    return pltpu.CompilerParams(dimension_semantics=sem, vmem_limit_bytes=VMEM_LIMIT)


def _pick(n, prefs):
    for p in prefs:
        if n % p == 0:
            return p
    return n


def _col_tile(k, n):
    best = LANES
    for t in range(LANES, n + 1, LANES):
        if n % t == 0 and k * t * 2 <= WEIGHT_TILE_BYTES:
            best = t
    return best


def _dot(a, b, prec=None):
    return jnp.dot(a, b, preferred_element_type=F32, precision=prec)


def _dot_nt(a, b, prec=None):
    return lax.dot_general(a, b, (((1,), (1,)), ((), ())), preferred_element_type=F32, precision=prec)


def _rms(x, g):
    return x * lax.rsqrt(jnp.mean(x * x, axis=-1, keepdims=True) + EPS) * g


def _norm_mm_kernel(x_ref, g_ref, w_ref, o_ref, h_ref):
    @pl.when(pl.program_id(1) == 0)
    def _():
        h_ref[...] = _rms(x_ref[...], g_ref[...]).astype(BF16)

    o_ref[...] = _dot(h_ref[...], w_ref[...])


def norm_matmul(x, g, w):
    m, k = x.shape
    n = w.shape[1]
    tm = _pick(m, (1024, 512, 256, 128))
    tn = _col_tile(k, n)
    return pl.pallas_call(
        _norm_mm_kernel,
        grid=(m // tm, n // tn),
        in_specs=[pl.BlockSpec((tm, k), lambda i, j: (i, 0)),
                  pl.BlockSpec((1, k), lambda i, j: (0, 0)),
                  pl.BlockSpec((k, tn), lambda i, j: (0, j))],
        out_specs=pl.BlockSpec((tm, tn), lambda i, j: (i, j)),
        out_shape=jax.ShapeDtypeStruct((m, n), F32),
        scratch_shapes=[pltpu.VMEM((tm, k), BF16)],
        compiler_params=_cparams("parallel", "arbitrary"),
    )(x, g.reshape(1, k), w)


def _token_shift(p, prev_row, mu):
    prev = pltpu.roll(p, 1, axis=0)
    rid = lax.broadcasted_iota(jnp.int32, p.shape, 0)
    prev = jnp.where(rid == 0, prev_row, prev)
    return p + (prev - p) * mu


def _rwkv_in_kernel(*refs, width, has_vres, tiles_per_seq):
    if has_vres:
        (x_ref, gn_ref, w_ref, vf_ref, mu_ref, w0_ref, a0_ref, wwa_ref, gup_ref, vmu_ref, vb_ref, vup_ref,
         r_ref, lw_ref, k_ref, v_ref, a_ref, g_ref, carry_ref, carry_v_ref) = refs
    else:
        (x_ref, gn_ref, w_ref, mu_ref, w0_ref, a0_ref, wwa_ref, gup_ref,
         r_ref, lw_ref, k_ref, v_ref, a_ref, g_ref, carry_ref) = refs
    ncol = 3 * width + 2 * LANES
    tm = x_ref.shape[0]
    n_parts = RWKV_IN_ROW_PARTS
    rp = tm // n_parts

    @pl.when(pl.program_id(0) % tiles_per_seq == 0)
    def _():
        carry_ref[...] = jnp.zeros_like(carry_ref)
        if has_vres:
            carry_v_ref[...] = jnp.zeros_like(carry_v_ref)

    projs = [_dot(_rms(x_ref[i * rp:(i + 1) * rp, :], gn_ref[...]).astype(BF16), w_ref[...]) for i in range(n_parts)]
    prev = carry_ref[...]
    prev_v = carry_v_ref[...] if has_vres else None
    for i, proj in enumerate(projs):
        rows = slice(i * rp, (i + 1) * rp)
        sh = _token_shift(proj[:, :ncol], prev, mu_ref[...])
        prev = proj[rp - 1:rp, :ncol]
        r_ref[rows, :] = sh[:, :width].astype(r_ref.dtype)
        k_ref[rows, :] = sh[:, width:2 * width].astype(k_ref.dtype)
        v = sh[:, 2 * width:3 * width]
        dwa = sh[:, 3 * width:3 * width + LANES]
        lane = lax.broadcasted_iota(jnp.int32, dwa.shape, 1)
        dwa = jnp.where(lane < LANES // 2, jnp.tanh(dwa), dwa)
        wa = _dot(dwa.astype(BF16), wwa_ref[...])
        z = -(w0_ref[...] + wa[:, :width])
        softplus = jnp.maximum(z, 0.0) + jnp.log(1.0 + jnp.exp(-jnp.abs(z)))
        lw_ref[rows, :] = -jnp.exp(-softplus - 0.5)
        a_ref[rows, :] = jax.nn.sigmoid(a0_ref[...] + wa[:, width:]).astype(a_ref.dtype)
        dg = jax.nn.sigmoid(sh[:, 3 * width + LANES:3 * width + 2 * LANES])
        g_ref[rows, :] = _dot(dg.astype(BF16), gup_ref[...]).astype(g_ref.dtype)
        if has_vres:
            shv = _token_shift(proj[:, ncol:ncol + LANES], prev_v, vmu_ref[...])
            prev_v = proj[rp - 1:rp, ncol:ncol + LANES]
            mix = jax.nn.sigmoid(vb_ref[...] + _dot(shv.astype(BF16), vup_ref[...]))
            v = v + (vf_ref[rows, :].astype(F32) - v) * mix
        v_ref[rows, :] = v.astype(v_ref.dtype)
    carry_ref[...] = prev
    if has_vres:
        carry_v_ref[...] = prev_v


def rwkv_in_proj(x, seq_len, gn, w, width, mu, w0, a0, wwa, gup, vres=None):
    m, d = x.shape
    tm = _pick(seq_len, (1024, 512, 256, 128))
    ncol = 3 * width + 2 * LANES
    assert mu.shape[-1] == ncol and m % seq_len == 0
    row = lambda a: a.reshape(1, -1)
    full = lambda a: pl.BlockSpec(a.shape, lambda i: (0,) * a.ndim)
    rows = lambda c: pl.BlockSpec((tm, c), lambda i: (i, 0))
    params = [row(mu), row(w0), row(a0), wwa, gup]
    inputs = [x, row(gn), w]
    in_specs = [rows(d), full(row(gn)), full(w)]
    scratch = [pltpu.VMEM((1, ncol), F32)]
    if vres is not None:
        v_first, vmu, vb, vup = vres
        inputs.append(v_first)
        in_specs.append(rows(width))
        params += [row(vmu), row(vb), vup]
        scratch.append(pltpu.VMEM((1, LANES), F32))
    in_specs += [full(p) for p in params]
    out = lambda dt: jax.ShapeDtypeStruct((m, width), dt)
    return pl.pallas_call(
        functools.partial(_rwkv_in_kernel, width=width, has_vres=vres is not None, tiles_per_seq=seq_len // tm),
        grid=(m // tm,),
        in_specs=in_specs,
        out_specs=[rows(width)] * 6,
        out_shape=[out(BF16), out(F32), out(BF16), out(BF16), out(BF16), out(BF16)],
        scratch_shapes=scratch,
        compiler_params=_cparams("arbitrary"),
    )(*inputs, *params)


def _rwkv_chunk_kernel(r_ref, lw_ref, k_ref, v_ref, a_ref, g_ref, kk_ref, ka_ref, rk_ref, gw_ref, gb_ref,
                       y_ref, state_ref, *, n_pairs):
    c = CHUNK
    half = LANES // 2

    @pl.when(pl.program_id(1) == 0)
    def _():
        state_ref[...] = jnp.zeros_like(state_ref)

    row = lax.broadcasted_iota(jnp.int32, (c, c), 0)
    col = lax.broadcasted_iota(jnp.int32, (c, c), 1)
    ltri = (row >= col).astype(BF16)
    same_head = (row < half) == (col < half)
    lo = col < half
    row4 = lax.broadcasted_iota(jnp.int32, (c, 4 * c), 0)
    col4 = lax.broadcasted_iota(jnp.int32, (c, 4 * c), 1)
    incl4 = row4 >= (col4 & (c - 1))
    row2 = lax.broadcasted_iota(jnp.int32, (c, 2 * c), 0)
    col2 = lax.broadcasted_iota(jnp.int32, (c, 2 * c), 1)
    strict2 = row2 > (col2 & (c - 1))
    first_block = col2 < c

    def gsum(x):
        s_lo = jnp.sum(jnp.where(lo, x, 0.0), axis=-1, keepdims=True)
        s_hi = jnp.sum(jnp.where(lo, 0.0, x), axis=-1, keepdims=True)
        return jnp.where(lo, s_lo, s_hi)

    def by_head(x):
        return jnp.concatenate([jnp.where(lo, x, 0.0), jnp.where(lo, 0.0, x)], axis=0).astype(BF16)

    def split_bf16(x):
        hi = x.astype(BF16)
        rest = x - hi.astype(F32)
        mid = rest.astype(BF16)
        return hi, mid, (rest - mid.astype(F32)).astype(BF16)

    def stage_factors(bi, p):
        sl = slice(p * LANES, (p + 1) * LANES)
        r, lw, k, v, a = (ref[bi, :, sl].astype(F32) for ref in (r_ref, lw_ref, k_ref, v_ref, a_ref))
        cum3 = _dot(ltri, jnp.concatenate(split_bf16(lw), axis=1))
        cum = cum3[:, :LANES] + cum3[:, LANES:2 * LANES] + cum3[:, 2 * LANES:]
        cmid = cum[c // 2 - 1:c // 2, :]
        clast = cum[c - 1:c, :]
        ci = cum - cmid
        e_neg = jnp.exp(-ci)
        p_mid = jnp.exp(cmid)
        kk = k * kk_ref[:, sl]
        kkn = kk * lax.rsqrt(jnp.maximum(gsum(kk * kk), 1e-24))
        k2 = k * (1.0 + (a - 1.0) * ka_ref[:, sl])
        beta = kkn * a
        al_m = -kkn * jnp.exp(ci - lw)
        r_m = r * jnp.exp(ci)
        e_end = jnp.exp(clast - cum)
        ends = jnp.concatenate([beta * e_end, k2 * e_end], axis=0).astype(BF16)
        al2, r2 = by_head(al_m), by_head(r_m)
        lhs = jnp.concatenate([al2[:c], r2[:c], al2[c:], r2[c:]], axis=0)
        rhs = jnp.concatenate([beta * e_neg, k2 * e_neg], axis=0).astype(BF16)
        big = _dot_nt(lhs, rhs)
        s0 = state_ref[bi * n_pairs + p]
        base = _dot_nt(jnp.concatenate([al_m * p_mid, r_m * p_mid], axis=0).astype(BF16), s0.astype(BF16))
        return dict(big=big, base=base, v2=by_head(v), ends=ends, decayed=s0 * jnp.exp(clast),
                    bonus=gsum(r * k2 * rk_ref[:, sl]) * v)

    def stage_masks(d):
        big = d.pop("big")
        blk = lambda i, j: big[i * c:(i + 1) * c, j * c:(j + 1) * c]
        zero2 = jnp.zeros((c, 2 * c), F32)
        d["pw"] = jnp.where(strict2, jnp.concatenate([blk(0, 0), blk(2, 0)], axis=1), zero2).astype(BF16)
        a_ak = jnp.where(strict2, jnp.concatenate([blk(0, 1), blk(2, 1)], axis=1), zero2).astype(BF16)
        d["a_y"] = jnp.where(incl4, jnp.concatenate([blk(1, 0), blk(3, 0), blk(1, 1), blk(3, 1)], axis=1),
                             0.0).astype(BF16)
        d["x"] = d["base"][:c] + _dot(a_ak, d["v2"])

    def stage_square(d, last):
        pw, x2 = d["pw"], by_head(d["x"])
        if last:
            d["x"] = d["x"] + _dot(pw, x2)
            return
        zero_b = jnp.zeros_like(pw)
        bd = jnp.concatenate([jnp.where(first_block, pw, zero_b), jnp.where(first_block, zero_b, pw)], axis=0)
        res = _dot(pw, jnp.concatenate([bd, x2], axis=1))
        d["pw"] = res[:, :2 * c].astype(BF16)
        d["x"] = d["x"] + res[:, 2 * c:]

    def stage_output(bi, p, d):
        sl = slice(p * LANES, (p + 1) * LANES)
        x = d["x"]
        y = d["base"][c:] + _dot(d["a_y"], jnp.concatenate([by_head(x), d["v2"]], axis=0))
        upd = _dot(jnp.concatenate([x.T, v_ref[bi, :, sl].astype(F32).T], axis=1).astype(BF16), d["ends"])
        state_ref[bi * n_pairs + p] = d["decayed"] + jnp.where(same_head, upd, 0.0)
        mean = gsum(y) * (1.0 / half)
        yc = y - mean
        var = gsum(yc * yc) * (1.0 / half)
        out = yc * lax.rsqrt(var + GN_EPS) * gw_ref[:, sl] + gb_ref[:, sl]
        y_ref[bi, :, sl] = ((out + d["bonus"]) * g_ref[bi, :, sl].astype(F32)).astype(y_ref.dtype)

    ids = [(bi, p) for bi in range(r_ref.shape[0]) for p in range(n_pairs)]
    pairs = [stage_factors(bi, p) for bi, p in ids]
    for d in pairs:
        stage_masks(d)
    n_steps = c.bit_length() - 1
    for step in range(n_steps):
        for d in pairs:
            stage_square(d, last=step + 1 == n_steps)
    for (bi, p), d in zip(ids, pairs):
        stage_output(bi, p, d)


def rwkv_chunked(r, lw, k, v, a, g, k_k, k_a, r_k, gn_w, gn_b):
    b, s, width = r.shape
    assert RWKV_HEAD * 2 == LANES and width % LANES == 0 and s % CHUNK == 0
    n_pairs = width // LANES
    nb = _pick(b, (RWKV_ROWS_PER_STEP, 1))
    seq = pl.BlockSpec((nb, CHUNK, width), lambda i, j: (i, j, 0))
    par = pl.BlockSpec((1, width), lambda i, j: (0, 0))
    row = lambda t: t.reshape(1, width)
    return pl.pallas_call(
        functools.partial(_rwkv_chunk_kernel, n_pairs=n_pairs),
        grid=(b // nb, s // CHUNK),
        in_specs=[seq] * 6 + [par] * 5,
        out_specs=seq,
        out_shape=jax.ShapeDtypeStruct((b, s, width), BF16),
        scratch_shapes=[pltpu.VMEM((nb * n_pairs, LANES, LANES), F32)],
        compiler_params=_cparams("parallel", "arbitrary"),
    )(r, lw, k, v, a, g, row(k_k), row(k_a), row(r_k), row(gn_w), row(gn_b))


def _diff_in_kernel(x_ref, gn_ref, w_ref, rot_ref, qg_ref, kg_ref, qk_ref, v_ref, *, heads):
    half = LANES // 2
    rope_half = half // 8
    proj = _dot(_rms(x_ref[...], gn_ref[...]).astype(BF16), w_ref[...])
    lane = lax.broadcasted_iota(jnp.int32, (x_ref.shape[0], LANES), 1)
    pos = lane & (half - 1)
    e_row = lax.broadcasted_iota(jnp.int32, (2 * rope_half, LANES), 0)
    e_col = lax.broadcasted_iota(jnp.int32, (2 * rope_half, LANES), 1)
    expand = ((e_col & (half - 1)) == e_row).astype(BF16)
    cs = rot_ref[...]
    rot = None
    for _ in range(3):
        piece = cs.astype(BF16)
        term = lax.dot_general(piece, expand, (((0,), (0,)), ((), ())), preferred_element_type=F32)
        rot = term if rot is None else rot + term
        cs = cs - piece.astype(F32)
    first, second = pos < rope_half, (pos >= rope_half) & (pos < 2 * rope_half)
    cf = jnp.where(first, rot, jnp.where(second, pltpu.roll(rot, rope_half, axis=1), 1.0))
    sn = jnp.where(first, -pltpu.roll(rot, LANES - rope_half, axis=1), jnp.where(second, rot, 0.0))

    g_row = lax.broadcasted_iota(jnp.int32, (LANES, LANES), 0)
    g_col = lax.broadcasted_iota(jnp.int32, (LANES, LANES), 1)
    same_half = ((g_row < half) == (g_col < half)).astype(BF16)
    c_pos = g_col & (half - 1)
    partner = jnp.where(c_pos < rope_half, g_col + rope_half, jnp.where(c_pos < 2 * rope_half, g_col - rope_half, -1))
    swap = (g_row == partner).astype(BF16)

    def norm_rope(x, g):
        ms = _dot((x * x).astype(BF16), same_half) * (1.0 / half)
        x = x * lax.rsqrt(ms + EPS) * g
        return x * cf + _dot(x.astype(BF16), swap) * sn

    ones = jnp.ones((x_ref.shape[0], LANES), BF16)
    for h in range(heads):
        tile = lambda i: proj[:, (i * heads + h) * LANES:(i * heads + h + 1) * LANES]
        q = norm_rope(tile(0), qg_ref[...]) * (half ** -0.5 * math.log2(math.e))
        qk_ref[:, h * LANES:(h + 1) * LANES] = q.astype(BF16)
        qk_ref[:, (heads + h) * LANES:(heads + h + 1) * LANES] = norm_rope(tile(1), kg_ref[...]).astype(BF16)
        v_ref[:, 2 * h * LANES:(2 * h + 1) * LANES] = tile(2).astype(BF16)
        v_ref[:, (2 * h + 1) * LANES:(2 * h + 2) * LANES] = ones


def _diff_attn_kernel(q_ref, k_ref, v_ref, lam_ref, sub_ref, o_ref, *, tq, lam_init):
    half = LANES // 2
    s_len = q_ref.shape[1]
    lv = lam_ref[...]
    lam = (jnp.exp(jnp.sum(lv[0:1] * lv[1:2], axis=-1, keepdims=True))
           - jnp.exp(jnp.sum(lv[2:3] * lv[3:4], axis=-1, keepdims=True)) + lam_init)
    lane = lax.broadcasted_iota(jnp.int32, (tq, LANES), 1)
    n_parts = ATTN_ROW_PARTS
    rp = tq // n_parts
    chains = [(m, part) for part in range(n_parts) for m in range(2)]

    def block(qs, j, carry, masked):
        kb = k_ref[0, j * tq:(j + 1) * tq, :]
        vb = v_ref[0, j * tq:(j + 1) * tq, :]
        scores, probs, maxes, out = {}, {}, {}, {}
        for i in range(len(chains) + 2):
            if i < len(chains):
                m, part = chains[i]
                keys = (part + 1) * rp if masked else tq
                s = _dot_nt(qs[m][part * rp:(part + 1) * rp], kb[:keys])
                if masked:
                    r_id = lax.broadcasted_iota(jnp.int32, s.shape, 0) + part * rp
                    c_id = lax.broadcasted_iota(jnp.int32, s.shape, 1)
                    s = jnp.where(c_id <= r_id, s, -1e30)
                scores[i] = s
            if 0 <= i - 1 < len(chains):
                c = i - 1
                s = scores.pop(c)
                mx = jnp.max(s, axis=-1, keepdims=True)
                maxes[c] = mx if carry is None else jnp.maximum(carry[2 * c], mx)
                probs[c] = jnp.exp2((s - maxes[c]).astype(BF16))
            if 0 <= i - 2 < len(chains):
                c = i - 2
                pr = probs.pop(c)
                pv = _dot(pr, vb[:pr.shape[1]])
                out[c] = pv if carry is None else jnp.exp2(carry[2 * c] - maxes[c]) * carry[2 * c + 1] + pv
        return tuple(t for c in range(len(chains)) for t in (maxes[c], out[c]))

    for qi in range(s_len // tq):
        q = q_ref[0, qi * tq:(qi + 1) * tq, :]
        zero = jnp.zeros_like(q)
        qs = [jnp.where(lane < half, q, zero), jnp.where(lane < half, zero, q)]
        carry = block(qs, qi, None, True)
        for j in range(qi):
            carry = block(qs, j, carry, False)
        acc1 = jnp.concatenate([carry[2 * c + 1] for c, (m, _) in enumerate(chains) if m == 0], axis=0)
        acc2 = jnp.concatenate([carry[2 * c + 1] for c, (m, _) in enumerate(chains) if m == 1], axis=0)
        o = acc1[:, :LANES] / acc1[:, LANES:] - lam * (acc2[:, :LANES] / acc2[:, LANES:])
        o_ref[0, qi * tq:(qi + 1) * tq, :] = (_rms(o, sub_ref[...]) * (1.0 - lam_init)).astype(o_ref.dtype)


def diff_attention(x, b, gn, w, rot, q_g, k_g, lam_vecs, subln_g, lam_init):
    m, d = x.shape
    s = m // b
    h = DIFF_HEADS
    assert w.shape[1] == 3 * h * LANES
    tm = _pick(m, (1024, 512, 256, 128))
    tile2 = lambda g: jnp.tile(g, 2).reshape(1, LANES)
    rows = lambda c: pl.BlockSpec((tm, c), lambda i: (i, 0))
    full = lambda a: pl.BlockSpec(a.shape, lambda i: (0,) * a.ndim)
    gspec2 = pl.BlockSpec((1, LANES), lambda i: (0, 0))
    qk, vext = pl.pallas_call(
        functools.partial(_diff_in_kernel, heads=h),
        grid=(m // tm,),
        in_specs=[rows(d), pl.BlockSpec((1, d), lambda i: (0, 0)), full(w),
                  pl.BlockSpec((rot.shape[0], tm), lambda i: (0, i)), gspec2, gspec2],
        out_specs=[rows(2 * h * LANES), rows(2 * h * LANES)],
        out_shape=[jax.ShapeDtypeStruct((m, 2 * h * LANES), BF16)] * 2,
        compiler_params=_cparams("parallel"),
    )(x, gn.reshape(1, d), w, rot, tile2(q_g), tile2(k_g))
    qk = qk.reshape(b, s, 2 * h * LANES)
    vext = vext.reshape(b, s, 2 * h * LANES)

    tq = _pick(s, (1024, 512, 256, 128))
    gspec = pl.BlockSpec((1, LANES), lambda bi, hi: (0, 0))
    return pl.pallas_call(
        functools.partial(_diff_attn_kernel, tq=tq, lam_init=lam_init),
        grid=(b, h),
        in_specs=[pl.BlockSpec((1, s, LANES), lambda bi, hi: (bi, 0, hi)),
                  pl.BlockSpec((1, s, LANES), lambda bi, hi: (bi, 0, h + hi)),
                  pl.BlockSpec((1, s, 2 * LANES), lambda bi, hi: (bi, 0, hi)),
                  pl.BlockSpec(lam_vecs.shape, lambda bi, hi: (0, 0)),
                  gspec],
        out_specs=pl.BlockSpec((1, s, LANES), lambda bi, hi: (bi, 0, hi)),
        out_shape=jax.ShapeDtypeStruct((b, s, h * LANES), BF16),
        compiler_params=_cparams("parallel", "parallel"),
    )(qk, qk, vext, lam_vecs, subln_g.reshape(1, LANES))


def rope_table(positions):
    rope_dims = LANES // 8
    inv_freq = ROPE_THETA ** (-jnp.arange(0, rope_dims, 2, dtype=F32) / rope_dims)
    ang = inv_freq[:, None] * positions.astype(F32).reshape(1, -1)
    return jnp.concatenate([jnp.cos(ang), jnp.sin(ang)], axis=0)


def _xattn_kernel(x_ref, ya_ref, yb_ref, wa_ref, wb_ref, gn_ref, wq_ref, kv_ref, qg_ref, kg_ref, wo_ref, o_ref,
                  *, heads):
    x = x_ref[0] + _dot(ya_ref[0], wa_ref[...]) + _dot(yb_ref[0], wb_ref[...])
    d = x.shape[-1] // heads
    q_all = _dot(_rms(x, gn_ref[...]).astype(BF16), wq_ref[...])
    outs = []
    for h in range(heads):
        q = _rms(q_all[:, h * d:(h + 1) * d], qg_ref[...]) * (d ** -0.5)
        k = _rms(kv_ref[0, :, h * d:(h + 1) * d], kg_ref[...])
        v = kv_ref[0, :, (heads + h) * d:(heads + h + 1) * d]
        s = _dot_nt(q.astype(BF16), k.astype(BF16))
        s = s - jnp.max(s, axis=-1, keepdims=True)
        pr = jnp.exp(s)
        o = _dot(pr.astype(BF16), v.astype(BF16)) / jnp.sum(pr, axis=-1, keepdims=True)
        outs.append(o.astype(BF16))
    o_ref[0] = x + _dot(jnp.concatenate(outs, axis=1), wo_ref[...])


def mixer_out_cross_attention(x, ya, yb, wa, wb, gn, wq, kv, q_g, k_g, wo):
    b, s, dm = x.shape
    mlen = kv.shape[1]
    ts = _pick(s, (1024, 512, 256, 128))
    const = lambda a: pl.BlockSpec(a.shape, lambda i, j: (0,) * a.ndim)
    seq = lambda a: pl.BlockSpec((1, ts, a.shape[2]), lambda i, j: (i, j, 0))
    row = lambda a: a.reshape(1, -1)
    return pl.pallas_call(
        functools.partial(_xattn_kernel, heads=XATTN_HEADS),
        grid=(b, s // ts),
        in_specs=[seq(x), seq(ya), seq(yb), const(wa), const(wb),
                  const(row(gn)), const(wq),
                  pl.BlockSpec((1, mlen, 2 * dm), lambda i, j: (i, 0, 0)),
                  const(row(q_g)), const(row(k_g)), const(wo)],
        out_specs=seq(x),
        out_shape=jax.ShapeDtypeStruct((b, s, dm), F32),
        compiler_params=_cparams("parallel", "arbitrary"),
    )(x, ya, yb, wa, wb, row(gn), wq, kv, row(q_g), row(k_g), wo)


def _swiglu(h, wgu, wd):
    f = wd.shape[0]
    tf = _col_tile(h.shape[1], f)
    y = None
    for lo in range(0, f, tf):
        gate = _dot(h, wgu[:, lo:lo + tf])
        up = _dot(h, wgu[:, f + lo:f + lo + tf])
        part = _dot((gate * jax.nn.sigmoid(gate) * up).astype(BF16), wd[lo:lo + tf, :])
        y = part if y is None else y + part
    return y


def _ffn_kernel(x_ref, g_ref, wgu_ref, wd_ref, o_ref):
    x = x_ref[...]
    o_ref[...] = x + _swiglu(_rms(x, g_ref[...]).astype(BF16), wgu_ref, wd_ref)


def ffn_dense(x, g, w_gu, w_down):
    m, d = x.shape
    tm = _pick(m, (1024, 512, 256, 128))
    resident = lambda a: pl.BlockSpec(a.shape, lambda i: (0,) * a.ndim, pipeline_mode=pl.Buffered(1))
    return pl.pallas_call(
        _ffn_kernel,
        grid=(m // tm,),
        in_specs=[pl.BlockSpec((tm, d), lambda i: (i, 0)),
                  pl.BlockSpec((1, d), lambda i: (0, 0)),
                  resident(w_gu), resident(w_down)],
        out_specs=pl.BlockSpec((tm, d), lambda i: (i, 0)),
        out_shape=jax.ShapeDtypeStruct((m, d), F32),
        compiler_params=_cparams("parallel"),
    )(x, g.reshape(1, d), w_gu, w_down)


META_I1, META_I2, META_W1, META_W2, META_R1, META_R2 = range(6)


def _lane_pick(x, lane, idx):
    return jnp.sum(jnp.where(lane == idx, x, 0.0), axis=-1, keepdims=True)


def _router_kernel(x_ref, g_ref, wr_ref, meta_ref, cnt_ref, tri_ref, *, n_experts):
    tm = x_ref.shape[0]

    @pl.when(pl.program_id(0) == 0)
    def _():
        cnt_ref[...] = jnp.zeros_like(cnt_ref)
        r_id = lax.broadcasted_iota(jnp.int32, (tm, tm), 0)
        c_id = lax.broadcasted_iota(jnp.int32, (tm, tm), 1)
        tri_ref[...] = (r_id > c_id).astype(BF16)

    split = lambda t: (t.astype(BF16), (t - t.astype(BF16).astype(F32)).astype(BF16))
    h_hi, h_lo = split(_rms(x_ref[...], g_ref[...]))
    w_hi, w_lo = split(wr_ref[...])
    logits = _dot(h_hi, w_hi) + (_dot(h_hi, w_lo) + _dot(h_lo, w_hi))
    lane = lax.broadcasted_iota(jnp.int32, logits.shape, 1).astype(F32)
    neg = -jnp.inf
    l1 = jnp.where(lane < n_experts, logits, neg)
    m1 = jnp.max(l1, axis=-1, keepdims=True)
    i1 = jnp.min(jnp.where(l1 == m1, lane, float(LANES)), axis=-1, keepdims=True)
    l2 = jnp.where(lane == i1, neg, l1)
    m2 = jnp.max(l2, axis=-1, keepdims=True)
    i2 = jnp.min(jnp.where(l2 == m2, lane, float(LANES)), axis=-1, keepdims=True)
    e2 = jnp.exp(m2 - m1)
    w1 = 1.0 / (1.0 + e2)
    onehot = jnp.where((lane == i1) | (lane == i2), 1.0, 0.0)
    before = _dot(tri_ref[...], onehot.astype(BF16)) + cnt_ref[0:1, :]
    cnt_ref[0:1, :] += jnp.sum(onehot, axis=0, keepdims=True)
    meta = jnp.zeros_like(logits)
    for slot, val in ((META_I1, i1), (META_I2, i2), (META_W1, w1), (META_W2, e2 * w1),
                      (META_R1, _lane_pick(before, lane, i1)), (META_R2, _lane_pick(before, lane, i2))):
        meta = jnp.where(lane == slot, val, meta)
    meta_ref[...] = meta


def router_top2(x, g, w_router):
    m, d = x.shape
    n_experts = w_router.shape[1]
    wr = jnp.zeros((d, LANES), F32).at[:, :n_experts].set(w_router)
    tm = _pick(m, (1024, 512, 256, 128))
    return pl.pallas_call(
        functools.partial(_router_kernel, n_experts=n_experts),
        grid=(m // tm,),
        in_specs=[pl.BlockSpec((tm, d), lambda i: (i, 0)),
                  pl.BlockSpec((1, d), lambda i: (0, 0)),
                  pl.BlockSpec((d, LANES), lambda i: (0, 0))],
        out_specs=[pl.BlockSpec((tm, LANES), lambda i: (i, 0)),
                   pl.BlockSpec((8, LANES), lambda i: (0, 0))],
        out_shape=[jax.ShapeDtypeStruct((m, LANES), F32), jax.ShapeDtypeStruct((8, LANES), F32)],
        scratch_shapes=[pltpu.VMEM((tm, tm), BF16)],
        compiler_params=_cparams("arbitrary"),
    )(x, g.reshape(1, d), wr)


def _idx_copy(dest_hbm, idx_smem, isem, tile, slot):
    return pltpu.make_async_copy(dest_hbm.at[tile], idx_smem.at[slot], isem.at[slot])


def _dispatch_kernel(ends_ref, dest_hbm, x_ref, xs_hbm, idx_smem, zbuf, isem, sem, zsem, *, tm, tg):
    i = pl.program_id(0)
    slot = i % 2
    n_experts = ends_ref.shape[0]
    n_out_tiles = xs_hbm.shape[0] // tg

    @pl.when(i == 0)
    def _():
        _idx_copy(dest_hbm, idx_smem, isem, 0, 0).start()
        zbuf[...] = jnp.zeros_like(zbuf)
        fills = []
        for e in range(n_experts):
            group_start = ends_ref[e - 1] if e > 0 else 0
            fills.append((ends_ref[e] > group_start, ends_ref[e] - tg))
            tail = ends_ref[n_experts - 1] + e * tg
            fills.append((tail < n_out_tiles * tg, tail))
        zero_fill = lambda start: pltpu.make_async_copy(zbuf, xs_hbm.at[pl.ds(pl.multiple_of(start, tg), tg)], zsem)
        for cond, start in fills:
            @pl.when(cond)
            def _():
                zero_fill(start).start()
        for cond, start in fills:
            @pl.when(cond)
            def _():
                zero_fill(start).wait()

    @pl.when(i + 1 < pl.num_programs(0))
    def _():
        _idx_copy(dest_hbm, idx_smem, isem, i + 1, 1 - slot).start()

    _idx_copy(dest_hbm, idx_smem, isem, i, slot).wait()

    def row_copy(t, k):
        d = idx_smem[slot, TOP_K * t + k]
        return pltpu.make_async_copy(x_ref.at[pl.ds(t, 1)], xs_hbm.at[pl.ds(d, 1)], sem)

    def issue(t, carry):
        for k in range(TOP_K):
            row_copy(t, k).start()
        return carry

    lax.fori_loop(0, tm, issue, 0, unroll=8)
    for _ in range(TOP_K):
        pltpu.make_async_copy(x_ref, xs_hbm.at[pl.ds(0, tm)], sem).wait()


def moe_dispatch(x, dest, ends, n_rows, tg):
    m, d = x.shape
    n_tiles, per_tile = dest.shape
    tm = per_tile // TOP_K
    grid_spec = pltpu.PrefetchScalarGridSpec(
        num_scalar_prefetch=1,
        grid=(n_tiles,),
        in_specs=[pl.BlockSpec(memory_space=pl.ANY),
                  pl.BlockSpec((tm, d), lambda i, ends: (i, 0))],
        out_specs=pl.BlockSpec(memory_space=pl.ANY),
        scratch_shapes=[pltpu.SMEM((2, per_tile), jnp.int32), pltpu.VMEM((tg, d), F32),
                        pltpu.SemaphoreType.DMA((2,)), pltpu.SemaphoreType.DMA(()), pltpu.SemaphoreType.DMA(())],
    )
    return pl.pallas_call(
        functools.partial(_dispatch_kernel, tm=tm, tg=tg),
        grid_spec=grid_spec,
        out_shape=jax.ShapeDtypeStruct((n_rows, d), F32),
        compiler_params=_cparams("arbitrary"),
    )(ends, dest, x)


def _moe_ffn_kernel(te_ref, nu_ref, x_ref, g_ref, wgu_ref, wd_ref, o_ref):
    del te_ref

    @pl.when(pl.program_id(0) < nu_ref[0])
    def _():
        o_ref[...] = _swiglu(_rms(x_ref[...], g_ref[...]).astype(BF16), wgu_ref.at[0], wd_ref.at[0])

    @pl.when(pl.program_id(0) >= nu_ref[0])
    def _():
        o_ref[...] = jnp.zeros_like(o_ref)


def moe_grouped_ffn(xs, g, tile_expert, n_used, w_gu, w_down, tm):
    p, d = xs.shape
    f = w_down.shape[1]
    last = lambda n, nu: jnp.minimum(n, nu[0] - 1)
    grid_spec = pltpu.PrefetchScalarGridSpec(
        num_scalar_prefetch=2,
        grid=(p // tm,),
        in_specs=[pl.BlockSpec((tm, d), lambda n, te, nu: (last(n, nu), 0)),
                  pl.BlockSpec((1, d), lambda n, te, nu: (0, 0)),
                  pl.BlockSpec((1, d, 2 * f), lambda n, te, nu: (te[n], 0, 0), pipeline_mode=pl.Buffered(1)),
                  pl.BlockSpec((1, f, d), lambda n, te, nu: (te[n], 0, 0), pipeline_mode=pl.Buffered(1))],
        out_specs=pl.BlockSpec((tm, d), lambda n, te, nu: (n, 0)),
    )
    return pl.pallas_call(
        _moe_ffn_kernel,
        grid_spec=grid_spec,
        out_shape=jax.ShapeDtypeStruct((p, d), F32),
        compiler_params=_cparams("arbitrary"),
    )(tile_expert, n_used, xs, g.reshape(1, d), w_gu, w_down)


def _combine_kernel(dest_hbm, x_ref, meta_ref, ys_hbm, o_ref, idx_smem, ybuf, isem, sem, *, tm, n_tiles):
    i = pl.program_id(0)
    slot = i % 2

    def row_copy(sl, t, k):
        d = idx_smem[sl, TOP_K * t + k]
        return pltpu.make_async_copy(ys_hbm.at[pl.ds(d, 1)], ybuf.at[sl, k, pl.ds(t, 1)], sem.at[sl])

    def gather_tile(tile, sl):
        _idx_copy(dest_hbm, idx_smem, isem, tile, sl).wait()

        def issue(t, carry):
            for k in range(TOP_K):
                row_copy(sl, t, k).start()
            return carry

        lax.fori_loop(0, tm, issue, 0, unroll=8)

    @pl.when(i == 0)
    def _():
        _idx_copy(dest_hbm, idx_smem, isem, 0, 0).start()
        gather_tile(0, 0)
        if n_tiles > 1:
            _idx_copy(dest_hbm, idx_smem, isem, 1, 1).start()

    @pl.when(i + 1 < pl.num_programs(0))
    def _():
        gather_tile(i + 1, 1 - slot)

    @pl.when(i + 2 < pl.num_programs(0))
    def _():
        _idx_copy(dest_hbm, idx_smem, isem, i + 2, slot).start()

    for k in range(TOP_K):
        pltpu.make_async_copy(ys_hbm.at[pl.ds(0, tm)], ybuf.at[slot, k], sem.at[slot]).wait()
    meta = meta_ref[...]
    lane = lax.broadcasted_iota(jnp.int32, meta.shape, 1)
    w1 = _lane_pick(meta, lane, META_W1)
    w2 = _lane_pick(meta, lane, META_W2)
    o_ref[...] = x_ref[...] + w1 * ybuf[slot, 0] + w2 * ybuf[slot, 1]


def moe_combine(x, meta, dest, ys):
    m, d = x.shape
    n_tiles, per_tile = dest.shape
    tm = per_tile // TOP_K
    return pl.pallas_call(
        functools.partial(_combine_kernel, tm=tm, n_tiles=n_tiles),
        grid=(n_tiles,),
        in_specs=[pl.BlockSpec(memory_space=pl.ANY),
                  pl.BlockSpec((tm, d), lambda i: (i, 0)),
                  pl.BlockSpec((tm, LANES), lambda i: (i, 0)),
                  pl.BlockSpec(memory_space=pl.ANY)],
        out_specs=pl.BlockSpec((tm, d), lambda i: (i, 0)),
        out_shape=jax.ShapeDtypeStruct((m, d), F32),
        scratch_shapes=[pltpu.SMEM((2, per_tile), jnp.int32), pltpu.VMEM((2, TOP_K, tm, d), F32),
                        pltpu.SemaphoreType.DMA((2,)), pltpu.SemaphoreType.DMA((2,))],
        compiler_params=_cparams("arbitrary"),
    )(dest, x, meta, ys)


def ffn_moe(x, g, w_router, w_gu, w_down):
    m, d = x.shape
    n_experts = w_down.shape[0]
    tg = _pick(m, (MOE_GROUP_TILE, 256, 128))
    td = _pick(m, (MOE_TOKEN_TILE, 256, 128))
    meta, counts = router_top2(x, g, w_router)
    counts = counts[0, :n_experts].astype(jnp.int32)
    padded = (counts + tg - 1) // tg * tg
    ends = jnp.cumsum(padded)
    offsets = ends - padded
    n_tiles = (TOP_K * m) // tg + n_experts
    n_used = (ends[-1] // tg).astype(jnp.int32)
    tile_id = jnp.minimum(jnp.arange(n_tiles, dtype=jnp.int32), n_used - 1)
    tile_expert = jnp.sum(ends[None, :] <= (tile_id * tg)[:, None], axis=1).astype(jnp.int32)
    picks = meta[:, META_I1:META_I2 + 1].astype(jnp.int32)
    ranks = meta[:, META_R1:META_R2 + 1].astype(jnp.int32)
    dest = (offsets[picks] + ranks).reshape(m // td, TOP_K * td)
    xs = moe_dispatch(x, dest, ends.astype(jnp.int32), n_tiles * tg, tg)
    ys = moe_grouped_ffn(xs, g, tile_expert, n_used.reshape(1), w_gu, w_down, tg)
    return moe_combine(x, meta, dest, ys)


def _block_diag(w_a, w_b):
    za = jnp.zeros_like(w_a)
    zb = jnp.zeros_like(w_b)
    return jnp.concatenate([jnp.concatenate([w_a, za], axis=1), jnp.concatenate([zb, w_b], axis=1)], axis=0)


def kernel(x, mem, positions, mix_norm, in_proj, shift_mu, decay_bias, decay_up, iclr_bias, iclr_up, gate_up,
           key_kk_scale, key_iclr_scale, bonus_rk, rwkv_gn_w, rwkv_gn_b, vres_down, vres_shift_mu, vres_bias,
           vres_up, q_norm, k_norm, lambda_q1, lambda_k1, lambda_q2, lambda_k2, diff_subln, out_proj, xattn_norm,
           mem_norm, xattn_wq, xattn_wkv, xattn_wo, xattn_q_norm, xattn_k_norm, ffn_norm, dense_w_gu,
           dense_w_down, router, expert_w_gu, expert_w_down):
    b, s, d = x.shape
    depth = mix_norm.shape[0]
    width = decay_bias.shape[1]
    rwkv_cols = shift_mu.shape[1]
    in_cols = in_proj.shape[2]
    vres_rank = vres_down.shape[2] if depth > 1 else 0
    assert decay_up.shape[1] == LANES // 2 and iclr_up.shape[1] == LANES // 2 and gate_up.shape[1] == LANES
    assert rwkv_cols == 3 * width + 2 * LANES and rwkv_cols % LANES == 0 and in_cols % LANES == 0
    assert vres_rank <= LANES
    rot = rope_table(positions)
    xf = x.reshape(b * s, d)
    memf = mem.reshape(-1, d)
    v_first = None
    for l in range(depth):
        w_rwkv = in_proj[l][:, :rwkv_cols]
        vres = None
        if l > 0:
            pad = jnp.zeros((d, LANES - vres_rank), F32)
            w_rwkv = jnp.concatenate([w_rwkv, vres_down[l - 1], pad], axis=1)
            vmu = jnp.zeros((LANES,), F32).at[:vres_rank].set(vres_shift_mu[l - 1])
            vup = jnp.zeros((LANES, width), F32).at[:vres_rank].set(vres_up[l - 1]).astype(BF16)
            vres = (v_first, vmu, vres_bias[l - 1], vup)
        wwa = _block_diag(decay_up[l], iclr_up[l]).astype(BF16)
        r_, lw_, k_, v_, a_, g_ = rwkv_in_proj(xf, s, mix_norm[l], w_rwkv.astype(BF16), width, shift_mu[l],
                                               decay_bias[l], iclr_bias[l], wwa, gate_up[l].astype(BF16), vres)
        if l == 0:
            v_first = v_
        seq3 = lambda t: t.reshape(b, s, width)
        y_rwkv = rwkv_chunked(seq3(r_), seq3(lw_), seq3(k_), seq3(v_), seq3(a_), seq3(g_), key_kk_scale[l],
                              key_iclr_scale[l], bonus_rk[l].reshape(-1), rwkv_gn_w[l], rwkv_gn_b[l])
        lam_init = 0.8 - 0.6 * math.exp(-0.3 * l)
        lam_vecs = jnp.stack([lambda_q1[l], lambda_k1[l], lambda_q2[l], lambda_k2[l]])
        y_diff = diff_attention(xf, b, mix_norm[l], in_proj[l][:, rwkv_cols:].astype(BF16), rot, q_norm[l],
                                k_norm[l], lam_vecs, diff_subln[l], lam_init)
        w_out = out_proj[l].astype(BF16)
        kv = norm_matmul(memf, mem_norm[l], xattn_wkv[l].astype(BF16)).reshape(b, -1, 2 * d)
        xf = mixer_out_cross_attention(xf.reshape(b, s, d), y_rwkv, y_diff, w_out[:width], w_out[width:],
                                       xattn_norm[l], xattn_wq[l].astype(BF16), kv, xattn_q_norm[l],
                                       xattn_k_norm[l], xattn_wo[l].astype(BF16)).reshape(b * s, d)
        if l % 2 == 0:
            xf = ffn_dense(xf, ffn_norm[l], dense_w_gu[l // 2].astype(BF16), dense_w_down[l // 2].astype(BF16))
        else:
            xf = ffn_moe(xf, ffn_norm[l], router[l // 2], expert_w_gu[l // 2].astype(BF16),
                         expert_w_down[l // 2].astype(BF16))
    return xf.reshape(b, s, d)
```

```python
import functools
import math

import jax
import jax.numpy as jnp
from jax import lax
from jax.experimental import pallas as pl
from jax.experimental.pallas import tpu as pltpu

F32 = jnp.float32
BF16 = jnp.bfloat16

EPS = 1e-6
GN_EPS = 64e-5
ROPE_THETA = 500000.0
RWKV_HEAD = 64
DIFF_HEADS = 4
XATTN_HEADS = 4
TOP_K = 2
LANES = 128
CHUNK = 128
VMEM_LIMIT = 56 * 1024 * 1024
RWKV_ROWS_PER_STEP = 4
RWKV_IN_ROW_PARTS = 4
ATTN_ROW_PARTS = 4
WEIGHT_TILE_BYTES = 4 * 1024 * 1024
MOE_GROUP_TILE = 512
MOE_DISPATCH_TILE = 2048
MOE_COMBINE_TILE = 1024


def _cparams(*sem):
    return pltpu.CompilerParams(dimension_semantics=sem, vmem_limit_bytes=VMEM_LIMIT)


def _pick(n, prefs):
    for p in prefs:
        if n % p == 0:
            return p
    return n


def _col_tile(k, n):
    best = LANES
    for t in range(LANES, n + 1, LANES):
        if n % t == 0 and k * t * 2 <= WEIGHT_TILE_BYTES:
            best = t
    return best


def _dot(a, b):
    return jnp.dot(a, b, preferred_element_type=F32)


def _dot_nt(a, b):
    return lax.dot_general(a, b, (((1,), (1,)), ((), ())), preferred_element_type=F32)


def _rms(x, g):
    return x * lax.rsqrt(jnp.mean(x * x, axis=-1, keepdims=True) + EPS) * g


def _norm_mm_kernel(x_ref, g_ref, w_ref, o_ref, h_ref):
    @pl.when(pl.program_id(1) == 0)
    def _():
        h_ref[...] = _rms(x_ref[...], g_ref[...]).astype(BF16)

    o_ref[...] = _dot(h_ref[...], w_ref[...])


def norm_matmul(x, g, w):
    m, k = x.shape
    n = w.shape[1]
    tm = _pick(m, (1024, 512, 256, 128))
    tn = _col_tile(k, n)
    return pl.pallas_call(
        _norm_mm_kernel,
        grid=(m // tm, n // tn),
        in_specs=[pl.BlockSpec((tm, k), lambda i, j: (i, 0)),
                  pl.BlockSpec((1, k), lambda i, j: (0, 0)),
                  pl.BlockSpec((k, tn), lambda i, j: (0, j))],
        out_specs=pl.BlockSpec((tm, tn), lambda i, j: (i, j)),
        out_shape=jax.ShapeDtypeStruct((m, n), F32),
        scratch_shapes=[pltpu.VMEM((tm, k), BF16)],
        compiler_params=_cparams("parallel", "arbitrary"),
    )(x, g.reshape(1, k), w)


def _token_shift(p, prev_row, mu):
    prev = pltpu.roll(p, 1, axis=0)
    rid = lax.broadcasted_iota(jnp.int32, p.shape, 0)
    prev = jnp.where(rid == 0, prev_row, prev)
    return p + (prev - p) * mu


def _rwkv_in_kernel(*refs, width, has_vres, tiles_per_seq):
    if has_vres:
        (x_ref, gn_ref, w_ref, vf_ref, mu_ref, w0_ref, a0_ref, wwa_ref, gup_ref, vmu_ref, vb_ref, vup_ref,
         r_ref, lw_ref, k_ref, v_ref, a_ref, g_ref, carry_ref, carry_v_ref) = refs
    else:
        (x_ref, gn_ref, w_ref, mu_ref, w0_ref, a0_ref, wwa_ref, gup_ref,
         r_ref, lw_ref, k_ref, v_ref, a_ref, g_ref, carry_ref) = refs
    ncol = 3 * width + 2 * LANES
    tm = x_ref.shape[0]
    n_parts = RWKV_IN_ROW_PARTS
    rp = tm // n_parts

    @pl.when(pl.program_id(0) % tiles_per_seq == 0)
    def _():
        carry_ref[...] = jnp.zeros_like(carry_ref)
        if has_vres:
            carry_v_ref[...] = jnp.zeros_like(carry_v_ref)

    projs = [_dot(_rms(x_ref[i * rp:(i + 1) * rp, :], gn_ref[...]).astype(BF16), w_ref[...]) for i in range(n_parts)]
    prev = carry_ref[...]
    prev_v = carry_v_ref[...] if has_vres else None
    for i, proj in enumerate(projs):
        rows = slice(i * rp, (i + 1) * rp)
        sh = _token_shift(proj[:, :ncol], prev, mu_ref[...])
        prev = proj[rp - 1:rp, :ncol]
        r_ref[rows, :] = sh[:, :width].astype(r_ref.dtype)
        k_ref[rows, :] = sh[:, width:2 * width].astype(k_ref.dtype)
        v = sh[:, 2 * width:3 * width]
        dwa = sh[:, 3 * width:3 * width + LANES]
        lane = lax.broadcasted_iota(jnp.int32, dwa.shape, 1)
        dwa = jnp.where(lane < LANES // 2, jnp.tanh(dwa), dwa)
        wa = _dot(dwa.astype(BF16), wwa_ref[...])
        z = -(w0_ref[...] + wa[:, :width])
        softplus = jnp.maximum(z, 0.0) + jnp.log(1.0 + jnp.exp(-jnp.abs(z)))
        lw_ref[rows, :] = -jnp.exp(-softplus - 0.5)
        a_ref[rows, :] = jax.nn.sigmoid(a0_ref[...] + wa[:, width:]).astype(a_ref.dtype)
        dg = jax.nn.sigmoid(sh[:, 3 * width + LANES:3 * width + 2 * LANES])
        g_ref[rows, :] = _dot(dg.astype(BF16), gup_ref[...]).astype(g_ref.dtype)
        if has_vres:
            shv = _token_shift(proj[:, ncol:ncol + LANES], prev_v, vmu_ref[...])
            prev_v = proj[rp - 1:rp, ncol:ncol + LANES]
            mix = jax.nn.sigmoid(vb_ref[...] + _dot(shv.astype(BF16), vup_ref[...]))
            v = v + (vf_ref[rows, :].astype(F32) - v) * mix
        v_ref[rows, :] = v.astype(v_ref.dtype)
    carry_ref[...] = prev
    if has_vres:
        carry_v_ref[...] = prev_v


def rwkv_in_proj(x, seq_len, gn, w, width, mu, w0, a0, wwa, gup, vres=None):
    m, d = x.shape
    tm = _pick(seq_len, (1024, 512, 256, 128))
    ncol = 3 * width + 2 * LANES
    assert mu.shape[-1] == ncol and m % seq_len == 0
    row = lambda a: a.reshape(1, -1)
    full = lambda a: pl.BlockSpec(a.shape, lambda i: (0,) * a.ndim)
    rows = lambda c: pl.BlockSpec((tm, c), lambda i: (i, 0))
    params = [row(mu), row(w0), row(a0), wwa, gup]
    inputs = [x, row(gn), w]
    in_specs = [rows(d), full(row(gn)), full(w)]
    scratch = [pltpu.VMEM((1, ncol), F32)]
    if vres is not None:
        v_first, vmu, vb, vup = vres
        inputs.append(v_first)
        in_specs.append(rows(width))
        params += [row(vmu), row(vb), vup]
        scratch.append(pltpu.VMEM((1, LANES), F32))
    in_specs += [full(p) for p in params]
    out = lambda dt: jax.ShapeDtypeStruct((m, width), dt)
    return pl.pallas_call(
        functools.partial(_rwkv_in_kernel, width=width, has_vres=vres is not None, tiles_per_seq=seq_len // tm),
        grid=(m // tm,),
        in_specs=in_specs,
        out_specs=[rows(width)] * 6,
        out_shape=[out(BF16), out(F32), out(BF16), out(BF16), out(BF16), out(BF16)],
        scratch_shapes=scratch,
        compiler_params=_cparams("arbitrary"),
    )(*inputs, *params)


def _rwkv_chunk_kernel(r_ref, lw_ref, k_ref, v_ref, a_ref, g_ref, kk_ref, ka_ref, rk_ref, gw_ref, gb_ref,
                       y_ref, state_ref, *, n_pairs):
    c = CHUNK
    half = LANES // 2

    @pl.when(pl.program_id(1) == 0)
    def _():
        state_ref[...] = jnp.zeros_like(state_ref)

    row = lax.broadcasted_iota(jnp.int32, (c, c), 0)
    col = lax.broadcasted_iota(jnp.int32, (c, c), 1)
    ltri = (row >= col).astype(BF16)
    same_head = (row < half) == (col < half)
    lo = col < half
    row4 = lax.broadcasted_iota(jnp.int32, (c, 4 * c), 0)
    col4 = lax.broadcasted_iota(jnp.int32, (c, 4 * c), 1)
    incl4 = row4 >= (col4 & (c - 1))
    row2 = lax.broadcasted_iota(jnp.int32, (c, 2 * c), 0)
    col2 = lax.broadcasted_iota(jnp.int32, (c, 2 * c), 1)
    strict2 = row2 > (col2 & (c - 1))
    first_block = col2 < c

    def gsum(x):
        s_lo = jnp.sum(jnp.where(lo, x, 0.0), axis=-1, keepdims=True)
        s_hi = jnp.sum(jnp.where(lo, 0.0, x), axis=-1, keepdims=True)
        return jnp.where(lo, s_lo, s_hi)

    def by_head(x):
        return jnp.concatenate([jnp.where(lo, x, 0.0), jnp.where(lo, 0.0, x)], axis=0).astype(BF16)

    def split_bf16(x):
        hi = x.astype(BF16)
        rest = x - hi.astype(F32)
        mid = rest.astype(BF16)
        return hi, mid, (rest - mid.astype(F32)).astype(BF16)

    def stage_factors(bi, p):
        sl = slice(p * LANES, (p + 1) * LANES)
        r, lw, k, v, a = (ref[bi, :, sl].astype(F32) for ref in (r_ref, lw_ref, k_ref, v_ref, a_ref))
        cum3 = _dot(ltri, jnp.concatenate(split_bf16(lw), axis=1))
        cum = cum3[:, :LANES] + cum3[:, LANES:2 * LANES] + cum3[:, 2 * LANES:]
        cmid = cum[c // 2 - 1:c // 2, :]
        clast = cum[c - 1:c, :]
        ci = cum - cmid
        e_neg = jnp.exp(-ci)
        p_mid = jnp.exp(cmid)
        kk = k * kk_ref[:, sl]
        kkn = kk * lax.rsqrt(jnp.maximum(gsum(kk * kk), 1e-24))
        k2 = k * (1.0 + (a - 1.0) * ka_ref[:, sl])
        beta = kkn * a
        al_m = -kkn * jnp.exp(ci - lw)
        r_m = r * jnp.exp(ci)
        e_end = jnp.exp(clast - cum)
        ends = jnp.concatenate([beta * e_end, k2 * e_end], axis=0).astype(BF16)
        al2, r2 = by_head(al_m), by_head(r_m)
        lhs = jnp.concatenate([al2[:c], r2[:c], al2[c:], r2[c:]], axis=0)
        rhs = jnp.concatenate([beta * e_neg, k2 * e_neg], axis=0).astype(BF16)
        big = _dot_nt(lhs, rhs)
        s0 = state_ref[bi * n_pairs + p]
        base = _dot_nt(jnp.concatenate([al_m * p_mid, r_m * p_mid], axis=0).astype(BF16), s0.astype(BF16))
        return dict(big=big, base=base, v2=by_head(v), ends=ends, decayed=s0 * jnp.exp(clast),
                    bonus=gsum(r * k2 * rk_ref[:, sl]) * v)

    def stage_masks(d):
        big = d.pop("big")
        blk = lambda i, j: big[i * c:(i + 1) * c, j * c:(j + 1) * c]
        zero2 = jnp.zeros((c, 2 * c), F32)
        d["pw"] = jnp.where(strict2, jnp.concatenate([blk(0, 0), blk(2, 0)], axis=1), zero2).astype(BF16)
        a_ak = jnp.where(strict2, jnp.concatenate([blk(0, 1), blk(2, 1)], axis=1), zero2).astype(BF16)
        d["a_y"] = jnp.where(incl4, jnp.concatenate([blk(1, 0), blk(3, 0), blk(1, 1), blk(3, 1)], axis=1),
                             0.0).astype(BF16)
        d["x"] = d["base"][:c] + _dot(a_ak, d["v2"])

    def stage_square(d, last):
        pw, x2 = d["pw"], by_head(d["x"])
        if last:
            d["x"] = d["x"] + _dot(pw, x2)
            return
        zero_b = jnp.zeros_like(pw)
        bd = jnp.concatenate([jnp.where(first_block, pw, zero_b), jnp.where(first_block, zero_b, pw)], axis=0)
        res = _dot(pw, jnp.concatenate([bd, x2], axis=1))
        d["pw"] = res[:, :2 * c].astype(BF16)
        d["x"] = d["x"] + res[:, 2 * c:]

    def stage_output(bi, p, d):
        sl = slice(p * LANES, (p + 1) * LANES)
        x = d["x"]
        y = d["base"][c:] + _dot(d["a_y"], jnp.concatenate([by_head(x), d["v2"]], axis=0))
        upd = _dot(jnp.concatenate([x.T, v_ref[bi, :, sl].astype(F32).T], axis=1).astype(BF16), d["ends"])
        state_ref[bi * n_pairs + p] = d["decayed"] + jnp.where(same_head, upd, 0.0)
        mean = gsum(y) * (1.0 / half)
        yc = y - mean
        var = gsum(yc * yc) * (1.0 / half)
        out = yc * lax.rsqrt(var + GN_EPS) * gw_ref[:, sl] + gb_ref[:, sl]
        y_ref[bi, :, sl] = ((out + d["bonus"]) * g_ref[bi, :, sl].astype(F32)).astype(y_ref.dtype)

    ids = [(bi, p) for bi in range(r_ref.shape[0]) for p in range(n_pairs)]
    pairs = [stage_factors(bi, p) for bi, p in ids]
    for d in pairs:
        stage_masks(d)
    n_steps = c.bit_length() - 1
    for step in range(n_steps):
        for d in pairs:
            stage_square(d, last=step + 1 == n_steps)
    for (bi, p), d in zip(ids, pairs):
        stage_output(bi, p, d)


def rwkv_chunked(r, lw, k, v, a, g, k_k, k_a, r_k, gn_w, gn_b):
    b, s, width = r.shape
    assert RWKV_HEAD * 2 == LANES and width % LANES == 0 and s % CHUNK == 0
    n_pairs = width // LANES
    nb = _pick(b, (RWKV_ROWS_PER_STEP, 1))
    seq = pl.BlockSpec((nb, CHUNK, width), lambda i, j: (i, j, 0))
    par = pl.BlockSpec((1, width), lambda i, j: (0, 0))
    row = lambda t: t.reshape(1, width)
    return pl.pallas_call(
        functools.partial(_rwkv_chunk_kernel, n_pairs=n_pairs),
        grid=(b // nb, s // CHUNK),
        in_specs=[seq] * 6 + [par] * 5,
        out_specs=seq,
        out_shape=jax.ShapeDtypeStruct((b, s, width), BF16),
        scratch_shapes=[pltpu.VMEM((nb * n_pairs, LANES, LANES), F32)],
        compiler_params=_cparams("parallel", "arbitrary"),
    )(r, lw, k, v, a, g, row(k_k), row(k_a), row(r_k), row(gn_w), row(gn_b))


def _diff_in_kernel(x_ref, gn_ref, w_ref, rot_ref, qg_ref, kg_ref, qk_ref, v_ref, *, heads):
    half = LANES // 2
    rope_half = half // 8
    proj = _dot(_rms(x_ref[...], gn_ref[...]).astype(BF16), w_ref[...])
    lane = lax.broadcasted_iota(jnp.int32, (x_ref.shape[0], LANES), 1)
    pos = lane & (half - 1)
    e_row = lax.broadcasted_iota(jnp.int32, (2 * rope_half, LANES), 0)
    e_col = lax.broadcasted_iota(jnp.int32, (2 * rope_half, LANES), 1)
    expand = ((e_col & (half - 1)) == e_row).astype(BF16)
    cs = rot_ref[...]
    rot = None
    for _ in range(3):
        piece = cs.astype(BF16)
        term = lax.dot_general(piece, expand, (((0,), (0,)), ((), ())), preferred_element_type=F32)
        rot = term if rot is None else rot + term
        cs = cs - piece.astype(F32)
    first, second = pos < rope_half, (pos >= rope_half) & (pos < 2 * rope_half)
    cf = jnp.where(first, rot, jnp.where(second, pltpu.roll(rot, rope_half, axis=1), 1.0))
    sn = jnp.where(first, -pltpu.roll(rot, LANES - rope_half, axis=1), jnp.where(second, rot, 0.0))

    g_row = lax.broadcasted_iota(jnp.int32, (LANES, LANES), 0)
    g_col = lax.broadcasted_iota(jnp.int32, (LANES, LANES), 1)
    same_half = ((g_row < half) == (g_col < half)).astype(BF16)
    c_pos = g_col & (half - 1)
    partner = jnp.where(c_pos < rope_half, g_col + rope_half, jnp.where(c_pos < 2 * rope_half, g_col - rope_half, -1))
    swap = (g_row == partner).astype(BF16)

    def norm_rope(x, g):
        ms = _dot((x * x).astype(BF16), same_half) * (1.0 / half)
        x = x * lax.rsqrt(ms + EPS) * g
        return x * cf + _dot(x.astype(BF16), swap) * sn

    ones = jnp.ones((x_ref.shape[0], LANES), BF16)
    for h in range(heads):
        tile = lambda i: proj[:, (i * heads + h) * LANES:(i * heads + h + 1) * LANES]
        q = norm_rope(tile(0), qg_ref[...]) * (half ** -0.5 * math.log2(math.e))
        qk_ref[:, h * LANES:(h + 1) * LANES] = q.astype(BF16)
        qk_ref[:, (heads + h) * LANES:(heads + h + 1) * LANES] = norm_rope(tile(1), kg_ref[...]).astype(BF16)
        v_ref[:, 2 * h * LANES:(2 * h + 1) * LANES] = tile(2).astype(BF16)
        v_ref[:, (2 * h + 1) * LANES:(2 * h + 2) * LANES] = ones


def _diff_attn_kernel(q_ref, k_ref, v_ref, lam_ref, sub_ref, o_ref, *, tq, lam_init):
    half = LANES // 2
    s_len = q_ref.shape[1]
    lv = lam_ref[...]
    lam = (jnp.exp(jnp.sum(lv[0:1] * lv[1:2], axis=-1, keepdims=True))
           - jnp.exp(jnp.sum(lv[2:3] * lv[3:4], axis=-1, keepdims=True)) + lam_init)
    lane = lax.broadcasted_iota(jnp.int32, (tq, LANES), 1)
    n_parts = ATTN_ROW_PARTS
    rp = tq // n_parts
    chains = [(m, part) for part in range(n_parts) for m in range(2)]

    def block(qs, j, carry, masked):
        kb = k_ref[0, j * tq:(j + 1) * tq, :]
        vb = v_ref[0, j * tq:(j + 1) * tq, :]
        scores, probs, maxes, out = {}, {}, {}, {}
        for i in range(len(chains) + 2):
            if i < len(chains):
                m, part = chains[i]
                keys = (part + 1) * rp if masked else tq
                s = _dot_nt(qs[m][part * rp:(part + 1) * rp], kb[:keys])
                if masked:
                    r_id = lax.broadcasted_iota(jnp.int32, s.shape, 0) + part * rp
                    c_id = lax.broadcasted_iota(jnp.int32, s.shape, 1)
                    s = jnp.where(c_id <= r_id, s, -1e30)
                scores[i] = s
            if 0 <= i - 1 < len(chains):
                c = i - 1
                s = scores.pop(c)
                mx = jnp.max(s, axis=-1, keepdims=True)
                maxes[c] = mx if carry is None else jnp.maximum(carry[2 * c], mx)
                probs[c] = jnp.exp2((s - maxes[c]).astype(BF16))
            if 0 <= i - 2 < len(chains):
                c = i - 2
                pr = probs.pop(c)
                pv = _dot(pr, vb[:pr.shape[1]])
                out[c] = pv if carry is None else jnp.exp2(carry[2 * c] - maxes[c]) * carry[2 * c + 1] + pv
        return tuple(t for c in range(len(chains)) for t in (maxes[c], out[c]))

    for qi in range(s_len // tq):
        q = q_ref[0, qi * tq:(qi + 1) * tq, :]
        zero = jnp.zeros_like(q)
        qs = [jnp.where(lane < half, q, zero), jnp.where(lane < half, zero, q)]
        carry = block(qs, qi, None, True)
        for j in range(qi):
            carry = block(qs, j, carry, False)
        acc1 = jnp.concatenate([carry[2 * c + 1] for c, (m, _) in enumerate(chains) if m == 0], axis=0)
        acc2 = jnp.concatenate([carry[2 * c + 1] for c, (m, _) in enumerate(chains) if m == 1], axis=0)
        o = acc1[:, :LANES] / acc1[:, LANES:] - lam * (acc2[:, :LANES] / acc2[:, LANES:])
        o_ref[0, qi * tq:(qi + 1) * tq, :] = (_rms(o, sub_ref[...]) * (1.0 - lam_init)).astype(o_ref.dtype)


def diff_attention(x, b, gn, w, rot, q_g, k_g, lam_vecs, subln_g, lam_init):
    m, d = x.shape
    s = m // b
    h = DIFF_HEADS
    assert w.shape[1] == 3 * h * LANES
    tm = _pick(m, (1024, 512, 256, 128))
    tile2 = lambda g: jnp.tile(g, 2).reshape(1, LANES)
    rows = lambda c: pl.BlockSpec((tm, c), lambda i: (i, 0))
    full = lambda a: pl.BlockSpec(a.shape, lambda i: (0,) * a.ndim)
    gspec2 = pl.BlockSpec((1, LANES), lambda i: (0, 0))
    qk, vext = pl.pallas_call(
        functools.partial(_diff_in_kernel, heads=h),
        grid=(m // tm,),
        in_specs=[rows(d), pl.BlockSpec((1, d), lambda i: (0, 0)), full(w),
                  pl.BlockSpec((rot.shape[0], tm), lambda i: (0, i)), gspec2, gspec2],
        out_specs=[rows(2 * h * LANES), rows(2 * h * LANES)],
        out_shape=[jax.ShapeDtypeStruct((m, 2 * h * LANES), BF16)] * 2,
        compiler_params=_cparams("parallel"),
    )(x, gn.reshape(1, d), w, rot, tile2(q_g), tile2(k_g))
    qk = qk.reshape(b, s, 2 * h * LANES)
    vext = vext.reshape(b, s, 2 * h * LANES)

    tq = _pick(s, (1024, 512, 256, 128))
    gspec = pl.BlockSpec((1, LANES), lambda bi, hi: (0, 0))
    return pl.pallas_call(
        functools.partial(_diff_attn_kernel, tq=tq, lam_init=lam_init),
        grid=(b, h),
        in_specs=[pl.BlockSpec((1, s, LANES), lambda bi, hi: (bi, 0, hi)),
                  pl.BlockSpec((1, s, LANES), lambda bi, hi: (bi, 0, h + hi)),
                  pl.BlockSpec((1, s, 2 * LANES), lambda bi, hi: (bi, 0, hi)),
                  pl.BlockSpec(lam_vecs.shape, lambda bi, hi: (0, 0)),
                  gspec],
        out_specs=pl.BlockSpec((1, s, LANES), lambda bi, hi: (bi, 0, hi)),
        out_shape=jax.ShapeDtypeStruct((b, s, h * LANES), BF16),
        compiler_params=_cparams("parallel", "parallel"),
    )(qk, qk, vext, lam_vecs, subln_g.reshape(1, LANES))


def rope_table(positions):
    rope_dims = LANES // 8
    inv_freq = ROPE_THETA ** (-jnp.arange(0, rope_dims, 2, dtype=F32) / rope_dims)
    ang = inv_freq[:, None] * positions.astype(F32).reshape(1, -1)
    return jnp.concatenate([jnp.cos(ang), jnp.sin(ang)], axis=0)


def _xattn_kernel(x_ref, ya_ref, yb_ref, wa_ref, wb_ref, gn_ref, wq_ref, kv_ref, qg_ref, kg_ref, wo_ref, o_ref,
                  *, heads):
    x = x_ref[0] + _dot(ya_ref[0], wa_ref[...]) + _dot(yb_ref[0], wb_ref[...])
    d = x.shape[-1] // heads
    q_all = _dot(_rms(x, gn_ref[...]).astype(BF16), wq_ref[...])
    outs = []
    for h in range(heads):
        q = _rms(q_all[:, h * d:(h + 1) * d], qg_ref[...]) * (d ** -0.5)
        k = _rms(kv_ref[0, :, h * d:(h + 1) * d], kg_ref[...])
        v = kv_ref[0, :, (heads + h) * d:(heads + h + 1) * d]
        s = _dot_nt(q.astype(BF16), k.astype(BF16))
        s = s - jnp.max(s, axis=-1, keepdims=True)
        pr = jnp.exp(s)
        o = _dot(pr.astype(BF16), v.astype(BF16)) / jnp.sum(pr, axis=-1, keepdims=True)
        outs.append(o.astype(BF16))
    o_ref[0] = x + _dot(jnp.concatenate(outs, axis=1), wo_ref[...])


def mixer_out_cross_attention(x, ya, yb, wa, wb, gn, wq, kv, q_g, k_g, wo):
    b, s, dm = x.shape
    mlen = kv.shape[1]
    ts = _pick(s, (1024, 512, 256, 128))
    const = lambda a: pl.BlockSpec(a.shape, lambda i, j: (0,) * a.ndim)
    seq = lambda a: pl.BlockSpec((1, ts, a.shape[2]), lambda i, j: (i, j, 0))
    row = lambda a: a.reshape(1, -1)
    return pl.pallas_call(
        functools.partial(_xattn_kernel, heads=XATTN_HEADS),
        grid=(b, s // ts),
        in_specs=[seq(x), seq(ya), seq(yb), const(wa), const(wb),
                  const(row(gn)), const(wq),
                  pl.BlockSpec((1, mlen, 2 * dm), lambda i, j: (i, 0, 0)),
                  const(row(q_g)), const(row(k_g)), const(wo)],
        out_specs=seq(x),
        out_shape=jax.ShapeDtypeStruct((b, s, dm), F32),
        compiler_params=_cparams("parallel", "arbitrary"),
    )(x, ya, yb, wa, wb, row(gn), wq, kv, row(q_g), row(k_g), wo)


def _swiglu(h, wgu, wd):
    f = wd.shape[0]
    tf = _col_tile(h.shape[1], f)
    y = None
    for lo in range(0, f, tf):
        gate = _dot(h, wgu[:, lo:lo + tf])
        up = _dot(h, wgu[:, f + lo:f + lo + tf])
        part = _dot((gate * jax.nn.sigmoid(gate) * up).astype(BF16), wd[lo:lo + tf, :])
        y = part if y is None else y + part
    return y


def _ffn_kernel(x_ref, g_ref, wgu_ref, wd_ref, o_ref):
    x = x_ref[...]
    o_ref[...] = x + _swiglu(_rms(x, g_ref[...]).astype(BF16), wgu_ref, wd_ref)


def ffn_dense(x, g, w_gu, w_down):
    m, d = x.shape
    tm = _pick(m, (1024, 512, 256, 128))
    resident = lambda a: pl.BlockSpec(a.shape, lambda i: (0,) * a.ndim, pipeline_mode=pl.Buffered(1))
    return pl.pallas_call(
        _ffn_kernel,
        grid=(m // tm,),
        in_specs=[pl.BlockSpec((tm, d), lambda i: (i, 0)),
                  pl.BlockSpec((1, d), lambda i: (0, 0)),
                  resident(w_gu), resident(w_down)],
        out_specs=pl.BlockSpec((tm, d), lambda i: (i, 0)),
        out_shape=jax.ShapeDtypeStruct((m, d), F32),
        compiler_params=_cparams("parallel"),
    )(x, g.reshape(1, d), w_gu, w_down)


META_I1, META_I2, META_W1, META_W2, META_R1, META_R2 = range(6)


def _lane_pick(x, lane, idx):
    return jnp.sum(jnp.where(lane == idx, x, 0.0), axis=-1, keepdims=True)


def _router_kernel(x_ref, g_ref, wr_ref, meta_ref, cnt_ref, tri_ref, *, n_experts):
    tm = x_ref.shape[0]

    @pl.when(pl.program_id(0) == 0)
    def _():
        cnt_ref[...] = jnp.zeros_like(cnt_ref)
        r_id = lax.broadcasted_iota(jnp.int32, (tm, tm), 0)
        c_id = lax.broadcasted_iota(jnp.int32, (tm, tm), 1)
        tri_ref[...] = (r_id > c_id).astype(BF16)

    split = lambda t: (t.astype(BF16), (t - t.astype(BF16).astype(F32)).astype(BF16))
    h_hi, h_lo = split(_rms(x_ref[...], g_ref[...]))
    w_hi, w_lo = split(wr_ref[...])
    logits = _dot(h_hi, w_hi) + (_dot(h_hi, w_lo) + _dot(h_lo, w_hi))
    lane = lax.broadcasted_iota(jnp.int32, logits.shape, 1).astype(F32)
    neg = -jnp.inf
    l1 = jnp.where(lane < n_experts, logits, neg)
    m1 = jnp.max(l1, axis=-1, keepdims=True)
    i1 = jnp.min(jnp.where(l1 == m1, lane, float(LANES)), axis=-1, keepdims=True)
    l2 = jnp.where(lane == i1, neg, l1)
    m2 = jnp.max(l2, axis=-1, keepdims=True)
    i2 = jnp.min(jnp.where(l2 == m2, lane, float(LANES)), axis=-1, keepdims=True)
    e2 = jnp.exp(m2 - m1)
    w1 = 1.0 / (1.0 + e2)
    onehot = jnp.where((lane == i1) | (lane == i2), 1.0, 0.0)
    before = _dot(tri_ref[...], onehot.astype(BF16)) + cnt_ref[0:1, :]
    cnt_ref[0:1, :] += jnp.sum(onehot, axis=0, keepdims=True)
    meta = jnp.zeros_like(logits)
    for slot, val in ((META_I1, i1), (META_I2, i2), (META_W1, w1), (META_W2, e2 * w1),
                      (META_R1, _lane_pick(before, lane, i1)), (META_R2, _lane_pick(before, lane, i2))):
        meta = jnp.where(lane == slot, val, meta)
    meta_ref[...] = meta


def router_top2(x, g, w_router):
    m, d = x.shape
    n_experts = w_router.shape[1]
    wr = jnp.zeros((d, LANES), F32).at[:, :n_experts].set(w_router)
    tm = _pick(m, (1024, 512, 256, 128))
    return pl.pallas_call(
        functools.partial(_router_kernel, n_experts=n_experts),
        grid=(m // tm,),
        in_specs=[pl.BlockSpec((tm, d), lambda i: (i, 0)),
                  pl.BlockSpec((1, d), lambda i: (0, 0)),
                  pl.BlockSpec((d, LANES), lambda i: (0, 0))],
        out_specs=[pl.BlockSpec((tm, LANES), lambda i: (i, 0)),
                   pl.BlockSpec((8, LANES), lambda i: (0, 0))],
        out_shape=[jax.ShapeDtypeStruct((m, LANES), F32), jax.ShapeDtypeStruct((8, LANES), F32)],
        scratch_shapes=[pltpu.VMEM((tm, tm), BF16)],
        compiler_params=_cparams("arbitrary"),
    )(x, g.reshape(1, d), wr)


def _idx_copy(dest_hbm, idx_smem, isem, tile, slot):
    return pltpu.make_async_copy(dest_hbm.at[tile], idx_smem.at[slot], isem.at[slot])


def _dispatch_kernel(ends_ref, dest_hbm, x_ref, xs_hbm, idx_smem, zbuf, isem, sem, zsem, *, tm, tg):
    i = pl.program_id(0)
    slot = i % 2
    n_experts = ends_ref.shape[0]
    n_out_tiles = xs_hbm.shape[0] // tg

    @pl.when(i == 0)
    def _():
        _idx_copy(dest_hbm, idx_smem, isem, 0, 0).start()
        zbuf[...] = jnp.zeros_like(zbuf)
        fills = []
        for e in range(n_experts):
            group_start = ends_ref[e - 1] if e > 0 else 0
            fills.append((ends_ref[e] > group_start, ends_ref[e] - tg))
            tail = ends_ref[n_experts - 1] + e * tg
            fills.append((tail < n_out_tiles * tg, tail))
        zero_fill = lambda start: pltpu.make_async_copy(zbuf, xs_hbm.at[pl.ds(pl.multiple_of(start, tg), tg)], zsem)
        for cond, start in fills:
            @pl.when(cond)
            def _():
                zero_fill(start).start()
        for cond, start in fills:
            @pl.when(cond)
            def _():
                zero_fill(start).wait()

    @pl.when(i + 1 < pl.num_programs(0))
    def _():
        _idx_copy(dest_hbm, idx_smem, isem, i + 1, 1 - slot).start()

    _idx_copy(dest_hbm, idx_smem, isem, i, slot).wait()

    def row_copy(t, k):
        d = idx_smem[slot, TOP_K * t + k]
        return pltpu.make_async_copy(x_ref.at[pl.ds(t, 1)], xs_hbm.at[pl.ds(d, 1)], sem)

    def issue(t, carry):
        for k in range(TOP_K):
            row_copy(t, k).start()
        return carry

    lax.fori_loop(0, tm, issue, 0, unroll=8)
    for _ in range(TOP_K):
        pltpu.make_async_copy(x_ref, xs_hbm.at[pl.ds(0, tm)], sem).wait()


def moe_dispatch(x, dest, ends, n_rows, tg):
    m, d = x.shape
    n_tiles, per_tile = dest.shape
    tm = per_tile // TOP_K
    grid_spec = pltpu.PrefetchScalarGridSpec(
        num_scalar_prefetch=1,
        grid=(n_tiles,),
        in_specs=[pl.BlockSpec(memory_space=pl.ANY),
                  pl.BlockSpec((tm, d), lambda i, ends: (i, 0))],
        out_specs=pl.BlockSpec(memory_space=pl.ANY),
        scratch_shapes=[pltpu.SMEM((2, per_tile), jnp.int32), pltpu.VMEM((tg, d), F32),
                        pltpu.SemaphoreType.DMA((2,)), pltpu.SemaphoreType.DMA(()), pltpu.SemaphoreType.DMA(())],
    )
    return pl.pallas_call(
        functools.partial(_dispatch_kernel, tm=tm, tg=tg),
        grid_spec=grid_spec,
        out_shape=jax.ShapeDtypeStruct((n_rows, d), F32),
        compiler_params=_cparams("arbitrary"),
    )(ends, dest, x)


def _moe_ffn_kernel(te_ref, nu_ref, x_ref, g_ref, wgu_ref, wd_ref, o_ref):
    del te_ref

    @pl.when(pl.program_id(0) < nu_ref[0])
    def _():
        o_ref[...] = _swiglu(_rms(x_ref[...], g_ref[...]).astype(BF16), wgu_ref.at[0], wd_ref.at[0])

    @pl.when(pl.program_id(0) >= nu_ref[0])
    def _():
        o_ref[...] = jnp.zeros_like(o_ref)


def moe_grouped_ffn(xs, g, tile_expert, n_used, w_gu, w_down, tm):
    p, d = xs.shape
    f = w_down.shape[1]
    last = lambda n, nu: jnp.minimum(n, nu[0] - 1)
    grid_spec = pltpu.PrefetchScalarGridSpec(
        num_scalar_prefetch=2,
        grid=(p // tm,),
        in_specs=[pl.BlockSpec((tm, d), lambda n, te, nu: (last(n, nu), 0)),
                  pl.BlockSpec((1, d), lambda n, te, nu: (0, 0)),
                  pl.BlockSpec((1, d, 2 * f), lambda n, te, nu: (te[n], 0, 0), pipeline_mode=pl.Buffered(1)),
                  pl.BlockSpec((1, f, d), lambda n, te, nu: (te[n], 0, 0), pipeline_mode=pl.Buffered(1))],
        out_specs=pl.BlockSpec((tm, d), lambda n, te, nu: (n, 0)),
    )
    return pl.pallas_call(
        _moe_ffn_kernel,
        grid_spec=grid_spec,
        out_shape=jax.ShapeDtypeStruct((p, d), F32),
        compiler_params=_cparams("arbitrary"),
    )(tile_expert, n_used, xs, g.reshape(1, d), w_gu, w_down)


def _combine_kernel(dest_hbm, x_ref, meta_ref, ys_hbm, o_ref, idx_smem, ybuf, isem, sem, *, tm, n_tiles):
    i = pl.program_id(0)
    slot = i % 2

    def row_copy(sl, t, k):
        d = idx_smem[sl, TOP_K * t + k]
        return pltpu.make_async_copy(ys_hbm.at[pl.ds(d, 1)], ybuf.at[sl, k, pl.ds(t, 1)], sem.at[sl])

    def gather_tile(tile, sl):
        _idx_copy(dest_hbm, idx_smem, isem, tile, sl).wait()

        def issue(t, carry):
            for k in range(TOP_K):
                row_copy(sl, t, k).start()
            return carry

        lax.fori_loop(0, tm, issue, 0, unroll=8)

    @pl.when(i == 0)
    def _():
        _idx_copy(dest_hbm, idx_smem, isem, 0, 0).start()
        gather_tile(0, 0)
        if n_tiles > 1:
            _idx_copy(dest_hbm, idx_smem, isem, 1, 1).start()

    @pl.when(i + 1 < pl.num_programs(0))
    def _():
        gather_tile(i + 1, 1 - slot)

    @pl.when(i + 2 < pl.num_programs(0))
    def _():
        _idx_copy(dest_hbm, idx_smem, isem, i + 2, slot).start()

    for k in range(TOP_K):
        pltpu.make_async_copy(ys_hbm.at[pl.ds(0, tm)], ybuf.at[slot, k], sem.at[slot]).wait()
    meta = meta_ref[...]
    lane = lax.broadcasted_iota(jnp.int32, meta.shape, 1)
    w1 = _lane_pick(meta, lane, META_W1)
    w2 = _lane_pick(meta, lane, META_W2)
    o_ref[...] = x_ref[...] + w1 * ybuf[slot, 0] + w2 * ybuf[slot, 1]


def moe_combine(x, meta, dest, ys):
    m, d = x.shape
    n_tiles, per_tile = dest.shape
    tm = per_tile // TOP_K
    return pl.pallas_call(
        functools.partial(_combine_kernel, tm=tm, n_tiles=n_tiles),
        grid=(n_tiles,),
        in_specs=[pl.BlockSpec(memory_space=pl.ANY),
                  pl.BlockSpec((tm, d), lambda i: (i, 0)),
                  pl.BlockSpec((tm, LANES), lambda i: (i, 0)),
                  pl.BlockSpec(memory_space=pl.ANY)],
        out_specs=pl.BlockSpec((tm, d), lambda i: (i, 0)),
        out_shape=jax.ShapeDtypeStruct((m, d), F32),
        scratch_shapes=[pltpu.SMEM((2, per_tile), jnp.int32), pltpu.VMEM((2, TOP_K, tm, d), F32),
                        pltpu.SemaphoreType.DMA((2,)), pltpu.SemaphoreType.DMA((2,))],
        compiler_params=_cparams("arbitrary"),
    )(dest, x, meta, ys)


def ffn_moe(x, g, w_router, w_gu, w_down):
    m, d = x.shape
    n_experts = w_down.shape[0]
    tg = _pick(m, (MOE_GROUP_TILE, 256, 128))
    t_disp = _pick(m, (MOE_DISPATCH_TILE, 512, 256, 128))
    t_comb = _pick(m, (MOE_COMBINE_TILE, 512, 256, 128))
    meta, counts = router_top2(x, g, w_router)
    counts = counts[0, :n_experts].astype(jnp.int32)
    padded = (counts + tg - 1) // tg * tg
    ends = jnp.cumsum(padded)
    offsets = ends - padded
    n_tiles = (TOP_K * m) // tg + n_experts
    n_used = (ends[-1] // tg).astype(jnp.int32)
    tile_id = jnp.minimum(jnp.arange(n_tiles, dtype=jnp.int32), n_used - 1)
    tile_expert = jnp.sum(ends[None, :] <= (tile_id * tg)[:, None], axis=1).astype(jnp.int32)
    picks = meta[:, META_I1:META_I2 + 1].astype(jnp.int32)
    ranks = meta[:, META_R1:META_R2 + 1].astype(jnp.int32)
    dest = offsets[picks] + ranks
    xs = moe_dispatch(x, dest.reshape(m // t_disp, TOP_K * t_disp), ends.astype(jnp.int32), n_tiles * tg, tg)
    ys = moe_grouped_ffn(xs, g, tile_expert, n_used.reshape(1), w_gu, w_down, tg)
    return moe_combine(x, meta, dest.reshape(m // t_comb, TOP_K * t_comb), ys)


def _block_diag(w_a, w_b):
    za = jnp.zeros_like(w_a)
    zb = jnp.zeros_like(w_b)
    return jnp.concatenate([jnp.concatenate([w_a, za], axis=1), jnp.concatenate([zb, w_b], axis=1)], axis=0)


def kernel(x, mem, positions, mix_norm, in_proj, shift_mu, decay_bias, decay_up, iclr_bias, iclr_up, gate_up,
           key_kk_scale, key_iclr_scale, bonus_rk, rwkv_gn_w, rwkv_gn_b, vres_down, vres_shift_mu, vres_bias,
           vres_up, q_norm, k_norm, lambda_q1, lambda_k1, lambda_q2, lambda_k2, diff_subln, out_proj, xattn_norm,
           mem_norm, xattn_wq, xattn_wkv, xattn_wo, xattn_q_norm, xattn_k_norm, ffn_norm, dense_w_gu,
           dense_w_down, router, expert_w_gu, expert_w_down):
    b, s, d = x.shape
    depth = mix_norm.shape[0]
    width = decay_bias.shape[1]
    rwkv_cols = shift_mu.shape[1]
    in_cols = in_proj.shape[2]
    vres_rank = vres_down.shape[2] if depth > 1 else 0
    assert decay_up.shape[1] == LANES // 2 and iclr_up.shape[1] == LANES // 2 and gate_up.shape[1] == LANES
    assert rwkv_cols == 3 * width + 2 * LANES and rwkv_cols % LANES == 0 and in_cols % LANES == 0
    assert vres_rank <= LANES
    rot = rope_table(positions)
    xf = x.reshape(b * s, d)
    memf = mem.reshape(-1, d)
    v_first = None
    for l in range(depth):
        w_rwkv = in_proj[l][:, :rwkv_cols]
        vres = None
        if l > 0:
            pad = jnp.zeros((d, LANES - vres_rank), F32)
            w_rwkv = jnp.concatenate([w_rwkv, vres_down[l - 1], pad], axis=1)
            vmu = jnp.zeros((LANES,), F32).at[:vres_rank].set(vres_shift_mu[l - 1])
            vup = jnp.zeros((LANES, width), F32).at[:vres_rank].set(vres_up[l - 1]).astype(BF16)
            vres = (v_first, vmu, vres_bias[l - 1], vup)
        wwa = _block_diag(decay_up[l], iclr_up[l]).astype(BF16)
        r_, lw_, k_, v_, a_, g_ = rwkv_in_proj(xf, s, mix_norm[l], w_rwkv.astype(BF16), width, shift_mu[l],
                                               decay_bias[l], iclr_bias[l], wwa, gate_up[l].astype(BF16), vres)
        if l == 0:
            v_first = v_
        seq3 = lambda t: t.reshape(b, s, width)
        y_rwkv = rwkv_chunked(seq3(r_), seq3(lw_), seq3(k_), seq3(v_), seq3(a_), seq3(g_), key_kk_scale[l],
                              key_iclr_scale[l], bonus_rk[l].reshape(-1), rwkv_gn_w[l], rwkv_gn_b[l])
        lam_init = 0.8 - 0.6 * math.exp(-0.3 * l)
        lam_vecs = jnp.stack([lambda_q1[l], lambda_k1[l], lambda_q2[l], lambda_k2[l]])
        y_diff = diff_attention(xf, b, mix_norm[l], in_proj[l][:, rwkv_cols:].astype(BF16), rot, q_norm[l],
                                k_norm[l], lam_vecs, diff_subln[l], lam_init)
        w_out = out_proj[l].astype(BF16)
        kv = norm_matmul(memf, mem_norm[l], xattn_wkv[l].astype(BF16)).reshape(b, -1, 2 * d)
        xf = mixer_out_cross_attention(xf.reshape(b, s, d), y_rwkv, y_diff, w_out[:width], w_out[width:],
                                       xattn_norm[l], xattn_wq[l].astype(BF16), kv, xattn_q_norm[l],
                                       xattn_k_norm[l], xattn_wo[l].astype(BF16)).reshape(b * s, d)
        if l % 2 == 0:
            xf = ffn_dense(xf, ffn_norm[l], dense_w_gu[l // 2].astype(BF16), dense_w_down[l // 2].astype(BF16))
        else:
            xf = ffn_moe(xf, ffn_norm[l], router[l // 2], expert_w_gu[l // 2].astype(BF16),
                         expert_w_down[l // 2].astype(BF16))
    return xf.reshape(b, s, d)
```

```python
import functools
import math

import jax
import jax.numpy as jnp
from jax import lax
from jax.experimental import pallas as pl
from jax.experimental.pallas import tpu as pltpu

F32 = jnp.float32
BF16 = jnp.bfloat16

EPS = 1e-6
GN_EPS = 64e-5
ROPE_THETA = 500000.0
RWKV_HEAD = 64
DIFF_HEADS = 4
XATTN_HEADS = 4
TOP_K = 2
LANES = 128
CHUNK = 128
VMEM_LIMIT = 56 * 1024 * 1024
RWKV_ROWS_PER_STEP = 4
RWKV_IN_ROW_PARTS = 4
ATTN_ROW_PARTS = 4
WEIGHT_TILE_BYTES = 4 * 1024 * 1024
MOE_GROUP_TILE = 512
MOE_TOKEN_TILE = 1024


def _cparams(*sem):
    return pltpu.CompilerParams(dimension_semantics=sem, vmem_limit_bytes=VMEM_LIMIT)


def _pick(n, prefs):
    for p in prefs:
        if n % p == 0:
            return p
    return n


def _col_tile(k, n):
    best = LANES
    for t in range(LANES, n + 1, LANES):
        if n % t == 0 and k * t * 2 <= WEIGHT_TILE_BYTES:
            best = t
    return best


def _dot(a, b, prec=None):
    return jnp.dot(a, b, preferred_element_type=F32, precision=prec)


def _dot_nt(a, b, prec=None):
    return lax.dot_general(a, b, (((1,), (1,)), ((), ())), preferred_element_type=F32, precision=prec)


def _rms(x, g):
    return x * lax.rsqrt(jnp.mean(x * x, axis=-1, keepdims=True) + EPS) * g


def _norm_mm_kernel(x_ref, g_ref, w_ref, o_ref, h_ref):
    @pl.when(pl.program_id(1) == 0)
    def _():
        h_ref[...] = _rms(x_ref[...], g_ref[...]).astype(BF16)

    o_ref[...] = _dot(h_ref[...], w_ref[...])


def norm_matmul(x, g, w):
    m, k = x.shape
    n = w.shape[1]
    tm = _pick(m, (1024, 512, 256, 128))
    tn = _col_tile(k, n)
    return pl.pallas_call(
        _norm_mm_kernel,
        grid=(m // tm, n // tn),
        in_specs=[pl.BlockSpec((tm, k), lambda i, j: (i, 0)),
                  pl.BlockSpec((1, k), lambda i, j: (0, 0)),
                  pl.BlockSpec((k, tn), lambda i, j: (0, j))],
        out_specs=pl.BlockSpec((tm, tn), lambda i, j: (i, j)),
        out_shape=jax.ShapeDtypeStruct((m, n), F32),
        scratch_shapes=[pltpu.VMEM((tm, k), BF16)],
        compiler_params=_cparams("parallel", "arbitrary"),
    )(x, g.reshape(1, k), w)


def _token_shift(p, prev_row, mu):
    prev = pltpu.roll(p, 1, axis=0)
    rid = lax.broadcasted_iota(jnp.int32, p.shape, 0)
    prev = jnp.where(rid == 0, prev_row, prev)
    return p + (prev - p) * mu


def _rwkv_in_kernel(*refs, width, has_vres, tiles_per_seq):
    if has_vres:
        (x_ref, gn_ref, w_ref, vf_ref, mu_ref, w0_ref, a0_ref, wwa_ref, gup_ref, vmu_ref, vb_ref, vup_ref,
         r_ref, lw_ref, k_ref, v_ref, a_ref, g_ref, carry_ref, carry_v_ref) = refs
    else:
        (x_ref, gn_ref, w_ref, mu_ref, w0_ref, a0_ref, wwa_ref, gup_ref,
         r_ref, lw_ref, k_ref, v_ref, a_ref, g_ref, carry_ref) = refs
    ncol = 3 * width + 2 * LANES
    tm = x_ref.shape[0]
    n_parts = RWKV_IN_ROW_PARTS
    rp = tm // n_parts

    @pl.when(pl.program_id(0) % tiles_per_seq == 0)
    def _():
        carry_ref[...] = jnp.zeros_like(carry_ref)
        if has_vres:
            carry_v_ref[...] = jnp.zeros_like(carry_v_ref)

    projs = [_dot(_rms(x_ref[i * rp:(i + 1) * rp, :], gn_ref[...]).astype(BF16), w_ref[...]) for i in range(n_parts)]
    prev = carry_ref[...]
    prev_v = carry_v_ref[...] if has_vres else None
    for i, proj in enumerate(projs):
        rows = slice(i * rp, (i + 1) * rp)
        sh = _token_shift(proj[:, :ncol], prev, mu_ref[...])
        prev = proj[rp - 1:rp, :ncol]
        r_ref[rows, :] = sh[:, :width].astype(r_ref.dtype)
        k_ref[rows, :] = sh[:, width:2 * width].astype(k_ref.dtype)
        v = sh[:, 2 * width:3 * width]
        dwa = sh[:, 3 * width:3 * width + LANES]
        lane = lax.broadcasted_iota(jnp.int32, dwa.shape, 1)
        dwa = jnp.where(lane < LANES // 2, jnp.tanh(dwa), dwa)
        wa = _dot(dwa.astype(BF16), wwa_ref[...])
        z = -(w0_ref[...] + wa[:, :width])
        softplus = jnp.maximum(z, 0.0) + jnp.log(1.0 + jnp.exp(-jnp.abs(z)))
        lw_ref[rows, :] = -jnp.exp(-softplus - 0.5)
        a_ref[rows, :] = jax.nn.sigmoid(a0_ref[...] + wa[:, width:]).astype(a_ref.dtype)
        dg = jax.nn.sigmoid(sh[:, 3 * width + LANES:3 * width + 2 * LANES])
        g_ref[rows, :] = _dot(dg.astype(BF16), gup_ref[...]).astype(g_ref.dtype)
        if has_vres:
            shv = _token_shift(proj[:, ncol:ncol + LANES], prev_v, vmu_ref[...])
            prev_v = proj[rp - 1:rp, ncol:ncol + LANES]
            mix = jax.nn.sigmoid(vb_ref[...] + _dot(shv.astype(BF16), vup_ref[...]))
            v = v + (vf_ref[rows, :].astype(F32) - v) * mix
        v_ref[rows, :] = v.astype(v_ref.dtype)
    carry_ref[...] = prev
    if has_vres:
        carry_v_ref[...] = prev_v


def rwkv_in_proj(x, seq_len, gn, w, width, mu, w0, a0, wwa, gup, vres=None):
    m, d = x.shape
    tm = _pick(seq_len, (1024, 512, 256, 128))
    ncol = 3 * width + 2 * LANES
    assert mu.shape[-1] == ncol and m % seq_len == 0
    row = lambda a: a.reshape(1, -1)
    full = lambda a: pl.BlockSpec(a.shape, lambda i: (0,) * a.ndim)
    rows = lambda c: pl.BlockSpec((tm, c), lambda i: (i, 0))
    params = [row(mu), row(w0), row(a0), wwa, gup]
    inputs = [x, row(gn), w]
    in_specs = [rows(d), full(row(gn)), full(w)]
    scratch = [pltpu.VMEM((1, ncol), F32)]
    if vres is not None:
        v_first, vmu, vb, vup = vres
        inputs.append(v_first)
        in_specs.append(rows(width))
        params += [row(vmu), row(vb), vup]
        scratch.append(pltpu.VMEM((1, LANES), F32))
    in_specs += [full(p) for p in params]
    out = lambda dt: jax.ShapeDtypeStruct((m, width), dt)
    return pl.pallas_call(
        functools.partial(_rwkv_in_kernel, width=width, has_vres=vres is not None, tiles_per_seq=seq_len // tm),
        grid=(m // tm,),
        in_specs=in_specs,
        out_specs=[rows(width)] * 6,
        out_shape=[out(BF16), out(F32), out(BF16), out(BF16), out(BF16), out(BF16)],
        scratch_shapes=scratch,
        compiler_params=_cparams("arbitrary"),
    )(*inputs, *params)


def _rwkv_chunk_kernel(r_ref, lw_ref, k_ref, v_ref, a_ref, g_ref, kk_ref, ka_ref, rk_ref, gw_ref, gb_ref,
                       y_ref, state_ref, *, n_pairs):
    c = CHUNK
    half = LANES // 2

    @pl.when(pl.program_id(1) == 0)
    def _():
        state_ref[...] = jnp.zeros_like(state_ref)

    row = lax.broadcasted_iota(jnp.int32, (c, c), 0)
    col = lax.broadcasted_iota(jnp.int32, (c, c), 1)
    ltri = (row >= col).astype(BF16)
    same_head = (row < half) == (col < half)
    lo = col < half
    row4 = lax.broadcasted_iota(jnp.int32, (c, 4 * c), 0)
    col4 = lax.broadcasted_iota(jnp.int32, (c, 4 * c), 1)
    incl4 = row4 >= (col4 & (c - 1))
    row2 = lax.broadcasted_iota(jnp.int32, (c, 2 * c), 0)
    col2 = lax.broadcasted_iota(jnp.int32, (c, 2 * c), 1)
    strict2 = row2 > (col2 & (c - 1))
    first_block = col2 < c

    def gsum(x):
        s_lo = jnp.sum(jnp.where(lo, x, 0.0), axis=-1, keepdims=True)
        s_hi = jnp.sum(jnp.where(lo, 0.0, x), axis=-1, keepdims=True)
        return jnp.where(lo, s_lo, s_hi)

    def by_head(x):
        return jnp.concatenate([jnp.where(lo, x, 0.0), jnp.where(lo, 0.0, x)], axis=0).astype(BF16)

    def split_bf16(x):
        hi = x.astype(BF16)
        rest = x - hi.astype(F32)
        mid = rest.astype(BF16)
        return hi, mid, (rest - mid.astype(F32)).astype(BF16)

    def stage_factors(bi, p):
        sl = slice(p * LANES, (p + 1) * LANES)
        r, lw, k, v, a = (ref[bi, :, sl].astype(F32) for ref in (r_ref, lw_ref, k_ref, v_ref, a_ref))
        cum3 = _dot(ltri, jnp.concatenate(split_bf16(lw), axis=1))
        cum = cum3[:, :LANES] + cum3[:, LANES:2 * LANES] + cum3[:, 2 * LANES:]
        cmid = cum[c // 2 - 1:c // 2, :]
        clast = cum[c - 1:c, :]
        ci = cum - cmid
        e_neg = jnp.exp(-ci)
        p_mid = jnp.exp(cmid)
        kk = k * kk_ref[:, sl]
        kkn = kk * lax.rsqrt(jnp.maximum(gsum(kk * kk), 1e-24))
        k2 = k * (1.0 + (a - 1.0) * ka_ref[:, sl])
        beta = kkn * a
        al_m = -kkn * jnp.exp(ci - lw)
        r_m = r * jnp.exp(ci)
        e_end = jnp.exp(clast - cum)
        ends = jnp.concatenate([beta * e_end, k2 * e_end], axis=0).astype(BF16)
        al2, r2 = by_head(al_m), by_head(r_m)
        lhs = jnp.concatenate([al2[:c], r2[:c], al2[c:], r2[c:]], axis=0)
        rhs = jnp.concatenate([beta * e_neg, k2 * e_neg], axis=0).astype(BF16)
        big = _dot_nt(lhs, rhs)
        s0 = state_ref[bi * n_pairs + p]
        base = _dot_nt(jnp.concatenate([al_m * p_mid, r_m * p_mid], axis=0).astype(BF16), s0.astype(BF16))
        return dict(big=big, base=base, v2=by_head(v), ends=ends, decayed=s0 * jnp.exp(clast),
                    bonus=gsum(r * k2 * rk_ref[:, sl]) * v)

    def stage_masks(d):
        big = d.pop("big")
        blk = lambda i, j: big[i * c:(i + 1) * c, j * c:(j + 1) * c]
        zero2 = jnp.zeros((c, 2 * c), F32)
        d["pw"] = jnp.where(strict2, jnp.concatenate([blk(0, 0), blk(2, 0)], axis=1), zero2).astype(BF16)
        a_ak = jnp.where(strict2, jnp.concatenate([blk(0, 1), blk(2, 1)], axis=1), zero2).astype(BF16)
        d["a_y"] = jnp.where(incl4, jnp.concatenate([blk(1, 0), blk(3, 0), blk(1, 1), blk(3, 1)], axis=1),
                             0.0).astype(BF16)
        d["x"] = d["base"][:c] + _dot(a_ak, d["v2"])

    def stage_square(d, last):
        pw, x2 = d["pw"], by_head(d["x"])
        if last:
            d["x"] = d["x"] + _dot(pw, x2)
            return
        zero_b = jnp.zeros_like(pw)
        bd = jnp.concatenate([jnp.where(first_block, pw, zero_b), jnp.where(first_block, zero_b, pw)], axis=0)
        res = _dot(pw, jnp.concatenate([bd, x2], axis=1))
        d["pw"] = res[:, :2 * c].astype(BF16)
        d["x"] = d["x"] + res[:, 2 * c:]

    def stage_output(bi, p, d):
        sl = slice(p * LANES, (p + 1) * LANES)
        x = d["x"]
        y = d["base"][c:] + _dot(d["a_y"], jnp.concatenate([by_head(x), d["v2"]], axis=0))
        upd = _dot(jnp.concatenate([x.T, v_ref[bi, :, sl].astype(F32).T], axis=1).astype(BF16), d["ends"])
        state_ref[bi * n_pairs + p] = d["decayed"] + jnp.where(same_head, upd, 0.0)
        mean = gsum(y) * (1.0 / half)
        yc = y - mean
        var = gsum(yc * yc) * (1.0 / half)
        out = yc * lax.rsqrt(var + GN_EPS) * gw_ref[:, sl] + gb_ref[:, sl]
        y_ref[bi, :, sl] = ((out + d["bonus"]) * g_ref[bi, :, sl].astype(F32)).astype(y_ref.dtype)

    ids = [(bi, p) for bi in range(r_ref.shape[0]) for p in range(n_pairs)]
    pairs = [stage_factors(bi, p) for bi, p in ids]
    for d in pairs:
        stage_masks(d)
    n_steps = c.bit_length() - 1
    for step in range(n_steps):
        for d in pairs:
            stage_square(d, last=step + 1 == n_steps)
    for (bi, p), d in zip(ids, pairs):
        stage_output(bi, p, d)


def rwkv_chunked(r, lw, k, v, a, g, k_k, k_a, r_k, gn_w, gn_b):
    b, s, width = r.shape
    assert RWKV_HEAD * 2 == LANES and width % LANES == 0 and s % CHUNK == 0
    n_pairs = width // LANES
    nb = _pick(b, (RWKV_ROWS_PER_STEP, 1))
    seq = pl.BlockSpec((nb, CHUNK, width), lambda i, j: (i, j, 0))
    par = pl.BlockSpec((1, width), lambda i, j: (0, 0))
    row = lambda t: t.reshape(1, width)
    return pl.pallas_call(
        functools.partial(_rwkv_chunk_kernel, n_pairs=n_pairs),
        grid=(b // nb, s // CHUNK),
        in_specs=[seq] * 6 + [par] * 5,
        out_specs=seq,
        out_shape=jax.ShapeDtypeStruct((b, s, width), BF16),
        scratch_shapes=[pltpu.VMEM((nb * n_pairs, LANES, LANES), F32)],
        compiler_params=_cparams("parallel", "arbitrary"),
    )(r, lw, k, v, a, g, row(k_k), row(k_a), row(r_k), row(gn_w), row(gn_b))


def _diff_in_kernel(x_ref, gn_ref, w_ref, rot_ref, qg_ref, kg_ref, qk_ref, v_ref, *, heads):
    half = LANES // 2
    rope_half = half // 8
    proj = _dot(_rms(x_ref[...], gn_ref[...]).astype(BF16), w_ref[...])
    lane = lax.broadcasted_iota(jnp.int32, (x_ref.shape[0], LANES), 1)
    pos = lane & (half - 1)
    e_row = lax.broadcasted_iota(jnp.int32, (2 * rope_half, LANES), 0)
    e_col = lax.broadcasted_iota(jnp.int32, (2 * rope_half, LANES), 1)
    expand = ((e_col & (half - 1)) == e_row).astype(BF16)
    cs = rot_ref[...]
    rot = None
    for _ in range(3):
        piece = cs.astype(BF16)
        term = lax.dot_general(piece, expand, (((0,), (0,)), ((), ())), preferred_element_type=F32)
        rot = term if rot is None else rot + term
        cs = cs - piece.astype(F32)
    first, second = pos < rope_half, (pos >= rope_half) & (pos < 2 * rope_half)
    cf = jnp.where(first, rot, jnp.where(second, pltpu.roll(rot, rope_half, axis=1), 1.0))
    sn = jnp.where(first, -pltpu.roll(rot, LANES - rope_half, axis=1), jnp.where(second, rot, 0.0))

    g_row = lax.broadcasted_iota(jnp.int32, (LANES, LANES), 0)
    g_col = lax.broadcasted_iota(jnp.int32, (LANES, LANES), 1)
    same_half = ((g_row < half) == (g_col < half)).astype(BF16)
    c_pos = g_col & (half - 1)
    partner = jnp.where(c_pos < rope_half, g_col + rope_half, jnp.where(c_pos < 2 * rope_half, g_col - rope_half, -1))
    swap = (g_row == partner).astype(BF16)

    def norm_rope(x, g):
        ms = _dot((x * x).astype(BF16), same_half) * (1.0 / half)
        x = x * lax.rsqrt(ms + EPS) * g
        return x * cf + _dot(x.astype(BF16), swap) * sn

    ones = jnp.ones((x_ref.shape[0], LANES), BF16)
    for h in range(heads):
        tile = lambda i: proj[:, (i * heads + h) * LANES:(i * heads + h + 1) * LANES]
        q = norm_rope(tile(0), qg_ref[...]) * (half ** -0.5 * math.log2(math.e))
        qk_ref[:, h * LANES:(h + 1) * LANES] = q.astype(BF16)
        qk_ref[:, (heads + h) * LANES:(heads + h + 1) * LANES] = norm_rope(tile(1), kg_ref[...]).astype(BF16)
        v_ref[:, 2 * h * LANES:(2 * h + 1) * LANES] = tile(2).astype(BF16)
        v_ref[:, (2 * h + 1) * LANES:(2 * h + 2) * LANES] = ones


def _diff_attn_kernel(q_ref, k_ref, v_ref, lam_ref, sub_ref, o_ref, *, tq, lam_init):
    half = LANES // 2
    s_len = q_ref.shape[1]
    lv = lam_ref[...]
    lam = (jnp.exp(jnp.sum(lv[0:1] * lv[1:2], axis=-1, keepdims=True))
           - jnp.exp(jnp.sum(lv[2:3] * lv[3:4], axis=-1, keepdims=True)) + lam_init)
    lane = lax.broadcasted_iota(jnp.int32, (tq, LANES), 1)
    n_parts = ATTN_ROW_PARTS
    rp = tq // n_parts
    chains = [(m, part) for part in range(n_parts) for m in range(2)]

    def block(qs, j, carry, masked):
        kb = k_ref[0, j * tq:(j + 1) * tq, :]
        vb = v_ref[0, j * tq:(j + 1) * tq, :]
        scores, probs, maxes, out = {}, {}, {}, {}
        for i in range(len(chains) + 2):
            if i < len(chains):
                m, part = chains[i]
                keys = (part + 1) * rp if masked else tq
                s = _dot_nt(qs[m][part * rp:(part + 1) * rp], kb[:keys])
                if masked:
                    r_id = lax.broadcasted_iota(jnp.int32, s.shape, 0) + part * rp
                    c_id = lax.broadcasted_iota(jnp.int32, s.shape, 1)
                    s = jnp.where(c_id <= r_id, s, -1e30)
                scores[i] = s
            if 0 <= i - 1 < len(chains):
                c = i - 1
                s = scores.pop(c)
                mx = jnp.max(s, axis=-1, keepdims=True)
                maxes[c] = mx if carry is None else jnp.maximum(carry[2 * c], mx)
                probs[c] = jnp.exp2((s - maxes[c]).astype(BF16))
            if 0 <= i - 2 < len(chains):
                c = i - 2
                pr = probs.pop(c)
                pv = _dot(pr, vb[:pr.shape[1]])
                out[c] = pv if carry is None else jnp.exp2(carry[2 * c] - maxes[c]) * carry[2 * c + 1] + pv
        return tuple(t for c in range(len(chains)) for t in (maxes[c], out[c]))

    for qi in range(s_len // tq):
        q = q_ref[0, qi * tq:(qi + 1) * tq, :]
        zero = jnp.zeros_like(q)
        qs = [jnp.where(lane < half, q, zero), jnp.where(lane < half, zero, q)]
        carry = block(qs, qi, None, True)
        for j in range(qi):
            carry = block(qs, j, carry, False)
        acc1 = jnp.concatenate([carry[2 * c + 1] for c, (m, _) in enumerate(chains) if m == 0], axis=0)
        acc2 = jnp.concatenate([carry[2 * c + 1] for c, (m, _) in enumerate(chains) if m == 1], axis=0)
        o = acc1[:, :LANES] / acc1[:, LANES:] - lam * (acc2[:, :LANES] / acc2[:, LANES:])
        o_ref[0, qi * tq:(qi + 1) * tq, :] = (_rms(o, sub_ref[...]) * (1.0 - lam_init)).astype(o_ref.dtype)


def diff_attention(x, b, gn, w, rot, q_g, k_g, lam_vecs, subln_g, lam_init):
    m, d = x.shape
    s = m // b
    h = DIFF_HEADS
    assert w.shape[1] == 3 * h * LANES
    tm = _pick(m, (1024, 512, 256, 128))
    tile2 = lambda g: jnp.tile(g, 2).reshape(1, LANES)
    rows = lambda c: pl.BlockSpec((tm, c), lambda i: (i, 0))
    full = lambda a: pl.BlockSpec(a.shape, lambda i: (0,) * a.ndim)
    gspec2 = pl.BlockSpec((1, LANES), lambda i: (0, 0))
    qk, vext = pl.pallas_call(
        functools.partial(_diff_in_kernel, heads=h),
        grid=(m // tm,),
        in_specs=[rows(d), pl.BlockSpec((1, d), lambda i: (0, 0)), full(w),
                  pl.BlockSpec((rot.shape[0], tm), lambda i: (0, i)), gspec2, gspec2],
        out_specs=[rows(2 * h * LANES), rows(2 * h * LANES)],
        out_shape=[jax.ShapeDtypeStruct((m, 2 * h * LANES), BF16)] * 2,
        compiler_params=_cparams("parallel"),
    )(x, gn.reshape(1, d), w, rot, tile2(q_g), tile2(k_g))
    qk = qk.reshape(b, s, 2 * h * LANES)
    vext = vext.reshape(b, s, 2 * h * LANES)

    tq = _pick(s, (1024, 512, 256, 128))
    gspec = pl.BlockSpec((1, LANES), lambda bi, hi: (0, 0))
    return pl.pallas_call(
        functools.partial(_diff_attn_kernel, tq=tq, lam_init=lam_init),
        grid=(b, h),
        in_specs=[pl.BlockSpec((1, s, LANES), lambda bi, hi: (bi, 0, hi)),
                  pl.BlockSpec((1, s, LANES), lambda bi, hi: (bi, 0, h + hi)),
                  pl.BlockSpec((1, s, 2 * LANES), lambda bi, hi: (bi, 0, hi)),
                  pl.BlockSpec(lam_vecs.shape, lambda bi, hi: (0, 0)),
                  gspec],
        out_specs=pl.BlockSpec((1, s, LANES), lambda bi, hi: (bi, 0, hi)),
        out_shape=jax.ShapeDtypeStruct((b, s, h * LANES), BF16),
        compiler_params=_cparams("parallel", "parallel"),
    )(qk, qk, vext, lam_vecs, subln_g.reshape(1, LANES))


def rope_table(positions):
    rope_dims = LANES // 8
    inv_freq = ROPE_THETA ** (-jnp.arange(0, rope_dims, 2, dtype=F32) / rope_dims)
    ang = inv_freq[:, None] * positions.astype(F32).reshape(1, -1)
    return jnp.concatenate([jnp.cos(ang), jnp.sin(ang)], axis=0)


def _xattn_kernel(x_ref, ya_ref, yb_ref, wa_ref, wb_ref, gn_ref, wq_ref, kv_ref, qg_ref, kg_ref, wo_ref, o_ref,
                  *, heads):
    x = x_ref[0] + _dot(ya_ref[0], wa_ref[...]) + _dot(yb_ref[0], wb_ref[...])
    d = x.shape[-1] // heads
    q_all = _dot(_rms(x, gn_ref[...]).astype(BF16), wq_ref[...])
    outs = []
    for h in range(heads):
        q = _rms(q_all[:, h * d:(h + 1) * d], qg_ref[...]) * (d ** -0.5)
        k = _rms(kv_ref[0, :, h * d:(h + 1) * d], kg_ref[...])
        v = kv_ref[0, :, (heads + h) * d:(heads + h + 1) * d]
        s = _dot_nt(q.astype(BF16), k.astype(BF16))
        s = s - jnp.max(s, axis=-1, keepdims=True)
        pr = jnp.exp(s)
        o = _dot(pr.astype(BF16), v.astype(BF16)) / jnp.sum(pr, axis=-1, keepdims=True)
        outs.append(o.astype(BF16))
    o_ref[0] = x + _dot(jnp.concatenate(outs, axis=1), wo_ref[...])


def mixer_out_cross_attention(x, ya, yb, wa, wb, gn, wq, kv, q_g, k_g, wo):
    b, s, dm = x.shape
    mlen = kv.shape[1]
    ts = _pick(s, (1024, 512, 256, 128))
    const = lambda a: pl.BlockSpec(a.shape, lambda i, j: (0,) * a.ndim)
    seq = lambda a: pl.BlockSpec((1, ts, a.shape[2]), lambda i, j: (i, j, 0))
    row = lambda a: a.reshape(1, -1)
    return pl.pallas_call(
        functools.partial(_xattn_kernel, heads=XATTN_HEADS),
        grid=(b, s // ts),
        in_specs=[seq(x), seq(ya), seq(yb), const(wa), const(wb),
                  const(row(gn)), const(wq),
                  pl.BlockSpec((1, mlen, 2 * dm), lambda i, j: (i, 0, 0)),
                  const(row(q_g)), const(row(k_g)), const(wo)],
        out_specs=seq(x),
        out_shape=jax.ShapeDtypeStruct((b, s, dm), F32),
        compiler_params=_cparams("parallel", "arbitrary"),
    )(x, ya, yb, wa, wb, row(gn), wq, kv, row(q_g), row(k_g), wo)


def _swiglu(h, wgu, wd):
    f = wd.shape[0]
    tf = _col_tile(h.shape[1], f)
    y = None
    for lo in range(0, f, tf):
        gate = _dot(h, wgu[:, lo:lo + tf])
        up = _dot(h, wgu[:, f + lo:f + lo + tf])
        part = _dot((gate * jax.nn.sigmoid(gate) * up).astype(BF16), wd[lo:lo + tf, :].astype(BF16))
        y = part if y is None else y + part
    return y


def _ffn_kernel(x_ref, g_ref, wgu_ref, wd_ref, o_ref):
    x = x_ref[...]
    o_ref[...] = x + _swiglu(_rms(x, g_ref[...]).astype(BF16), wgu_ref, wd_ref)


def ffn_dense(x, g, w_gu, w_down):
    m, d = x.shape
    tm = _pick(m, (1024, 512, 256, 128))
    resident = lambda a: pl.BlockSpec(a.shape, lambda i: (0,) * a.ndim, pipeline_mode=pl.Buffered(1))
    return pl.pallas_call(
        _ffn_kernel,
        grid=(m // tm,),
        in_specs=[pl.BlockSpec((tm, d), lambda i: (i, 0)),
                  pl.BlockSpec((1, d), lambda i: (0, 0)),
                  resident(w_gu), resident(w_down)],
        out_specs=pl.BlockSpec((tm, d), lambda i: (i, 0)),
        out_shape=jax.ShapeDtypeStruct((m, d), F32),
        compiler_params=_cparams("parallel"),
    )(x, g.reshape(1, d), w_gu, w_down)


META_I1, META_I2, META_W1, META_W2, META_R1, META_R2 = range(6)


def _lane_pick(x, lane, idx):
    return jnp.sum(jnp.where(lane == idx, x, 0.0), axis=-1, keepdims=True)


def _router_kernel(x_ref, g_ref, wr_ref, meta_ref, cnt_ref, tri_ref, *, n_experts):
    tm = x_ref.shape[0]

    @pl.when(pl.program_id(0) == 0)
    def _():
        cnt_ref[...] = jnp.zeros_like(cnt_ref)
        r_id = lax.broadcasted_iota(jnp.int32, (tm, tm), 0)
        c_id = lax.broadcasted_iota(jnp.int32, (tm, tm), 1)
        tri_ref[...] = (r_id > c_id).astype(BF16)

    split = lambda t: (t.astype(BF16), (t - t.astype(BF16).astype(F32)).astype(BF16))
    h_hi, h_lo = split(_rms(x_ref[...], g_ref[...]))
    w_hi, w_lo = split(wr_ref[...])
    logits = _dot(h_hi, w_hi) + (_dot(h_hi, w_lo) + _dot(h_lo, w_hi))
    lane = lax.broadcasted_iota(jnp.int32, logits.shape, 1).astype(F32)
    neg = -jnp.inf
    l1 = jnp.where(lane < n_experts, logits, neg)
    m1 = jnp.max(l1, axis=-1, keepdims=True)
    i1 = jnp.min(jnp.where(l1 == m1, lane, float(LANES)), axis=-1, keepdims=True)
    l2 = jnp.where(lane == i1, neg, l1)
    m2 = jnp.max(l2, axis=-1, keepdims=True)
    i2 = jnp.min(jnp.where(l2 == m2, lane, float(LANES)), axis=-1, keepdims=True)
    e2 = jnp.exp(m2 - m1)
    w1 = 1.0 / (1.0 + e2)
    onehot = jnp.where((lane == i1) | (lane == i2), 1.0, 0.0)
    before = _dot(tri_ref[...], onehot.astype(BF16)) + cnt_ref[0:1, :]
    cnt_ref[0:1, :] += jnp.sum(onehot, axis=0, keepdims=True)
    meta = jnp.zeros_like(logits)
    for slot, val in ((META_I1, i1), (META_I2, i2), (META_W1, w1), (META_W2, e2 * w1),
                      (META_R1, _lane_pick(before, lane, i1)), (META_R2, _lane_pick(before, lane, i2))):
        meta = jnp.where(lane == slot, val, meta)
    meta_ref[...] = meta


def router_top2(x, g, w_router):
    m, d = x.shape
    n_experts = w_router.shape[1]
    wr = jnp.zeros((d, LANES), F32).at[:, :n_experts].set(w_router)
    tm = _pick(m, (1024, 512, 256, 128))
    return pl.pallas_call(
        functools.partial(_router_kernel, n_experts=n_experts),
        grid=(m // tm,),
        in_specs=[pl.BlockSpec((tm, d), lambda i: (i, 0)),
                  pl.BlockSpec((1, d), lambda i: (0, 0)),
                  pl.BlockSpec((d, LANES), lambda i: (0, 0))],
        out_specs=[pl.BlockSpec((tm, LANES), lambda i: (i, 0)),
                   pl.BlockSpec((8, LANES), lambda i: (0, 0))],
        out_shape=[jax.ShapeDtypeStruct((m, LANES), F32), jax.ShapeDtypeStruct((8, LANES), F32)],
        scratch_shapes=[pltpu.VMEM((tm, tm), BF16)],
        compiler_params=_cparams("arbitrary"),
    )(x, g.reshape(1, d), wr)


def _idx_copy(dest_hbm, idx_smem, isem, tile, slot):
    return pltpu.make_async_copy(dest_hbm.at[tile], idx_smem.at[slot], isem.at[slot])


def _dispatch_kernel(ends_ref, dest_hbm, x_ref, xs_hbm, idx_smem, zbuf, isem, sem, zsem, *, tm, tg):
    i = pl.program_id(0)
    slot = i % 2
    n_experts = ends_ref.shape[0]
    n_out_tiles = xs_hbm.shape[0] // tg

    @pl.when(i == 0)
    def _():
        _idx_copy(dest_hbm, idx_smem, isem, 0, 0).start()
        zbuf[...] = jnp.zeros_like(zbuf)
        fills = []
        for e in range(n_experts):
            group_start = ends_ref[e - 1] if e > 0 else 0
            fills.append((ends_ref[e] > group_start, ends_ref[e] - tg))
            tail = ends_ref[n_experts - 1] + e * tg
            fills.append((tail < n_out_tiles * tg, tail))
        zero_fill = lambda start: pltpu.make_async_copy(zbuf, xs_hbm.at[pl.ds(pl.multiple_of(start, tg), tg)], zsem)
        for cond, start in fills:
            @pl.when(cond)
            def _():
                zero_fill(start).start()
        for cond, start in fills:
            @pl.when(cond)
            def _():
                zero_fill(start).wait()

    @pl.when(i + 1 < pl.num_programs(0))
    def _():
        _idx_copy(dest_hbm, idx_smem, isem, i + 1, 1 - slot).start()

    _idx_copy(dest_hbm, idx_smem, isem, i, slot).wait()

    def row_copy(t, k):
        d = idx_smem[slot, TOP_K * t + k]
        return pltpu.make_async_copy(x_ref.at[pl.ds(t, 1)], xs_hbm.at[pl.ds(d, 1)], sem)

    def issue(t, carry):
        for k in range(TOP_K):
            row_copy(t, k).start()
        return carry

    lax.fori_loop(0, tm, issue, 0, unroll=8)
    for _ in range(TOP_K):
        pltpu.make_async_copy(x_ref, xs_hbm.at[pl.ds(0, tm)], sem).wait()


def moe_dispatch(x, dest, ends, n_rows, tg):
    m, d = x.shape
    n_tiles, per_tile = dest.shape
    tm = per_tile // TOP_K
    grid_spec = pltpu.PrefetchScalarGridSpec(
        num_scalar_prefetch=1,
        grid=(n_tiles,),
        in_specs=[pl.BlockSpec(memory_space=pl.ANY),
                  pl.BlockSpec((tm, d), lambda i, ends: (i, 0))],
        out_specs=pl.BlockSpec(memory_space=pl.ANY),
        scratch_shapes=[pltpu.SMEM((2, per_tile), jnp.int32), pltpu.VMEM((tg, d), F32),
                        pltpu.SemaphoreType.DMA((2,)), pltpu.SemaphoreType.DMA(()), pltpu.SemaphoreType.DMA(())],
    )
    return pl.pallas_call(
        functools.partial(_dispatch_kernel, tm=tm, tg=tg),
        grid_spec=grid_spec,
        out_shape=jax.ShapeDtypeStruct((n_rows, d), F32),
        compiler_params=_cparams("arbitrary"),
    )(ends, dest, x)


def _moe_ffn_kernel(te_ref, nu_ref, x_ref, g_ref, wgu_ref, wd_ref, o_ref):
    del te_ref

    @pl.when(pl.program_id(0) < nu_ref[0])
    def _():
        o_ref[...] = _swiglu(_rms(x_ref[...], g_ref[...]).astype(BF16), wgu_ref.at[0], wd_ref.at[0])

    @pl.when(pl.program_id(0) >= nu_ref[0])
    def _():
        o_ref[...] = jnp.zeros_like(o_ref)


def moe_grouped_ffn(xs, g, tile_expert, n_used, w_gu, w_down, tm):
    p, d = xs.shape
    f = w_down.shape[1]
    last = lambda n, nu: jnp.minimum(n, nu[0] - 1)
    grid_spec = pltpu.PrefetchScalarGridSpec(
        num_scalar_prefetch=2,
        grid=(p // tm,),
        in_specs=[pl.BlockSpec((tm, d), lambda n, te, nu: (last(n, nu), 0)),
                  pl.BlockSpec((1, d), lambda n, te, nu: (0, 0)),
                  pl.BlockSpec((1, d, 2 * f), lambda n, te, nu: (te[n], 0, 0), pipeline_mode=pl.Buffered(1)),
                  pl.BlockSpec((1, f, d), lambda n, te, nu: (te[n], 0, 0), pipeline_mode=pl.Buffered(1))],
        out_specs=pl.BlockSpec((tm, d), lambda n, te, nu: (n, 0)),
    )
    return pl.pallas_call(
        _moe_ffn_kernel,
        grid_spec=grid_spec,
        out_shape=jax.ShapeDtypeStruct((p, d), F32),
        compiler_params=_cparams("arbitrary"),
    )(tile_expert, n_used, xs, g.reshape(1, d), w_gu, w_down)


def _combine_kernel(dest_hbm, x_ref, meta_ref, ys_hbm, o_ref, idx_smem, ybuf, isem, sem, *, tm, n_tiles):
    i = pl.program_id(0)
    slot = i % 2

    def row_copy(sl, t, k):
        d = idx_smem[sl, TOP_K * t + k]
        return pltpu.make_async_copy(ys_hbm.at[pl.ds(d, 1)], ybuf.at[sl, k, pl.ds(t, 1)], sem.at[sl])

    def gather_tile(tile, sl):
        _idx_copy(dest_hbm, idx_smem, isem, tile, sl).wait()

        def issue(t, carry):
            for k in range(TOP_K):
                row_copy(sl, t, k).start()
            return carry

        lax.fori_loop(0, tm, issue, 0, unroll=8)

    @pl.when(i == 0)
    def _():
        _idx_copy(dest_hbm, idx_smem, isem, 0, 0).start()
        gather_tile(0, 0)
        if n_tiles > 1:
            _idx_copy(dest_hbm, idx_smem, isem, 1, 1).start()

    @pl.when(i + 1 < pl.num_programs(0))
    def _():
        gather_tile(i + 1, 1 - slot)

    @pl.when(i + 2 < pl.num_programs(0))
    def _():
        _idx_copy(dest_hbm, idx_smem, isem, i + 2, slot).start()

    for k in range(TOP_K):
        pltpu.make_async_copy(ys_hbm.at[pl.ds(0, tm)], ybuf.at[slot, k], sem.at[slot]).wait()
    meta = meta_ref[...]
    lane = lax.broadcasted_iota(jnp.int32, meta.shape, 1)
    w1 = _lane_pick(meta, lane, META_W1)
    w2 = _lane_pick(meta, lane, META_W2)
    o_ref[...] = x_ref[...] + w1 * ybuf[slot, 0] + w2 * ybuf[slot, 1]


def moe_combine(x, meta, dest, ys):
    m, d = x.shape
    n_tiles, per_tile = dest.shape
    tm = per_tile // TOP_K
    return pl.pallas_call(
        functools.partial(_combine_kernel, tm=tm, n_tiles=n_tiles),
        grid=(n_tiles,),
        in_specs=[pl.BlockSpec(memory_space=pl.ANY),
                  pl.BlockSpec((tm, d), lambda i: (i, 0)),
                  pl.BlockSpec((tm, LANES), lambda i: (i, 0)),
                  pl.BlockSpec(memory_space=pl.ANY)],
        out_specs=pl.BlockSpec((tm, d), lambda i: (i, 0)),
        out_shape=jax.ShapeDtypeStruct((m, d), F32),
        scratch_shapes=[pltpu.SMEM((2, per_tile), jnp.int32), pltpu.VMEM((2, TOP_K, tm, d), F32),
                        pltpu.SemaphoreType.DMA((2,)), pltpu.SemaphoreType.DMA((2,))],
        compiler_params=_cparams("arbitrary"),
    )(dest, x, meta, ys)


def ffn_moe(x, g, w_router, w_gu, w_down):
    m, d = x.shape
    n_experts = w_down.shape[0]
    tg = _pick(m, (MOE_GROUP_TILE, 256, 128))
    td = _pick(m, (MOE_TOKEN_TILE, 256, 128))
    meta, counts = router_top2(x, g, w_router)
    counts = counts[0, :n_experts].astype(jnp.int32)
    padded = (counts + tg - 1) // tg * tg
    ends = jnp.cumsum(padded)
    offsets = ends - padded
    n_tiles = (TOP_K * m) // tg + n_experts
    n_used = (ends[-1] // tg).astype(jnp.int32)
    tile_id = jnp.minimum(jnp.arange(n_tiles, dtype=jnp.int32), n_used - 1)
    tile_expert = jnp.sum(ends[None, :] <= (tile_id * tg)[:, None], axis=1).astype(jnp.int32)
    picks = meta[:, META_I1:META_I2 + 1].astype(jnp.int32)
    ranks = meta[:, META_R1:META_R2 + 1].astype(jnp.int32)
    dest = (offsets[picks] + ranks).reshape(m // td, TOP_K * td)
    xs = moe_dispatch(x, dest, ends.astype(jnp.int32), n_tiles * tg, tg)
    ys = moe_grouped_ffn(xs, g, tile_expert, n_used.reshape(1), w_gu, w_down, tg)
    return moe_combine(x, meta, dest, ys)


def _block_diag(w_a, w_b):
    za = jnp.zeros_like(w_a)
    zb = jnp.zeros_like(w_b)
    return jnp.concatenate([jnp.concatenate([w_a, za], axis=1), jnp.concatenate([zb, w_b], axis=1)], axis=0)


def kernel(x, mem, positions, mix_norm, in_proj, shift_mu, decay_bias, decay_up, iclr_bias, iclr_up, gate_up,
           key_kk_scale, key_iclr_scale, bonus_rk, rwkv_gn_w, rwkv_gn_b, vres_down, vres_shift_mu, vres_bias,
           vres_up, q_norm, k_norm, lambda_q1, lambda_k1, lambda_q2, lambda_k2, diff_subln, out_proj, xattn_norm,
           mem_norm, xattn_wq, xattn_wkv, xattn_wo, xattn_q_norm, xattn_k_norm, ffn_norm, dense_w_gu,
           dense_w_down, router, expert_w_gu, expert_w_down):
    b, s, d = x.shape
    depth = mix_norm.shape[0]
    width = decay_bias.shape[1]
    rwkv_cols = shift_mu.shape[1]
    in_cols = in_proj.shape[2]
    vres_rank = vres_down.shape[2] if depth > 1 else 0
    assert decay_up.shape[1] == LANES // 2 and iclr_up.shape[1] == LANES // 2 and gate_up.shape[1] == LANES
    assert rwkv_cols == 3 * width + 2 * LANES and rwkv_cols % LANES == 0 and in_cols % LANES == 0
    assert vres_rank <= LANES
    rot = rope_table(positions)
    xf = x.reshape(b * s, d)
    memf = mem.reshape(-1, d)
    v_first = None
    for l in range(depth):
        w_rwkv = in_proj[l][:, :rwkv_cols]
        vres = None
        if l > 0:
            pad = jnp.zeros((d, LANES - vres_rank), F32)
            w_rwkv = jnp.concatenate([w_rwkv, vres_down[l - 1], pad], axis=1)
            vmu = jnp.zeros((LANES,), F32).at[:vres_rank].set(vres_shift_mu[l - 1])
            vup = jnp.zeros((LANES, width), F32).at[:vres_rank].set(vres_up[l - 1]).astype(BF16)
            vres = (v_first, vmu, vres_bias[l - 1], vup)
        wwa = _block_diag(decay_up[l], iclr_up[l]).astype(BF16)
        r_, lw_, k_, v_, a_, g_ = rwkv_in_proj(xf, s, mix_norm[l], w_rwkv.astype(BF16), width, shift_mu[l],
                                               decay_bias[l], iclr_bias[l], wwa, gate_up[l].astype(BF16), vres)
        if l == 0:
            v_first = v_
        seq3 = lambda t: t.reshape(b, s, width)
        y_rwkv = rwkv_chunked(seq3(r_), seq3(lw_), seq3(k_), seq3(v_), seq3(a_), seq3(g_), key_kk_scale[l],
                              key_iclr_scale[l], bonus_rk[l].reshape(-1), rwkv_gn_w[l], rwkv_gn_b[l])
        lam_init = 0.8 - 0.6 * math.exp(-0.3 * l)
        lam_vecs = jnp.stack([lambda_q1[l], lambda_k1[l], lambda_q2[l], lambda_k2[l]])
        y_diff = diff_attention(xf, b, mix_norm[l], in_proj[l][:, rwkv_cols:].astype(BF16), rot, q_norm[l],
                                k_norm[l], lam_vecs, diff_subln[l], lam_init)
        w_out = out_proj[l].astype(BF16)
        kv = norm_matmul(memf, mem_norm[l], xattn_wkv[l].astype(BF16)).reshape(b, -1, 2 * d)
        xf = mixer_out_cross_attention(xf.reshape(b, s, d), y_rwkv, y_diff, w_out[:width], w_out[width:],
                                       xattn_norm[l], xattn_wq[l].astype(BF16), kv, xattn_q_norm[l],
                                       xattn_k_norm[l], xattn_wo[l].astype(BF16)).reshape(b * s, d)
        if l % 2 == 0:
            xf = ffn_dense(xf, ffn_norm[l], dense_w_gu[l // 2].astype(BF16), dense_w_down[l // 2].astype(BF16))
        else:
            xf = ffn_moe(xf, ffn_norm[l], router[l // 2], expert_w_gu[l // 2].astype(BF16),
                         expert_w_down[l // 2])
    return xf.reshape(b, s, d)
```

```python
import functools
import math

import jax
import jax.numpy as jnp
from jax import lax
from jax.experimental import pallas as pl
from jax.experimental.pallas import tpu as pltpu

F32 = jnp.float32
BF16 = jnp.bfloat16

EPS = 1e-6
GN_EPS = 64e-5
ROPE_THETA = 500000.0
RWKV_HEAD = 64
DIFF_HEADS = 4
XATTN_HEADS = 4
TOP_K = 2
LANES = 128
CHUNK = 128
VMEM_LIMIT = 56 * 1024 * 1024
RWKV_ROWS_PER_STEP = 4
RWKV_IN_ROW_PARTS = 4
ATTN_ROW_PARTS = 4
WEIGHT_TILE_BYTES = 4 * 1024 * 1024
MOE_GROUP_TILE = 512
MOE_TOKEN_TILE = 1024


def _cparams(*sem):
    return pltpu.CompilerParams(dimension_semantics=sem, vmem_limit_bytes=VMEM_LIMIT)


def _pick(n, prefs):
    for p in prefs:
        if n % p == 0:
            return p
    return n


def _col_tile(k, n):
    best = LANES
    for t in range(LANES, n + 1, LANES):
        if n % t == 0 and k * t * 2 <= WEIGHT_TILE_BYTES:
            best = t
    return best


def _dot(a, b, prec=None):
    return jnp.dot(a, b, preferred_element_type=F32, precision=prec)


def _dot_nt(a, b, prec=None):
    return lax.dot_general(a, b, (((1,), (1,)), ((), ())), preferred_element_type=F32, precision=prec)


def _rms(x, g):
    return x * lax.rsqrt(jnp.mean(x * x, axis=-1, keepdims=True) + EPS) * g


def _norm_mm_kernel(x_ref, g_ref, w_ref, o_ref, h_ref):
    @pl.when(pl.program_id(1) == 0)
    def _():
        h_ref[...] = _rms(x_ref[...], g_ref[...]).astype(BF16)

    o_ref[...] = _dot(h_ref[...], w_ref[...])


def norm_matmul(x, g, w):
    m, k = x.shape
    n = w.shape[1]
    tm = _pick(m, (1024, 512, 256, 128))
    tn = _col_tile(k, n)
    return pl.pallas_call(
        _norm_mm_kernel,
        grid=(m // tm, n // tn),
        in_specs=[pl.BlockSpec((tm, k), lambda i, j: (i, 0)),
                  pl.BlockSpec((1, k), lambda i, j: (0, 0)),
                  pl.BlockSpec((k, tn), lambda i, j: (0, j))],
        out_specs=pl.BlockSpec((tm, tn), lambda i, j: (i, j)),
        out_shape=jax.ShapeDtypeStruct((m, n), F32),
        scratch_shapes=[pltpu.VMEM((tm, k), BF16)],
        compiler_params=_cparams("parallel", "arbitrary"),
    )(x, g.reshape(1, k), w)


def _token_shift(p, prev_row, mu):
    prev = pltpu.roll(p, 1, axis=0)
    rid = lax.broadcasted_iota(jnp.int32, p.shape, 0)
    prev = jnp.where(rid == 0, prev_row, prev)
    return p + (prev - p) * mu


def _rwkv_in_kernel(*refs, width, has_vres, tiles_per_seq):
    if has_vres:
        (x_ref, gn_ref, w_ref, vf_ref, mu_ref, w0_ref, a0_ref, wwa_ref, gup_ref, vmu_ref, vb_ref, vup_ref,
         r_ref, lw_ref, k_ref, v_ref, a_ref, g_ref, carry_ref, carry_v_ref) = refs
    else:
        (x_ref, gn_ref, w_ref, mu_ref, w0_ref, a0_ref, wwa_ref, gup_ref,
         r_ref, lw_ref, k_ref, v_ref, a_ref, g_ref, carry_ref) = refs
    ncol = 3 * width + 2 * LANES
    tm = x_ref.shape[0]
    n_parts = RWKV_IN_ROW_PARTS
    rp = tm // n_parts

    @pl.when(pl.program_id(0) % tiles_per_seq == 0)
    def _():
        carry_ref[...] = jnp.zeros_like(carry_ref)
        if has_vres:
            carry_v_ref[...] = jnp.zeros_like(carry_v_ref)

    projs = [_dot(_rms(x_ref[i * rp:(i + 1) * rp, :], gn_ref[...]).astype(BF16), w_ref[...]) for i in range(n_parts)]
    prev = carry_ref[...]
    prev_v = carry_v_ref[...] if has_vres else None
    for i, proj in enumerate(projs):
        rows = slice(i * rp, (i + 1) * rp)
        sh = _token_shift(proj[:, :ncol], prev, mu_ref[...])
        prev = proj[rp - 1:rp, :ncol]
        r_ref[rows, :] = sh[:, :width].astype(r_ref.dtype)
        k_ref[rows, :] = sh[:, width:2 * width].astype(k_ref.dtype)
        v = sh[:, 2 * width:3 * width]
        dwa = sh[:, 3 * width:3 * width + LANES]
        lane = lax.broadcasted_iota(jnp.int32, dwa.shape, 1)
        dwa = jnp.where(lane < LANES // 2, jnp.tanh(dwa), dwa)
        wa = _dot(dwa.astype(BF16), wwa_ref[...])
        z = -(w0_ref[...] + wa[:, :width])
        softplus = jnp.maximum(z, 0.0) + jnp.log(1.0 + jnp.exp(-jnp.abs(z)))
        lw_ref[rows, :] = -jnp.exp(-softplus - 0.5)
        a_ref[rows, :] = jax.nn.sigmoid(a0_ref[...] + wa[:, width:]).astype(a_ref.dtype)
        dg = jax.nn.sigmoid(sh[:, 3 * width + LANES:3 * width + 2 * LANES])
        g_ref[rows, :] = _dot(dg.astype(BF16), gup_ref[...]).astype(g_ref.dtype)
        if has_vres:
            shv = _token_shift(proj[:, ncol:ncol + LANES], prev_v, vmu_ref[...])
            prev_v = proj[rp - 1:rp, ncol:ncol + LANES]
            mix = jax.nn.sigmoid(vb_ref[...] + _dot(shv.astype(BF16), vup_ref[...]))
            v = v + (vf_ref[rows, :].astype(F32) - v) * mix
        v_ref[rows, :] = v.astype(v_ref.dtype)
    carry_ref[...] = prev
    if has_vres:
        carry_v_ref[...] = prev_v


def rwkv_in_proj(x, seq_len, gn, w, width, mu, w0, a0, wwa, gup, vres=None):
    m, d = x.shape
    tm = _pick(seq_len, (1024, 512, 256, 128))
    ncol = 3 * width + 2 * LANES
    assert mu.shape[-1] == ncol and m % seq_len == 0
    row = lambda a: a.reshape(1, -1)
    full = lambda a: pl.BlockSpec(a.shape, lambda i: (0,) * a.ndim)
    rows = lambda c: pl.BlockSpec((tm, c), lambda i: (i, 0))
    params = [row(mu), row(w0), row(a0), wwa, gup]
    inputs = [x, row(gn), w]
    in_specs = [rows(d), full(row(gn)), full(w)]
    scratch = [pltpu.VMEM((1, ncol), F32)]
    if vres is not None:
        v_first, vmu, vb, vup = vres
        inputs.append(v_first)
        in_specs.append(rows(width))
        params += [row(vmu), row(vb), vup]
        scratch.append(pltpu.VMEM((1, LANES), F32))
    in_specs += [full(p) for p in params]
    out = lambda dt: jax.ShapeDtypeStruct((m, width), dt)
    return pl.pallas_call(
        functools.partial(_rwkv_in_kernel, width=width, has_vres=vres is not None, tiles_per_seq=seq_len // tm),
        grid=(m // tm,),
        in_specs=in_specs,
        out_specs=[rows(width)] * 6,
        out_shape=[out(BF16), out(F32), out(BF16), out(BF16), out(BF16), out(BF16)],
        scratch_shapes=scratch,
        compiler_params=_cparams("arbitrary"),
    )(*inputs, *params)


def _rwkv_chunk_kernel(r_ref, lw_ref, k_ref, v_ref, a_ref, g_ref, kk_ref, ka_ref, rk_ref, gw_ref, gb_ref,
                       y_ref, state_ref, *, n_pairs):
    c = CHUNK
    half = LANES // 2

    @pl.when(pl.program_id(1) == 0)
    def _():
        state_ref[...] = jnp.zeros_like(state_ref)

    row = lax.broadcasted_iota(jnp.int32, (c, c), 0)
    col = lax.broadcasted_iota(jnp.int32, (c, c), 1)
    ltri = (row >= col).astype(BF16)
    same_head = (row < half) == (col < half)
    lo = col < half
    row4 = lax.broadcasted_iota(jnp.int32, (c, 4 * c), 0)
    col4 = lax.broadcasted_iota(jnp.int32, (c, 4 * c), 1)
    incl4 = row4 >= (col4 & (c - 1))
    row2 = lax.broadcasted_iota(jnp.int32, (c, 2 * c), 0)
    col2 = lax.broadcasted_iota(jnp.int32, (c, 2 * c), 1)
    strict2 = row2 > (col2 & (c - 1))
    first_block = col2 < c

    def gsum(x):
        s_lo = jnp.sum(jnp.where(lo, x, 0.0), axis=-1, keepdims=True)
        s_hi = jnp.sum(jnp.where(lo, 0.0, x), axis=-1, keepdims=True)
        return jnp.where(lo, s_lo, s_hi)

    def by_head(x):
        return jnp.concatenate([jnp.where(lo, x, 0.0), jnp.where(lo, 0.0, x)], axis=0).astype(BF16)

    def split_bf16(x):
        hi = x.astype(BF16)
        rest = x - hi.astype(F32)
        mid = rest.astype(BF16)
        return hi, mid, (rest - mid.astype(F32)).astype(BF16)

    def stage_factors(bi, p):
        sl = slice(p * LANES, (p + 1) * LANES)
        r, lw, k, v, a = (ref[bi, :, sl].astype(F32) for ref in (r_ref, lw_ref, k_ref, v_ref, a_ref))
        cum3 = _dot(ltri, jnp.concatenate(split_bf16(lw), axis=1))
        cum = cum3[:, :LANES] + cum3[:, LANES:2 * LANES] + cum3[:, 2 * LANES:]
        cmid = cum[c // 2 - 1:c // 2, :]
        clast = cum[c - 1:c, :]
        ci = cum - cmid
        e_neg = jnp.exp(-ci)
        p_mid = jnp.exp(cmid)
        kk = k * kk_ref[:, sl]
        kkn = kk * lax.rsqrt(jnp.maximum(gsum(kk * kk), 1e-24))
        k2 = k * (1.0 + (a - 1.0) * ka_ref[:, sl])
        beta = kkn * a
        al_m = -kkn * jnp.exp(ci - lw)
        r_m = r * jnp.exp(ci)
        e_end = jnp.exp(clast - cum)
        ends = jnp.concatenate([beta * e_end, k2 * e_end], axis=0).astype(BF16)
        al2, r2 = by_head(al_m), by_head(r_m)
        lhs = jnp.concatenate([al2[:c], r2[:c], al2[c:], r2[c:]], axis=0)
        rhs = jnp.concatenate([beta * e_neg, k2 * e_neg], axis=0).astype(BF16)
        big = _dot_nt(lhs, rhs)
        s0 = state_ref[bi * n_pairs + p]
        base = _dot_nt(jnp.concatenate([al_m * p_mid, r_m * p_mid], axis=0).astype(BF16), s0.astype(BF16))
        return dict(big=big, base=base, v2=by_head(v), ends=ends, decayed=s0 * jnp.exp(clast),
                    bonus=gsum(r * k2 * rk_ref[:, sl]) * v)

    def stage_masks(d):
        big = d.pop("big")
        blk = lambda i, j: big[i * c:(i + 1) * c, j * c:(j + 1) * c]
        zero2 = jnp.zeros((c, 2 * c), F32)
        d["pw"] = jnp.where(strict2, jnp.concatenate([blk(0, 0), blk(2, 0)], axis=1), zero2).astype(BF16)
        a_ak = jnp.where(strict2, jnp.concatenate([blk(0, 1), blk(2, 1)], axis=1), zero2).astype(BF16)
        d["a_y"] = jnp.where(incl4, jnp.concatenate([blk(1, 0), blk(3, 0), blk(1, 1), blk(3, 1)], axis=1),
                             0.0).astype(BF16)
        d["x"] = d["base"][:c] + _dot(a_ak, d["v2"])

    def stage_square(d, last):
        pw, x2 = d["pw"], by_head(d["x"])
        if last:
            d["x"] = d["x"] + _dot(pw, x2)
            return
        zero_b = jnp.zeros_like(pw)
        bd = jnp.concatenate([jnp.where(first_block, pw, zero_b), jnp.where(first_block, zero_b, pw)], axis=0)
        res = _dot(pw, jnp.concatenate([bd, x2], axis=1))
        d["pw"] = res[:, :2 * c].astype(BF16)
        d["x"] = d["x"] + res[:, 2 * c:]

    def stage_output(bi, p, d):
        sl = slice(p * LANES, (p + 1) * LANES)
        x = d["x"]
        y = d["base"][c:] + _dot(d["a_y"], jnp.concatenate([by_head(x), d["v2"]], axis=0))
        upd = _dot(jnp.concatenate([x.T, v_ref[bi, :, sl].astype(F32).T], axis=1).astype(BF16), d["ends"])
        state_ref[bi * n_pairs + p] = d["decayed"] + jnp.where(same_head, upd, 0.0)
        mean = gsum(y) * (1.0 / half)
        yc = y - mean
        var = gsum(yc * yc) * (1.0 / half)
        out = yc * lax.rsqrt(var + GN_EPS) * gw_ref[:, sl] + gb_ref[:, sl]
        y_ref[bi, :, sl] = ((out + d["bonus"]) * g_ref[bi, :, sl].astype(F32)).astype(y_ref.dtype)

    ids = [(bi, p) for bi in range(r_ref.shape[0]) for p in range(n_pairs)]
    pairs = [stage_factors(bi, p) for bi, p in ids]
    for d in pairs:
        stage_masks(d)
    n_steps = c.bit_length() - 1
    for step in range(n_steps):
        for d in pairs:
            stage_square(d, last=step + 1 == n_steps)
    for (bi, p), d in zip(ids, pairs):
        stage_output(bi, p, d)


def rwkv_chunked(r, lw, k, v, a, g, k_k, k_a, r_k, gn_w, gn_b):
    b, s, width = r.shape
    assert RWKV_HEAD * 2 == LANES and width % LANES == 0 and s % CHUNK == 0
    n_pairs = width // LANES
    nb = _pick(b, (RWKV_ROWS_PER_STEP, 1))
    seq = pl.BlockSpec((nb, CHUNK, width), lambda i, j: (i, j, 0))
    par = pl.BlockSpec((1, width), lambda i, j: (0, 0))
    row = lambda t: t.reshape(1, width)
    return pl.pallas_call(
        functools.partial(_rwkv_chunk_kernel, n_pairs=n_pairs),
        grid=(b // nb, s // CHUNK),
        in_specs=[seq] * 6 + [par] * 5,
        out_specs=seq,
        out_shape=jax.ShapeDtypeStruct((b, s, width), BF16),
        scratch_shapes=[pltpu.VMEM((nb * n_pairs, LANES, LANES), F32)],
        compiler_params=_cparams("parallel", "arbitrary"),
    )(r, lw, k, v, a, g, row(k_k), row(k_a), row(r_k), row(gn_w), row(gn_b))


def _diff_in_kernel(x_ref, gn_ref, w_ref, rot_ref, qg_ref, kg_ref, qk_ref, v_ref, *, heads):
    half = LANES // 2
    rope_half = half // 8
    proj = _dot(_rms(x_ref[...], gn_ref[...]).astype(BF16), w_ref[...])
    lane = lax.broadcasted_iota(jnp.int32, (x_ref.shape[0], LANES), 1)
    pos = lane & (half - 1)
    e_row = lax.broadcasted_iota(jnp.int32, (2 * rope_half, LANES), 0)
    e_col = lax.broadcasted_iota(jnp.int32, (2 * rope_half, LANES), 1)
    expand = ((e_col & (half - 1)) == e_row).astype(BF16)
    cs = rot_ref[...]
    rot = None
    for _ in range(3):
        piece = cs.astype(BF16)
        term = lax.dot_general(piece, expand, (((0,), (0,)), ((), ())), preferred_element_type=F32)
        rot = term if rot is None else rot + term
        cs = cs - piece.astype(F32)
    first, second = pos < rope_half, (pos >= rope_half) & (pos < 2 * rope_half)
    cf = jnp.where(first, rot, jnp.where(second, pltpu.roll(rot, rope_half, axis=1), 1.0))
    sn = jnp.where(first, -pltpu.roll(rot, LANES - rope_half, axis=1), jnp.where(second, rot, 0.0))

    g_row = lax.broadcasted_iota(jnp.int32, (LANES, LANES), 0)
    g_col = lax.broadcasted_iota(jnp.int32, (LANES, LANES), 1)
    same_half = ((g_row < half) == (g_col < half)).astype(BF16)
    c_pos = g_col & (half - 1)
    partner = jnp.where(c_pos < rope_half, g_col + rope_half, jnp.where(c_pos < 2 * rope_half, g_col - rope_half, -1))
    swap = (g_row == partner).astype(BF16)

    def norm_rope(x, g):
        ms = _dot((x * x).astype(BF16), same_half) * (1.0 / half)
        x = x * lax.rsqrt(ms + EPS) * g
        return x * cf + _dot(x.astype(BF16), swap) * sn

    ones = jnp.ones((x_ref.shape[0], LANES), BF16)
    for h in range(heads):
        tile = lambda i: proj[:, (i * heads + h) * LANES:(i * heads + h + 1) * LANES]
        q = norm_rope(tile(0), qg_ref[...]) * (half ** -0.5 * math.log2(math.e))
        qk_ref[:, h * LANES:(h + 1) * LANES] = q.astype(BF16)
        qk_ref[:, (heads + h) * LANES:(heads + h + 1) * LANES] = norm_rope(tile(1), kg_ref[...]).astype(BF16)
        v_ref[:, 2 * h * LANES:(2 * h + 1) * LANES] = tile(2).astype(BF16)
        v_ref[:, (2 * h + 1) * LANES:(2 * h + 2) * LANES] = ones


def _diff_attn_kernel(q_ref, k_ref, v_ref, lam_ref, sub_ref, o_ref, *, tq, lam_init):
    half = LANES // 2
    s_len = q_ref.shape[1]
    lv = lam_ref[...]
    lam = (jnp.exp(jnp.sum(lv[0:1] * lv[1:2], axis=-1, keepdims=True))
           - jnp.exp(jnp.sum(lv[2:3] * lv[3:4], axis=-1, keepdims=True)) + lam_init)
    lane = lax.broadcasted_iota(jnp.int32, (tq, LANES), 1)
    n_parts = ATTN_ROW_PARTS
    rp = tq // n_parts
    chains = [(m, part) for part in range(n_parts) for m in range(2)]

    def block(qs, j, carry, masked):
        kb = k_ref[0, j * tq:(j + 1) * tq, :]
        vb = v_ref[0, j * tq:(j + 1) * tq, :]
        scores, probs, maxes, out = {}, {}, {}, {}
        for i in range(len(chains) + 2):
            if i < len(chains):
                m, part = chains[i]
                keys = (part + 1) * rp if masked else tq
                s = _dot_nt(qs[m][part * rp:(part + 1) * rp], kb[:keys])
                if masked:
                    r_id = lax.broadcasted_iota(jnp.int32, s.shape, 0) + part * rp
                    c_id = lax.broadcasted_iota(jnp.int32, s.shape, 1)
                    s = jnp.where(c_id <= r_id, s, -1e30)
                scores[i] = s
            if 0 <= i - 1 < len(chains):
                c = i - 1
                s = scores.pop(c)
                mx = jnp.max(s, axis=-1, keepdims=True)
                maxes[c] = mx if carry is None else jnp.maximum(carry[2 * c], mx)
                probs[c] = jnp.exp2((s - maxes[c]).astype(BF16))
            if 0 <= i - 2 < len(chains):
                c = i - 2
                pr = probs.pop(c)
                pv = _dot(pr, vb[:pr.shape[1]])
                out[c] = pv if carry is None else jnp.exp2(carry[2 * c] - maxes[c]) * carry[2 * c + 1] + pv
        return tuple(t for c in range(len(chains)) for t in (maxes[c], out[c]))

    for qi in range(s_len // tq):
        q = q_ref[0, qi * tq:(qi + 1) * tq, :]
        zero = jnp.zeros_like(q)
        qs = [jnp.where(lane < half, q, zero), jnp.where(lane < half, zero, q)]
        carry = block(qs, qi, None, True)
        for j in range(qi):
            carry = block(qs, j, carry, False)
        acc1 = jnp.concatenate([carry[2 * c + 1] for c, (m, _) in enumerate(chains) if m == 0], axis=0)
        acc2 = jnp.concatenate([carry[2 * c + 1] for c, (m, _) in enumerate(chains) if m == 1], axis=0)
        o = acc1[:, :LANES] / acc1[:, LANES:] - lam * (acc2[:, :LANES] / acc2[:, LANES:])
        o_ref[0, qi * tq:(qi + 1) * tq, :] = (_rms(o, sub_ref[...]) * (1.0 - lam_init)).astype(o_ref.dtype)


def diff_attention(x, b, gn, w, rot, q_g, k_g, lam_vecs, subln_g, lam_init):
    m, d = x.shape
    s = m // b
    h = DIFF_HEADS
    assert w.shape[1] == 3 * h * LANES
    tm = _pick(m, (1024, 512, 256, 128))
    tile2 = lambda g: jnp.tile(g, 2).reshape(1, LANES)
    rows = lambda c: pl.BlockSpec((tm, c), lambda i: (i, 0))
    full = lambda a: pl.BlockSpec(a.shape, lambda i: (0,) * a.ndim)
    gspec2 = pl.BlockSpec((1, LANES), lambda i: (0, 0))
    qk, vext = pl.pallas_call(
        functools.partial(_diff_in_kernel, heads=h),
        grid=(m // tm,),
        in_specs=[rows(d), pl.BlockSpec((1, d), lambda i: (0, 0)), full(w),
                  pl.BlockSpec((rot.shape[0], tm), lambda i: (0, i)), gspec2, gspec2],
        out_specs=[rows(2 * h * LANES), rows(2 * h * LANES)],
        out_shape=[jax.ShapeDtypeStruct((m, 2 * h * LANES), BF16)] * 2,
        compiler_params=_cparams("parallel"),
    )(x, gn.reshape(1, d), w, rot, tile2(q_g), tile2(k_g))
    qk = qk.reshape(b, s, 2 * h * LANES)
    vext = vext.reshape(b, s, 2 * h * LANES)

    tq = _pick(s, (1024, 512, 256, 128))
    gspec = pl.BlockSpec((1, LANES), lambda bi, hi: (0, 0))
    return pl.pallas_call(
        functools.partial(_diff_attn_kernel, tq=tq, lam_init=lam_init),
        grid=(b, h),
        in_specs=[pl.BlockSpec((1, s, LANES), lambda bi, hi: (bi, 0, hi)),
                  pl.BlockSpec((1, s, LANES), lambda bi, hi: (bi, 0, h + hi)),
                  pl.BlockSpec((1, s, 2 * LANES), lambda bi, hi: (bi, 0, hi)),
                  pl.BlockSpec(lam_vecs.shape, lambda bi, hi: (0, 0)),
                  gspec],
        out_specs=pl.BlockSpec((1, s, LANES), lambda bi, hi: (bi, 0, hi)),
        out_shape=jax.ShapeDtypeStruct((b, s, h * LANES), BF16),
        compiler_params=_cparams("parallel", "parallel"),
    )(qk, qk, vext, lam_vecs, subln_g.reshape(1, LANES))


def rope_table(positions):
    rope_dims = LANES // 8
    inv_freq = ROPE_THETA ** (-jnp.arange(0, rope_dims, 2, dtype=F32) / rope_dims)
    ang = inv_freq[:, None] * positions.astype(F32).reshape(1, -1)
    return jnp.concatenate([jnp.cos(ang), jnp.sin(ang)], axis=0)


def _xattn_kernel(x_ref, ya_ref, yb_ref, wa_ref, wb_ref, gn_ref, wq_ref, kv_ref, qg_ref, kg_ref, wo_ref, o_ref,
                  *, heads):
    x = x_ref[0] + _dot(ya_ref[0], wa_ref[...]) + _dot(yb_ref[0], wb_ref[...])
    d = x.shape[-1] // heads
    q_all = _dot(_rms(x, gn_ref[...]).astype(BF16), wq_ref[...])
    outs = []
    for h in range(heads):
        q = _rms(q_all[:, h * d:(h + 1) * d], qg_ref[...]) * (d ** -0.5)
        k = _rms(kv_ref[0, :, h * d:(h + 1) * d], kg_ref[...])
        v = kv_ref[0, :, (heads + h) * d:(heads + h + 1) * d]
        s = _dot_nt(q.astype(BF16), k.astype(BF16))
        s = s - jnp.max(s, axis=-1, keepdims=True)
        pr = jnp.exp(s)
        o = _dot(pr.astype(BF16), v.astype(BF16)) / jnp.sum(pr, axis=-1, keepdims=True)
        outs.append(o.astype(BF16))
    o_ref[0] = x + _dot(jnp.concatenate(outs, axis=1), wo_ref[...])


def mixer_out_cross_attention(x, ya, yb, wa, wb, gn, wq, kv, q_g, k_g, wo):
    b, s, dm = x.shape
    mlen = kv.shape[1]
    ts = _pick(s, (1024, 512, 256, 128))
    const = lambda a: pl.BlockSpec(a.shape, lambda i, j: (0,) * a.ndim)
    seq = lambda a: pl.BlockSpec((1, ts, a.shape[2]), lambda i, j: (i, j, 0))
    row = lambda a: a.reshape(1, -1)
    return pl.pallas_call(
        functools.partial(_xattn_kernel, heads=XATTN_HEADS),
        grid=(b, s // ts),
        in_specs=[seq(x), seq(ya), seq(yb), const(wa), const(wb),
                  const(row(gn)), const(wq),
                  pl.BlockSpec((1, mlen, 2 * dm), lambda i, j: (i, 0, 0)),
                  const(row(q_g)), const(row(k_g)), const(wo)],
        out_specs=seq(x),
        out_shape=jax.ShapeDtypeStruct((b, s, dm), F32),
        compiler_params=_cparams("parallel", "arbitrary"),
    )(x, ya, yb, wa, wb, row(gn), wq, kv, row(q_g), row(k_g), wo)


def _swiglu(h, wgu, wd):
    f = wd.shape[0]
    tf = _col_tile(h.shape[1], f)
    y = None
    for lo in range(0, f, tf):
        gate = _dot(h, wgu[:, lo:lo + tf])
        up = _dot(h, wgu[:, f + lo:f + lo + tf])
        part = _dot((gate * jax.nn.sigmoid(gate) * up).astype(BF16), wd[lo:lo + tf, :].astype(BF16))
        y = part if y is None else y + part
    return y


def _ffn_kernel(x_ref, g_ref, wgu_ref, wd_ref, o_ref):
    x = x_ref[...]
    o_ref[...] = x + _swiglu(_rms(x, g_ref[...]).astype(BF16), wgu_ref, wd_ref)


def ffn_dense(x, g, w_gu, w_down):
    m, d = x.shape
    tm = _pick(m, (1024, 512, 256, 128))
    resident = lambda a: pl.BlockSpec(a.shape, lambda i: (0,) * a.ndim, pipeline_mode=pl.Buffered(1))
    return pl.pallas_call(
        _ffn_kernel,
        grid=(m // tm,),
        in_specs=[pl.BlockSpec((tm, d), lambda i: (i, 0)),
                  pl.BlockSpec((1, d), lambda i: (0, 0)),
                  resident(w_gu), resident(w_down)],
        out_specs=pl.BlockSpec((tm, d), lambda i: (i, 0)),
        out_shape=jax.ShapeDtypeStruct((m, d), F32),
        compiler_params=_cparams("parallel"),
    )(x, g.reshape(1, d), w_gu, w_down)


META_I1, META_I2, META_W1, META_W2, META_R1, META_R2 = range(6)


def _lane_pick(x, lane, idx):
    return jnp.sum(jnp.where(lane == idx, x, 0.0), axis=-1, keepdims=True)


def _router_kernel(x_ref, g_ref, wr_ref, meta_ref, cnt_ref, tri_ref, *, n_experts):
    tm = x_ref.shape[0]

    @pl.when(pl.program_id(0) == 0)
    def _():
        cnt_ref[...] = jnp.zeros_like(cnt_ref)
        r_id = lax.broadcasted_iota(jnp.int32, (tm, tm), 0)
        c_id = lax.broadcasted_iota(jnp.int32, (tm, tm), 1)
        tri_ref[...] = (r_id > c_id).astype(BF16)

    split = lambda t: (t.astype(BF16), (t - t.astype(BF16).astype(F32)).astype(BF16))
    h_hi, h_lo = split(_rms(x_ref[...], g_ref[...]))
    w_hi, w_lo = split(wr_ref[...])
    logits = _dot(h_hi, w_hi) + (_dot(h_hi, w_lo) + _dot(h_lo, w_hi))
    lane = lax.broadcasted_iota(jnp.int32, logits.shape, 1).astype(F32)
    neg = -jnp.inf
    l1 = jnp.where(lane < n_experts, logits, neg)
    m1 = jnp.max(l1, axis=-1, keepdims=True)
    i1 = jnp.min(jnp.where(l1 == m1, lane, float(LANES)), axis=-1, keepdims=True)
    l2 = jnp.where(lane == i1, neg, l1)
    m2 = jnp.max(l2, axis=-1, keepdims=True)
    i2 = jnp.min(jnp.where(l2 == m2, lane, float(LANES)), axis=-1, keepdims=True)
    e2 = jnp.exp(m2 - m1)
    w1 = 1.0 / (1.0 + e2)
    onehot = jnp.where((lane == i1) | (lane == i2), 1.0, 0.0)
    before = _dot(tri_ref[...], onehot.astype(BF16)) + cnt_ref[0:1, :]
    cnt_ref[0:1, :] += jnp.sum(onehot, axis=0, keepdims=True)
    meta = jnp.zeros_like(logits)
    for slot, val in ((META_I1, i1), (META_I2, i2), (META_W1, w1), (META_W2, e2 * w1),
                      (META_R1, _lane_pick(before, lane, i1)), (META_R2, _lane_pick(before, lane, i2))):
        meta = jnp.where(lane == slot, val, meta)
    meta_ref[...] = meta


def router_top2(x, g, w_router):
    m, d = x.shape
    n_experts = w_router.shape[1]
    wr = jnp.zeros((d, LANES), F32).at[:, :n_experts].set(w_router)
    tm = _pick(m, (1024, 512, 256, 128))
    return pl.pallas_call(
        functools.partial(_router_kernel, n_experts=n_experts),
        grid=(m // tm,),
        in_specs=[pl.BlockSpec((tm, d), lambda i: (i, 0)),
                  pl.BlockSpec((1, d), lambda i: (0, 0)),
                  pl.BlockSpec((d, LANES), lambda i: (0, 0))],
        out_specs=[pl.BlockSpec((tm, LANES), lambda i: (i, 0)),
                   pl.BlockSpec((8, LANES), lambda i: (0, 0))],
        out_shape=[jax.ShapeDtypeStruct((m, LANES), F32), jax.ShapeDtypeStruct((8, LANES), F32)],
        scratch_shapes=[pltpu.VMEM((tm, tm), BF16)],
        compiler_params=_cparams("arbitrary"),
    )(x, g.reshape(1, d), wr)


def _idx_copy(dest_hbm, idx_smem, isem, tile, slot):
    return pltpu.make_async_copy(dest_hbm.at[tile], idx_smem.at[slot], isem.at[slot])


def _dispatch_kernel(ends_ref, dest_hbm, x_ref, xs_hbm, idx_smem, zbuf, isem, sem, zsem, *, tm, tg):
    i = pl.program_id(0)
    slot = i % 2
    n_experts = ends_ref.shape[0]
    n_out_tiles = xs_hbm.shape[0] // tg

    @pl.when(i == 0)
    def _():
        _idx_copy(dest_hbm, idx_smem, isem, 0, 0).start()
        zbuf[...] = jnp.zeros_like(zbuf)
        fills = []
        for e in range(n_experts):
            group_start = ends_ref[e - 1] if e > 0 else 0
            fills.append((ends_ref[e] > group_start, ends_ref[e] - tg))
            tail = ends_ref[n_experts - 1] + e * tg
            fills.append((tail < n_out_tiles * tg, tail))
        zero_fill = lambda start: pltpu.make_async_copy(zbuf, xs_hbm.at[pl.ds(pl.multiple_of(start, tg), tg)], zsem)
        for cond, start in fills:
            @pl.when(cond)
            def _():
                zero_fill(start).start()
        for cond, start in fills:
            @pl.when(cond)
            def _():
                zero_fill(start).wait()

    @pl.when(i + 1 < pl.num_programs(0))
    def _():
        _idx_copy(dest_hbm, idx_smem, isem, i + 1, 1 - slot).start()

    _idx_copy(dest_hbm, idx_smem, isem, i, slot).wait()

    def row_copy(t, k):
        d = idx_smem[slot, TOP_K * t + k]
        return pltpu.make_async_copy(x_ref.at[pl.ds(t, 1)], xs_hbm.at[pl.ds(d, 1)], sem)

    def issue(t, carry):
        for k in range(TOP_K):
            row_copy(t, k).start(priority=k % 2)
        return carry

    lax.fori_loop(0, tm, issue, 0, unroll=8)
    for _ in range(TOP_K):
        pltpu.make_async_copy(x_ref, xs_hbm.at[pl.ds(0, tm)], sem).wait()


def moe_dispatch(x, dest, ends, n_rows, tg):
    m, d = x.shape
    n_tiles, per_tile = dest.shape
    tm = per_tile // TOP_K
    grid_spec = pltpu.PrefetchScalarGridSpec(
        num_scalar_prefetch=1,
        grid=(n_tiles,),
        in_specs=[pl.BlockSpec(memory_space=pl.ANY),
                  pl.BlockSpec((tm, d), lambda i, ends: (i, 0))],
        out_specs=pl.BlockSpec(memory_space=pl.ANY),
        scratch_shapes=[pltpu.SMEM((2, per_tile), jnp.int32), pltpu.VMEM((tg, d), F32),
                        pltpu.SemaphoreType.DMA((2,)), pltpu.SemaphoreType.DMA(()), pltpu.SemaphoreType.DMA(())],
    )
    return pl.pallas_call(
        functools.partial(_dispatch_kernel, tm=tm, tg=tg),
        grid_spec=grid_spec,
        out_shape=jax.ShapeDtypeStruct((n_rows, d), F32),
        compiler_params=_cparams("arbitrary"),
    )(ends, dest, x)


def _moe_ffn_kernel(te_ref, nu_ref, x_ref, g_ref, wgu_ref, wd_ref, o_ref):
    del te_ref

    @pl.when(pl.program_id(0) < nu_ref[0])
    def _():
        o_ref[...] = _swiglu(_rms(x_ref[...], g_ref[...]).astype(BF16), wgu_ref.at[0], wd_ref.at[0])

    @pl.when(pl.program_id(0) >= nu_ref[0])
    def _():
        o_ref[...] = jnp.zeros_like(o_ref)


def moe_grouped_ffn(xs, g, tile_expert, n_used, w_gu, w_down, tm):
    p, d = xs.shape
    f = w_down.shape[1]
    last = lambda n, nu: jnp.minimum(n, nu[0] - 1)
    grid_spec = pltpu.PrefetchScalarGridSpec(
        num_scalar_prefetch=2,
        grid=(p // tm,),
        in_specs=[pl.BlockSpec((tm, d), lambda n, te, nu: (last(n, nu), 0)),
                  pl.BlockSpec((1, d), lambda n, te, nu: (0, 0)),
                  pl.BlockSpec((1, d, 2 * f), lambda n, te, nu: (te[n], 0, 0), pipeline_mode=pl.Buffered(1)),
                  pl.BlockSpec((1, f, d), lambda n, te, nu: (te[n], 0, 0), pipeline_mode=pl.Buffered(1))],
        out_specs=pl.BlockSpec((tm, d), lambda n, te, nu: (n, 0)),
    )
    return pl.pallas_call(
        _moe_ffn_kernel,
        grid_spec=grid_spec,
        out_shape=jax.ShapeDtypeStruct((p, d), F32),
        compiler_params=_cparams("arbitrary"),
    )(tile_expert, n_used, xs, g.reshape(1, d), w_gu, w_down)


def _combine_kernel(dest_hbm, x_ref, meta_ref, ys_hbm, o_ref, idx_smem, ybuf, isem, sem, *, tm, n_tiles):
    i = pl.program_id(0)
    slot = i % 2

    def row_copy(sl, t, k):
        d = idx_smem[sl, TOP_K * t + k]
        return pltpu.make_async_copy(ys_hbm.at[pl.ds(d, 1)], ybuf.at[sl, k, pl.ds(t, 1)], sem.at[sl])

    def gather_tile(tile, sl):
        _idx_copy(dest_hbm, idx_smem, isem, tile, sl).wait()

        def issue(t, carry):
            for k in range(TOP_K):
                row_copy(sl, t, k).start(priority=k % 2)
            return carry

        lax.fori_loop(0, tm, issue, 0, unroll=8)

    @pl.when(i == 0)
    def _():
        _idx_copy(dest_hbm, idx_smem, isem, 0, 0).start()
        gather_tile(0, 0)
        if n_tiles > 1:
            _idx_copy(dest_hbm, idx_smem, isem, 1, 1).start()

    @pl.when(i + 1 < pl.num_programs(0))
    def _():
        gather_tile(i + 1, 1 - slot)

    @pl.when(i + 2 < pl.num_programs(0))
    def _():
        _idx_copy(dest_hbm, idx_smem, isem, i + 2, slot).start()

    for k in range(TOP_K):
        pltpu.make_async_copy(ys_hbm.at[pl.ds(0, tm)], ybuf.at[slot, k], sem.at[slot]).wait()
    meta = meta_ref[...]
    lane = lax.broadcasted_iota(jnp.int32, meta.shape, 1)
    w1 = _lane_pick(meta, lane, META_W1)
    w2 = _lane_pick(meta, lane, META_W2)
    o_ref[...] = x_ref[...] + w1 * ybuf[slot, 0] + w2 * ybuf[slot, 1]


def moe_combine(x, meta, dest, ys):
    m, d = x.shape
    n_tiles, per_tile = dest.shape
    tm = per_tile // TOP_K
    return pl.pallas_call(
        functools.partial(_combine_kernel, tm=tm, n_tiles=n_tiles),
        grid=(n_tiles,),
        in_specs=[pl.BlockSpec(memory_space=pl.ANY),
                  pl.BlockSpec((tm, d), lambda i: (i, 0)),
                  pl.BlockSpec((tm, LANES), lambda i: (i, 0)),
                  pl.BlockSpec(memory_space=pl.ANY)],
        out_specs=pl.BlockSpec((tm, d), lambda i: (i, 0)),
        out_shape=jax.ShapeDtypeStruct((m, d), F32),
        scratch_shapes=[pltpu.SMEM((2, per_tile), jnp.int32), pltpu.VMEM((2, TOP_K, tm, d), F32),
                        pltpu.SemaphoreType.DMA((2,)), pltpu.SemaphoreType.DMA((2,))],
        compiler_params=_cparams("arbitrary"),
    )(dest, x, meta, ys)


def ffn_moe(x, g, w_router, w_gu, w_down):
    m, d = x.shape
    n_experts = w_down.shape[0]
    tg = _pick(m, (MOE_GROUP_TILE, 256, 128))
    td = _pick(m, (MOE_TOKEN_TILE, 256, 128))
    meta, counts = router_top2(x, g, w_router)
    counts = counts[0, :n_experts].astype(jnp.int32)
    padded = (counts + tg - 1) // tg * tg
    ends = jnp.cumsum(padded)
    offsets = ends - padded
    n_tiles = (TOP_K * m) // tg + n_experts
    n_used = (ends[-1] // tg).astype(jnp.int32)
    tile_id = jnp.minimum(jnp.arange(n_tiles, dtype=jnp.int32), n_used - 1)
    tile_expert = jnp.sum(ends[None, :] <= (tile_id * tg)[:, None], axis=1).astype(jnp.int32)
    picks = meta[:, META_I1:META_I2 + 1].astype(jnp.int32)
    ranks = meta[:, META_R1:META_R2 + 1].astype(jnp.int32)
    dest = (offsets[picks] + ranks).reshape(m // td, TOP_K * td)
    xs = moe_dispatch(x, dest, ends.astype(jnp.int32), n_tiles * tg, tg)
    ys = moe_grouped_ffn(xs, g, tile_expert, n_used.reshape(1), w_gu, w_down, tg)
    return moe_combine(x, meta, dest, ys)


def _block_diag(w_a, w_b):
    za = jnp.zeros_like(w_a)
    zb = jnp.zeros_like(w_b)
    return jnp.concatenate([jnp.concatenate([w_a, za], axis=1), jnp.concatenate([zb, w_b], axis=1)], axis=0)


def kernel(x, mem, positions, mix_norm, in_proj, shift_mu, decay_bias, decay_up, iclr_bias, iclr_up, gate_up,
           key_kk_scale, key_iclr_scale, bonus_rk, rwkv_gn_w, rwkv_gn_b, vres_down, vres_shift_mu, vres_bias,
           vres_up, q_norm, k_norm, lambda_q1, lambda_k1, lambda_q2, lambda_k2, diff_subln, out_proj, xattn_norm,
           mem_norm, xattn_wq, xattn_wkv, xattn_wo, xattn_q_norm, xattn_k_norm, ffn_norm, dense_w_gu,
           dense_w_down, router, expert_w_gu, expert_w_down):
    b, s, d = x.shape
    depth = mix_norm.shape[0]
    width = decay_bias.shape[1]
    rwkv_cols = shift_mu.shape[1]
    in_cols = in_proj.shape[2]
    vres_rank = vres_down.shape[2] if depth > 1 else 0
    assert decay_up.shape[1] == LANES // 2 and iclr_up.shape[1] == LANES // 2 and gate_up.shape[1] == LANES
    assert rwkv_cols == 3 * width + 2 * LANES and rwkv_cols % LANES == 0 and in_cols % LANES == 0
    assert vres_rank <= LANES
    rot = rope_table(positions)
    xf = x.reshape(b * s, d)
    memf = mem.reshape(-1, d)
    v_first = None
    for l in range(depth):
        w_rwkv = in_proj[l][:, :rwkv_cols]
        vres = None
        if l > 0:
            pad = jnp.zeros((d, LANES - vres_rank), F32)
            w_rwkv = jnp.concatenate([w_rwkv, vres_down[l - 1], pad], axis=1)
            vmu = jnp.zeros((LANES,), F32).at[:vres_rank].set(vres_shift_mu[l - 1])
            vup = jnp.zeros((LANES, width), F32).at[:vres_rank].set(vres_up[l - 1]).astype(BF16)
            vres = (v_first, vmu, vres_bias[l - 1], vup)
        wwa = _block_diag(decay_up[l], iclr_up[l]).astype(BF16)
        r_, lw_, k_, v_, a_, g_ = rwkv_in_proj(xf, s, mix_norm[l], w_rwkv.astype(BF16), width, shift_mu[l],
                                               decay_bias[l], iclr_bias[l], wwa, gate_up[l].astype(BF16), vres)
        if l == 0:
            v_first = v_
        seq3 = lambda t: t.reshape(b, s, width)
        y_rwkv = rwkv_chunked(seq3(r_), seq3(lw_), seq3(k_), seq3(v_), seq3(a_), seq3(g_), key_kk_scale[l],
                              key_iclr_scale[l], bonus_rk[l].reshape(-1), rwkv_gn_w[l], rwkv_gn_b[l])
        lam_init = 0.8 - 0.6 * math.exp(-0.3 * l)
        lam_vecs = jnp.stack([lambda_q1[l], lambda_k1[l], lambda_q2[l], lambda_k2[l]])
        y_diff = diff_attention(xf, b, mix_norm[l], in_proj[l][:, rwkv_cols:].astype(BF16), rot, q_norm[l],
                                k_norm[l], lam_vecs, diff_subln[l], lam_init)
        w_out = out_proj[l].astype(BF16)
        kv = norm_matmul(memf, mem_norm[l], xattn_wkv[l].astype(BF16)).reshape(b, -1, 2 * d)
        xf = mixer_out_cross_attention(xf.reshape(b, s, d), y_rwkv, y_diff, w_out[:width], w_out[width:],
                                       xattn_norm[l], xattn_wq[l].astype(BF16), kv, xattn_q_norm[l],
                                       xattn_k_norm[l], xattn_wo[l].astype(BF16)).reshape(b * s, d)
        if l % 2 == 0:
            xf = ffn_dense(xf, ffn_norm[l], dense_w_gu[l // 2].astype(BF16), dense_w_down[l // 2].astype(BF16))
        else:
            xf = ffn_moe(xf, ffn_norm[l], router[l // 2], expert_w_gu[l // 2].astype(BF16),
                         expert_w_down[l // 2])
    return xf.reshape(b, s, d)
```
